```python
import math
import jax
import jax.numpy as jnp
from jax import lax
import numpy as np

D_MODEL = 2048
BATCH = 8
SEQ = 2048
DEPTH = 2
DEC_BATCH = 32
DEC_SEQ = 1
PAST_LEN = 8192
PAGE_SIZE = 128

EPS = 1e-6
NEG_INF = -1e30
F32 = jnp.float32

POOL_WINDOWS = (2, 4, 8, 16)
POOL_GROUP = D_MODEL // 8
POOL_WIDTH = POOL_GROUP * len(POOL_WINDOWS)
POOL_HIST = max(POOL_WINDOWS) - 1

DIL_CONFIGS = ((128, 1), (512, 4), (2048, 16))
N_DIL = len(DIL_CONFIGS)
DIL_HEAD_DIM = 128
DIL_HEADS = D_MODEL // 256
DIL_WIDTH = DIL_HEADS * DIL_HEAD_DIM
DIL_BLOCK = 128

GLA_HEADS = 4
GLA_KEY_WIDTH = D_MODEL // 2
GLA_VAL_WIDTH = D_MODEL
GLA_DK = GLA_KEY_WIDTH // GLA_HEADS
GLA_DV = GLA_VAL_WIDTH // GLA_HEADS
GLA_RANK = 16
GLA_TAU = 16.0
GLA_CHUNK = 64

N_BRANCH = 3
COL_SIZES = (POOL_WIDTH,) + (DIL_WIDTH,) * (3 * N_DIL) + (GLA_KEY_WIDTH, GLA_KEY_WIDTH, GLA_VAL_WIDTH, GLA_VAL_WIDTH, GLA_RANK, N_BRANCH * D_MODEL)
N_IN = sum(COL_SIZES)

REL_BUCKETS = 32
REL_MAX_DIST = 2048
N_REL_HEADS = N_DIL * DIL_HEADS

MEM_LEN = 256
X_HEADS = 4
X_HEAD_DIM = D_MODEL // 8
X_WIDTH = X_HEADS * X_HEAD_DIM

D_FF = 11 * D_MODEL // 4
FFN_CONV = 3

kernel_name = 'hybrid_pool_dilated_gla_decoder_step'


def rmsnorm(x, g):
    x32 = x.astype(F32)
    y = x32 * lax.rsqrt(jnp.mean(x32 * x32, axis=-1, keepdims=True) + EPS) * g.astype(F32)
    return y.astype(x.dtype)


def rel_bucket(dist):
    max_exact = REL_BUCKETS // 2
    scaled = jnp.log(jnp.maximum(dist, 1).astype(F32) / max_exact) / math.log(REL_MAX_DIST / max_exact)
    large = jnp.minimum(max_exact + (scaled * (REL_BUCKETS - max_exact)).astype(jnp.int32), REL_BUCKETS - 1)
    return jnp.where(dist < max_exact, dist, large)


def dilated_slot_biases(rel_bias):
    out = []
    for g, (win, dil) in enumerate(DIL_CONFIGS):
        dist = jnp.arange(win // dil + 1, dtype=jnp.int32) * dil
        b = rel_bias[rel_bucket(dist), g * DIL_HEADS:(g + 1) * DIL_HEADS]
        out.append(b.T.astype(F32))
    return out


def pool_mix(u_ctx, pos0, pool_w, pool_scale):
    B_, Lc, _ = u_ctx.shape
    T = Lc - POOL_HIST
    cs = jnp.pad(jnp.cumsum(u_ctx.astype(F32), axis=1), ((0, 0), (1, 0), (0, 0)))
    pos = pos0 + jnp.arange(T)
    u = u_ctx[:, POOL_HIST:].astype(F32)
    end = POOL_HIST + 1
    outs = []
    for g, w in enumerate(POOL_WINDOWS):
        sl = slice(g * POOL_GROUP, (g + 1) * POOL_GROUP)
        win_sum = cs[:, end:end + T, sl] - cs[:, end - w:end - w + T, sl]
        cnt = jnp.minimum(w, pos + 1).astype(F32)[None, :, None]
        outs.append(win_sum / cnt - u[..., sl])
    d = jnp.stack(outs, axis=2)
    y = jnp.einsum('btgc,gcd->btgd', d, pool_w.astype(F32)).reshape(B_, T, POOL_WIDTH) * pool_scale.astype(F32)
    return y.astype(u_ctx.dtype)


def dilated_attn_prompt(q, k, v, dil, n_slots, bias_hj):
    B_, S_, H, E = q.shape
    L = S_ // dil
    nb = -(-L // DIL_BLOCK)
    Lp = nb * DIL_BLOCK

    def to_sub(t):
        t = t.reshape(B_, L, dil, H, E).transpose(0, 2, 1, 3, 4)
        return jnp.pad(t, ((0, 0), (0, 0), (0, Lp - L), (0, 0), (0, 0)))

    def band(t):
        t = jnp.pad(t, ((0, 0), (0, 0), (DIL_BLOCK, 0), (0, 0), (0, 0))).reshape(B_, dil, nb + 1, DIL_BLOCK, H, E)
        return jnp.concatenate([t[:, :, :-1], t[:, :, 1:]], axis=3)

    qb = to_sub(q).reshape(B_, dil, nb, DIL_BLOCK, H, E)
    kb = band(to_sub(k))
    vb = band(to_sub(v))
    s = jnp.einsum('bdnqhe,bdnkhe->bdnhqk', qb, kb, preferred_element_type=F32) * DIL_HEAD_DIM ** -0.5
    qi = jnp.arange(DIL_BLOCK)[:, None] + DIL_BLOCK
    ki = jnp.arange(2 * DIL_BLOCK)[None, :]
    rel = qi - ki
    key_ok = (jnp.arange(nb)[:, None, None] * DIL_BLOCK + ki[None] - DIL_BLOCK) >= 0
    ok = (rel >= 0) & (rel <= n_slots) & key_ok
    s = s + bias_hj[:, jnp.clip(rel, 0, n_slots)][None, None, None]
    s = jnp.where(ok[None, None, :, None], s, NEG_INF)
    lse = jax.nn.logsumexp(s, axis=-1)
    p = jnp.exp(s - lse[..., None])
    o = jnp.einsum('bdnhqk,bdnkhe->bdnqhe', p, vb.astype(F32))
    o = o.reshape(B_, dil, Lp, H, E)[:, :, :L].transpose(0, 2, 1, 3, 4).reshape(B_, S_, H, E)
    lse = lse.transpose(0, 1, 2, 4, 3).reshape(B_, dil, Lp, H)[:, :, :L].transpose(0, 2, 1, 3).reshape(B_, S_, H)
    return o, lse


def dilated_attn_step(q, kc, vc, dil, n_slots, bias_hj, n_past):
    T = q.shape[1]
    idx = n_past + jnp.arange(T)[:, None] - jnp.arange(n_slots + 1)[None, :] * dil
    ok = idx >= 0
    idx = jnp.maximum(idx, 0)
    kg = kc[:, idx]
    vg = vc[:, idx]
    s = jnp.einsum('bthe,btjhe->bthj', q, kg, preferred_element_type=F32) * DIL_HEAD_DIM ** -0.5 + bias_hj[None, None]
    s = jnp.where(ok[None, :, None, :], s, NEG_INF)
    lse = jax.nn.logsumexp(s, axis=-1)
    p = jnp.exp(s - lse[..., None])
    o = jnp.einsum('bthj,btjhe->bthe', p, vg.astype(F32))
    return o, lse


def gla_chunked(q, k, v, log_a):
    B_, T, H, K = q.shape
    V = v.shape[-1]
    C = GLA_CHUNK
    n = T // C
    q, k, v, log_a = [t.reshape(B_, n, C, H, t.shape[-1]) for t in (q, k, v, log_a)]
    b = jnp.cumsum(log_a, axis=2)
    b_last = b[:, :, -1:]
    q_t = q * jnp.exp(b)
    k_t = k * jnp.exp(-b)
    k_h = k * jnp.exp(b_last - b)
    decay = jnp.exp(b_last[:, :, 0])
    A = jnp.einsum('bnchk,bnshk->bnhcs', q_t, k_t)
    A = jnp.where(jnp.tril(jnp.ones((C, C), dtype=bool)), A, 0.0)
    o_intra = jnp.einsum('bnhcs,bnshv->bnchv', A, v)

    def step(S, xs):
        qc, kc, vc, dc = xs
        o = jnp.einsum('bchk,bhkv->bchv', qc, S)
        S = dc[..., None] * S + jnp.einsum('bchk,bchv->bhkv', kc, vc)
        return S, o

    S0 = jnp.zeros((B_, H, K, V), F32)
    S, o_inter = lax.scan(step, S0, (jnp.moveaxis(q_t, 1, 0), jnp.moveaxis(k_h, 1, 0), jnp.moveaxis(v, 1, 0), jnp.moveaxis(decay, 1, 0)))
    o = o_intra + jnp.moveaxis(o_inter, 0, 1)
    return o.reshape(B_, T, H, V), S


def gla_step(q, k, v, log_a, S):
    def step(S, xs):
        qt, kt, vt, at = xs
        S = jnp.exp(at)[..., None] * S + kt[..., None] * vt[..., None, :]
        return S, jnp.einsum('bhk,bhkv->bhv', qt, S)

    S, o = lax.scan(step, S.astype(F32), tuple(jnp.moveaxis(t, 1, 0) for t in (q, k, v, log_a)))
    return jnp.moveaxis(o, 0, 1), S


def token_mixer(h, w_in, pool_w, pool_scale, gla_w_a2, gla_b_a, gla_norm, w_br_pool, w_br_dil, w_br_gla, w_mix_out, biases, hist, pos0):
    B_, T, _ = h.shape
    z = jnp.einsum('btd,dn->btn', h, w_in)
    points = [int(c) for c in np.cumsum(COL_SIZES)[:-1]]
    parts = jnp.split(z, points, axis=-1)
    u_pool = parts[0]
    dil_parts = parts[1:1 + 3 * N_DIL]
    gq, gk, gv, gr, ga, gates = parts[1 + 3 * N_DIL:]
    prompt = hist is None
    if prompt:
        pool_hist = jnp.zeros((B_, POOL_HIST, POOL_WIDTH), u_pool.dtype)
    else:
        pool_hist, dil_hists, gla_state = hist

    u_ctx = jnp.concatenate([pool_hist.astype(u_pool.dtype), u_pool], axis=1)
    y_pool = pool_mix(u_ctx, pos0, pool_w, pool_scale)
    new_pool = u_ctx[:, -POOL_HIST:]

    outs, lses, new_dil = [], [], []
    for g, (win, dil) in enumerate(DIL_CONFIGS):
        q, k, v = [t.reshape(B_, T, DIL_HEADS, DIL_HEAD_DIM) for t in dil_parts[3 * g:3 * g + 3]]
        if prompt:
            o, lse = dilated_attn_prompt(q, k, v, dil, win // dil, biases[g])
            new_dil.append(jnp.stack([k, v], axis=2)[:, -min(win, T):])
        else:
            kv_hist = dil_hists[g]
            kc = jnp.concatenate([kv_hist[:, :, 0], k.astype(kv_hist.dtype)], axis=1)
            vc = jnp.concatenate([kv_hist[:, :, 1], v.astype(kv_hist.dtype)], axis=1)
            o, lse = dilated_attn_step(q.astype(kv_hist.dtype), kc, vc, dil, win // dil, biases[g], kv_hist.shape[1])
            new_dil.append(jnp.stack([k, v], axis=2).astype(kv_hist.dtype))
        outs.append(o)
        lses.append(lse)
    wts = jax.nn.softmax(jnp.stack(lses, axis=0), axis=0)
    y_dil = jnp.sum(wts[..., None] * jnp.stack(outs, axis=0), axis=0).reshape(B_, T, DIL_WIDTH).astype(h.dtype)

    q = (gq.astype(F32) * GLA_DK ** -0.5).reshape(B_, T, GLA_HEADS, GLA_DK)
    k = gk.astype(F32).reshape(B_, T, GLA_HEADS, GLA_DK)
    v = gv.astype(F32).reshape(B_, T, GLA_HEADS, GLA_DV)
    log_a = jax.nn.log_sigmoid(jnp.einsum('btr,rk->btk', ga.astype(F32), gla_w_a2.astype(F32)) + gla_b_a.astype(F32)) / GLA_TAU
    log_a = log_a.reshape(B_, T, GLA_HEADS, GLA_DK)
    if prompt:
        o, S = gla_chunked(q, k, v, log_a)
        new_gla = S.astype(h.dtype)
    else:
        o, S = gla_step(q, k, v, log_a, gla_state)
        new_gla = S.astype(gla_state.dtype)
    o = o * lax.rsqrt(jnp.mean(o * o, axis=-1, keepdims=True) + EPS) * gla_norm.astype(F32).reshape(GLA_HEADS, GLA_DV)
    y_gla = (o.reshape(B_, T, GLA_VAL_WIDTH) * jax.nn.silu(gr.astype(F32))).astype(h.dtype)

    bp = jnp.einsum('btc,cd->btd', y_pool, w_br_pool).astype(F32)
    bd = jnp.einsum('btc,cd->btd', y_dil, w_br_dil).astype(F32)
    bg = jnp.einsum('btc,cd->btd', y_gla, w_br_gla).astype(F32)
    gate = jax.nn.sigmoid(gates.astype(F32)).reshape(B_, T, N_BRANCH, D_MODEL)
    merged = gate[:, :, 0] * bp + gate[:, :, 1] * bd + gate[:, :, 2] * bg
    y = jnp.einsum('btd,de->bte', merged.astype(h.dtype), w_mix_out)
    return y, new_pool, new_dil, new_gla


def memory_kv(mem, g, w_xkv):
    B_, M, _ = mem.shape
    kv = jnp.einsum('bmd,dn->bmn', rmsnorm(mem, g), w_xkv)
    return kv.reshape(B_, M, 2, X_HEADS, X_HEAD_DIM)


def cross_attn(h, mem_kv, w_xq, w_xo):
    B_, T, _ = h.shape
    q = jnp.einsum('btd,dn->btn', h, w_xq).reshape(B_, T, X_HEADS, X_HEAD_DIM)
    s = jnp.einsum('bthe,bmhe->bhtm', q, mem_kv[:, :, 0].astype(q.dtype), preferred_element_type=F32) * X_HEAD_DIM ** -0.5
    p = jax.nn.softmax(s, axis=-1)
    o = jnp.einsum('bhtm,bmhe->bthe', p, mem_kv[:, :, 1].astype(F32))
    return jnp.einsum('btn,nd->btd', o.reshape(B_, T, X_WIDTH).astype(h.dtype), w_xo)


def conv_ffn(h, w_up, conv_w, conv_b, w_down, conv_hist):
    B_, T, _ = h.shape
    u = jnp.einsum('btd,df->btf', h, w_up)
    if conv_hist is None:
        conv_hist = jnp.zeros((B_, FFN_CONV - 1, 2 * D_FF), u.dtype)
    u_ctx = jnp.concatenate([conv_hist.astype(u.dtype), u], axis=1)
    c = conv_b.astype(F32)
    for i in range(FFN_CONV):
        c = c + conv_w[i].astype(F32) * u_ctx[:, i:i + T].astype(F32)
    gate, val = jnp.split(c, 2, axis=-1)
    y = jnp.einsum('btf,fd->btd', (jax.nn.gelu(gate) * val).astype(h.dtype), w_down)
    return y, u_ctx[:, -(FFN_CONV - 1):]


def decoder_layer(x, mem_kv, hist, conv_hist, pos0, biases, lw):
    (n_mix_pre, n_mix_post, w_in, pool_w, pool_scale, gla_w_a2, gla_b_a, gla_norm, w_br_pool, w_br_dil, w_br_gla, w_mix_out,
     n_x_pre, n_x_post, w_xq, w_xo, n_ffn_pre, n_ffn_post, w_up, conv_w, conv_b, w_down) = lw
    y, new_pool, new_dil, new_gla = token_mixer(rmsnorm(x, n_mix_pre), w_in, pool_w, pool_scale, gla_w_a2, gla_b_a, gla_norm,
                                                w_br_pool, w_br_dil, w_br_gla, w_mix_out, biases, hist, pos0)
    x = x + rmsnorm(y, n_mix_post)
    x = x + rmsnorm(cross_attn(rmsnorm(x, n_x_pre), mem_kv, w_xq, w_xo), n_x_post)
    y, new_conv = conv_ffn(rmsnorm(x, n_ffn_pre), w_up, conv_w, conv_b, w_down, conv_hist)
    x = x + rmsnorm(y, n_ffn_post)
    return x, new_pool, new_dil, new_gla, new_conv


def setup_inputs(seed: int = 0) -> dict:
    key = jax.random.key(seed)
    keys = jax.random.split(key, 48)
    counter = iter(range(48))

    def nrm(shape, scale=1.0):
        return jax.random.normal(keys[next(counter)], shape, jnp.float32) * scale

    def gain(shape):
        return 1.0 + nrm(shape, 0.05)

    L = DEPTH
    x_prompt = nrm((BATCH, SEQ, D_MODEL))
    x_sample = nrm((DEC_BATCH, DEC_SEQ, D_MODEL))
    state_pool = nrm((L, DEC_BATCH, POOL_HIST, POOL_WIDTH))
    dil_caches = [nrm((L, DEC_BATCH, min(win, PAST_LEN), 2, DIL_HEADS, DIL_HEAD_DIM)) for (win, _) in DIL_CONFIGS]
    state_gla = nrm((L, DEC_BATCH, GLA_HEADS, GLA_DK, GLA_DV))
    cache_mem_kv = nrm((L, DEC_BATCH, MEM_LEN, 2, X_HEADS, X_HEAD_DIM))
    state_ffn_conv = nrm((L, DEC_BATCH, FFN_CONV - 1, 2 * D_FF))
    mem_prompt = nrm((BATCH, MEM_LEN, D_MODEL))
    return {
        'x_prompt': x_prompt,
        'x_sample': x_sample,
        'state_pool': state_pool,
        'cache_dil1_kv': dil_caches[0],
        'cache_dil2_kv': dil_caches[1],
        'cache_dil3_kv': dil_caches[2],
        'state_gla': state_gla,
        'cache_mem_kv': cache_mem_kv,
        'state_ffn_conv': state_ffn_conv,
        'mem_prompt': mem_prompt,
        'rel_bias': nrm((REL_BUCKETS, N_REL_HEADS), 0.2),
        'norm_mix_pre': gain((L, D_MODEL)),
        'norm_mix_post': gain((L, D_MODEL)),
        'w_in': nrm((L, D_MODEL, N_IN), D_MODEL ** -0.5),
        'pool_w': nrm((L, len(POOL_WINDOWS), POOL_GROUP, POOL_GROUP), POOL_GROUP ** -0.5),
        'pool_scale': 1.0 + nrm((L, POOL_WIDTH), 0.1),
        'gla_w_a2': nrm((L, GLA_RANK, GLA_KEY_WIDTH), GLA_RANK ** -0.5),
        'gla_b_a': nrm((L, GLA_KEY_WIDTH), 0.01),
        'gla_norm': gain((L, GLA_VAL_WIDTH)),
        'w_br_pool': nrm((L, POOL_WIDTH, D_MODEL), POOL_WIDTH ** -0.5),
        'w_br_dil': nrm((L, DIL_WIDTH, D_MODEL), DIL_WIDTH ** -0.5),
        'w_br_gla': nrm((L, GLA_VAL_WIDTH, D_MODEL), GLA_VAL_WIDTH ** -0.5),
        'w_mix_out': nrm((L, D_MODEL, D_MODEL), D_MODEL ** -0.5),
        'norm_x_pre': gain((L, D_MODEL)),
        'norm_x_post': gain((L, D_MODEL)),
        'norm_mem': gain((L, D_MODEL)),
        'w_xq': nrm((L, D_MODEL, X_WIDTH), D_MODEL ** -0.5),
        'w_xkv': nrm((L, D_MODEL, 2 * X_WIDTH), D_MODEL ** -0.5),
        'w_xo': nrm((L, X_WIDTH, D_MODEL), X_WIDTH ** -0.5),
        'norm_ffn_pre': gain((L, D_MODEL)),
        'norm_ffn_post': gain((L, D_MODEL)),
        'w_up': nrm((L, D_MODEL, 2 * D_FF), D_MODEL ** -0.5),
        'conv_w': nrm((L, FFN_CONV, 2 * D_FF), FFN_CONV ** -0.5),
        'conv_b': nrm((L, 2 * D_FF), 0.01),
        'w_down': nrm((L, D_FF, D_MODEL), D_FF ** -0.5),
    }


def reference(x_prompt, x_sample, state_pool, cache_dil1_kv, cache_dil2_kv, cache_dil3_kv, state_gla, cache_mem_kv, state_ffn_conv,
              mem_prompt, rel_bias, norm_mix_pre, norm_mix_post, w_in, pool_w, pool_scale, gla_w_a2, gla_b_a, gla_norm,
              w_br_pool, w_br_dil, w_br_gla, w_mix_out, norm_x_pre, norm_x_post, norm_mem, w_xq, w_xkv, w_xo,
              norm_ffn_pre, norm_ffn_post, w_up, conv_w, conv_b, w_down):
    biases = dilated_slot_biases(rel_bias)
    dil_caches = (cache_dil1_kv, cache_dil2_kv, cache_dil3_kv)
    xp, xs = x_prompt, x_sample
    pool_p, gla_p, mem_p, conv_p = [], [], [], []
    pool_s, gla_s, conv_s = [], [], []
    dil_p = [[] for _ in range(N_DIL)]
    dil_s = [[] for _ in range(N_DIL)]
    for l in range(DEPTH):
        lw = (norm_mix_pre[l], norm_mix_post[l], w_in[l], pool_w[l], pool_scale[l], gla_w_a2[l], gla_b_a[l], gla_norm[l],
              w_br_pool[l], w_br_dil[l], w_br_gla[l], w_mix_out[l], norm_x_pre[l], norm_x_post[l], w_xq[l], w_xo[l],
              norm_ffn_pre[l], norm_ffn_post[l], w_up[l], conv_w[l], conv_b[l], w_down[l])
        mem_kv_p = memory_kv(mem_prompt, norm_mem[l], w_xkv[l])
        xp, pool_new, dil_new, gla_new, conv_new = decoder_layer(xp, mem_kv_p, None, None, 0, biases, lw)
        pool_p.append(pool_new)
        gla_p.append(gla_new)
        mem_p.append(mem_kv_p)
        conv_p.append(conv_new)
        for g in range(N_DIL):
            dil_p[g].append(dil_new[g])
        hist = (state_pool[l], [c[l] for c in dil_caches], state_gla[l])
        xs, pool_new, dil_new, gla_new, conv_new = decoder_layer(xs, cache_mem_kv[l], hist, state_ffn_conv[l], PAST_LEN, biases, lw)
        pool_s.append(pool_new)
        gla_s.append(gla_new)
        conv_s.append(conv_new)
        for g in range(N_DIL):
            dil_s[g].append(dil_new[g])
    return (xp, xs,
            jnp.stack(pool_p), jnp.stack(dil_p[0]), jnp.stack(dil_p[1]), jnp.stack(dil_p[2]), jnp.stack(gla_p), jnp.stack(mem_p), jnp.stack(conv_p),
            jnp.stack(pool_s), jnp.stack(dil_s[0]), jnp.stack(dil_s[1]), jnp.stack(dil_s[2]), jnp.stack(gla_s), jnp.stack(conv_s))
```

```python
import functools
import math

import jax
import jax.numpy as jnp
from jax import lax
from jax.experimental import pallas as pl
from jax.experimental.pallas import tpu as pltpu

F32 = jnp.float32
BF16 = jnp.bfloat16
EPS = 1e-6
NEG_INF = -1e30

VMEM_LIMIT_BYTES = 56 * 1024 * 1024
LANES = 128

D_MODEL = 2048
POOL_WINDOWS = (2, 4, 8, 16)
POOL_GROUP = 256
POOL_WIDTH = 1024
POOL_HIST = 15
DIL_CONFIGS = ((128, 1), (512, 4), (2048, 16))
DIL_HEADS = 8
DIL_HEAD_DIM = 128
DIL_WIDTH = 1024
DIL_BLOCK = 128
GLA_HEADS = 4
GLA_DK = 256
GLA_DV = 512
GLA_KEY_WIDTH = 1024
GLA_VAL_WIDTH = 2048
GLA_RANK = 16
GLA_TAU = 16.0
GLA_CHUNK = 64
REL_BUCKETS = 32
REL_MAX_DIST = 2048
MEM_LEN = 256
X_HEADS = 4
X_HEAD_DIM = 256
X_WIDTH = 1024
D_FF = 5632
PAST_LEN = 8192

COL_POOL = 0
COL_DIL_Q = tuple(1024 + 3072 * g for g in range(3))
COL_DIL_K = tuple(2048 + 3072 * g for g in range(3))
COL_DIL_V = tuple(3072 + 3072 * g for g in range(3))
COL_GQ = 10240
COL_GK = 11264
COL_GV = 12288
COL_GR = 14336
COL_GATES = 16384
N_MAIN = 22528
COL_GA_SRC = 16384


def _params(*semantics):
    return pltpu.CompilerParams(dimension_semantics=semantics, vmem_limit_bytes=VMEM_LIMIT_BYTES)


def _sigmoid(x):
    return 1.0 / (1.0 + jnp.exp(-x))


def _log_sigmoid(x):
    return jnp.minimum(x, 0.0) - jnp.log(1.0 + jnp.exp(-jnp.abs(x)))


def _gelu_tanh(x):
    return x * (0.5 * (1.0 + jnp.tanh(math.sqrt(2.0 / math.pi) * (x + 0.044715 * (x * x * x)))))


def _rms_scale(y, g):
    return y * lax.rsqrt(jnp.mean(y * y, axis=-1, keepdims=True) + EPS) * g


def _norm_rows(x_ref, g_ref, xn_ref, rows, chunk, dst_offset=0):
    g = g_ref[...]

    def body(c, carry):
        r0 = pl.multiple_of(c * chunk, chunk)
        xn_ref[pl.ds(dst_offset + r0, chunk), :] = _rms_scale(x_ref[pl.ds(r0, chunk), :], g).astype(BF16)
        return carry

    lax.fori_loop(0, rows // chunk, body, 0)


def _norm_mm_kernel(x_ref, g_ref, w_ref, o_ref, xn_ref, *, tm, chunk):
    @pl.when(pl.program_id(1) == 0)
    def _():
        _norm_rows(x_ref, g_ref, xn_ref, tm, chunk)

    o_ref[...] = jnp.dot(xn_ref[...], w_ref[...], preferred_element_type=F32).astype(o_ref.dtype)


def norm_mm(x, g, w, *, tm, tn, out_dtype=F32):
    m, k = x.shape
    n = w.shape[1]
    tm = min(tm, m)
    tn = min(tn, n)
    assert m % tm == 0 and n % tn == 0
    chunk = min(256, tm)
    return pl.pallas_call(
        functools.partial(_norm_mm_kernel, tm=tm, chunk=chunk),
        grid=(m // tm, n // tn),
        in_specs=[
            pl.BlockSpec((tm, k), lambda i, j: (i, 0)),
            pl.BlockSpec((1, k), lambda i, j: (0, 0)),
            pl.BlockSpec((k, tn), lambda i, j: (0, j)),
        ],
        out_specs=pl.BlockSpec((tm, tn), lambda i, j: (i, j)),
        out_shape=jax.ShapeDtypeStruct((m, n), out_dtype),
        scratch_shapes=[pltpu.VMEM((tm, k), BF16)],
        compiler_params=_params("parallel", "arbitrary"),
        name="norm_mm",
    )(x, g.reshape(1, k), w)


def _mm_post_kernel(a_ref, w_ref, g_ref, res_ref, o_ref, acc_ref):
    kk = pl.program_id(1)
    part = jnp.dot(a_ref[...], w_ref[...], preferred_element_type=F32)

    @pl.when(kk == 0)
    def _():
        acc_ref[...] = part

    @pl.when(kk > 0)
    def _():
        acc_ref[...] += part

    @pl.when(kk == pl.num_programs(1) - 1)
    def _():
        o_ref[...] = res_ref[...] + _rms_scale(acc_ref[...], g_ref[...])


def mm_post(a, w, g, res, *, tm, tk):
    m, k = a.shape
    n = w.shape[1]
    tm = min(tm, m)
    assert m % tm == 0 and k % tk == 0
    return pl.pallas_call(
        _mm_post_kernel,
        grid=(m // tm, k // tk),
        in_specs=[
            pl.BlockSpec((tm, tk), lambda i, kk: (i, kk)),
            pl.BlockSpec((tk, n), lambda i, kk: (kk, 0)),
            pl.BlockSpec((1, n), lambda i, kk: (0, 0)),
            pl.BlockSpec((tm, n), lambda i, kk: (i, 0)),
        ],
        out_specs=pl.BlockSpec((tm, n), lambda i, kk: (i, 0)),
        out_shape=jax.ShapeDtypeStruct((m, n), F32),
        scratch_shapes=[pltpu.VMEM((tm, n), F32)],
        compiler_params=_params("parallel", "arbitrary"),
        name="mm_post",
    )(a, w, g.reshape(1, n), res)


def _branch_merge_kernel(yp_ref, yd_ref, yg_ref, wp_ref, wd_ref, wg_ref, g0_ref, g1_ref, g2_ref, o_ref):
    bp = jnp.dot(yp_ref[...], wp_ref[...], preferred_element_type=F32)
    bd = jnp.dot(yd_ref[...], wd_ref[...], preferred_element_type=F32)
    bg = jnp.dot(yg_ref[...], wg_ref[...], preferred_element_type=F32)
    merged = _sigmoid(g0_ref[...]) * bp + _sigmoid(g1_ref[...]) * bd + _sigmoid(g2_ref[...]) * bg
    o_ref[...] = merged.astype(o_ref.dtype)


def branch_merge(y_pool, y_dil, y_gla, w_pool, w_dil, w_gla, z_main, *, tm, tn):
    m = y_pool.shape[0]
    tm = min(tm, m)
    assert m % tm == 0 and D_MODEL % tn == 0
    gate_blk = [(COL_GATES + b * D_MODEL) // tn for b in range(3)]

    def gate_spec(b):
        return pl.BlockSpec((tm, tn), lambda i, j: (i, gate_blk[b] + j))

    return pl.pallas_call(
        _branch_merge_kernel,
        grid=(m // tm, D_MODEL // tn),
        in_specs=[
            pl.BlockSpec((tm, POOL_WIDTH), lambda i, j: (i, 0)),
            pl.BlockSpec((tm, DIL_WIDTH), lambda i, j: (i, 0)),
            pl.BlockSpec((tm, GLA_VAL_WIDTH), lambda i, j: (i, 0)),
            pl.BlockSpec((POOL_WIDTH, tn), lambda i, j: (0, j)),
            pl.BlockSpec((DIL_WIDTH, tn), lambda i, j: (0, j)),
            pl.BlockSpec((GLA_VAL_WIDTH, tn), lambda i, j: (0, j)),
            gate_spec(0),
            gate_spec(1),
            gate_spec(2),
        ],
        out_specs=pl.BlockSpec((tm, tn), lambda i, j: (i, j)),
        out_shape=jax.ShapeDtypeStruct((m, D_MODEL), BF16),
        compiler_params=_params("parallel", "arbitrary"),
        name="branch_merge",
    )(y_pool, y_dil, y_gla, w_pool, w_dil, w_gla, z_main, z_main, z_main)


def _pool_prompt_kernel(u_ref, w_ref, s_ref, o_ref, buf_a, buf_b, *, seq):
    pad = POOL_HIST + 1
    zeros = jnp.zeros((pad, POOL_GROUP), F32)
    t = lax.broadcasted_iota(jnp.int32, (seq, 1), 0)
    for g, win in enumerate(POOL_WINDOWS):
        cols = slice(g * POOL_GROUP, (g + 1) * POOL_GROUP)
        u = u_ref[:, cols]
        cur, nxt = buf_a, buf_b
        cur[0:pad, :] = zeros
        nxt[0:pad, :] = zeros
        cur[pad:pad + seq, :] = u
        k = 1
        while k < win:
            nxt[pad:pad + seq, :] = cur[pad:pad + seq, :] + cur[pad - k:pad - k + seq, :]
            cur, nxt = nxt, cur
            k *= 2
        cnt = jnp.minimum(win, t + 1).astype(F32)
        d = cur[pad:pad + seq, :] / cnt - u
        y = jnp.dot(d.astype(BF16), w_ref[g], preferred_element_type=F32) * s_ref[:, cols]
        o_ref[:, cols] = y.astype(o_ref.dtype)


def pool_prompt(z3, pool_w, pool_scale):
    b, seq, _ = z3.shape
    return pl.pallas_call(
        functools.partial(_pool_prompt_kernel, seq=seq),
        grid=(b,),
        in_specs=[
            pl.BlockSpec((None, seq, POOL_WIDTH), lambda i: (i, 0, 0)),
            pl.BlockSpec((len(POOL_WINDOWS), POOL_GROUP, POOL_GROUP), lambda i: (0, 0, 0)),
            pl.BlockSpec((1, POOL_WIDTH), lambda i: (0, 0)),
        ],
        out_specs=pl.BlockSpec((None, seq, POOL_WIDTH), lambda i: (i, 0, 0)),
        out_shape=jax.ShapeDtypeStruct((b, seq, POOL_WIDTH), BF16),
        scratch_shapes=[pltpu.VMEM((seq + POOL_HIST + 1, POOL_GROUP), F32)] * 2,
        compiler_params=_params("parallel"),
        name="pool_prompt",
    )(z3, pool_w, pool_scale.reshape(1, POOL_WIDTH))


def _pool_step_kernel(hist_ref, u_ref, w_ref, s_ref, y_ref, new_ref):
    u = u_ref[...]
    for r in range(POOL_HIST - 1):
        new_ref[r] = hist_ref[r + 1]
    new_ref[POOL_HIST - 1] = u
    for g, win in enumerate(POOL_WINDOWS):
        cols = slice(g * POOL_GROUP, (g + 1) * POOL_GROUP)
        ug = u[:, cols]
        acc = ug
        for r in range(POOL_HIST - (win - 1), POOL_HIST):
            acc = acc + hist_ref[r, :, cols]
        d = acc / float(win) - ug
        y = jnp.dot(d.astype(BF16), w_ref[g], preferred_element_type=F32) * s_ref[:, cols]
        y_ref[:, cols] = y.astype(y_ref.dtype)


def pool_step(hist_t, z_s, pool_w, pool_scale):
    bd = z_s.shape[0]
    return pl.pallas_call(
        _pool_step_kernel,
        grid=(1,),
        in_specs=[
            pl.BlockSpec((POOL_HIST, bd, POOL_WIDTH), lambda i: (0, 0, 0)),
            pl.BlockSpec((bd, POOL_WIDTH), lambda i: (0, 0)),
            pl.BlockSpec((len(POOL_WINDOWS), POOL_GROUP, POOL_GROUP), lambda i: (0, 0, 0)),
            pl.BlockSpec((1, POOL_WIDTH), lambda i: (0, 0)),
        ],
        out_specs=[
            pl.BlockSpec((bd, POOL_WIDTH), lambda i: (0, 0)),
            pl.BlockSpec((POOL_HIST, bd, POOL_WIDTH), lambda i: (0, 0, 0)),
        ],
        out_shape=[
            jax.ShapeDtypeStruct((bd, POOL_WIDTH), BF16),
            jax.ShapeDtypeStruct((POOL_HIST, bd, POOL_WIDTH), F32),
        ],
        compiler_params=_params("arbitrary"),
        name="pool_step",
    )(hist_t, z_s, pool_w, pool_scale.reshape(1, POOL_WIDTH))


def _rel_bucket(dist):
    max_exact = REL_BUCKETS // 2
    scaled = jnp.log(jnp.maximum(dist, 1).astype(F32) / max_exact) / math.log(REL_MAX_DIST / max_exact)
    large = jnp.minimum(max_exact + (scaled * (REL_BUCKETS - max_exact)).astype(jnp.int32), REL_BUCKETS - 1)
    return jnp.where(dist < max_exact, dist, large)


def _slot_biases(rel_bias):
    out = []
    for g, (win, dil) in enumerate(DIL_CONFIGS):
        dist = jnp.arange(win // dil + 1, dtype=jnp.int32) * dil
        out.append(rel_bias[_rel_bucket(dist), g * DIL_HEADS:(g + 1) * DIL_HEADS].T.astype(F32))
    return out


def _band_bias(slot_bias):
    qi = jnp.arange(DIL_BLOCK)[:, None] + DIL_BLOCK
    ki = jnp.arange(2 * DIL_BLOCK)[None, :]
    rel = qi - ki
    out = []
    for g, (win, dil) in enumerate(DIL_CONFIGS):
        n_slots = win // dil
        ok = (rel >= 0) & (rel <= n_slots)
        b = slot_bias[g][:, jnp.clip(rel, 0, n_slots)]
        out.append(jnp.where(ok[None], b, NEG_INF))
    return jnp.stack(out, axis=0)


def _dil_prompt_kernel(q0, k0, v0, q1, k1, v1, q2, k2, v2, bias_ref, o_ref, o_scr, lse_scr, *, seq):
    qs, ks, vs = (q0, q1, q2), (k0, k1, k2), (v0, v1, v2)
    scale = DIL_HEAD_DIM ** -0.5
    blk = DIL_BLOCK
    nt = (((1,), (1,)), ((), ()))

    def rows(ref, start, dil):
        if dil == 1:
            return ref[pl.ds(start, blk), :]
        return ref[pl.ds(start, blk, stride=dil), :]

    for g, (_, dil) in enumerate(DIL_CONFIGS):
        nb = seq // dil // blk
        for r in range(dil):
            for ub in range(nb):
                start = r + dil * ub * blk
                q = rows(qs[g], start, dil).astype(BF16)
                kc = rows(ks[g], start, dil).astype(BF16)
                vc = rows(vs[g], start, dil).astype(BF16)
                if ub == 0:
                    kk, vv, bias = kc, vc, bias_ref[g, :, blk:]
                else:
                    prev = start - dil * blk
                    kk = jnp.concatenate([rows(ks[g], prev, dil).astype(BF16), kc], axis=0)
                    vv = jnp.concatenate([rows(vs[g], prev, dil).astype(BF16), vc], axis=0)
                    bias = bias_ref[g]
                s = lax.dot_general(q, kk, nt, preferred_element_type=F32) * scale + bias
                m = jnp.max(s, axis=-1, keepdims=True)
                e = jnp.exp(s - m)
                l = jnp.sum(e, axis=-1, keepdims=True)
                o = jnp.dot(e.astype(BF16), vv, preferred_element_type=F32) / l
                lse = jnp.broadcast_to(m + jnp.log(l), (blk, LANES))
                if dil == 1:
                    o_scr[g, pl.ds(start, blk), :] = o
                    lse_scr[g, pl.ds(start, blk), :] = lse
                else:
                    o_scr[g, pl.ds(start, blk, stride=dil), :] = o
                    lse_scr[g, pl.ds(start, blk, stride=dil), :] = lse

    chunk = 256

    def combine(c, carry):
        sl = pl.ds(pl.multiple_of(c * chunk, chunk), chunk)
        l0, l1, l2 = lse_scr[0, sl, :], lse_scr[1, sl, :], lse_scr[2, sl, :]
        mx = jnp.maximum(jnp.maximum(l0, l1), l2)
        w0, w1, w2 = jnp.exp(l0 - mx), jnp.exp(l1 - mx), jnp.exp(l2 - mx)
        y = (w0 * o_scr[0, sl, :] + w1 * o_scr[1, sl, :] + w2 * o_scr[2, sl, :]) / (w0 + w1 + w2)
        o_ref[sl, :] = y.astype(o_ref.dtype)
        return carry

    lax.fori_loop(0, seq // chunk, combine, 0)


def dil_prompt(z3, band_bias):
    b, seq, _ = z3.shape
    assert seq % (DIL_BLOCK * 16) == 0

    def col_spec(col):
        blk0 = col // DIL_HEAD_DIM
        return pl.BlockSpec((None, seq, DIL_HEAD_DIM), lambda i, h: (i, 0, blk0 + h))

    in_specs = []
    for g in range(3):
        in_specs += [col_spec(COL_DIL_Q[g]), col_spec(COL_DIL_K[g]), col_spec(COL_DIL_V[g])]
    in_specs.append(pl.BlockSpec((3, None, DIL_BLOCK, 2 * DIL_BLOCK), lambda i, h: (0, h, 0, 0)))
    return pl.pallas_call(
        functools.partial(_dil_prompt_kernel, seq=seq),
        grid=(b, DIL_HEADS),
        in_specs=in_specs,
        out_specs=pl.BlockSpec((None, seq, DIL_HEAD_DIM), lambda i, h: (i, 0, h)),
        out_shape=jax.ShapeDtypeStruct((b, seq, DIL_WIDTH), BF16),
        scratch_shapes=[pltpu.VMEM((3, seq, DIL_HEAD_DIM), F32), pltpu.VMEM((3, seq, LANES), F32)],
        compiler_params=_params("parallel", "arbitrary"),
        name="dil_prompt",
    )(*([z3] * 9), band_bias)


def _dil_step_kernel(q0, k0, v0, q1, k1, v1, q2, k2, v2, c0, c1, c2, bias_ref, bias0_ref, o_ref):
    qs, ks, vs, caches = (q0, q1, q2), (k0, k1, k2), (v0, v1, v2), (c0, c1, c2)
    scale = DIL_HEAD_DIM ** -0.5
    outs, lses = [], []
    for g in range(3):
        q, kn, vn = qs[g][...], ks[g][...], vs[g][...]
        kc = caches[g][:, 0:DIL_WIDTH]
        prod = kc * q
        o_g, lse_g = [], []
        for h in range(DIL_HEADS):
            hs = slice(h * DIL_HEAD_DIM, (h + 1) * DIL_HEAD_DIM)
            s = jnp.sum(prod[:, hs], axis=-1, keepdims=True) * scale + bias_ref[g, h]
            s_new = jnp.sum(q[:, hs] * kn[:, hs], axis=-1, keepdims=True) * scale + bias0_ref[g, h:h + 1, :]
            m = jnp.maximum(jnp.max(s, axis=0, keepdims=True), s_new)
            p = jnp.exp(s - m)
            p_new = jnp.exp(s_new - m)
            l = jnp.sum(p, axis=0, keepdims=True) + p_new
            vc = caches[g][:, DIL_WIDTH + h * DIL_HEAD_DIM:DIL_WIDTH + (h + 1) * DIL_HEAD_DIM]
            o = (jnp.sum(p * vc, axis=0, keepdims=True) + p_new * vn[:, hs]) / l
            o_g.append(o)
            lse_g.append(m + jnp.log(l))
        outs.append(o_g)
        lses.append(lse_g)
    for h in range(DIL_HEADS):
        l0, l1, l2 = lses[0][h], lses[1][h], lses[2][h]
        mx = jnp.maximum(jnp.maximum(l0, l1), l2)
        w0, w1, w2 = jnp.exp(l0 - mx), jnp.exp(l1 - mx), jnp.exp(l2 - mx)
        y = (w0 * outs[0][h] + w1 * outs[1][h] + w2 * outs[2][h]) / (w0 + w1 + w2)
        o_ref[:, h * DIL_HEAD_DIM:(h + 1) * DIL_HEAD_DIM] = y


def dil_step(z_s, caches, slot_bias):
    bd = z_s.shape[0]
    zs3 = z_s.reshape(bd, 1, N_MAIN)
    row_w = 2 * DIL_WIDTH

    def col_spec(col):
        blk0 = col // DIL_WIDTH
        return pl.BlockSpec((None, 1, DIL_WIDTH), lambda i: (i, 0, blk0))

    in_specs, cache_views = [], []
    for g, (win, dil) in enumerate(DIL_CONFIGS):
        in_specs += [col_spec(COL_DIL_Q[g]), col_spec(COL_DIL_K[g]), col_spec(COL_DIL_V[g])]
        n_slots = win // dil
        assert caches[g].shape[1] == win
        cache_views.append(caches[g].reshape(bd, n_slots, dil * row_w))
    n_slots = DIL_CONFIGS[0][0] // DIL_CONFIGS[0][1]
    for _ in range(3):
        in_specs.append(pl.BlockSpec((None, n_slots, row_w), lambda i: (i, 0, 0)))
    bias_rows = jnp.stack([sb[:, n_slots:0:-1] for sb in slot_bias], axis=0)
    bias_rows = jnp.broadcast_to(bias_rows[..., None], (3, DIL_HEADS, n_slots, LANES))
    bias_new = jnp.broadcast_to(jnp.stack([sb[:, 0] for sb in slot_bias], axis=0)[..., None], (3, DIL_HEADS, LANES))
    in_specs.append(pl.BlockSpec((3, DIL_HEADS, n_slots, LANES), lambda i: (0, 0, 0, 0)))
    in_specs.append(pl.BlockSpec((3, DIL_HEADS, LANES), lambda i: (0, 0, 0)))
    out = pl.pallas_call(
        _dil_step_kernel,
        grid=(bd,),
        in_specs=in_specs,
        out_specs=pl.BlockSpec((None, 1, DIL_WIDTH), lambda i: (i, 0, 0)),
        out_shape=jax.ShapeDtypeStruct((bd, 1, DIL_WIDTH), F32),
        compiler_params=_params("parallel"),
        name="dil_step",
    )(*([zs3] * 9), *cache_views, bias_rows, bias_new)
    return out.reshape(bd, DIL_WIDTH)


def _gla_prompt_kernel(q_ref, k_ref, v_ref, r_ref, ga_ref, wa_ref, ba_ref, gn_ref, y_ref, s_out_ref, st_ref, o_scr, *, tq):
    t = pl.program_id(2)
    ck = GLA_CHUNK

    @pl.when(t == 0)
    def _():
        st_ref[...] = jnp.zeros_like(st_ref)

    la = jnp.dot(ga_ref[...].astype(BF16), wa_ref[...], preferred_element_type=F32) + ba_ref[...]
    log_a = _log_sigmoid(la) / GLA_TAU
    row = lax.broadcasted_iota(jnp.int32, (ck, ck), 0)
    col = lax.broadcasted_iota(jnp.int32, (ck, ck), 1)
    tril = row >= col
    tril_f = tril.astype(F32)
    nt = (((1,), (1,)), ((), ()))
    tn = (((0,), (0,)), ((), ()))
    for c in range(tq // ck):
        sl = slice(c * ck, (c + 1) * ck)
        b = jnp.dot(tril_f, log_a[sl], precision=lax.Precision.HIGHEST, preferred_element_type=F32)
        b_last = b[ck - 1:ck, :]
        q = q_ref[sl, :] * (GLA_DK ** -0.5)
        k = k_ref[sl, :]
        v = v_ref[sl, :].astype(BF16)
        q_t = (q * jnp.exp(b)).astype(BF16)
        k_t = (k * jnp.exp(-b)).astype(BF16)
        k_h = (k * jnp.exp(b_last - b)).astype(BF16)
        decay = jnp.exp(b_last)
        a = lax.dot_general(q_t, k_t, nt, preferred_element_type=F32)
        a = jnp.where(tril, a, 0.0)
        st = st_ref[...]
        o = jnp.dot(a.astype(BF16), v, preferred_element_type=F32)
        o = o + lax.dot_general(q_t, st.astype(BF16), nt, preferred_element_type=F32)
        o_scr[sl, :] = o
        st_ref[...] = decay * st + lax.dot_general(v, k_h, tn, preferred_element_type=F32)

    o = o_scr[...]
    o = _rms_scale(o, gn_ref[...])
    r = r_ref[...]
    y_ref[...] = (o * (r * _sigmoid(r))).astype(y_ref.dtype)

    @pl.when(t == pl.num_programs(2) - 1)
    def _():
        s_out_ref[...] = st_ref[...].T


def gla_prompt(z3, za3, w_a2p, b_a, gla_norm, *, tq=512):
    b, seq, _ = z3.shape
    assert seq % tq == 0 and tq % GLA_CHUNK == 0
    return pl.pallas_call(
        functools.partial(_gla_prompt_kernel, tq=tq),
        grid=(b, GLA_HEADS, seq // tq),
        in_specs=[
            pl.BlockSpec((None, tq, GLA_DK), lambda i, h, t: (i, t, COL_GQ // GLA_DK + h)),
            pl.BlockSpec((None, tq, GLA_DK), lambda i, h, t: (i, t, COL_GK // GLA_DK + h)),
            pl.BlockSpec((None, tq, GLA_DV), lambda i, h, t: (i, t, COL_GV // GLA_DV + h)),
            pl.BlockSpec((None, tq, GLA_DV), lambda i, h, t: (i, t, COL_GR // GLA_DV + h)),
            pl.BlockSpec((None, tq, LANES), lambda i, h, t: (i, t, 0)),
            pl.BlockSpec((LANES, GLA_DK), lambda i, h, t: (0, h)),
            pl.BlockSpec((1, GLA_DK), lambda i, h, t: (0, h)),
            pl.BlockSpec((1, GLA_DV), lambda i, h, t: (0, h)),
        ],
        out_specs=[
            pl.BlockSpec((None, tq, GLA_DV), lambda i, h, t: (i, t, h)),
            pl.BlockSpec((None, None, GLA_DK, GLA_DV), lambda i, h, t: (i, h, 0, 0)),
        ],
        out_shape=[
            jax.ShapeDtypeStruct((b, seq, GLA_VAL_WIDTH), BF16),
            jax.ShapeDtypeStruct((b, GLA_HEADS, GLA_DK, GLA_DV), F32),
        ],
        scratch_shapes=[pltpu.VMEM((GLA_DV, GLA_DK), F32), pltpu.VMEM((tq, GLA_DV), F32)],
        compiler_params=_params("parallel", "parallel", "arbitrary"),
        name="gla_prompt",
    )(z3, z3, z3, z3, za3, w_a2p, b_a.reshape(1, GLA_KEY_WIDTH), gla_norm.reshape(1, GLA_VAL_WIDTH))


def _gla_step_kernel(q_ref, k_ref, v_ref, r_ref, ga_ref, wa_ref, ba_ref, gn_ref, s_ref, y_ref, s_out_ref):
    ga8 = jnp.broadcast_to(ga_ref[...], (8, LANES)).astype(BF16)
    la = jnp.dot(ga8, wa_ref[...], preferred_element_type=F32)[0:1, :] + ba_ref[...]
    ea = jnp.exp(_log_sigmoid(la) / GLA_TAU)
    q = q_ref[...] * (GLA_DK ** -0.5)
    k = k_ref[...]
    eye = lax.broadcasted_iota(jnp.int32, (GLA_DK, GLA_DK), 0) == lax.broadcasted_iota(jnp.int32, (GLA_DK, GLA_DK), 1)

    def column(row):
        return jnp.sum(jnp.where(eye, jnp.broadcast_to(row, (GLA_DK, GLA_DK)), 0.0), axis=-1, keepdims=True)

    for h in range(GLA_HEADS):
        ks = slice(h * GLA_DK, (h + 1) * GLA_DK)
        vs = slice(h * GLA_DV, (h + 1) * GLA_DV)
        s_new = column(ea[:, ks]) * s_ref[h] + column(k[:, ks]) * v_ref[:, vs]
        s_out_ref[h] = s_new
        o = jnp.sum(column(q[:, ks]) * s_new, axis=0, keepdims=True)
        o = _rms_scale(o, gn_ref[:, vs])
        r = r_ref[:, vs]
        y_ref[:, vs] = o * (r * _sigmoid(r))


def gla_step(z_s, za_s, state, w_a2p, b_a, gla_norm):
    bd = z_s.shape[0]
    zs3 = z_s.reshape(bd, 1, N_MAIN)
    y, s_new = pl.pallas_call(
        _gla_step_kernel,
        grid=(bd,),
        in_specs=[
            pl.BlockSpec((None, 1, GLA_KEY_WIDTH), lambda i: (i, 0, COL_GQ // GLA_KEY_WIDTH)),
            pl.BlockSpec((None, 1, GLA_KEY_WIDTH), lambda i: (i, 0, COL_GK // GLA_KEY_WIDTH)),
            pl.BlockSpec((None, 1, GLA_VAL_WIDTH), lambda i: (i, 0, COL_GV // GLA_VAL_WIDTH)),
            pl.BlockSpec((None, 1, GLA_VAL_WIDTH), lambda i: (i, 0, COL_GR // GLA_VAL_WIDTH)),
            pl.BlockSpec((None, 1, LANES), lambda i: (i, 0, 0)),
            pl.BlockSpec((LANES, GLA_KEY_WIDTH), lambda i: (0, 0)),
            pl.BlockSpec((1, GLA_KEY_WIDTH), lambda i: (0, 0)),
            pl.BlockSpec((1, GLA_VAL_WIDTH), lambda i: (0, 0)),
            pl.BlockSpec((None, GLA_HEADS, GLA_DK, GLA_DV), lambda i: (i, 0, 0, 0)),
        ],
        out_specs=[
            pl.BlockSpec((None, 1, GLA_VAL_WIDTH), lambda i: (i, 0, 0)),
            pl.BlockSpec((None, GLA_HEADS, GLA_DK, GLA_DV), lambda i: (i, 0, 0, 0)),
        ],
        out_shape=[
            jax.ShapeDtypeStruct((bd, 1, GLA_VAL_WIDTH), F32),
            jax.ShapeDtypeStruct((bd, GLA_HEADS, GLA_DK, GLA_DV), F32),
        ],
        compiler_params=_params("parallel"),
        name="gla_step",
    )(zs3, zs3, zs3, zs3, za_s.reshape(bd, 1, LANES), w_a2p, b_a.reshape(1, GLA_KEY_WIDTH),
      gla_norm.reshape(1, GLA_VAL_WIDTH), state)
    return y.reshape(bd, GLA_VAL_WIDTH), s_new


def _xattn_kernel(q_ref, kv_ref, o_ref):
    scale = X_HEAD_DIM ** -0.5
    nt = (((1,), (1,)), ((), ()))
    for h in range(X_HEADS):
        hs = slice(h * X_HEAD_DIM, (h + 1) * X_HEAD_DIM)
        k = kv_ref[:, h * X_HEAD_DIM:(h + 1) * X_HEAD_DIM].astype(BF16)
        v = kv_ref[:, X_WIDTH + h * X_HEAD_DIM:X_WIDTH + (h + 1) * X_HEAD_DIM].astype(BF16)
        s = lax.dot_general(q_ref[:, hs], k, nt, preferred_element_type=F32) * scale
        e = jnp.exp(s - jnp.max(s, axis=-1, keepdims=True))
        l = jnp.sum(e, axis=-1, keepdims=True)
        o = jnp.dot(e.astype(BF16), v, preferred_element_type=F32) / l
        o_ref[:, hs] = o.astype(o_ref.dtype)


def xattn(q3, mem_kv, *, tt):
    b, t, _ = q3.shape
    tt = min(tt, t)
    assert t % tt == 0
    return pl.pallas_call(
        _xattn_kernel,
        grid=(b, t // tt),
        in_specs=[
            pl.BlockSpec((None, tt, X_WIDTH), lambda i, j: (i, j, 0)),
            pl.BlockSpec((None, MEM_LEN, 2 * X_WIDTH), lambda i, j: (i, 0, 0)),
        ],
        out_specs=pl.BlockSpec((None, tt, X_WIDTH), lambda i, j: (i, j, 0)),
        out_shape=jax.ShapeDtypeStruct((b, t, X_WIDTH), BF16),
        compiler_params=_params("parallel", "arbitrary"),
        name="xattn",
    )(q3, mem_kv)


FFN_HALO = 16
FFN_TAIL = 8


def _ffn_up_prompt_kernel(x_ref, xh_ref, g_ref, wg_ref, wv_ref, cwg_ref, cwv_ref, cbg_ref, cbv_ref,
                          act_ref, tg_ref, tv_ref, xn_ref, ug_ref, uv_ref, *, tm, seq_tiles, chunk):
    i = pl.program_id(0)

    @pl.when(pl.program_id(1) == 0)
    def _():
        _norm_rows(x_ref, g_ref, xn_ref, tm, chunk, dst_offset=FFN_HALO)
        halo = _rms_scale(xh_ref[...], g_ref[...])
        halo = jnp.where(i % seq_tiles == 0, 0.0, halo)
        xn_ref[0:FFN_HALO, :] = halo.astype(BF16)

    xn = xn_ref[...]
    ug_ref[...] = jnp.dot(xn, wg_ref[...], preferred_element_type=F32)
    uv_ref[...] = jnp.dot(xn, wv_ref[...], preferred_element_type=F32)

    def conv(u_ref, cw_ref, cb_ref):
        c = cb_ref[...] + cw_ref[0:1, :] * u_ref[FFN_HALO - 2:FFN_HALO - 2 + tm, :]
        c = c + cw_ref[1:2, :] * u_ref[FFN_HALO - 1:FFN_HALO - 1 + tm, :]
        return c + cw_ref[2:3, :] * u_ref[FFN_HALO:FFN_HALO + tm, :]

    act_ref[...] = (_gelu_tanh(conv(ug_ref, cwg_ref, cbg_ref)) * conv(uv_ref, cwv_ref, cbv_ref)).astype(act_ref.dtype)
    tg_ref[...] = ug_ref[FFN_HALO + tm - FFN_TAIL:FFN_HALO + tm, :]
    tv_ref[...] = uv_ref[FFN_HALO + tm - FFN_TAIL:FFN_HALO + tm, :]


def ffn_up_prompt(x, g, w_up, conv_w, conv_b, *, seq, tm, tn):
    m, k = x.shape
    assert seq % tm == 0 and D_FF % tn == 0 and tm % FFN_HALO == 0
    nj = D_FF // tn
    seq_tiles = seq // tm
    chunk = min(256, tm)
    halo_blocks = tm // FFN_HALO
    act, tail_g, tail_v = pl.pallas_call(
        functools.partial(_ffn_up_prompt_kernel, tm=tm, seq_tiles=seq_tiles, chunk=chunk),
        grid=(m // tm, nj),
        in_specs=[
            pl.BlockSpec((tm, k), lambda i, j: (i, 0)),
            pl.BlockSpec((FFN_HALO, k), lambda i, j: (jnp.maximum(i * halo_blocks - 1, 0), 0)),
            pl.BlockSpec((1, k), lambda i, j: (0, 0)),
            pl.BlockSpec((k, tn), lambda i, j: (0, j)),
            pl.BlockSpec((k, tn), lambda i, j: (0, j + nj)),
            pl.BlockSpec((3, tn), lambda i, j: (0, j)),
            pl.BlockSpec((3, tn), lambda i, j: (0, j + nj)),
            pl.BlockSpec((1, tn), lambda i, j: (0, j)),
            pl.BlockSpec((1, tn), lambda i, j: (0, j + nj)),
        ],
        out_specs=[
            pl.BlockSpec((tm, tn), lambda i, j: (i, j)),
            pl.BlockSpec((None, FFN_TAIL, tn), lambda i, j: (i, 0, j)),
            pl.BlockSpec((None, FFN_TAIL, tn), lambda i, j: (i, 0, j)),
        ],
        out_shape=[
            jax.ShapeDtypeStruct((m, D_FF), BF16),
            jax.ShapeDtypeStruct((m // tm, FFN_TAIL, D_FF), F32),
            jax.ShapeDtypeStruct((m // tm, FFN_TAIL, D_FF), F32),
        ],
        scratch_shapes=[
            pltpu.VMEM((tm + FFN_HALO, k), BF16),
            pltpu.VMEM((tm + FFN_HALO, tn), F32),
            pltpu.VMEM((tm + FFN_HALO, tn), F32),
        ],
        compiler_params=_params("parallel", "arbitrary"),
        name="ffn_up_prompt",
    )(x, x, g.reshape(1, k), w_up, w_up, conv_w, conv_w, conv_b.reshape(1, 2 * D_FF), conv_b.reshape(1, 2 * D_FF))
    return act, jnp.concatenate([tail_g, tail_v], axis=-1)


def _ffn_up_step_kernel(x_ref, g_ref, wg_ref, wv_ref, cwg_ref, cwv_ref, cbg_ref, cbv_ref, hg_ref, hv_ref,
                        act_ref, ng_ref, nv_ref, xn_ref, *, bd):
    @pl.when(pl.program_id(0) == 0)
    def _():
        _norm_rows(x_ref, g_ref, xn_ref, bd, bd)

    xn = xn_ref[...]
    ug = jnp.dot(xn, wg_ref[...], preferred_element_type=F32)
    uv = jnp.dot(xn, wv_ref[...], preferred_element_type=F32)

    def conv(u, h_ref, cw_ref, cb_ref):
        return cb_ref[...] + cw_ref[0:1, :] * h_ref[0] + cw_ref[1:2, :] * h_ref[1] + cw_ref[2:3, :] * u

    act_ref[...] = (_gelu_tanh(conv(ug, hg_ref, cwg_ref, cbg_ref)) * conv(uv, hv_ref, cwv_ref, cbv_ref)).astype(act_ref.dtype)
    ng_ref[0] = hg_ref[1]
    ng_ref[1] = ug
    nv_ref[0] = hv_ref[1]
    nv_ref[1] = uv


def ffn_up_step(x, g, w_up, conv_w, conv_b, hist_t, *, tn):
    bd, k = x.shape
    nj = D_FF // tn
    act, new_g, new_v = pl.pallas_call(
        functools.partial(_ffn_up_step_kernel, bd=bd),
        grid=(nj,),
        in_specs=[
            pl.BlockSpec((bd, k), lambda j: (0, 0)),
            pl.BlockSpec((1, k), lambda j: (0, 0)),
            pl.BlockSpec((k, tn), lambda j: (0, j)),
            pl.BlockSpec((k, tn), lambda j: (0, j + nj)),
            pl.BlockSpec((3, tn), lambda j: (0, j)),
            pl.BlockSpec((3, tn), lambda j: (0, j + nj)),
            pl.BlockSpec((1, tn), lambda j: (0, j)),
            pl.BlockSpec((1, tn), lambda j: (0, j + nj)),
            pl.BlockSpec((2, bd, tn), lambda j: (0, 0, j)),
            pl.BlockSpec((2, bd, tn), lambda j: (0, 0, j + nj)),
        ],
        out_specs=[
            pl.BlockSpec((bd, tn), lambda j: (0, j)),
            pl.BlockSpec((2, bd, tn), lambda j: (0, 0, j)),
            pl.BlockSpec((2, bd, tn), lambda j: (0, 0, j)),
        ],
        out_shape=[
            jax.ShapeDtypeStruct((bd, D_FF), BF16),
            jax.ShapeDtypeStruct((2, bd, D_FF), F32),
            jax.ShapeDtypeStruct((2, bd, D_FF), F32),
        ],
        scratch_shapes=[pltpu.VMEM((bd, k), BF16)],
        compiler_params=_params("arbitrary"),
        name="ffn_up_step",
    )(x, g.reshape(1, k), w_up, w_up, conv_w, conv_w, conv_b.reshape(1, 2 * D_FF), conv_b.reshape(1, 2 * D_FF),
      hist_t, hist_t)
    return act, jnp.concatenate([new_g, new_v], axis=-1)


TM = 1024
TN = 1024
TM_POST = 512
TK_POST = 512
TN_MERGE = 512
TN_FFN = 512
XATTN_ROWS = 512
STEP_Q_ROWS = 16


def _layer_weights(l, w_in, pool_w, gla_w_a2, w_br_pool, w_br_dil, w_br_gla, w_mix_out, w_xq, w_xkv, w_xo, w_up, w_down):
    w_in_l = w_in[l]
    ga_end = COL_GA_SRC + GLA_RANK
    return dict(
        w_main=jnp.concatenate([w_in_l[:, :COL_GA_SRC], w_in_l[:, ga_end:]], axis=1).astype(BF16),
        w_ga=jnp.pad(w_in_l[:, COL_GA_SRC:ga_end], ((0, 0), (0, LANES - GLA_RANK))).astype(BF16),
        w_a2p=jnp.pad(gla_w_a2[l], ((0, LANES - GLA_RANK), (0, 0))).astype(BF16),
        pool_w=pool_w[l].astype(BF16),
        w_br_pool=w_br_pool[l].astype(BF16),
        w_br_dil=w_br_dil[l].astype(BF16),
        w_br_gla=w_br_gla[l].astype(BF16),
        w_mix_out=w_mix_out[l].astype(BF16),
        w_xq=w_xq[l].astype(BF16),
        w_xkv=w_xkv[l].astype(BF16),
        w_xo=w_xo[l].astype(BF16),
        w_up=w_up[l].astype(BF16),
        w_down=w_down[l].astype(BF16),
    )


def kernel(x_prompt, x_sample, state_pool, cache_dil1_kv, cache_dil2_kv, cache_dil3_kv, state_gla, cache_mem_kv, state_ffn_conv, mem_prompt, rel_bias, norm_mix_pre, norm_mix_post, w_in, pool_w, pool_scale, gla_w_a2, gla_b_a, gla_norm, w_br_pool, w_br_dil, w_br_gla, w_mix_out, norm_x_pre, norm_x_post, norm_mem, w_xq, w_xkv, w_xo, norm_ffn_pre, norm_ffn_post, w_up, conv_w, conv_b, w_down):
    b, seq, d = x_prompt.shape
    bd = x_sample.shape[0]
    depth = w_in.shape[0]
    m = b * seq
    assert x_sample.shape[1] == 1 and d == D_MODEL
    slot_bias = _slot_biases(rel_bias)
    band_bias = _band_bias(slot_bias)
    dil_caches = (cache_dil1_kv, cache_dil2_kv, cache_dil3_kv)

    xp = x_prompt.reshape(m, d)
    xs = x_sample.reshape(bd, d)
    pool_p, gla_p, mem_p, conv_p = [], [], [], []
    pool_s, gla_s, conv_s = [], [], []
    dil_p = [[] for _ in range(3)]
    dil_s = [[] for _ in range(3)]
    seq_tiles = seq // TM

    for l in range(depth):
        w = _layer_weights(l, w_in, pool_w, gla_w_a2, w_br_pool, w_br_dil, w_br_gla, w_mix_out, w_xq, w_xkv, w_xo, w_up, w_down)

        zm = norm_mm(xp, norm_mix_pre[l], w["w_main"], tm=TM, tn=TN)
        za = norm_mm(xp, norm_mix_pre[l], w["w_ga"], tm=TM, tn=LANES)
        z3 = zm.reshape(b, seq, N_MAIN)
        y_pool = pool_prompt(z3, w["pool_w"], pool_scale[l]).reshape(m, POOL_WIDTH)
        y_dil = dil_prompt(z3, band_bias).reshape(m, DIL_WIDTH)
        y_gla, gla_new = gla_prompt(z3, za.reshape(b, seq, LANES), w["w_a2p"], gla_b_a[l], gla_norm[l])
        merged = branch_merge(y_pool, y_dil, y_gla.reshape(m, GLA_VAL_WIDTH), w["w_br_pool"], w["w_br_dil"], w["w_br_gla"],
                              zm, tm=TM, tn=TN_MERGE)
        xp = mm_post(merged, w["w_mix_out"], norm_mix_post[l], xp, tm=TM_POST, tk=TK_POST)
        mem_kv = norm_mm(mem_prompt.reshape(b * MEM_LEN, d), norm_mem[l], w["w_xkv"], tm=TM, tn=TN)
        q = norm_mm(xp, norm_x_pre[l], w["w_xq"], tm=TM, tn=TN, out_dtype=BF16)
        o = xattn(q.reshape(b, seq, X_WIDTH), mem_kv.reshape(b, MEM_LEN, 2 * X_WIDTH), tt=XATTN_ROWS)
        xp = mm_post(o.reshape(m, X_WIDTH), w["w_xo"], norm_x_post[l], xp, tm=TM_POST, tk=TK_POST)
        act, tails = ffn_up_prompt(xp, norm_ffn_pre[l], w["w_up"], conv_w[l], conv_b[l], seq=seq, tm=TM, tn=TN_FFN)
        xp = mm_post(act, w["w_down"], norm_ffn_post[l], xp, tm=TM_POST, tk=TK_POST)

        pool_p.append(z3[:, seq - POOL_HIST:, COL_POOL:COL_POOL + POOL_WIDTH])
        for g, (win, _) in enumerate(DIL_CONFIGS):
            keep = min(win, seq)
            kv = z3[:, seq - keep:, COL_DIL_K[g]:COL_DIL_K[g] + 2 * DIL_WIDTH]
            dil_p[g].append(kv.reshape(b, keep, 2, DIL_HEADS, DIL_HEAD_DIM))
        gla_p.append(gla_new)
        mem_p.append(mem_kv.reshape(b, MEM_LEN, 2, X_HEADS, X_HEAD_DIM))
        conv_p.append(tails.reshape(b, seq_tiles, FFN_TAIL, 2 * D_FF)[:, seq_tiles - 1, FFN_TAIL - 2:, :])

        zs = norm_mm(xs, norm_mix_pre[l], w["w_main"], tm=bd, tn=TN)
        zas = norm_mm(xs, norm_mix_pre[l], w["w_ga"], tm=bd, tn=LANES)
        y_pool_s, pool_new_t = pool_step(jnp.swapaxes(state_pool[l], 0, 1), zs, w["pool_w"], pool_scale[l])
        y_dil_s = dil_step(zs, [c[l] for c in dil_caches], slot_bias).astype(BF16)
        y_gla_s, gla_new_s = gla_step(zs, zas, state_gla[l], w["w_a2p"], gla_b_a[l], gla_norm[l])
        merged_s = branch_merge(y_pool_s, y_dil_s, y_gla_s.astype(BF16), w["w_br_pool"], w["w_br_dil"], w["w_br_gla"],
                                zs, tm=bd, tn=TN_MERGE)
        xs = mm_post(merged_s, w["w_mix_out"], norm_mix_post[l], xs, tm=bd, tk=TK_POST)
        q_s = norm_mm(xs, norm_x_pre[l], w["w_xq"], tm=bd, tn=TN, out_dtype=BF16)
        q_s = jnp.pad(q_s[:, None, :], ((0, 0), (0, STEP_Q_ROWS - 1), (0, 0)))
        o_s = xattn(q_s, cache_mem_kv[l].reshape(bd, MEM_LEN, 2 * X_WIDTH), tt=STEP_Q_ROWS)[:, 0, :]
        xs = mm_post(o_s, w["w_xo"], norm_x_post[l], xs, tm=bd, tk=TK_POST)
        act_s, conv_new_t = ffn_up_step(xs, norm_ffn_pre[l], w["w_up"], conv_w[l], conv_b[l],
                                        jnp.swapaxes(state_ffn_conv[l], 0, 1), tn=TN_FFN)
        xs = mm_post(act_s, w["w_down"], norm_ffn_post[l], xs, tm=bd, tk=TK_POST)

        pool_s.append(jnp.swapaxes(pool_new_t, 0, 1))
        for g in range(3):
            kv = zs[:, COL_DIL_K[g]:COL_DIL_K[g] + 2 * DIL_WIDTH]
            dil_s[g].append(kv.reshape(bd, 1, 2, DIL_HEADS, DIL_HEAD_DIM))
        gla_s.append(gla_new_s)
        conv_s.append(jnp.swapaxes(conv_new_t, 0, 1))

    return (xp.reshape(b, seq, d), xs.reshape(bd, 1, d),
            jnp.stack(pool_p), jnp.stack(dil_p[0]), jnp.stack(dil_p[1]), jnp.stack(dil_p[2]), jnp.stack(gla_p), jnp.stack(mem_p), jnp.stack(conv_p),
            jnp.stack(pool_s), jnp.stack(dil_s[0]), jnp.stack(dil_s[1]), jnp.stack(dil_s[2]), jnp.stack(gla_s), jnp.stack(conv_s))
```

```python
import functools
import math

import jax
import jax.numpy as jnp
from jax import lax
from jax.experimental import pallas as pl
from jax.experimental.pallas import tpu as pltpu

F32 = jnp.float32
BF16 = jnp.bfloat16
EPS = 1e-6
NEG_INF = -1e30

VMEM_LIMIT_BYTES = 56 * 1024 * 1024
LANES = 128

D_MODEL = 2048
POOL_WINDOWS = (2, 4, 8, 16)
POOL_GROUP = 256
POOL_WIDTH = 1024
POOL_HIST = 15
DIL_CONFIGS = ((128, 1), (512, 4), (2048, 16))
DIL_HEADS = 8
DIL_HEAD_DIM = 128
DIL_WIDTH = 1024
DIL_BLOCK = 128
GLA_HEADS = 4
GLA_DK = 256
GLA_DV = 512
GLA_KEY_WIDTH = 1024
GLA_VAL_WIDTH = 2048
GLA_RANK = 16
GLA_TAU = 16.0
GLA_CHUNK = 64
REL_BUCKETS = 32
REL_MAX_DIST = 2048
MEM_LEN = 256
X_HEADS = 4
X_HEAD_DIM = 256
X_WIDTH = 1024
D_FF = 5632
PAST_LEN = 8192

COL_POOL = 0
COL_DIL_Q = tuple(1024 + 3072 * g for g in range(3))
COL_DIL_K = tuple(2048 + 3072 * g for g in range(3))
COL_DIL_V = tuple(3072 + 3072 * g for g in range(3))
COL_GQ = 10240
COL_GK = 11264
COL_GV = 12288
COL_GR = 14336
COL_GATES = 16384
N_MAIN = 22528
COL_GA_SRC = 16384


def _params(*semantics):
    return pltpu.CompilerParams(dimension_semantics=semantics, vmem_limit_bytes=VMEM_LIMIT_BYTES)


def _sigmoid(x):
    return 1.0 / (1.0 + jnp.exp(-x))


def _log_sigmoid(x):
    return jnp.minimum(x, 0.0) - jnp.log(1.0 + jnp.exp(-jnp.abs(x)))


def _gelu_tanh(x):
    return x * (0.5 * (1.0 + jnp.tanh(math.sqrt(2.0 / math.pi) * (x + 0.044715 * (x * x * x)))))


def _rms_scale(y, g):
    return y * lax.rsqrt(jnp.mean(y * y, axis=-1, keepdims=True) + EPS) * g


def _norm_rows(x_ref, g_ref, xn_ref, rows, chunk, dst_offset=0):
    g = g_ref[...]

    def body(c, carry):
        r0 = pl.multiple_of(c * chunk, chunk)
        xn_ref[pl.ds(dst_offset + r0, chunk), :] = _rms_scale(x_ref[pl.ds(r0, chunk), :], g).astype(BF16)
        return carry

    lax.fori_loop(0, rows // chunk, body, 0)


def _norm_mm_kernel(x_ref, g_ref, w_ref, o_ref, xn_ref, *, tm, chunk):
    @pl.when(pl.program_id(1) == 0)
    def _():
        _norm_rows(x_ref, g_ref, xn_ref, tm, chunk)

    o_ref[...] = jnp.dot(xn_ref[...], w_ref[...], preferred_element_type=F32).astype(o_ref.dtype)


def norm_mm(x, g, w, *, tm, tn, out_dtype=F32):
    m, k = x.shape
    n = w.shape[1]
    tm = min(tm, m)
    tn = min(tn, n)
    assert m % tm == 0 and n % tn == 0
    chunk = min(256, tm)
    return pl.pallas_call(
        functools.partial(_norm_mm_kernel, tm=tm, chunk=chunk),
        grid=(m // tm, n // tn),
        in_specs=[
            pl.BlockSpec((tm, k), lambda i, j: (i, 0)),
            pl.BlockSpec((1, k), lambda i, j: (0, 0)),
            pl.BlockSpec((k, tn), lambda i, j: (0, j)),
        ],
        out_specs=pl.BlockSpec((tm, tn), lambda i, j: (i, j)),
        out_shape=jax.ShapeDtypeStruct((m, n), out_dtype),
        scratch_shapes=[pltpu.VMEM((tm, k), BF16)],
        compiler_params=_params("parallel", "arbitrary"),
        name="norm_mm",
    )(x, g.reshape(1, k), w)


def _mm_post_kernel(a_ref, w_ref, g_ref, res_ref, o_ref, acc_ref, *, nk):
    kk = pl.program_id(1)
    part = jnp.dot(a_ref[...], w_ref[...], preferred_element_type=F32)
    if nk == 1:
        o_ref[...] = res_ref[...] + _rms_scale(part, g_ref[...])
        return

    @pl.when(kk == 0)
    def _():
        acc_ref[...] = part

    @pl.when(jnp.logical_and(kk > 0, kk < nk - 1))
    def _():
        acc_ref[...] += part

    @pl.when(kk == nk - 1)
    def _():
        o_ref[...] = res_ref[...] + _rms_scale(acc_ref[...] + part, g_ref[...])


def mm_post(a, w, g, res, *, tm, tk):
    m, k = a.shape
    n = w.shape[1]
    tm = min(tm, m)
    tk = k if k <= tk else k // 4
    assert m % tm == 0 and k % tk == 0 and tk % LANES == 0
    return pl.pallas_call(
        functools.partial(_mm_post_kernel, nk=k // tk),
        grid=(m // tm, k // tk),
        in_specs=[
            pl.BlockSpec((tm, tk), lambda i, kk: (i, kk)),
            pl.BlockSpec((tk, n), lambda i, kk: (kk, 0)),
            pl.BlockSpec((1, n), lambda i, kk: (0, 0)),
            pl.BlockSpec((tm, n), lambda i, kk: (i, 0)),
        ],
        out_specs=pl.BlockSpec((tm, n), lambda i, kk: (i, 0)),
        out_shape=jax.ShapeDtypeStruct((m, n), F32),
        scratch_shapes=[pltpu.VMEM((tm, n), F32)],
        compiler_params=_params("parallel", "arbitrary"),
        name="mm_post",
    )(a, w, g.reshape(1, n), res)


def _branch_merge_kernel(yp_ref, yd_ref, yg_ref, wp_ref, wd_ref, wg_ref, g0_ref, g1_ref, g2_ref, o_ref):
    bp = jnp.dot(yp_ref[...], wp_ref[...], preferred_element_type=F32)
    bd = jnp.dot(yd_ref[...], wd_ref[...], preferred_element_type=F32)
    bg = jnp.dot(yg_ref[...], wg_ref[...], preferred_element_type=F32)
    merged = _sigmoid(g0_ref[...]) * bp + _sigmoid(g1_ref[...]) * bd + _sigmoid(g2_ref[...]) * bg
    o_ref[...] = merged.astype(o_ref.dtype)


def branch_merge(y_pool, y_dil, y_gla, w_pool, w_dil, w_gla, z_main, *, tm, tn):
    m = y_pool.shape[0]
    tm = min(tm, m)
    assert m % tm == 0 and D_MODEL % tn == 0
    gate_blk = [(COL_GATES + b * D_MODEL) // tn for b in range(3)]

    def gate_spec(b):
        return pl.BlockSpec((tm, tn), lambda i, j: (i, gate_blk[b] + j))

    return pl.pallas_call(
        _branch_merge_kernel,
        grid=(m // tm, D_MODEL // tn),
        in_specs=[
            pl.BlockSpec((tm, POOL_WIDTH), lambda i, j: (i, 0)),
            pl.BlockSpec((tm, DIL_WIDTH), lambda i, j: (i, 0)),
            pl.BlockSpec((tm, GLA_VAL_WIDTH), lambda i, j: (i, 0)),
            pl.BlockSpec((POOL_WIDTH, tn), lambda i, j: (0, j)),
            pl.BlockSpec((DIL_WIDTH, tn), lambda i, j: (0, j)),
            pl.BlockSpec((GLA_VAL_WIDTH, tn), lambda i, j: (0, j)),
            gate_spec(0),
            gate_spec(1),
            gate_spec(2),
        ],
        out_specs=pl.BlockSpec((tm, tn), lambda i, j: (i, j)),
        out_shape=jax.ShapeDtypeStruct((m, D_MODEL), BF16),
        compiler_params=_params("parallel", "arbitrary"),
        name="branch_merge",
    )(y_pool, y_dil, y_gla, w_pool, w_dil, w_gla, z_main, z_main, z_main)


def _pool_prompt_kernel(u_ref, w_ref, s_ref, o_ref, buf_a, buf_b, *, seq):
    pad = POOL_HIST + 1
    zeros = jnp.zeros((pad, POOL_GROUP), F32)
    t = lax.broadcasted_iota(jnp.int32, (seq, 1), 0)
    for g, win in enumerate(POOL_WINDOWS):
        cols = slice(g * POOL_GROUP, (g + 1) * POOL_GROUP)
        u = u_ref[:, cols]
        cur, nxt = buf_a, buf_b
        cur[0:pad, :] = zeros
        nxt[0:pad, :] = zeros
        cur[pad:pad + seq, :] = u
        k = 1
        while k < win:
            nxt[pad:pad + seq, :] = cur[pad:pad + seq, :] + cur[pad - k:pad - k + seq, :]
            cur, nxt = nxt, cur
            k *= 2
        cnt = jnp.minimum(win, t + 1).astype(F32)
        d = cur[pad:pad + seq, :] / cnt - u
        y = jnp.dot(d.astype(BF16), w_ref[g], preferred_element_type=F32) * s_ref[:, cols]
        o_ref[:, cols] = y.astype(o_ref.dtype)


def pool_prompt(z3, pool_w, pool_scale):
    b, seq, _ = z3.shape
    return pl.pallas_call(
        functools.partial(_pool_prompt_kernel, seq=seq),
        grid=(b,),
        in_specs=[
            pl.BlockSpec((None, seq, POOL_WIDTH), lambda i: (i, 0, 0)),
            pl.BlockSpec((len(POOL_WINDOWS), POOL_GROUP, POOL_GROUP), lambda i: (0, 0, 0)),
            pl.BlockSpec((1, POOL_WIDTH), lambda i: (0, 0)),
        ],
        out_specs=pl.BlockSpec((None, seq, POOL_WIDTH), lambda i: (i, 0, 0)),
        out_shape=jax.ShapeDtypeStruct((b, seq, POOL_WIDTH), BF16),
        scratch_shapes=[pltpu.VMEM((seq + POOL_HIST + 1, POOL_GROUP), F32)] * 2,
        compiler_params=_params("parallel"),
        name="pool_prompt",
    )(z3, pool_w, pool_scale.reshape(1, POOL_WIDTH))


def _pool_step_kernel(hist_ref, u_ref, w_ref, s_ref, y_ref, new_ref):
    u = u_ref[...]
    for r in range(POOL_HIST - 1):
        new_ref[r] = hist_ref[r + 1]
    new_ref[POOL_HIST - 1] = u
    for g, win in enumerate(POOL_WINDOWS):
        cols = slice(g * POOL_GROUP, (g + 1) * POOL_GROUP)
        ug = u[:, cols]
        acc = ug
        for r in range(POOL_HIST - (win - 1), POOL_HIST):
            acc = acc + hist_ref[r, :, cols]
        d = acc / float(win) - ug
        y = jnp.dot(d.astype(BF16), w_ref[g], preferred_element_type=F32) * s_ref[:, cols]
        y_ref[:, cols] = y.astype(y_ref.dtype)


def pool_step(hist_t, z_s, pool_w, pool_scale):
    bd = z_s.shape[0]
    return pl.pallas_call(
        _pool_step_kernel,
        grid=(1,),
        in_specs=[
            pl.BlockSpec((POOL_HIST, bd, POOL_WIDTH), lambda i: (0, 0, 0)),
            pl.BlockSpec((bd, POOL_WIDTH), lambda i: (0, 0)),
            pl.BlockSpec((len(POOL_WINDOWS), POOL_GROUP, POOL_GROUP), lambda i: (0, 0, 0)),
            pl.BlockSpec((1, POOL_WIDTH), lambda i: (0, 0)),
        ],
        out_specs=[
            pl.BlockSpec((bd, POOL_WIDTH), lambda i: (0, 0)),
            pl.BlockSpec((POOL_HIST, bd, POOL_WIDTH), lambda i: (0, 0, 0)),
        ],
        out_shape=[
            jax.ShapeDtypeStruct((bd, POOL_WIDTH), BF16),
            jax.ShapeDtypeStruct((POOL_HIST, bd, POOL_WIDTH), F32),
        ],
        compiler_params=_params("arbitrary"),
        name="pool_step",
    )(hist_t, z_s, pool_w, pool_scale.reshape(1, POOL_WIDTH))


def _rel_bucket(dist):
    max_exact = REL_BUCKETS // 2
    scaled = jnp.log(jnp.maximum(dist, 1).astype(F32) / max_exact) / math.log(REL_MAX_DIST / max_exact)
    large = jnp.minimum(max_exact + (scaled * (REL_BUCKETS - max_exact)).astype(jnp.int32), REL_BUCKETS - 1)
    return jnp.where(dist < max_exact, dist, large)


def _select_rows(table_t, index, n):
    onehot = (index.reshape(1, -1) == jnp.arange(n, dtype=jnp.int32)[:, None]).astype(F32)
    out = jnp.dot(table_t, onehot, precision=lax.Precision.HIGHEST, preferred_element_type=F32)
    return out.reshape((table_t.shape[0],) + index.shape)


def _slot_biases(rel_bias):
    out = []
    for g, (win, dil) in enumerate(DIL_CONFIGS):
        dist = jnp.arange(win // dil + 1, dtype=jnp.int32) * dil
        table_t = rel_bias[:, g * DIL_HEADS:(g + 1) * DIL_HEADS].T.astype(F32)
        out.append(_select_rows(table_t, _rel_bucket(dist), REL_BUCKETS))
    return out


def _band_bias(slot_bias):
    qi = jnp.arange(DIL_BLOCK, dtype=jnp.int32)[:, None] + DIL_BLOCK
    ki = jnp.arange(2 * DIL_BLOCK, dtype=jnp.int32)[None, :]
    rel = qi - ki
    out = []
    for g, (win, dil) in enumerate(DIL_CONFIGS):
        n_slots = win // dil
        ok = (rel >= 0) & (rel <= n_slots)
        b = _select_rows(slot_bias[g], jnp.clip(rel, 0, n_slots), n_slots + 1)
        out.append(jnp.where(ok[None], b, NEG_INF))
    return jnp.stack(out, axis=0)


def _dil_prompt_kernel(q0, k0, v0, q1, k1, v1, q2, k2, v2, bias_ref, o_ref, o_scr, lse_scr, *, seq):
    qs, ks, vs = (q0, q1, q2), (k0, k1, k2), (v0, v1, v2)
    scale = DIL_HEAD_DIM ** -0.5
    blk = DIL_BLOCK
    nt = (((1,), (1,)), ((), ()))

    def rows(ref, start, dil):
        if dil == 1:
            return ref[pl.ds(start, blk), :]
        return ref[pl.ds(start, blk, stride=dil), :]

    for g, (_, dil) in enumerate(DIL_CONFIGS):
        nb = seq // dil // blk
        for r in range(dil):
            for ub in range(nb):
                start = r + dil * ub * blk
                q = rows(qs[g], start, dil).astype(BF16)
                kc = rows(ks[g], start, dil).astype(BF16)
                vc = rows(vs[g], start, dil).astype(BF16)
                if ub == 0:
                    kk, vv, bias = kc, vc, bias_ref[g, :, blk:]
                else:
                    prev = start - dil * blk
                    kk = jnp.concatenate([rows(ks[g], prev, dil).astype(BF16), kc], axis=0)
                    vv = jnp.concatenate([rows(vs[g], prev, dil).astype(BF16), vc], axis=0)
                    bias = bias_ref[g]
                s = lax.dot_general(q, kk, nt, preferred_element_type=F32) * scale + bias
                m = jnp.max(s, axis=-1, keepdims=True)
                e = jnp.exp(s - m)
                l = jnp.sum(e, axis=-1, keepdims=True)
                o = jnp.dot(e.astype(BF16), vv, preferred_element_type=F32) / l
                lse = jnp.broadcast_to(m + jnp.log(l), (blk, LANES))
                if dil == 1:
                    o_scr[g, pl.ds(start, blk), :] = o
                    lse_scr[g, pl.ds(start, blk), :] = lse
                else:
                    o_scr[g, pl.ds(start, blk, stride=dil), :] = o
                    lse_scr[g, pl.ds(start, blk, stride=dil), :] = lse

    chunk = 256

    def combine(c, carry):
        sl = pl.ds(pl.multiple_of(c * chunk, chunk), chunk)
        l0, l1, l2 = lse_scr[0, sl, :], lse_scr[1, sl, :], lse_scr[2, sl, :]
        mx = jnp.maximum(jnp.maximum(l0, l1), l2)
        w0, w1, w2 = jnp.exp(l0 - mx), jnp.exp(l1 - mx), jnp.exp(l2 - mx)
        y = (w0 * o_scr[0, sl, :] + w1 * o_scr[1, sl, :] + w2 * o_scr[2, sl, :]) / (w0 + w1 + w2)
        o_ref[sl, :] = y.astype(o_ref.dtype)
        return carry

    lax.fori_loop(0, seq // chunk, combine, 0)


def dil_prompt(z3, band_bias):
    b, seq, _ = z3.shape
    assert seq % (DIL_BLOCK * 16) == 0

    def col_spec(col):
        blk0 = col // DIL_HEAD_DIM
        return pl.BlockSpec((None, seq, DIL_HEAD_DIM), lambda i, h: (i, 0, blk0 + h))

    in_specs = []
    for g in range(3):
        in_specs += [col_spec(COL_DIL_Q[g]), col_spec(COL_DIL_K[g]), col_spec(COL_DIL_V[g])]
    in_specs.append(pl.BlockSpec((3, None, DIL_BLOCK, 2 * DIL_BLOCK), lambda i, h: (0, h, 0, 0)))
    return pl.pallas_call(
        functools.partial(_dil_prompt_kernel, seq=seq),
        grid=(b, DIL_HEADS),
        in_specs=in_specs,
        out_specs=pl.BlockSpec((None, seq, DIL_HEAD_DIM), lambda i, h: (i, 0, h)),
        out_shape=jax.ShapeDtypeStruct((b, seq, DIL_WIDTH), BF16),
        scratch_shapes=[pltpu.VMEM((3, seq, DIL_HEAD_DIM), F32), pltpu.VMEM((3, seq, LANES), F32)],
        compiler_params=_params("parallel", "arbitrary"),
        name="dil_prompt",
    )(*([z3] * 9), band_bias)


def _dil_step_kernel(qkv_ref, c0, c1, c2, bias_ref, bias0_ref, o_ref):
    caches = (c0, c1, c2)
    scale = DIL_HEAD_DIM ** -0.5
    outs, lses = [], []
    for g in range(3):
        q, kn, vn = qkv_ref[3 * g], qkv_ref[3 * g + 1], qkv_ref[3 * g + 2]
        kc = caches[g][:, 0]
        vc = caches[g][:, 1]
        s = jnp.sum(kc * q[None], axis=-1, keepdims=True) * scale + bias_ref[g]
        s_new = jnp.sum(q * kn, axis=-1, keepdims=True) * scale + bias0_ref[g]
        m = jnp.maximum(jnp.max(s, axis=0), s_new)
        p = jnp.exp(s - m[None])
        p_new = jnp.exp(s_new - m)
        l = jnp.sum(p, axis=0) + p_new
        outs.append((jnp.sum(p * vc, axis=0) + p_new * vn) / l)
        lses.append(m + jnp.log(l))
    mx = jnp.maximum(jnp.maximum(lses[0], lses[1]), lses[2])
    w0, w1, w2 = jnp.exp(lses[0] - mx), jnp.exp(lses[1] - mx), jnp.exp(lses[2] - mx)
    o_ref[...] = (w0 * outs[0] + w1 * outs[1] + w2 * outs[2]) / (w0 + w1 + w2)


def dil_step(z_s, caches, layer, slot_bias):
    bd = z_s.shape[0]
    n_slots = DIL_CONFIGS[0][0] // DIL_CONFIGS[0][1]
    qkv = z_s[:, COL_DIL_Q[0]:COL_DIL_V[2] + DIL_WIDTH].reshape(bd, 9, DIL_HEADS, DIL_HEAD_DIM)
    in_specs = [pl.BlockSpec((None, 9, DIL_HEADS, DIL_HEAD_DIM), lambda i: (i, 0, 0, 0))]
    cache_views = []
    for g, (win, dil) in enumerate(DIL_CONFIGS):
        depth = caches[g].shape[0]
        assert caches[g].shape[2] == win and win // dil == n_slots
        cache_views.append(caches[g].reshape(depth, bd, n_slots, dil, 2, DIL_HEADS, DIL_HEAD_DIM))
        in_specs.append(pl.BlockSpec((None, None, n_slots, None, 2, DIL_HEADS, DIL_HEAD_DIM),
                                     lambda i: (layer, i, 0, 0, 0, 0, 0)))
    bias_rows = jnp.stack([sb[:, n_slots:0:-1].T for sb in slot_bias], axis=0)
    bias_rows = jnp.broadcast_to(bias_rows[..., None], (3, n_slots, DIL_HEADS, LANES))
    bias_new = jnp.broadcast_to(jnp.stack([sb[:, 0] for sb in slot_bias], axis=0)[..., None], (3, DIL_HEADS, LANES))
    in_specs.append(pl.BlockSpec((3, n_slots, DIL_HEADS, LANES), lambda i: (0, 0, 0, 0)))
    in_specs.append(pl.BlockSpec((3, DIL_HEADS, LANES), lambda i: (0, 0, 0)))
    out = pl.pallas_call(
        _dil_step_kernel,
        grid=(bd,),
        in_specs=in_specs,
        out_specs=pl.BlockSpec((None, DIL_HEADS, DIL_HEAD_DIM), lambda i: (i, 0, 0)),
        out_shape=jax.ShapeDtypeStruct((bd, DIL_HEADS, DIL_HEAD_DIM), F32),
        compiler_params=_params("parallel"),
        name="dil_step",
    )(qkv, *cache_views, bias_rows, bias_new)
    return out.reshape(bd, DIL_WIDTH)


def _gla_prompt_kernel(q_ref, k_ref, v_ref, r_ref, ga_ref, wa_ref, ba_ref, gn_ref, y_ref, s_out_ref, st_ref, o_scr, *, tq):
    t = pl.program_id(2)
    ck = GLA_CHUNK

    @pl.when(t == 0)
    def _():
        st_ref[...] = jnp.zeros_like(st_ref)

    la = jnp.dot(ga_ref[...].astype(BF16), wa_ref[...], preferred_element_type=F32) + ba_ref[...]
    log_a = _log_sigmoid(la) / GLA_TAU
    row = lax.broadcasted_iota(jnp.int32, (ck, ck), 0)
    col = lax.broadcasted_iota(jnp.int32, (ck, ck), 1)
    tril = row >= col
    tril_f = tril.astype(F32)
    nt = (((1,), (1,)), ((), ()))
    tn = (((0,), (0,)), ((), ()))
    for c in range(tq // ck):
        sl = slice(c * ck, (c + 1) * ck)
        b = jnp.dot(tril_f, log_a[sl], precision=lax.Precision.HIGHEST, preferred_element_type=F32)
        b_last = b[ck - 1:ck, :]
        q = q_ref[sl, :] * (GLA_DK ** -0.5)
        k = k_ref[sl, :]
        v = v_ref[sl, :].astype(BF16)
        q_t = (q * jnp.exp(b)).astype(BF16)
        k_t = (k * jnp.exp(-b)).astype(BF16)
        k_h = (k * jnp.exp(b_last - b)).astype(BF16)
        decay = jnp.exp(b_last)
        a = lax.dot_general(q_t, k_t, nt, preferred_element_type=F32)
        a = jnp.where(tril, a, 0.0)
        st = st_ref[...]
        o = jnp.dot(a.astype(BF16), v, preferred_element_type=F32)
        o = o + lax.dot_general(q_t, st.astype(BF16), nt, preferred_element_type=F32)
        o_scr[sl, :] = o
        st_ref[...] = decay * st + lax.dot_general(v, k_h, tn, preferred_element_type=F32)

    o = o_scr[...]
    o = _rms_scale(o, gn_ref[...])
    r = r_ref[...]
    y_ref[...] = (o * (r * _sigmoid(r))).astype(y_ref.dtype)

    @pl.when(t == pl.num_programs(2) - 1)
    def _():
        s_out_ref[...] = st_ref[...].T


def gla_prompt(z3, za3, w_a2p, b_a, gla_norm, *, tq=512):
    b, seq, _ = z3.shape
    assert seq % tq == 0 and tq % GLA_CHUNK == 0
    return pl.pallas_call(
        functools.partial(_gla_prompt_kernel, tq=tq),
        grid=(b, GLA_HEADS, seq // tq),
        in_specs=[
            pl.BlockSpec((None, tq, GLA_DK), lambda i, h, t: (i, t, COL_GQ // GLA_DK + h)),
            pl.BlockSpec((None, tq, GLA_DK), lambda i, h, t: (i, t, COL_GK // GLA_DK + h)),
            pl.BlockSpec((None, tq, GLA_DV), lambda i, h, t: (i, t, COL_GV // GLA_DV + h)),
            pl.BlockSpec((None, tq, GLA_DV), lambda i, h, t: (i, t, COL_GR // GLA_DV + h)),
            pl.BlockSpec((None, tq, LANES), lambda i, h, t: (i, t, 0)),
            pl.BlockSpec((LANES, GLA_DK), lambda i, h, t: (0, h)),
            pl.BlockSpec((1, GLA_DK), lambda i, h, t: (0, h)),
            pl.BlockSpec((1, GLA_DV), lambda i, h, t: (0, h)),
        ],
        out_specs=[
            pl.BlockSpec((None, tq, GLA_DV), lambda i, h, t: (i, t, h)),
            pl.BlockSpec((None, None, GLA_DK, GLA_DV), lambda i, h, t: (i, h, 0, 0)),
        ],
        out_shape=[
            jax.ShapeDtypeStruct((b, seq, GLA_VAL_WIDTH), BF16),
            jax.ShapeDtypeStruct((b, GLA_HEADS, GLA_DK, GLA_DV), F32),
        ],
        scratch_shapes=[pltpu.VMEM((GLA_DV, GLA_DK), F32), pltpu.VMEM((tq, GLA_DV), F32)],
        compiler_params=_params("parallel", "parallel", "arbitrary"),
        name="gla_prompt",
    )(z3, z3, z3, z3, za3, w_a2p, b_a.reshape(1, GLA_KEY_WIDTH), gla_norm.reshape(1, GLA_VAL_WIDTH))


def _gla_step_kernel(q_ref, k_ref, v_ref, r_ref, ga_ref, wa_ref, ba_ref, gn_ref, s_ref, y_ref, s_out_ref):
    ga8 = jnp.broadcast_to(ga_ref[...], (8, LANES)).astype(BF16)
    la = jnp.dot(ga8, wa_ref[...], preferred_element_type=F32)[0:1, :] + ba_ref[...]
    ea = jnp.exp(_log_sigmoid(la) / GLA_TAU)
    q = q_ref[...] * (GLA_DK ** -0.5)
    k = k_ref[...]
    eye = lax.broadcasted_iota(jnp.int32, (GLA_DK, GLA_DK), 0) == lax.broadcasted_iota(jnp.int32, (GLA_DK, GLA_DK), 1)

    def column(row):
        return jnp.sum(jnp.where(eye, jnp.broadcast_to(row, (GLA_DK, GLA_DK)), 0.0), axis=-1, keepdims=True)

    for h in range(GLA_HEADS):
        ks = slice(h * GLA_DK, (h + 1) * GLA_DK)
        vs = slice(h * GLA_DV, (h + 1) * GLA_DV)
        s_new = column(ea[:, ks]) * s_ref[h] + column(k[:, ks]) * v_ref[:, vs]
        s_out_ref[h] = s_new
        o = jnp.sum(column(q[:, ks]) * s_new, axis=0, keepdims=True)
        o = _rms_scale(o, gn_ref[:, vs])
        r = r_ref[:, vs]
        y_ref[:, vs] = o * (r * _sigmoid(r))


def gla_step(z_s, za_s, state, layer, w_a2p, b_a, gla_norm):
    bd = z_s.shape[0]
    zs3 = z_s.reshape(bd, 1, N_MAIN)
    y, s_new = pl.pallas_call(
        _gla_step_kernel,
        grid=(bd,),
        in_specs=[
            pl.BlockSpec((None, 1, GLA_KEY_WIDTH), lambda i: (i, 0, COL_GQ // GLA_KEY_WIDTH)),
            pl.BlockSpec((None, 1, GLA_KEY_WIDTH), lambda i: (i, 0, COL_GK // GLA_KEY_WIDTH)),
            pl.BlockSpec((None, 1, GLA_VAL_WIDTH), lambda i: (i, 0, COL_GV // GLA_VAL_WIDTH)),
            pl.BlockSpec((None, 1, GLA_VAL_WIDTH), lambda i: (i, 0, COL_GR // GLA_VAL_WIDTH)),
            pl.BlockSpec((None, 1, LANES), lambda i: (i, 0, 0)),
            pl.BlockSpec((LANES, GLA_KEY_WIDTH), lambda i: (0, 0)),
            pl.BlockSpec((1, GLA_KEY_WIDTH), lambda i: (0, 0)),
            pl.BlockSpec((1, GLA_VAL_WIDTH), lambda i: (0, 0)),
            pl.BlockSpec((None, None, GLA_HEADS, GLA_DK, GLA_DV), lambda i: (layer, i, 0, 0, 0)),
        ],
        out_specs=[
            pl.BlockSpec((None, 1, GLA_VAL_WIDTH), lambda i: (i, 0, 0)),
            pl.BlockSpec((None, GLA_HEADS, GLA_DK, GLA_DV), lambda i: (i, 0, 0, 0)),
        ],
        out_shape=[
            jax.ShapeDtypeStruct((bd, 1, GLA_VAL_WIDTH), F32),
            jax.ShapeDtypeStruct((bd, GLA_HEADS, GLA_DK, GLA_DV), F32),
        ],
        compiler_params=_params("parallel"),
        name="gla_step",
    )(zs3, zs3, zs3, zs3, za_s.reshape(bd, 1, LANES), w_a2p, b_a.reshape(1, GLA_KEY_WIDTH),
      gla_norm.reshape(1, GLA_VAL_WIDTH), state)
    return y.reshape(bd, GLA_VAL_WIDTH), s_new


def _xattn_kernel(q_ref, kv_ref, o_ref):
    scale = X_HEAD_DIM ** -0.5
    nt = (((1,), (1,)), ((), ()))
    for h in range(X_HEADS):
        hs = slice(h * X_HEAD_DIM, (h + 1) * X_HEAD_DIM)
        k = kv_ref[:, h * X_HEAD_DIM:(h + 1) * X_HEAD_DIM].astype(BF16)
        v = kv_ref[:, X_WIDTH + h * X_HEAD_DIM:X_WIDTH + (h + 1) * X_HEAD_DIM].astype(BF16)
        s = lax.dot_general(q_ref[:, hs], k, nt, preferred_element_type=F32) * scale
        e = jnp.exp(s - jnp.max(s, axis=-1, keepdims=True))
        l = jnp.sum(e, axis=-1, keepdims=True)
        o = jnp.dot(e.astype(BF16), v, preferred_element_type=F32) / l
        o_ref[:, hs] = o.astype(o_ref.dtype)


def _xattn_step_kernel(q_ref, kv_ref, o_ref):
    scale = X_HEAD_DIM ** -0.5
    nt = (((1,), (1,)), ((), ()))
    for h in range(X_HEADS):
        hs = slice(h * X_HEAD_DIM, (h + 1) * X_HEAD_DIM)
        k = kv_ref[:, 0, h, :].astype(BF16)
        v = kv_ref[:, 1, h, :].astype(BF16)
        s = lax.dot_general(q_ref[:, hs], k, nt, preferred_element_type=F32) * scale
        e = jnp.exp(s - jnp.max(s, axis=-1, keepdims=True))
        l = jnp.sum(e, axis=-1, keepdims=True)
        o = jnp.dot(e.astype(BF16), v, preferred_element_type=F32) / l
        o_ref[:, hs] = o.astype(o_ref.dtype)


def xattn_step(q3, mem_kv, layer):
    bd, rows, _ = q3.shape
    return pl.pallas_call(
        _xattn_step_kernel,
        grid=(bd,),
        in_specs=[
            pl.BlockSpec((None, rows, X_WIDTH), lambda i: (i, 0, 0)),
            pl.BlockSpec((None, None, MEM_LEN, 2, X_HEADS, X_HEAD_DIM), lambda i: (layer, i, 0, 0, 0, 0)),
        ],
        out_specs=pl.BlockSpec((None, rows, X_WIDTH), lambda i: (i, 0, 0)),
        out_shape=jax.ShapeDtypeStruct((bd, rows, X_WIDTH), BF16),
        compiler_params=_params("parallel"),
        name="xattn_step",
    )(q3, mem_kv)


def xattn(q3, mem_kv, *, tt):
    b, t, _ = q3.shape
    tt = min(tt, t)
    assert t % tt == 0
    return pl.pallas_call(
        _xattn_kernel,
        grid=(b, t // tt),
        in_specs=[
            pl.BlockSpec((None, tt, X_WIDTH), lambda i, j: (i, j, 0)),
            pl.BlockSpec((None, MEM_LEN, 2 * X_WIDTH), lambda i, j: (i, 0, 0)),
        ],
        out_specs=pl.BlockSpec((None, tt, X_WIDTH), lambda i, j: (i, j, 0)),
        out_shape=jax.ShapeDtypeStruct((b, t, X_WIDTH), BF16),
        compiler_params=_params("parallel", "arbitrary"),
        name="xattn",
    )(q3, mem_kv)


FFN_HALO = 16
FFN_TAIL = 8


def _ffn_up_prompt_kernel(x_ref, xh_ref, g_ref, wg_ref, wv_ref, cwg_ref, cwv_ref, cbg_ref, cbv_ref,
                          act_ref, tg_ref, tv_ref, xn_ref, ug_ref, uv_ref, *, tm, seq_tiles, chunk):
    i = pl.program_id(0)

    @pl.when(pl.program_id(1) == 0)
    def _():
        _norm_rows(x_ref, g_ref, xn_ref, tm, chunk, dst_offset=FFN_HALO)
        halo = _rms_scale(xh_ref[...], g_ref[...])
        halo = jnp.where(i % seq_tiles == 0, 0.0, halo)
        xn_ref[0:FFN_HALO, :] = halo.astype(BF16)

    xn = xn_ref[...]
    ug_ref[...] = jnp.dot(xn, wg_ref[...], preferred_element_type=F32)
    uv_ref[...] = jnp.dot(xn, wv_ref[...], preferred_element_type=F32)

    def conv(u_ref, cw_ref, cb_ref):
        c = cb_ref[...] + cw_ref[0:1, :] * u_ref[FFN_HALO - 2:FFN_HALO - 2 + tm, :]
        c = c + cw_ref[1:2, :] * u_ref[FFN_HALO - 1:FFN_HALO - 1 + tm, :]
        return c + cw_ref[2:3, :] * u_ref[FFN_HALO:FFN_HALO + tm, :]

    act_ref[...] = (_gelu_tanh(conv(ug_ref, cwg_ref, cbg_ref)) * conv(uv_ref, cwv_ref, cbv_ref)).astype(act_ref.dtype)
    tg_ref[...] = ug_ref[FFN_HALO + tm - FFN_TAIL:FFN_HALO + tm, :]
    tv_ref[...] = uv_ref[FFN_HALO + tm - FFN_TAIL:FFN_HALO + tm, :]


def ffn_up_prompt(x, g, w_up, conv_w, conv_b, *, seq, tm, tn):
    m, k = x.shape
    assert seq % tm == 0 and D_FF % tn == 0 and tm % FFN_HALO == 0
    nj = D_FF // tn
    seq_tiles = seq // tm
    chunk = min(256, tm)
    halo_blocks = tm // FFN_HALO
    act, tail_g, tail_v = pl.pallas_call(
        functools.partial(_ffn_up_prompt_kernel, tm=tm, seq_tiles=seq_tiles, chunk=chunk),
        grid=(m // tm, nj),
        in_specs=[
            pl.BlockSpec((tm, k), lambda i, j: (i, 0)),
            pl.BlockSpec((FFN_HALO, k), lambda i, j: (jnp.maximum(i * halo_blocks - 1, 0), 0)),
            pl.BlockSpec((1, k), lambda i, j: (0, 0)),
            pl.BlockSpec((k, tn), lambda i, j: (0, j)),
            pl.BlockSpec((k, tn), lambda i, j: (0, j + nj)),
            pl.BlockSpec((3, tn), lambda i, j: (0, j)),
            pl.BlockSpec((3, tn), lambda i, j: (0, j + nj)),
            pl.BlockSpec((1, tn), lambda i, j: (0, j)),
            pl.BlockSpec((1, tn), lambda i, j: (0, j + nj)),
        ],
        out_specs=[
            pl.BlockSpec((tm, tn), lambda i, j: (i, j)),
            pl.BlockSpec((None, FFN_TAIL, tn), lambda i, j: (i, 0, j)),
            pl.BlockSpec((None, FFN_TAIL, tn), lambda i, j: (i, 0, j)),
        ],
        out_shape=[
            jax.ShapeDtypeStruct((m, D_FF), BF16),
            jax.ShapeDtypeStruct((m // tm, FFN_TAIL, D_FF), F32),
            jax.ShapeDtypeStruct((m // tm, FFN_TAIL, D_FF), F32),
        ],
        scratch_shapes=[
            pltpu.VMEM((tm + FFN_HALO, k), BF16),
            pltpu.VMEM((tm + FFN_HALO, tn), F32),
            pltpu.VMEM((tm + FFN_HALO, tn), F32),
        ],
        compiler_params=_params("parallel", "arbitrary"),
        name="ffn_up_prompt",
    )(x, x, g.reshape(1, k), w_up, w_up, conv_w, conv_w, conv_b.reshape(1, 2 * D_FF), conv_b.reshape(1, 2 * D_FF))
    return act, jnp.concatenate([tail_g, tail_v], axis=-1)


def _ffn_up_step_kernel(x_ref, g_ref, wg_ref, wv_ref, cwg_ref, cwv_ref, cbg_ref, cbv_ref, hg_ref, hv_ref,
                        act_ref, ng_ref, nv_ref, xn_ref, *, bd):
    @pl.when(pl.program_id(0) == 0)
    def _():
        _norm_rows(x_ref, g_ref, xn_ref, bd, bd)

    xn = xn_ref[...]
    ug = jnp.dot(xn, wg_ref[...], preferred_element_type=F32)
    uv = jnp.dot(xn, wv_ref[...], preferred_element_type=F32)

    def conv(u, h_ref, cw_ref, cb_ref):
        return cb_ref[...] + cw_ref[0:1, :] * h_ref[0] + cw_ref[1:2, :] * h_ref[1] + cw_ref[2:3, :] * u

    act_ref[...] = (_gelu_tanh(conv(ug, hg_ref, cwg_ref, cbg_ref)) * conv(uv, hv_ref, cwv_ref, cbv_ref)).astype(act_ref.dtype)
    ng_ref[0] = hg_ref[1]
    ng_ref[1] = ug
    nv_ref[0] = hv_ref[1]
    nv_ref[1] = uv


def ffn_up_step(x, g, w_up, conv_w, conv_b, hist_t, *, tn):
    bd, k = x.shape
    nj = D_FF // tn
    act, new_g, new_v = pl.pallas_call(
        functools.partial(_ffn_up_step_kernel, bd=bd),
        grid=(nj,),
        in_specs=[
            pl.BlockSpec((bd, k), lambda j: (0, 0)),
            pl.BlockSpec((1, k), lambda j: (0, 0)),
            pl.BlockSpec((k, tn), lambda j: (0, j)),
            pl.BlockSpec((k, tn), lambda j: (0, j + nj)),
            pl.BlockSpec((3, tn), lambda j: (0, j)),
            pl.BlockSpec((3, tn), lambda j: (0, j + nj)),
            pl.BlockSpec((1, tn), lambda j: (0, j)),
            pl.BlockSpec((1, tn), lambda j: (0, j + nj)),
            pl.BlockSpec((2, bd, tn), lambda j: (0, 0, j)),
            pl.BlockSpec((2, bd, tn), lambda j: (0, 0, j + nj)),
        ],
        out_specs=[
            pl.BlockSpec((bd, tn), lambda j: (0, j)),
            pl.BlockSpec((2, bd, tn), lambda j: (0, 0, j)),
            pl.BlockSpec((2, bd, tn), lambda j: (0, 0, j)),
        ],
        out_shape=[
            jax.ShapeDtypeStruct((bd, D_FF), BF16),
            jax.ShapeDtypeStruct((2, bd, D_FF), F32),
            jax.ShapeDtypeStruct((2, bd, D_FF), F32),
        ],
        scratch_shapes=[pltpu.VMEM((bd, k), BF16)],
        compiler_params=_params("arbitrary"),
        name="ffn_up_step",
    )(x, g.reshape(1, k), w_up, w_up, conv_w, conv_w, conv_b.reshape(1, 2 * D_FF), conv_b.reshape(1, 2 * D_FF),
      hist_t, hist_t)
    return act, jnp.concatenate([new_g, new_v], axis=-1)


TM = 1024
TN = 1024
TM_POST = 512
TK_POST = 2048
TN_MERGE = 512
TN_FFN = 512
XATTN_ROWS = 512
STEP_Q_ROWS = 16


def _layer_weights(l, w_in, pool_w, gla_w_a2, w_br_pool, w_br_dil, w_br_gla, w_mix_out, w_xq, w_xkv, w_xo, w_up, w_down):
    w_in_l = w_in[l]
    ga_end = COL_GA_SRC + GLA_RANK
    return dict(
        w_main=jnp.concatenate([w_in_l[:, :COL_GA_SRC], w_in_l[:, ga_end:]], axis=1).astype(BF16),
        w_ga=jnp.pad(w_in_l[:, COL_GA_SRC:ga_end], ((0, 0), (0, LANES - GLA_RANK))).astype(BF16),
        w_a2p=jnp.pad(gla_w_a2[l], ((0, LANES - GLA_RANK), (0, 0))).astype(BF16),
        pool_w=pool_w[l].astype(BF16),
        w_br_pool=w_br_pool[l].astype(BF16),
        w_br_dil=w_br_dil[l].astype(BF16),
        w_br_gla=w_br_gla[l].astype(BF16),
        w_mix_out=w_mix_out[l].astype(BF16),
        w_xq=w_xq[l].astype(BF16),
        w_xkv=w_xkv[l].astype(BF16),
        w_xo=w_xo[l].astype(BF16),
        w_up=w_up[l].astype(BF16),
        w_down=w_down[l].astype(BF16),
    )


def kernel(x_prompt, x_sample, state_pool, cache_dil1_kv, cache_dil2_kv, cache_dil3_kv, state_gla, cache_mem_kv, state_ffn_conv, mem_prompt, rel_bias, norm_mix_pre, norm_mix_post, w_in, pool_w, pool_scale, gla_w_a2, gla_b_a, gla_norm, w_br_pool, w_br_dil, w_br_gla, w_mix_out, norm_x_pre, norm_x_post, norm_mem, w_xq, w_xkv, w_xo, norm_ffn_pre, norm_ffn_post, w_up, conv_w, conv_b, w_down):
    b, seq, d = x_prompt.shape
    bd = x_sample.shape[0]
    depth = w_in.shape[0]
    m = b * seq
    assert x_sample.shape[1] == 1 and d == D_MODEL
    slot_bias = _slot_biases(rel_bias)
    band_bias = _band_bias(slot_bias)
    dil_caches = (cache_dil1_kv, cache_dil2_kv, cache_dil3_kv)

    xp = x_prompt.reshape(m, d)
    xs = x_sample.reshape(bd, d)
    pool_p, gla_p, mem_p, conv_p = [], [], [], []
    pool_s, gla_s, conv_s = [], [], []
    dil_p = [[] for _ in range(3)]
    dil_s = [[] for _ in range(3)]
    seq_tiles = seq // TM

    for l in range(depth):
        w = _layer_weights(l, w_in, pool_w, gla_w_a2, w_br_pool, w_br_dil, w_br_gla, w_mix_out, w_xq, w_xkv, w_xo, w_up, w_down)

        zm = norm_mm(xp, norm_mix_pre[l], w["w_main"], tm=TM, tn=TN)
        za = norm_mm(xp, norm_mix_pre[l], w["w_ga"], tm=TM, tn=LANES)
        z3 = zm.reshape(b, seq, N_MAIN)
        y_pool = pool_prompt(z3, w["pool_w"], pool_scale[l]).reshape(m, POOL_WIDTH)
        y_dil = dil_prompt(z3, band_bias).reshape(m, DIL_WIDTH)
        y_gla, gla_new = gla_prompt(z3, za.reshape(b, seq, LANES), w["w_a2p"], gla_b_a[l], gla_norm[l])
        merged = branch_merge(y_pool, y_dil, y_gla.reshape(m, GLA_VAL_WIDTH), w["w_br_pool"], w["w_br_dil"], w["w_br_gla"],
                              zm, tm=TM, tn=TN_MERGE)
        xp = mm_post(merged, w["w_mix_out"], norm_mix_post[l], xp, tm=TM_POST, tk=TK_POST)
        mem_kv = norm_mm(mem_prompt.reshape(b * MEM_LEN, d), norm_mem[l], w["w_xkv"], tm=TM, tn=TN)
        q = norm_mm(xp, norm_x_pre[l], w["w_xq"], tm=TM, tn=TN, out_dtype=BF16)
        o = xattn(q.reshape(b, seq, X_WIDTH), mem_kv.reshape(b, MEM_LEN, 2 * X_WIDTH), tt=XATTN_ROWS)
        xp = mm_post(o.reshape(m, X_WIDTH), w["w_xo"], norm_x_post[l], xp, tm=TM_POST, tk=TK_POST)
        act, tails = ffn_up_prompt(xp, norm_ffn_pre[l], w["w_up"], conv_w[l], conv_b[l], seq=seq, tm=TM, tn=TN_FFN)
        xp = mm_post(act, w["w_down"], norm_ffn_post[l], xp, tm=TM_POST, tk=TK_POST)

        pool_p.append(z3[:, seq - POOL_HIST:, COL_POOL:COL_POOL + POOL_WIDTH])
        for g, (win, _) in enumerate(DIL_CONFIGS):
            keep = min(win, seq)
            kv = z3[:, seq - keep:, COL_DIL_K[g]:COL_DIL_K[g] + 2 * DIL_WIDTH]
            dil_p[g].append(kv.reshape(b, keep, 2, DIL_HEADS, DIL_HEAD_DIM))
        gla_p.append(gla_new)
        mem_p.append(mem_kv.reshape(b, MEM_LEN, 2, X_HEADS, X_HEAD_DIM))
        conv_p.append(tails.reshape(b, seq_tiles, FFN_TAIL, 2 * D_FF)[:, seq_tiles - 1, FFN_TAIL - 2:, :])

        zs = norm_mm(xs, norm_mix_pre[l], w["w_main"], tm=bd, tn=TN)
        zas = norm_mm(xs, norm_mix_pre[l], w["w_ga"], tm=bd, tn=LANES)
        y_pool_s, pool_new_t = pool_step(jnp.swapaxes(state_pool[l], 0, 1), zs, w["pool_w"], pool_scale[l])
        y_dil_s = dil_step(zs, dil_caches, l, slot_bias).astype(BF16)
        y_gla_s, gla_new_s = gla_step(zs, zas, state_gla, l, w["w_a2p"], gla_b_a[l], gla_norm[l])
        merged_s = branch_merge(y_pool_s, y_dil_s, y_gla_s.astype(BF16), w["w_br_pool"], w["w_br_dil"], w["w_br_gla"],
                                zs, tm=bd, tn=TN_MERGE)
        xs = mm_post(merged_s, w["w_mix_out"], norm_mix_post[l], xs, tm=bd, tk=TK_POST)
        q_s = norm_mm(xs, norm_x_pre[l], w["w_xq"], tm=bd, tn=TN, out_dtype=BF16)
        q_s = jnp.pad(q_s[:, None, :], ((0, 0), (0, STEP_Q_ROWS - 1), (0, 0)))
        o_s = xattn_step(q_s, cache_mem_kv, l)[:, 0, :]
        xs = mm_post(o_s, w["w_xo"], norm_x_post[l], xs, tm=bd, tk=TK_POST)
        act_s, conv_new_t = ffn_up_step(xs, norm_ffn_pre[l], w["w_up"], conv_w[l], conv_b[l],
                                        jnp.swapaxes(state_ffn_conv[l], 0, 1), tn=TN_FFN)
        xs = mm_post(act_s, w["w_down"], norm_ffn_post[l], xs, tm=bd, tk=TK_POST)

        pool_s.append(jnp.swapaxes(pool_new_t, 0, 1))
        for g in range(3):
            kv = zs[:, COL_DIL_K[g]:COL_DIL_K[g] + 2 * DIL_WIDTH]
            dil_s[g].append(kv.reshape(bd, 1, 2, DIL_HEADS, DIL_HEAD_DIM))
        gla_s.append(gla_new_s)
        conv_s.append(jnp.swapaxes(conv_new_t, 0, 1))

    return (xp.reshape(b, seq, d), xs.reshape(bd, 1, d),
            jnp.stack(pool_p), jnp.stack(dil_p[0]), jnp.stack(dil_p[1]), jnp.stack(dil_p[2]), jnp.stack(gla_p), jnp.stack(mem_p), jnp.stack(conv_p),
            jnp.stack(pool_s), jnp.stack(dil_s[0]), jnp.stack(dil_s[1]), jnp.stack(dil_s[2]), jnp.stack(gla_s), jnp.stack(conv_s))
```

```python
import functools
import math

import jax
import jax.numpy as jnp
from jax import lax
from jax.experimental import pallas as pl
from jax.experimental.pallas import tpu as pltpu

F32 = jnp.float32
BF16 = jnp.bfloat16
EPS = 1e-6
NEG_INF = -1e30

VMEM_LIMIT_BYTES = 56 * 1024 * 1024
LANES = 128

D_MODEL = 2048
POOL_WINDOWS = (2, 4, 8, 16)
POOL_GROUP = 256
POOL_WIDTH = 1024
POOL_HIST = 15
DIL_CONFIGS = ((128, 1), (512, 4), (2048, 16))
DIL_HEADS = 8
DIL_HEAD_DIM = 128
DIL_WIDTH = 1024
DIL_BLOCK = 128
GLA_HEADS = 4
GLA_DK = 256
GLA_DV = 512
GLA_KEY_WIDTH = 1024
GLA_VAL_WIDTH = 2048
GLA_RANK = 16
GLA_TAU = 16.0
GLA_CHUNK = 64
REL_BUCKETS = 32
REL_MAX_DIST = 2048
MEM_LEN = 256
X_HEADS = 4
X_HEAD_DIM = 256
X_WIDTH = 1024
D_FF = 5632
PAST_LEN = 8192

W_TILE = 1024
SRC_POOL, SRC_GQ, SRC_GK, SRC_GV, SRC_GR = 0, 10, 11, 12, 14
SRC_GA_COL = 16384
SRC_GATES_COL = SRC_GA_COL + GLA_RANK
Z1_GV, Z1_GR, Z1_GQ, Z1_GK, Z1_POOL = 0, 2048, 4096, 5120, 6144
Z1_DIL_Q = tuple(7168 + 1024 * g for g in range(3))
N_Z1 = 10240
ZKV_K = tuple(2048 * g for g in range(3))
ZKV_V = tuple(2048 * g + 1024 for g in range(3))
N_ZKV = 6144
N_ZG = 3 * D_MODEL


def _z1_src_tile(j):
    return jnp.where(j < 4, SRC_GV + j, jnp.where(j < 6, SRC_GQ - 4 + j, jnp.where(j == 6, SRC_POOL, 3 * j - 20)))


def _zkv_src_tile(j):
    return 2 + 3 * (j // 2) + j % 2


def _params(*semantics):
    return pltpu.CompilerParams(dimension_semantics=semantics, vmem_limit_bytes=VMEM_LIMIT_BYTES)


def _sigmoid(x):
    return 1.0 / (1.0 + jnp.exp(-x))


def _log_sigmoid(x):
    return jnp.minimum(x, 0.0) - jnp.log(1.0 + jnp.exp(-jnp.abs(x)))


def _gelu_tanh(x):
    return x * (0.5 * (1.0 + jnp.tanh(math.sqrt(2.0 / math.pi) * (x + 0.044715 * (x * x * x)))))


def _rms_scale(y, g):
    return y * lax.rsqrt(jnp.mean(y * y, axis=-1, keepdims=True) + EPS) * g


def _rms_cast_kernel(x_ref, g_ref, o_ref):
    o_ref[...] = _rms_scale(x_ref[...], g_ref[...]).astype(o_ref.dtype)


def rms_cast(x, g, *, tm=256):
    m, k = x.shape
    tm = min(tm, m)
    assert m % tm == 0
    return pl.pallas_call(
        _rms_cast_kernel,
        grid=(m // tm,),
        in_specs=[pl.BlockSpec((tm, k), lambda i: (i, 0)), pl.BlockSpec((1, k), lambda i: (0, 0))],
        out_specs=pl.BlockSpec((tm, k), lambda i: (i, 0)),
        out_shape=jax.ShapeDtypeStruct((m, k), BF16),
        compiler_params=_params("parallel"),
        name="rms_cast",
    )(x, g.reshape(1, k))


def _mm_cols_kernel(*refs, shift, with_cache):
    if shift:
        x_ref, w_ref, wx_ref = refs[:3]
        refs = refs[3:]
    else:
        x_ref, w_ref = refs[:2]
        refs = refs[2:]
    o_ref = refs[0]
    c_ref = refs[1] if with_cache else None
    wb_ref = refs[-1]

    @pl.when(pl.program_id(1) == 0)
    def _():
        if shift:
            tn = wb_ref.shape[1]
            wb_ref[:, 0:tn - shift] = w_ref[:, shift:].astype(BF16)
            wb_ref[:, tn - shift:tn] = wx_ref[:, 0:shift].astype(BF16)
        else:
            wb_ref[...] = w_ref[...].astype(BF16)

    res = jnp.dot(x_ref[...], wb_ref[...], preferred_element_type=F32)
    o_ref[...] = res.astype(o_ref.dtype)
    if with_cache:
        for h in range(DIL_HEADS):
            c_ref[:, h, :] = res[:, h * DIL_HEAD_DIM:(h + 1) * DIL_HEAD_DIM]


def mm_cols(xn, w, layer, *, src_tile, nj, tn, tm, out_dtype=F32, shift=0, with_cache=False):
    m, k = xn.shape
    tm = min(tm, m)
    assert m % tm == 0 and 0 <= shift < LANES
    in_specs = [
        pl.BlockSpec((tm, k), lambda j, i: (i, 0)),
        pl.BlockSpec((None, k, tn), lambda j, i: (layer, 0, src_tile(j))),
    ]
    operands = [xn, w]
    if shift:
        in_specs.append(pl.BlockSpec((None, k, LANES), lambda j, i: (layer, 0, (src_tile(j) + 1) * (tn // LANES))))
        operands.append(w)
    out_specs = [pl.BlockSpec((tm, tn), lambda j, i: (i, j))]
    out_shape = [jax.ShapeDtypeStruct((m, nj * tn), out_dtype)]
    if with_cache:
        assert tn == DIL_WIDTH and nj % 2 == 0
        out_specs.append(pl.BlockSpec((None, tm, None, DIL_HEADS, DIL_HEAD_DIM), lambda j, i: (j // 2, i, j % 2, 0, 0)))
        out_shape.append(jax.ShapeDtypeStruct((nj // 2, m, 2, DIL_HEADS, DIL_HEAD_DIM), F32))
    out = pl.pallas_call(
        functools.partial(_mm_cols_kernel, shift=shift, with_cache=with_cache),
        grid=(nj, m // tm),
        in_specs=in_specs,
        out_specs=out_specs,
        out_shape=out_shape,
        scratch_shapes=[pltpu.VMEM((k, tn), BF16)],
        compiler_params=_params("parallel", "arbitrary"),
        name="mm_cols",
    )(*operands)
    return out if with_cache else out[0]


def _mm_post_kernel(a_ref, w_ref, g_ref, res_ref, o_ref, acc_ref, *, nk):
    kk = pl.program_id(1)
    part = jnp.dot(a_ref[...], w_ref[...], preferred_element_type=F32)
    if nk == 1:
        o_ref[...] = res_ref[...] + _rms_scale(part, g_ref[...])
        return

    @pl.when(kk == 0)
    def _():
        acc_ref[...] = part

    @pl.when(jnp.logical_and(kk > 0, kk < nk - 1))
    def _():
        acc_ref[...] += part

    @pl.when(kk == nk - 1)
    def _():
        o_ref[...] = res_ref[...] + _rms_scale(acc_ref[...] + part, g_ref[...])


def mm_post(a, w, g, res, *, tm, tk):
    m, k = a.shape
    n = w.shape[1]
    tm = min(tm, m)
    tk = k if k <= tk else k // 4
    assert m % tm == 0 and k % tk == 0 and tk % LANES == 0
    return pl.pallas_call(
        functools.partial(_mm_post_kernel, nk=k // tk),
        grid=(m // tm, k // tk),
        in_specs=[
            pl.BlockSpec((tm, tk), lambda i, kk: (i, kk)),
            pl.BlockSpec((tk, n), lambda i, kk: (kk, 0)),
            pl.BlockSpec((1, n), lambda i, kk: (0, 0)),
            pl.BlockSpec((tm, n), lambda i, kk: (i, 0)),
        ],
        out_specs=pl.BlockSpec((tm, n), lambda i, kk: (i, 0)),
        out_shape=jax.ShapeDtypeStruct((m, n), F32),
        scratch_shapes=[pltpu.VMEM((tm, n), F32)],
        compiler_params=_params("parallel", "arbitrary"),
        name="mm_post",
    )(a, w, g.reshape(1, n), res)


def _branch_merge_kernel(yp_ref, yd_ref, yg_ref, wp_ref, wd_ref, wg_ref, g0_ref, g1_ref, g2_ref, o_ref):
    bp = jnp.dot(yp_ref[...], wp_ref[...], preferred_element_type=F32)
    bd = jnp.dot(yd_ref[...], wd_ref[...], preferred_element_type=F32)
    bg = jnp.dot(yg_ref[...], wg_ref[...], preferred_element_type=F32)
    merged = _sigmoid(g0_ref[...]) * bp + _sigmoid(g1_ref[...]) * bd + _sigmoid(g2_ref[...]) * bg
    o_ref[...] = merged.astype(o_ref.dtype)


def branch_merge(y_pool, y_dil, y_gla, w_pool, w_dil, w_gla, z_gates, *, tm, tn):
    m = y_pool.shape[0]
    tm = min(tm, m)
    assert m % tm == 0 and D_MODEL % tn == 0
    gate_blk = [b * D_MODEL // tn for b in range(3)]

    def gate_spec(b):
        return pl.BlockSpec((tm, tn), lambda i, j: (i, gate_blk[b] + j))

    return pl.pallas_call(
        _branch_merge_kernel,
        grid=(m // tm, D_MODEL // tn),
        in_specs=[
            pl.BlockSpec((tm, POOL_WIDTH), lambda i, j: (i, 0)),
            pl.BlockSpec((tm, DIL_WIDTH), lambda i, j: (i, 0)),
            pl.BlockSpec((tm, GLA_VAL_WIDTH), lambda i, j: (i, 0)),
            pl.BlockSpec((POOL_WIDTH, tn), lambda i, j: (0, j)),
            pl.BlockSpec((DIL_WIDTH, tn), lambda i, j: (0, j)),
            pl.BlockSpec((GLA_VAL_WIDTH, tn), lambda i, j: (0, j)),
            gate_spec(0),
            gate_spec(1),
            gate_spec(2),
        ],
        out_specs=pl.BlockSpec((tm, tn), lambda i, j: (i, j)),
        out_shape=jax.ShapeDtypeStruct((m, D_MODEL), BF16),
        compiler_params=_params("parallel", "arbitrary"),
        name="branch_merge",
    )(y_pool, y_dil, y_gla, w_pool, w_dil, w_gla, z_gates, z_gates, z_gates)


def _pool_prompt_kernel(u_ref, w_ref, s_ref, o_ref, buf_a, buf_b, *, seq):
    pad = POOL_HIST + 1
    zeros = jnp.zeros((pad, POOL_GROUP), F32)
    t = lax.broadcasted_iota(jnp.int32, (seq, 1), 0)
    for g, win in enumerate(POOL_WINDOWS):
        cols = slice(g * POOL_GROUP, (g + 1) * POOL_GROUP)
        u = u_ref[:, cols]
        cur, nxt = buf_a, buf_b
        cur[0:pad, :] = zeros
        nxt[0:pad, :] = zeros
        cur[pad:pad + seq, :] = u
        k = 1
        while k < win:
            nxt[pad:pad + seq, :] = cur[pad:pad + seq, :] + cur[pad - k:pad - k + seq, :]
            cur, nxt = nxt, cur
            k *= 2
        cnt = jnp.minimum(win, t + 1).astype(F32)
        d = cur[pad:pad + seq, :] / cnt - u
        y = jnp.dot(d.astype(BF16), w_ref[g], preferred_element_type=F32) * s_ref[:, cols]
        o_ref[:, cols] = y.astype(o_ref.dtype)


def pool_prompt(z3, pool_w, pool_scale):
    b, seq, _ = z3.shape
    return pl.pallas_call(
        functools.partial(_pool_prompt_kernel, seq=seq),
        grid=(b,),
        in_specs=[
            pl.BlockSpec((None, seq, POOL_WIDTH), lambda i: (i, 0, Z1_POOL // POOL_WIDTH)),
            pl.BlockSpec((len(POOL_WINDOWS), POOL_GROUP, POOL_GROUP), lambda i: (0, 0, 0)),
            pl.BlockSpec((1, POOL_WIDTH), lambda i: (0, 0)),
        ],
        out_specs=pl.BlockSpec((None, seq, POOL_WIDTH), lambda i: (i, 0, 0)),
        out_shape=jax.ShapeDtypeStruct((b, seq, POOL_WIDTH), BF16),
        scratch_shapes=[pltpu.VMEM((seq + POOL_HIST + 1, POOL_GROUP), F32)] * 2,
        compiler_params=_params("parallel"),
        name="pool_prompt",
    )(z3, pool_w, pool_scale.reshape(1, POOL_WIDTH))


def _pool_step_kernel(hist_ref, u_ref, w_ref, s_ref, y_ref, new_ref):
    u = u_ref[...]
    for r in range(POOL_HIST - 1):
        new_ref[r] = hist_ref[r + 1]
    new_ref[POOL_HIST - 1] = u
    for g, win in enumerate(POOL_WINDOWS):
        cols = slice(g * POOL_GROUP, (g + 1) * POOL_GROUP)
        ug = u[:, cols]
        acc = ug
        for r in range(POOL_HIST - (win - 1), POOL_HIST):
            acc = acc + hist_ref[r, :, cols]
        d = acc / float(win) - ug
        y = jnp.dot(d.astype(BF16), w_ref[g], preferred_element_type=F32) * s_ref[:, cols]
        y_ref[:, cols] = y.astype(y_ref.dtype)


def pool_step(hist_t, z_s, pool_w, pool_scale):
    bd = z_s.shape[0]
    return pl.pallas_call(
        _pool_step_kernel,
        grid=(1,),
        in_specs=[
            pl.BlockSpec((POOL_HIST, bd, POOL_WIDTH), lambda i: (0, 0, 0)),
            pl.BlockSpec((bd, POOL_WIDTH), lambda i: (0, Z1_POOL // POOL_WIDTH)),
            pl.BlockSpec((len(POOL_WINDOWS), POOL_GROUP, POOL_GROUP), lambda i: (0, 0, 0)),
            pl.BlockSpec((1, POOL_WIDTH), lambda i: (0, 0)),
        ],
        out_specs=[
            pl.BlockSpec((bd, POOL_WIDTH), lambda i: (0, 0)),
            pl.BlockSpec((POOL_HIST, bd, POOL_WIDTH), lambda i: (0, 0, 0)),
        ],
        out_shape=[
            jax.ShapeDtypeStruct((bd, POOL_WIDTH), BF16),
            jax.ShapeDtypeStruct((POOL_HIST, bd, POOL_WIDTH), F32),
        ],
        compiler_params=_params("arbitrary"),
        name="pool_step",
    )(hist_t, z_s, pool_w, pool_scale.reshape(1, POOL_WIDTH))


def _rel_bucket(dist):
    max_exact = REL_BUCKETS // 2
    scaled = jnp.log(jnp.maximum(dist, 1).astype(F32) / max_exact) / math.log(REL_MAX_DIST / max_exact)
    large = jnp.minimum(max_exact + (scaled * (REL_BUCKETS - max_exact)).astype(jnp.int32), REL_BUCKETS - 1)
    return jnp.where(dist < max_exact, dist, large)


def _select_rows(table_t, index, n):
    onehot = (index.reshape(1, -1) == jnp.arange(n, dtype=jnp.int32)[:, None]).astype(F32)
    out = jnp.dot(table_t, onehot, precision=lax.Precision.HIGHEST, preferred_element_type=F32)
    return out.reshape((table_t.shape[0],) + index.shape)


def _slot_biases(rel_bias):
    out = []
    for g, (win, dil) in enumerate(DIL_CONFIGS):
        dist = jnp.arange(win // dil + 1, dtype=jnp.int32) * dil
        table_t = rel_bias[:, g * DIL_HEADS:(g + 1) * DIL_HEADS].T.astype(F32)
        out.append(_select_rows(table_t, _rel_bucket(dist), REL_BUCKETS))
    return out


def _band_bias(slot_bias):
    qi = jnp.arange(DIL_BLOCK, dtype=jnp.int32)[:, None] + DIL_BLOCK
    ki = jnp.arange(2 * DIL_BLOCK, dtype=jnp.int32)[None, :]
    rel = qi - ki
    out = []
    for g, (win, dil) in enumerate(DIL_CONFIGS):
        n_slots = win // dil
        ok = (rel >= 0) & (rel <= n_slots)
        b = _select_rows(slot_bias[g], jnp.clip(rel, 0, n_slots), n_slots + 1)
        out.append(jnp.where(ok[None], b, NEG_INF))
    return jnp.stack(out, axis=0)


def _dil_prompt_kernel(q0, k0, v0, q1, k1, v1, q2, k2, v2, bias_ref, o_ref, o_scr, lse_scr, *, seq):
    qs, ks, vs = (q0, q1, q2), (k0, k1, k2), (v0, v1, v2)
    scale = DIL_HEAD_DIM ** -0.5
    blk = DIL_BLOCK
    nt = (((1,), (1,)), ((), ()))

    def rows(ref, start, dil):
        if dil == 1:
            return ref[pl.ds(start, blk), :]
        return ref[pl.ds(start, blk, stride=dil), :]

    for g, (_, dil) in enumerate(DIL_CONFIGS):
        nb = seq // dil // blk
        for r in range(dil):
            for ub in range(nb):
                start = r + dil * ub * blk
                q = rows(qs[g], start, dil).astype(BF16)
                kc = rows(ks[g], start, dil).astype(BF16)
                vc = rows(vs[g], start, dil).astype(BF16)
                if ub == 0:
                    kk, vv, bias = kc, vc, bias_ref[g, :, blk:]
                else:
                    prev = start - dil * blk
                    kk = jnp.concatenate([rows(ks[g], prev, dil).astype(BF16), kc], axis=0)
                    vv = jnp.concatenate([rows(vs[g], prev, dil).astype(BF16), vc], axis=0)
                    bias = bias_ref[g]
                s = lax.dot_general(q, kk, nt, preferred_element_type=F32) * scale + bias
                m = jnp.max(s, axis=-1, keepdims=True)
                e = jnp.exp(s - m)
                l = jnp.sum(e, axis=-1, keepdims=True)
                o = jnp.dot(e.astype(BF16), vv, preferred_element_type=F32) / l
                lse = jnp.broadcast_to(m + jnp.log(l), (blk, LANES))
                if dil == 1:
                    o_scr[g, pl.ds(start, blk), :] = o
                    lse_scr[g, pl.ds(start, blk), :] = lse
                else:
                    o_scr[g, pl.ds(start, blk, stride=dil), :] = o
                    lse_scr[g, pl.ds(start, blk, stride=dil), :] = lse

    chunk = 256

    def combine(c, carry):
        sl = pl.ds(pl.multiple_of(c * chunk, chunk), chunk)
        l0, l1, l2 = lse_scr[0, sl, :], lse_scr[1, sl, :], lse_scr[2, sl, :]
        mx = jnp.maximum(jnp.maximum(l0, l1), l2)
        w0, w1, w2 = jnp.exp(l0 - mx), jnp.exp(l1 - mx), jnp.exp(l2 - mx)
        y = (w0 * o_scr[0, sl, :] + w1 * o_scr[1, sl, :] + w2 * o_scr[2, sl, :]) / (w0 + w1 + w2)
        o_ref[sl, :] = y.astype(o_ref.dtype)
        return carry

    lax.fori_loop(0, seq // chunk, combine, 0)


def dil_prompt(z3, zkv3, band_bias):
    b, seq, _ = z3.shape
    assert seq % (DIL_BLOCK * 16) == 0

    def col_spec(col):
        blk0 = col // DIL_HEAD_DIM
        return pl.BlockSpec((None, seq, DIL_HEAD_DIM), lambda i, h: (i, 0, blk0 + h))

    in_specs, operands = [], []
    for g in range(3):
        in_specs += [col_spec(Z1_DIL_Q[g]), col_spec(ZKV_K[g]), col_spec(ZKV_V[g])]
        operands += [z3, zkv3, zkv3]
    in_specs.append(pl.BlockSpec((3, None, DIL_BLOCK, 2 * DIL_BLOCK), lambda i, h: (0, h, 0, 0)))
    return pl.pallas_call(
        functools.partial(_dil_prompt_kernel, seq=seq),
        grid=(b, DIL_HEADS),
        in_specs=in_specs,
        out_specs=pl.BlockSpec((None, seq, DIL_HEAD_DIM), lambda i, h: (i, 0, h)),
        out_shape=jax.ShapeDtypeStruct((b, seq, DIL_WIDTH), BF16),
        scratch_shapes=[pltpu.VMEM((3, seq, DIL_HEAD_DIM), F32), pltpu.VMEM((3, seq, LANES), F32)],
        compiler_params=_params("parallel", "arbitrary"),
        name="dil_prompt",
    )(*operands, band_bias)


def _dil_step_kernel(q_ref, kvn_ref, c0, c1, c2, bias_ref, bias0_ref, o_ref):
    caches = (c0, c1, c2)
    scale = DIL_HEAD_DIM ** -0.5
    outs, lses = [], []
    for g in range(3):
        q, kn, vn = q_ref[g], kvn_ref[g, 0], kvn_ref[g, 1]
        kc = caches[g][:, 0]
        vc = caches[g][:, 1]
        s = jnp.sum(kc * q[None], axis=-1, keepdims=True) * scale + bias_ref[g]
        s_new = jnp.sum(q * kn, axis=-1, keepdims=True) * scale + bias0_ref[g]
        m = jnp.maximum(jnp.max(s, axis=0), s_new)
        p = jnp.exp(s - m[None])
        p_new = jnp.exp(s_new - m)
        l = jnp.sum(p, axis=0) + p_new
        outs.append((jnp.sum(p * vc, axis=0) + p_new * vn) / l)
        lses.append(m + jnp.log(l))
    mx = jnp.maximum(jnp.maximum(lses[0], lses[1]), lses[2])
    w0, w1, w2 = jnp.exp(lses[0] - mx), jnp.exp(lses[1] - mx), jnp.exp(lses[2] - mx)
    o_ref[...] = (w0 * outs[0] + w1 * outs[1] + w2 * outs[2]) / (w0 + w1 + w2)


def dil_step(z_s, kv_new, caches, layer, slot_bias):
    bd = z_s.shape[0]
    n_slots = DIL_CONFIGS[0][0] // DIL_CONFIGS[0][1]
    q = z_s[:, Z1_DIL_Q[0]:Z1_DIL_Q[2] + DIL_WIDTH].reshape(bd, 3, DIL_HEADS, DIL_HEAD_DIM)
    in_specs = [
        pl.BlockSpec((None, 3, DIL_HEADS, DIL_HEAD_DIM), lambda i: (i, 0, 0, 0)),
        pl.BlockSpec((3, None, 2, DIL_HEADS, DIL_HEAD_DIM), lambda i: (0, i, 0, 0, 0)),
    ]
    cache_views = []
    for g, (win, dil) in enumerate(DIL_CONFIGS):
        depth = caches[g].shape[0]
        assert caches[g].shape[2] == win and win // dil == n_slots
        cache_views.append(caches[g].reshape(depth, bd, n_slots, dil, 2, DIL_HEADS, DIL_HEAD_DIM))
        in_specs.append(pl.BlockSpec((None, None, n_slots, None, 2, DIL_HEADS, DIL_HEAD_DIM),
                                     lambda i: (layer, i, 0, 0, 0, 0, 0)))
    bias_rows = jnp.stack([sb[:, n_slots:0:-1].T for sb in slot_bias], axis=0)
    bias_rows = jnp.broadcast_to(bias_rows[..., None], (3, n_slots, DIL_HEADS, LANES))
    bias_new = jnp.broadcast_to(jnp.stack([sb[:, 0] for sb in slot_bias], axis=0)[..., None], (3, DIL_HEADS, LANES))
    in_specs.append(pl.BlockSpec((3, n_slots, DIL_HEADS, LANES), lambda i: (0, 0, 0, 0)))
    in_specs.append(pl.BlockSpec((3, DIL_HEADS, LANES), lambda i: (0, 0, 0)))
    out = pl.pallas_call(
        _dil_step_kernel,
        grid=(bd,),
        in_specs=in_specs,
        out_specs=pl.BlockSpec((None, DIL_HEADS, DIL_HEAD_DIM), lambda i: (i, 0, 0)),
        out_shape=jax.ShapeDtypeStruct((bd, DIL_HEADS, DIL_HEAD_DIM), F32),
        compiler_params=_params("parallel"),
        name="dil_step",
    )(q, kv_new, *cache_views, bias_rows, bias_new)
    return out.reshape(bd, DIL_WIDTH)


def _rank_lanes(za):
    lane = lax.broadcasted_iota(jnp.int32, za.shape, za.ndim - 1)
    return jnp.where(lane < GLA_RANK, za, 0.0)


def _gla_prompt_kernel(q_ref, k_ref, v_ref, r_ref, ga_ref, wa_ref, ba_ref, gn_ref, y_ref, s_out_ref, st_ref, o_scr, *, tq):
    t = pl.program_id(2)
    ck = GLA_CHUNK

    @pl.when(t == 0)
    def _():
        st_ref[...] = jnp.zeros_like(st_ref)

    la = jnp.dot(_rank_lanes(ga_ref[...]).astype(BF16), wa_ref[...], preferred_element_type=F32) + ba_ref[...]
    log_a = _log_sigmoid(la) / GLA_TAU
    row = lax.broadcasted_iota(jnp.int32, (ck, ck), 0)
    col = lax.broadcasted_iota(jnp.int32, (ck, ck), 1)
    tril = row >= col
    tril_f = tril.astype(F32)
    nt = (((1,), (1,)), ((), ()))
    tn = (((0,), (0,)), ((), ()))
    for c in range(tq // ck):
        sl = slice(c * ck, (c + 1) * ck)
        b = jnp.dot(tril_f, log_a[sl], precision=lax.Precision.HIGHEST, preferred_element_type=F32)
        b_last = b[ck - 1:ck, :]
        q = q_ref[sl, :] * (GLA_DK ** -0.5)
        k = k_ref[sl, :]
        v = v_ref[sl, :].astype(BF16)
        q_t = (q * jnp.exp(b)).astype(BF16)
        k_t = (k * jnp.exp(-b)).astype(BF16)
        k_h = (k * jnp.exp(b_last - b)).astype(BF16)
        decay = jnp.exp(b_last)
        a = lax.dot_general(q_t, k_t, nt, preferred_element_type=F32)
        a = jnp.where(tril, a, 0.0)
        st = st_ref[...]
        o = jnp.dot(a.astype(BF16), v, preferred_element_type=F32)
        o = o + lax.dot_general(q_t, st.astype(BF16), nt, preferred_element_type=F32)
        o_scr[sl, :] = o
        st_ref[...] = decay * st + lax.dot_general(v, k_h, tn, preferred_element_type=F32)

    o = o_scr[...]
    o = _rms_scale(o, gn_ref[...])
    r = r_ref[...]
    y_ref[...] = (o * (r * _sigmoid(r))).astype(y_ref.dtype)

    @pl.when(t == pl.num_programs(2) - 1)
    def _():
        s_out_ref[...] = st_ref[...].T


def gla_prompt(z3, za3, w_a2p, b_a, gla_norm, *, tq=512):
    b, seq, _ = z3.shape
    assert seq % tq == 0 and tq % GLA_CHUNK == 0
    return pl.pallas_call(
        functools.partial(_gla_prompt_kernel, tq=tq),
        grid=(b, GLA_HEADS, seq // tq),
        in_specs=[
            pl.BlockSpec((None, tq, GLA_DK), lambda i, h, t: (i, t, Z1_GQ // GLA_DK + h)),
            pl.BlockSpec((None, tq, GLA_DK), lambda i, h, t: (i, t, Z1_GK // GLA_DK + h)),
            pl.BlockSpec((None, tq, GLA_DV), lambda i, h, t: (i, t, Z1_GV // GLA_DV + h)),
            pl.BlockSpec((None, tq, GLA_DV), lambda i, h, t: (i, t, Z1_GR // GLA_DV + h)),
            pl.BlockSpec((None, tq, LANES), lambda i, h, t: (i, t, 0)),
            pl.BlockSpec((LANES, GLA_DK), lambda i, h, t: (0, h)),
            pl.BlockSpec((1, GLA_DK), lambda i, h, t: (0, h)),
            pl.BlockSpec((1, GLA_DV), lambda i, h, t: (0, h)),
        ],
        out_specs=[
            pl.BlockSpec((None, tq, GLA_DV), lambda i, h, t: (i, t, h)),
            pl.BlockSpec((None, None, GLA_DK, GLA_DV), lambda i, h, t: (i, h, 0, 0)),
        ],
        out_shape=[
            jax.ShapeDtypeStruct((b, seq, GLA_VAL_WIDTH), BF16),
            jax.ShapeDtypeStruct((b, GLA_HEADS, GLA_DK, GLA_DV), F32),
        ],
        scratch_shapes=[pltpu.VMEM((GLA_DV, GLA_DK), F32), pltpu.VMEM((tq, GLA_DV), F32)],
        compiler_params=_params("parallel", "parallel", "arbitrary"),
        name="gla_prompt",
    )(z3, z3, z3, z3, za3, w_a2p, b_a.reshape(1, GLA_KEY_WIDTH), gla_norm.reshape(1, GLA_VAL_WIDTH))


def _gla_step_kernel(q_ref, k_ref, v_ref, r_ref, ga_ref, wa_ref, ba_ref, gn_ref, s_ref, y_ref, s_out_ref):
    ga8 = jnp.broadcast_to(_rank_lanes(ga_ref[...]), (8, LANES)).astype(BF16)
    la = jnp.dot(ga8, wa_ref[...], preferred_element_type=F32)[0:1, :] + ba_ref[...]
    ea = jnp.exp(_log_sigmoid(la) / GLA_TAU)
    q = q_ref[...] * (GLA_DK ** -0.5)
    k = k_ref[...]
    eye = lax.broadcasted_iota(jnp.int32, (GLA_DK, GLA_DK), 0) == lax.broadcasted_iota(jnp.int32, (GLA_DK, GLA_DK), 1)

    def column(row):
        return jnp.sum(jnp.where(eye, jnp.broadcast_to(row, (GLA_DK, GLA_DK)), 0.0), axis=-1, keepdims=True)

    for h in range(GLA_HEADS):
        ks = slice(h * GLA_DK, (h + 1) * GLA_DK)
        vs = slice(h * GLA_DV, (h + 1) * GLA_DV)
        s_new = column(ea[:, ks]) * s_ref[h] + column(k[:, ks]) * v_ref[:, vs]
        s_out_ref[h] = s_new
        o = jnp.sum(column(q[:, ks]) * s_new, axis=0, keepdims=True)
        o = _rms_scale(o, gn_ref[:, vs])
        r = r_ref[:, vs]
        y_ref[:, vs] = o * (r * _sigmoid(r))


def gla_step(z_s, za_s, state, layer, w_a2p, b_a, gla_norm):
    bd = z_s.shape[0]
    zs3 = z_s.reshape(bd, 1, N_Z1)
    y, s_new = pl.pallas_call(
        _gla_step_kernel,
        grid=(bd,),
        in_specs=[
            pl.BlockSpec((None, 1, GLA_KEY_WIDTH), lambda i: (i, 0, Z1_GQ // GLA_KEY_WIDTH)),
            pl.BlockSpec((None, 1, GLA_KEY_WIDTH), lambda i: (i, 0, Z1_GK // GLA_KEY_WIDTH)),
            pl.BlockSpec((None, 1, GLA_VAL_WIDTH), lambda i: (i, 0, Z1_GV // GLA_VAL_WIDTH)),
            pl.BlockSpec((None, 1, GLA_VAL_WIDTH), lambda i: (i, 0, Z1_GR // GLA_VAL_WIDTH)),
            pl.BlockSpec((None, 1, LANES), lambda i: (i, 0, 0)),
            pl.BlockSpec((LANES, GLA_KEY_WIDTH), lambda i: (0, 0)),
            pl.BlockSpec((1, GLA_KEY_WIDTH), lambda i: (0, 0)),
            pl.BlockSpec((1, GLA_VAL_WIDTH), lambda i: (0, 0)),
            pl.BlockSpec((None, None, GLA_HEADS, GLA_DK, GLA_DV), lambda i: (layer, i, 0, 0, 0)),
        ],
        out_specs=[
            pl.BlockSpec((None, 1, GLA_VAL_WIDTH), lambda i: (i, 0, 0)),
            pl.BlockSpec((None, GLA_HEADS, GLA_DK, GLA_DV), lambda i: (i, 0, 0, 0)),
        ],
        out_shape=[
            jax.ShapeDtypeStruct((bd, 1, GLA_VAL_WIDTH), F32),
            jax.ShapeDtypeStruct((bd, GLA_HEADS, GLA_DK, GLA_DV), F32),
        ],
        compiler_params=_params("parallel"),
        name="gla_step",
    )(zs3, zs3, zs3, zs3, za_s.reshape(bd, 1, LANES), w_a2p, b_a.reshape(1, GLA_KEY_WIDTH),
      gla_norm.reshape(1, GLA_VAL_WIDTH), state)
    return y.reshape(bd, GLA_VAL_WIDTH), s_new


def _xattn_kernel(q_ref, kv_ref, o_ref):
    scale = X_HEAD_DIM ** -0.5
    nt = (((1,), (1,)), ((), ()))
    for h in range(X_HEADS):
        hs = slice(h * X_HEAD_DIM, (h + 1) * X_HEAD_DIM)
        k = kv_ref[:, h * X_HEAD_DIM:(h + 1) * X_HEAD_DIM].astype(BF16)
        v = kv_ref[:, X_WIDTH + h * X_HEAD_DIM:X_WIDTH + (h + 1) * X_HEAD_DIM].astype(BF16)
        s = lax.dot_general(q_ref[:, hs], k, nt, preferred_element_type=F32) * scale
        e = jnp.exp(s - jnp.max(s, axis=-1, keepdims=True))
        l = jnp.sum(e, axis=-1, keepdims=True)
        o = jnp.dot(e.astype(BF16), v, preferred_element_type=F32) / l
        o_ref[:, hs] = o.astype(o_ref.dtype)


def _xattn_step_kernel(q_ref, kv_ref, o_ref):
    scale = X_HEAD_DIM ** -0.5
    nt = (((1,), (1,)), ((), ()))
    for h in range(X_HEADS):
        hs = slice(h * X_HEAD_DIM, (h + 1) * X_HEAD_DIM)
        k = kv_ref[:, 0, h, :].astype(BF16)
        v = kv_ref[:, 1, h, :].astype(BF16)
        s = lax.dot_general(q_ref[:, hs], k, nt, preferred_element_type=F32) * scale
        e = jnp.exp(s - jnp.max(s, axis=-1, keepdims=True))
        l = jnp.sum(e, axis=-1, keepdims=True)
        o = jnp.dot(e.astype(BF16), v, preferred_element_type=F32) / l
        o_ref[:, hs] = o.astype(o_ref.dtype)


def xattn_step(q3, mem_kv, layer):
    bd, rows, _ = q3.shape
    return pl.pallas_call(
        _xattn_step_kernel,
        grid=(bd,),
        in_specs=[
            pl.BlockSpec((None, rows, X_WIDTH), lambda i: (i, 0, 0)),
            pl.BlockSpec((None, None, MEM_LEN, 2, X_HEADS, X_HEAD_DIM), lambda i: (layer, i, 0, 0, 0, 0)),
        ],
        out_specs=pl.BlockSpec((None, rows, X_WIDTH), lambda i: (i, 0, 0)),
        out_shape=jax.ShapeDtypeStruct((bd, rows, X_WIDTH), BF16),
        compiler_params=_params("parallel"),
        name="xattn_step",
    )(q3, mem_kv)


def xattn(q3, mem_kv, *, tt):
    b, t, _ = q3.shape
    tt = min(tt, t)
    assert t % tt == 0
    return pl.pallas_call(
        _xattn_kernel,
        grid=(b, t // tt),
        in_specs=[
            pl.BlockSpec((None, tt, X_WIDTH), lambda i, j: (i, j, 0)),
            pl.BlockSpec((None, MEM_LEN, 2 * X_WIDTH), lambda i, j: (i, 0, 0)),
        ],
        out_specs=pl.BlockSpec((None, tt, X_WIDTH), lambda i, j: (i, j, 0)),
        out_shape=jax.ShapeDtypeStruct((b, t, X_WIDTH), BF16),
        compiler_params=_params("parallel", "arbitrary"),
        name="xattn",
    )(q3, mem_kv)


FFN_HALO = 16
FFN_TAIL = 8


def _ffn_up_prompt_kernel(x_ref, xh_ref, wg_ref, wv_ref, cwg_ref, cwv_ref, cbg_ref, cbv_ref,
                          act_ref, tg_ref, tv_ref, wbg_ref, wbv_ref, ug_ref, uv_ref, *, tm, seq_tiles):
    i = pl.program_id(1)

    @pl.when(i == 0)
    def _():
        wbg_ref[...] = wg_ref[...].astype(BF16)
        wbv_ref[...] = wv_ref[...].astype(BF16)

    keep_halo = (i % seq_tiles != 0).astype(F32)
    x, xh = x_ref[...], xh_ref[...]
    for w_ref, u_ref in ((wbg_ref, ug_ref), (wbv_ref, uv_ref)):
        w = w_ref[...]
        u_ref[0:FFN_HALO, :] = jnp.dot(xh, w, preferred_element_type=F32) * keep_halo
        u_ref[FFN_HALO:FFN_HALO + tm, :] = jnp.dot(x, w, preferred_element_type=F32)

    def conv(u_ref, cw_ref, cb_ref):
        c = cb_ref[...] + cw_ref[0:1, :] * u_ref[FFN_HALO - 2:FFN_HALO - 2 + tm, :]
        c = c + cw_ref[1:2, :] * u_ref[FFN_HALO - 1:FFN_HALO - 1 + tm, :]
        return c + cw_ref[2:3, :] * u_ref[FFN_HALO:FFN_HALO + tm, :]

    act_ref[...] = (_gelu_tanh(conv(ug_ref, cwg_ref, cbg_ref)) * conv(uv_ref, cwv_ref, cbv_ref)).astype(act_ref.dtype)
    tg_ref[...] = ug_ref[FFN_HALO + tm - FFN_TAIL:FFN_HALO + tm, :]
    tv_ref[...] = uv_ref[FFN_HALO + tm - FFN_TAIL:FFN_HALO + tm, :]


def ffn_up_prompt(xn, w_up, conv_w, conv_b3, layer, *, seq, tm, tn):
    m, k = xn.shape
    assert seq % tm == 0 and D_FF % tn == 0 and tm % FFN_HALO == 0
    nj = D_FF // tn
    seq_tiles = seq // tm
    halo_blocks = tm // FFN_HALO
    act, tail_g, tail_v = pl.pallas_call(
        functools.partial(_ffn_up_prompt_kernel, tm=tm, seq_tiles=seq_tiles),
        grid=(nj, m // tm),
        in_specs=[
            pl.BlockSpec((tm, k), lambda j, i: (i, 0)),
            pl.BlockSpec((FFN_HALO, k), lambda j, i: (jnp.maximum(i * halo_blocks - 1, 0), 0)),
            pl.BlockSpec((None, k, tn), lambda j, i: (layer, 0, j)),
            pl.BlockSpec((None, k, tn), lambda j, i: (layer, 0, j + nj)),
            pl.BlockSpec((None, 3, tn), lambda j, i: (layer, 0, j)),
            pl.BlockSpec((None, 3, tn), lambda j, i: (layer, 0, j + nj)),
            pl.BlockSpec((None, 1, tn), lambda j, i: (layer, 0, j)),
            pl.BlockSpec((None, 1, tn), lambda j, i: (layer, 0, j + nj)),
        ],
        out_specs=[
            pl.BlockSpec((tm, tn), lambda j, i: (i, j)),
            pl.BlockSpec((None, FFN_TAIL, tn), lambda j, i: (i, 0, j)),
            pl.BlockSpec((None, FFN_TAIL, tn), lambda j, i: (i, 0, j)),
        ],
        out_shape=[
            jax.ShapeDtypeStruct((m, D_FF), BF16),
            jax.ShapeDtypeStruct((m // tm, FFN_TAIL, D_FF), F32),
            jax.ShapeDtypeStruct((m // tm, FFN_TAIL, D_FF), F32),
        ],
        scratch_shapes=[
            pltpu.VMEM((k, tn), BF16),
            pltpu.VMEM((k, tn), BF16),
            pltpu.VMEM((tm + FFN_HALO, tn), F32),
            pltpu.VMEM((tm + FFN_HALO, tn), F32),
        ],
        compiler_params=_params("parallel", "arbitrary"),
        name="ffn_up_prompt",
    )(xn, xn, w_up, w_up, conv_w, conv_w, conv_b3, conv_b3)
    return act, jnp.concatenate([tail_g, tail_v], axis=-1)


def _ffn_up_step_kernel(x_ref, wg_ref, wv_ref, cwg_ref, cwv_ref, cbg_ref, cbv_ref, hg_ref, hv_ref,
                        act_ref, ng_ref, nv_ref):
    xn = x_ref[...]
    ug = jnp.dot(xn, wg_ref[...].astype(BF16), preferred_element_type=F32)
    uv = jnp.dot(xn, wv_ref[...].astype(BF16), preferred_element_type=F32)

    def conv(u, h_ref, cw_ref, cb_ref):
        return cb_ref[...] + cw_ref[0:1, :] * h_ref[0] + cw_ref[1:2, :] * h_ref[1] + cw_ref[2:3, :] * u

    act_ref[...] = (_gelu_tanh(conv(ug, hg_ref, cwg_ref, cbg_ref)) * conv(uv, hv_ref, cwv_ref, cbv_ref)).astype(act_ref.dtype)
    ng_ref[0] = hg_ref[1]
    ng_ref[1] = ug
    nv_ref[0] = hv_ref[1]
    nv_ref[1] = uv


def ffn_up_step(xn, w_up, conv_w, conv_b3, layer, hist_t, *, tn):
    bd, k = xn.shape
    nj = D_FF // tn
    act, new_g, new_v = pl.pallas_call(
        _ffn_up_step_kernel,
        grid=(nj,),
        in_specs=[
            pl.BlockSpec((bd, k), lambda j: (0, 0)),
            pl.BlockSpec((None, k, tn), lambda j: (layer, 0, j)),
            pl.BlockSpec((None, k, tn), lambda j: (layer, 0, j + nj)),
            pl.BlockSpec((None, 3, tn), lambda j: (layer, 0, j)),
            pl.BlockSpec((None, 3, tn), lambda j: (layer, 0, j + nj)),
            pl.BlockSpec((None, 1, tn), lambda j: (layer, 0, j)),
            pl.BlockSpec((None, 1, tn), lambda j: (layer, 0, j + nj)),
            pl.BlockSpec((2, bd, tn), lambda j: (0, 0, j)),
            pl.BlockSpec((2, bd, tn), lambda j: (0, 0, j + nj)),
        ],
        out_specs=[
            pl.BlockSpec((bd, tn), lambda j: (0, j)),
            pl.BlockSpec((2, bd, tn), lambda j: (0, 0, j)),
            pl.BlockSpec((2, bd, tn), lambda j: (0, 0, j)),
        ],
        out_shape=[
            jax.ShapeDtypeStruct((bd, D_FF), BF16),
            jax.ShapeDtypeStruct((2, bd, D_FF), F32),
            jax.ShapeDtypeStruct((2, bd, D_FF), F32),
        ],
        compiler_params=_params("parallel"),
        name="ffn_up_step",
    )(xn, w_up, w_up, conv_w, conv_w, conv_b3, conv_b3, hist_t, hist_t)
    return act, jnp.concatenate([new_g, new_v], axis=-1)


TM = 1024
TM_POST = 512
TK_POST = 2048
TN_MERGE = 512
TN_FFN = 512
XATTN_ROWS = 512
STEP_Q_ROWS = 16


def _layer_weights(l, pool_w, gla_w_a2, w_br_pool, w_br_dil, w_br_gla, w_mix_out, w_xo, w_down):
    return dict(
        w_a2p=jnp.pad(gla_w_a2[l], ((0, LANES - GLA_RANK), (0, 0))).astype(BF16),
        pool_w=pool_w[l].astype(BF16),
        w_br_pool=w_br_pool[l].astype(BF16),
        w_br_dil=w_br_dil[l].astype(BF16),
        w_br_gla=w_br_gla[l].astype(BF16),
        w_mix_out=w_mix_out[l].astype(BF16),
        w_xo=w_xo[l].astype(BF16),
        w_down=w_down[l].astype(BF16),
    )


def _in_projection(xn, w_in, l, tm):
    z1 = mm_cols(xn, w_in, l, src_tile=_z1_src_tile, nj=N_Z1 // W_TILE, tn=W_TILE, tm=tm)
    zkv, kv_cache = mm_cols(xn, w_in, l, src_tile=_zkv_src_tile, nj=N_ZKV // W_TILE, tn=W_TILE, tm=tm, with_cache=True)
    zg = mm_cols(xn, w_in, l, src_tile=lambda j: SRC_GA_COL // W_TILE + j, nj=N_ZG // W_TILE, tn=W_TILE, tm=tm,
                 shift=SRC_GATES_COL - SRC_GA_COL)
    za = mm_cols(xn, w_in, l, src_tile=lambda j: SRC_GA_COL // LANES + j, nj=1, tn=LANES, tm=tm)
    return z1, zkv, kv_cache, zg, za


def kernel(x_prompt, x_sample, state_pool, cache_dil1_kv, cache_dil2_kv, cache_dil3_kv, state_gla, cache_mem_kv, state_ffn_conv, mem_prompt, rel_bias, norm_mix_pre, norm_mix_post, w_in, pool_w, pool_scale, gla_w_a2, gla_b_a, gla_norm, w_br_pool, w_br_dil, w_br_gla, w_mix_out, norm_x_pre, norm_x_post, norm_mem, w_xq, w_xkv, w_xo, norm_ffn_pre, norm_ffn_post, w_up, conv_w, conv_b, w_down):
    b, seq, d = x_prompt.shape
    bd = x_sample.shape[0]
    depth = w_in.shape[0]
    m = b * seq
    assert x_sample.shape[1] == 1 and d == D_MODEL
    slot_bias = _slot_biases(rel_bias)
    band_bias = _band_bias(slot_bias)
    dil_caches = (cache_dil1_kv, cache_dil2_kv, cache_dil3_kv)

    xp = x_prompt.reshape(m, d)
    xs = x_sample.reshape(bd, d)
    pool_p, gla_p, mem_p, conv_p = [], [], [], []
    pool_s, gla_s, conv_s = [], [], []
    dil_p = [[] for _ in range(3)]
    dil_s = [[] for _ in range(3)]
    seq_tiles = seq // TM

    conv_b3 = conv_b.reshape(depth, 1, 2 * D_FF)
    mem_rows = mem_prompt.reshape(b * MEM_LEN, d)
    for l in range(depth):
        w = _layer_weights(l, pool_w, gla_w_a2, w_br_pool, w_br_dil, w_br_gla, w_mix_out, w_xo, w_down)

        z1, zkv, kv_cache, zg, za = _in_projection(rms_cast(xp, norm_mix_pre[l]), w_in, l, TM)
        z3 = z1.reshape(b, seq, N_Z1)
        y_pool = pool_prompt(z3, w["pool_w"], pool_scale[l]).reshape(m, POOL_WIDTH)
        y_dil = dil_prompt(z3, zkv.reshape(b, seq, N_ZKV), band_bias).reshape(m, DIL_WIDTH)
        y_gla, gla_new = gla_prompt(z3, za.reshape(b, seq, LANES), w["w_a2p"], gla_b_a[l], gla_norm[l])
        merged = branch_merge(y_pool, y_dil, y_gla.reshape(m, GLA_VAL_WIDTH), w["w_br_pool"], w["w_br_dil"], w["w_br_gla"],
                              zg, tm=TM, tn=TN_MERGE)
        xp = mm_post(merged, w["w_mix_out"], norm_mix_post[l], xp, tm=TM_POST, tk=TK_POST)
        mem_kv = mm_cols(rms_cast(mem_rows, norm_mem[l]), w_xkv, l, src_tile=lambda j: j, nj=2 * X_WIDTH // W_TILE,
                         tn=W_TILE, tm=TM)
        q = mm_cols(rms_cast(xp, norm_x_pre[l]), w_xq, l, src_tile=lambda j: j, nj=X_WIDTH // W_TILE, tn=W_TILE, tm=TM,
                    out_dtype=BF16)
        o = xattn(q.reshape(b, seq, X_WIDTH), mem_kv.reshape(b, MEM_LEN, 2 * X_WIDTH), tt=XATTN_ROWS)
        xp = mm_post(o.reshape(m, X_WIDTH), w["w_xo"], norm_x_post[l], xp, tm=TM_POST, tk=TK_POST)
        act, tails = ffn_up_prompt(rms_cast(xp, norm_ffn_pre[l]), w_up, conv_w, conv_b3, l, seq=seq, tm=TM, tn=TN_FFN)
        xp = mm_post(act, w["w_down"], norm_ffn_post[l], xp, tm=TM_POST, tk=TK_POST)

        pool_p.append(z3[:, seq - POOL_HIST:, Z1_POOL:Z1_POOL + POOL_WIDTH])
        for g, (win, _) in enumerate(DIL_CONFIGS):
            keep = min(win, seq)
            dil_p[g].append(kv_cache[g].reshape(b, seq, 2, DIL_HEADS, DIL_HEAD_DIM)[:, seq - keep:])
        gla_p.append(gla_new)
        mem_p.append(mem_kv.reshape(b, MEM_LEN, 2, X_HEADS, X_HEAD_DIM))
        conv_p.append(tails.reshape(b, seq_tiles, FFN_TAIL, 2 * D_FF)[:, seq_tiles - 1, FFN_TAIL - 2:, :])

        zs, _, kv_new, zgs, zas = _in_projection(rms_cast(xs, norm_mix_pre[l]), w_in, l, bd)
        y_pool_s, pool_new_t = pool_step(jnp.swapaxes(state_pool[l], 0, 1), zs, w["pool_w"], pool_scale[l])
        y_dil_s = dil_step(zs, kv_new, dil_caches, l, slot_bias).astype(BF16)
        y_gla_s, gla_new_s = gla_step(zs, zas, state_gla, l, w["w_a2p"], gla_b_a[l], gla_norm[l])
        merged_s = branch_merge(y_pool_s, y_dil_s, y_gla_s.astype(BF16), w["w_br_pool"], w["w_br_dil"], w["w_br_gla"],
                                zgs, tm=bd, tn=TN_MERGE)
        xs = mm_post(merged_s, w["w_mix_out"], norm_mix_post[l], xs, tm=bd, tk=TK_POST)
        q_s = mm_cols(rms_cast(xs, norm_x_pre[l]), w_xq, l, src_tile=lambda j: j, nj=X_WIDTH // W_TILE, tn=W_TILE, tm=bd,
                      out_dtype=BF16)
        q_s = jnp.pad(q_s[:, None, :], ((0, 0), (0, STEP_Q_ROWS - 1), (0, 0)))
        o_s = xattn_step(q_s, cache_mem_kv, l)[:, 0, :]
        xs = mm_post(o_s, w["w_xo"], norm_x_post[l], xs, tm=bd, tk=TK_POST)
        act_s, conv_new_t = ffn_up_step(rms_cast(xs, norm_ffn_pre[l]), w_up, conv_w, conv_b3, l,
                                        jnp.swapaxes(state_ffn_conv[l], 0, 1), tn=TN_FFN)
        xs = mm_post(act_s, w["w_down"], norm_ffn_post[l], xs, tm=bd, tk=TK_POST)

        pool_s.append(jnp.swapaxes(pool_new_t, 0, 1))
        for g in range(3):
            dil_s[g].append(kv_new[g].reshape(bd, 1, 2, DIL_HEADS, DIL_HEAD_DIM))
        gla_s.append(gla_new_s)
        conv_s.append(jnp.swapaxes(conv_new_t, 0, 1))

    return (xp.reshape(b, seq, d), xs.reshape(bd, 1, d),
            jnp.stack(pool_p), jnp.stack(dil_p[0]), jnp.stack(dil_p[1]), jnp.stack(dil_p[2]), jnp.stack(gla_p), jnp.stack(mem_p), jnp.stack(conv_p),
            jnp.stack(pool_s), jnp.stack(dil_s[0]), jnp.stack(dil_s[1]), jnp.stack(dil_s[2]), jnp.stack(gla_s), jnp.stack(conv_s))
```

```python
import functools
import math

import jax
import jax.numpy as jnp
from jax import lax
from jax.experimental import pallas as pl
from jax.experimental.pallas import tpu as pltpu

F32 = jnp.float32
BF16 = jnp.bfloat16
EPS = 1e-6
NEG_INF = -1e30

VMEM_LIMIT_BYTES = 56 * 1024 * 1024
LANES = 128

D_MODEL = 2048
POOL_WINDOWS = (2, 4, 8, 16)
POOL_GROUP = 256
POOL_WIDTH = 1024
POOL_HIST = 15
DIL_CONFIGS = ((128, 1), (512, 4), (2048, 16))
DIL_HEADS = 8
DIL_HEAD_DIM = 128
DIL_WIDTH = 1024
DIL_BLOCK = 128
GLA_HEADS = 4
GLA_DK = 256
GLA_DV = 512
GLA_KEY_WIDTH = 1024
GLA_VAL_WIDTH = 2048
GLA_RANK = 16
GLA_TAU = 16.0
GLA_CHUNK = 64
REL_BUCKETS = 32
REL_MAX_DIST = 2048
MEM_LEN = 256
X_HEADS = 4
X_HEAD_DIM = 256
X_WIDTH = 1024
D_FF = 5632
PAST_LEN = 8192

COL_POOL = 0
COL_DIL_Q = tuple(1024 + 3072 * g for g in range(3))
COL_DIL_K = tuple(2048 + 3072 * g for g in range(3))
COL_DIL_V = tuple(3072 + 3072 * g for g in range(3))
COL_GQ = 10240
COL_GK = 11264
COL_GV = 12288
COL_GR = 14336
N_MAIN = 16384
N_GATES = 3 * D_MODEL


def _params(*semantics):
    return pltpu.CompilerParams(dimension_semantics=semantics, vmem_limit_bytes=VMEM_LIMIT_BYTES)


def _sigmoid(x):
    return 1.0 / (1.0 + jnp.exp(-x))


def _log_sigmoid(x):
    return jnp.minimum(x, 0.0) - jnp.log(1.0 + jnp.exp(-jnp.abs(x)))


def _gelu_tanh(x):
    return x * (0.5 * (1.0 + jnp.tanh(math.sqrt(2.0 / math.pi) * (x + 0.044715 * (x * x * x)))))


def _rms_scale(y, g):
    return y * lax.rsqrt(jnp.mean(y * y, axis=-1, keepdims=True) + EPS) * g


def _norm_rows(x_ref, g_ref, xn_ref, rows, chunk, dst_offset=0):
    g = g_ref[...]

    def body(c, carry):
        r0 = pl.multiple_of(c * chunk, chunk)
        xn_ref[pl.ds(dst_offset + r0, chunk), :] = _rms_scale(x_ref[pl.ds(r0, chunk), :], g).astype(BF16)
        return carry

    lax.fori_loop(0, rows // chunk, body, 0)


def _row_blocks(rows, block):
    block = min(block, rows)
    assert rows % block == 0
    return [(s, block) for s in range(0, rows, block)]


def _norm_mm_kernel(x_ref, g_ref, w_ref, o_ref, xn_ref, *, tm, chunk):
    @pl.when(pl.program_id(1) == 0)
    def _():
        _norm_rows(x_ref, g_ref, xn_ref, tm, chunk)

    o_ref[...] = jnp.dot(xn_ref[...], w_ref[...], preferred_element_type=F32).astype(o_ref.dtype)


def norm_mm(x, g, w, *, tm, tn, out_dtype=F32):
    m, k = x.shape
    n = w.shape[1]
    tm = min(tm, m)
    tn = min(tn, n)
    assert m % tm == 0 and n % tn == 0
    chunk = min(256, tm)
    return pl.pallas_call(
        functools.partial(_norm_mm_kernel, tm=tm, chunk=chunk),
        grid=(m // tm, n // tn),
        in_specs=[
            pl.BlockSpec((tm, k), lambda i, j: (i, 0)),
            pl.BlockSpec((1, k), lambda i, j: (0, 0)),
            pl.BlockSpec((k, tn), lambda i, j: (0, j)),
        ],
        out_specs=pl.BlockSpec((tm, tn), lambda i, j: (i, j)),
        out_shape=jax.ShapeDtypeStruct((m, n), out_dtype),
        scratch_shapes=[pltpu.VMEM((tm, k), BF16)],
        compiler_params=_params("parallel", "arbitrary"),
        name="norm_mm",
    )(x, g.reshape(1, k), w)


def _mm_post_kernel(a_ref, w_ref, g_ref, res_ref, o_ref, acc_ref, *, nk):
    kk = pl.program_id(1)
    part = jnp.dot(a_ref[...], w_ref[...], preferred_element_type=F32)
    if nk == 1:
        o_ref[...] = res_ref[...] + _rms_scale(part, g_ref[...])
        return

    @pl.when(kk == 0)
    def _():
        acc_ref[...] = part

    @pl.when(jnp.logical_and(kk > 0, kk < nk - 1))
    def _():
        acc_ref[...] += part

    @pl.when(kk == nk - 1)
    def _():
        o_ref[...] = res_ref[...] + _rms_scale(acc_ref[...] + part, g_ref[...])


def mm_post(a, w, g, res, *, tm, tk):
    m, k = a.shape
    n = w.shape[1]
    tm = min(tm, m)
    tk = k if k <= tk else k // 4
    assert m % tm == 0 and k % tk == 0 and tk % LANES == 0
    return pl.pallas_call(
        functools.partial(_mm_post_kernel, nk=k // tk),
        grid=(m // tm, k // tk),
        in_specs=[
            pl.BlockSpec((tm, tk), lambda i, kk: (i, kk)),
            pl.BlockSpec((tk, n), lambda i, kk: (kk, 0)),
            pl.BlockSpec((1, n), lambda i, kk: (0, 0)),
            pl.BlockSpec((tm, n), lambda i, kk: (i, 0)),
        ],
        out_specs=pl.BlockSpec((tm, n), lambda i, kk: (i, 0)),
        out_shape=jax.ShapeDtypeStruct((m, n), F32),
        scratch_shapes=[pltpu.VMEM((tm, n), F32)],
        compiler_params=_params("parallel", "arbitrary"),
        name="mm_post",
    )(a, w, g.reshape(1, n), res)


def _branch_merge_kernel(yp_ref, yd_ref, yg_ref, wp_ref, wd_ref, wg_ref, g0_ref, g1_ref, g2_ref, o_ref, *, tm):
    wp, wd, wg = wp_ref[...], wd_ref[...], wg_ref[...]
    for r0, rows in _row_blocks(tm, MERGE_ROWS):
        sl = slice(r0, r0 + rows)
        bp = jnp.dot(yp_ref[sl, :], wp, preferred_element_type=F32)
        bd = jnp.dot(yd_ref[sl, :], wd, preferred_element_type=F32)
        bg = jnp.dot(yg_ref[sl, :], wg, preferred_element_type=F32)
        merged = _sigmoid(g0_ref[sl, :]) * bp + _sigmoid(g1_ref[sl, :]) * bd + _sigmoid(g2_ref[sl, :]) * bg
        o_ref[sl, :] = merged.astype(o_ref.dtype)


def branch_merge(y_pool, y_dil, y_gla, w_pool, w_dil, w_gla, z_gates, *, tm, tn):
    m = y_pool.shape[0]
    tm = min(tm, m)
    assert m % tm == 0 and D_MODEL % tn == 0
    gate_blk = [b * D_MODEL // tn for b in range(3)]

    def gate_spec(b):
        return pl.BlockSpec((tm, tn), lambda i, j: (i, gate_blk[b] + j))

    return pl.pallas_call(
        functools.partial(_branch_merge_kernel, tm=tm),
        grid=(m // tm, D_MODEL // tn),
        in_specs=[
            pl.BlockSpec((tm, POOL_WIDTH), lambda i, j: (i, 0)),
            pl.BlockSpec((tm, DIL_WIDTH), lambda i, j: (i, 0)),
            pl.BlockSpec((tm, GLA_VAL_WIDTH), lambda i, j: (i, 0)),
            pl.BlockSpec((POOL_WIDTH, tn), lambda i, j: (0, j)),
            pl.BlockSpec((DIL_WIDTH, tn), lambda i, j: (0, j)),
            pl.BlockSpec((GLA_VAL_WIDTH, tn), lambda i, j: (0, j)),
            gate_spec(0),
            gate_spec(1),
            gate_spec(2),
        ],
        out_specs=pl.BlockSpec((tm, tn), lambda i, j: (i, j)),
        out_shape=jax.ShapeDtypeStruct((m, D_MODEL), BF16),
        compiler_params=_params("parallel", "arbitrary"),
        name="branch_merge",
    )(y_pool, y_dil, y_gla, w_pool, w_dil, w_gla, z_gates, z_gates, z_gates)


def _pool_prompt_kernel(u_ref, w_ref, s_ref, o_ref, buf_a, buf_b, *, seq):
    pad = POOL_HIST + 1
    zeros = jnp.zeros((pad, POOL_GROUP), F32)
    t = lax.broadcasted_iota(jnp.int32, (seq, 1), 0)
    for g, win in enumerate(POOL_WINDOWS):
        cols = slice(g * POOL_GROUP, (g + 1) * POOL_GROUP)
        u = u_ref[:, cols]
        cur, nxt = buf_a, buf_b
        cur[0:pad, :] = zeros
        nxt[0:pad, :] = zeros
        cur[pad:pad + seq, :] = u
        k = 1
        while k < win:
            nxt[pad:pad + seq, :] = cur[pad:pad + seq, :] + cur[pad - k:pad - k + seq, :]
            cur, nxt = nxt, cur
            k *= 2
        cnt = jnp.minimum(win, t + 1).astype(F32)
        d = cur[pad:pad + seq, :] / cnt - u
        y = jnp.dot(d.astype(BF16), w_ref[g], preferred_element_type=F32) * s_ref[:, cols]
        o_ref[:, cols] = y.astype(o_ref.dtype)


def pool_prompt(z3, pool_w, pool_scale):
    b, seq, _ = z3.shape
    return pl.pallas_call(
        functools.partial(_pool_prompt_kernel, seq=seq),
        grid=(b,),
        in_specs=[
            pl.BlockSpec((None, seq, POOL_WIDTH), lambda i: (i, 0, COL_POOL // POOL_WIDTH)),
            pl.BlockSpec((len(POOL_WINDOWS), POOL_GROUP, POOL_GROUP), lambda i: (0, 0, 0)),
            pl.BlockSpec((1, POOL_WIDTH), lambda i: (0, 0)),
        ],
        out_specs=pl.BlockSpec((None, seq, POOL_WIDTH), lambda i: (i, 0, 0)),
        out_shape=jax.ShapeDtypeStruct((b, seq, POOL_WIDTH), BF16),
        scratch_shapes=[pltpu.VMEM((seq + POOL_HIST + 1, POOL_GROUP), F32)] * 2,
        compiler_params=_params("parallel"),
        name="pool_prompt",
    )(z3, pool_w, pool_scale.reshape(1, POOL_WIDTH))


def _pool_step_kernel(hist_ref, u_ref, w_ref, s_ref, y_ref, new_ref):
    u = u_ref[...]
    for r in range(POOL_HIST - 1):
        new_ref[r] = hist_ref[r + 1]
    new_ref[POOL_HIST - 1] = u
    for g, win in enumerate(POOL_WINDOWS):
        cols = slice(g * POOL_GROUP, (g + 1) * POOL_GROUP)
        ug = u[:, cols]
        acc = ug
        for r in range(POOL_HIST - (win - 1), POOL_HIST):
            acc = acc + hist_ref[r, :, cols]
        d = acc / float(win) - ug
        y = jnp.dot(d.astype(BF16), w_ref[g], preferred_element_type=F32) * s_ref[:, cols]
        y_ref[:, cols] = y.astype(y_ref.dtype)


def pool_step(hist_t, z_s, pool_w, pool_scale):
    bd = z_s.shape[0]
    return pl.pallas_call(
        _pool_step_kernel,
        grid=(1,),
        in_specs=[
            pl.BlockSpec((POOL_HIST, bd, POOL_WIDTH), lambda i: (0, 0, 0)),
            pl.BlockSpec((bd, POOL_WIDTH), lambda i: (0, COL_POOL // POOL_WIDTH)),
            pl.BlockSpec((len(POOL_WINDOWS), POOL_GROUP, POOL_GROUP), lambda i: (0, 0, 0)),
            pl.BlockSpec((1, POOL_WIDTH), lambda i: (0, 0)),
        ],
        out_specs=[
            pl.BlockSpec((bd, POOL_WIDTH), lambda i: (0, 0)),
            pl.BlockSpec((POOL_HIST, bd, POOL_WIDTH), lambda i: (0, 0, 0)),
        ],
        out_shape=[
            jax.ShapeDtypeStruct((bd, POOL_WIDTH), BF16),
            jax.ShapeDtypeStruct((POOL_HIST, bd, POOL_WIDTH), F32),
        ],
        compiler_params=_params("arbitrary"),
        name="pool_step",
    )(hist_t, z_s, pool_w, pool_scale.reshape(1, POOL_WIDTH))


def _rel_bucket(dist):
    max_exact = REL_BUCKETS // 2
    scaled = jnp.log(jnp.maximum(dist, 1).astype(F32) / max_exact) / math.log(REL_MAX_DIST / max_exact)
    large = jnp.minimum(max_exact + (scaled * (REL_BUCKETS - max_exact)).astype(jnp.int32), REL_BUCKETS - 1)
    return jnp.where(dist < max_exact, dist, large)


def _select_rows(table_t, index, n):
    onehot = (index.reshape(1, -1) == jnp.arange(n, dtype=jnp.int32)[:, None]).astype(F32)
    out = jnp.dot(table_t, onehot, precision=lax.Precision.HIGHEST, preferred_element_type=F32)
    return out.reshape((table_t.shape[0],) + index.shape)


def _slot_biases(rel_bias):
    out = []
    for g, (win, dil) in enumerate(DIL_CONFIGS):
        dist = jnp.arange(win // dil + 1, dtype=jnp.int32) * dil
        table_t = rel_bias[:, g * DIL_HEADS:(g + 1) * DIL_HEADS].T.astype(F32)
        out.append(_select_rows(table_t, _rel_bucket(dist), REL_BUCKETS))
    return out


def _band_bias(slot_bias):
    qi = jnp.arange(DIL_BLOCK, dtype=jnp.int32)[:, None] + DIL_BLOCK
    ki = jnp.arange(2 * DIL_BLOCK, dtype=jnp.int32)[None, :]
    rel = qi - ki
    out = []
    for g, (win, dil) in enumerate(DIL_CONFIGS):
        n_slots = win // dil
        ok = (rel >= 0) & (rel <= n_slots)
        b = _select_rows(slot_bias[g], jnp.clip(rel, 0, n_slots), n_slots + 1)
        out.append(jnp.where(ok[None], b, NEG_INF))
    return jnp.stack(out, axis=0)


def _dil_prompt_kernel(q0, k0, v0, q1, k1, v1, q2, k2, v2, bias_ref, o_ref, o_scr, lse_scr, *, seq):
    qs, ks, vs = (q0, q1, q2), (k0, k1, k2), (v0, v1, v2)
    scale = DIL_HEAD_DIM ** -0.5
    blk = DIL_BLOCK
    nt = (((1,), (1,)), ((), ()))

    def rows(ref, start, dil):
        if dil == 1:
            return ref[pl.ds(start, blk), :]
        return ref[pl.ds(start, blk, stride=dil), :]

    for g, (_, dil) in enumerate(DIL_CONFIGS):
        nb = seq // dil // blk
        for r in range(dil):
            for ub in range(nb):
                start = r + dil * ub * blk
                q = rows(qs[g], start, dil).astype(BF16)
                kc = rows(ks[g], start, dil).astype(BF16)
                vc = rows(vs[g], start, dil).astype(BF16)
                if ub == 0:
                    kk, vv, bias = kc, vc, bias_ref[g, :, blk:]
                else:
                    prev = start - dil * blk
                    kk = jnp.concatenate([rows(ks[g], prev, dil).astype(BF16), kc], axis=0)
                    vv = jnp.concatenate([rows(vs[g], prev, dil).astype(BF16), vc], axis=0)
                    bias = bias_ref[g]
                s = lax.dot_general(q, kk, nt, preferred_element_type=F32) * scale + bias
                m = jnp.max(s, axis=-1, keepdims=True)
                e = jnp.exp(s - m)
                l = jnp.sum(e, axis=-1, keepdims=True)
                o = jnp.dot(e.astype(BF16), vv, preferred_element_type=F32) / l
                lse = jnp.broadcast_to(m + jnp.log(l), (blk, LANES))
                if dil == 1:
                    o_scr[g, pl.ds(start, blk), :] = o
                    lse_scr[g, pl.ds(start, blk), :] = lse
                else:
                    o_scr[g, pl.ds(start, blk, stride=dil), :] = o
                    lse_scr[g, pl.ds(start, blk, stride=dil), :] = lse

    chunk = 256

    def combine(c, carry):
        sl = pl.ds(pl.multiple_of(c * chunk, chunk), chunk)
        l0, l1, l2 = lse_scr[0, sl, :], lse_scr[1, sl, :], lse_scr[2, sl, :]
        mx = jnp.maximum(jnp.maximum(l0, l1), l2)
        w0, w1, w2 = jnp.exp(l0 - mx), jnp.exp(l1 - mx), jnp.exp(l2 - mx)
        y = (w0 * o_scr[0, sl, :] + w1 * o_scr[1, sl, :] + w2 * o_scr[2, sl, :]) / (w0 + w1 + w2)
        o_ref[sl, :] = y.astype(o_ref.dtype)
        return carry

    lax.fori_loop(0, seq // chunk, combine, 0)


def dil_prompt(z3, band_bias):
    b, seq, _ = z3.shape
    assert seq % (DIL_BLOCK * 16) == 0

    def col_spec(col):
        blk0 = col // DIL_HEAD_DIM
        return pl.BlockSpec((None, seq, DIL_HEAD_DIM), lambda i, h: (i, 0, blk0 + h))

    in_specs = []
    for g in range(3):
        in_specs += [col_spec(COL_DIL_Q[g]), col_spec(COL_DIL_K[g]), col_spec(COL_DIL_V[g])]
    in_specs.append(pl.BlockSpec((3, None, DIL_BLOCK, 2 * DIL_BLOCK), lambda i, h: (0, h, 0, 0)))
    return pl.pallas_call(
        functools.partial(_dil_prompt_kernel, seq=seq),
        grid=(b, DIL_HEADS),
        in_specs=in_specs,
        out_specs=pl.BlockSpec((None, seq, DIL_HEAD_DIM), lambda i, h: (i, 0, h)),
        out_shape=jax.ShapeDtypeStruct((b, seq, DIL_WIDTH), BF16),
        scratch_shapes=[pltpu.VMEM((3, seq, DIL_HEAD_DIM), F32), pltpu.VMEM((3, seq, LANES), F32)],
        compiler_params=_params("parallel", "arbitrary"),
        name="dil_prompt",
    )(*([z3] * 9), band_bias)


def _dil_step_kernel(qkv_ref, c0, c1, c2, bias_ref, bias0_ref, o_ref):
    caches = (c0, c1, c2)
    scale = DIL_HEAD_DIM ** -0.5
    outs, lses = [], []
    for g in range(3):
        q, kn, vn = qkv_ref[3 * g], qkv_ref[3 * g + 1], qkv_ref[3 * g + 2]
        kc = caches[g][:, 0]
        vc = caches[g][:, 1]
        s = jnp.sum(kc * q[None], axis=-1, keepdims=True) * scale + bias_ref[g]
        s_new = jnp.sum(q * kn, axis=-1, keepdims=True) * scale + bias0_ref[g]
        m = jnp.maximum(jnp.max(s, axis=0), s_new)
        p = jnp.exp(s - m[None])
        p_new = jnp.exp(s_new - m)
        l = jnp.sum(p, axis=0) + p_new
        outs.append((jnp.sum(p * vc, axis=0) + p_new * vn) / l)
        lses.append(m + jnp.log(l))
    mx = jnp.maximum(jnp.maximum(lses[0], lses[1]), lses[2])
    w0, w1, w2 = jnp.exp(lses[0] - mx), jnp.exp(lses[1] - mx), jnp.exp(lses[2] - mx)
    o_ref[...] = (w0 * outs[0] + w1 * outs[1] + w2 * outs[2]) / (w0 + w1 + w2)


def dil_step(z_s, caches, layer, slot_bias):
    bd = z_s.shape[0]
    n_slots = DIL_CONFIGS[0][0] // DIL_CONFIGS[0][1]
    qkv = z_s[:, COL_DIL_Q[0]:COL_DIL_V[2] + DIL_WIDTH].reshape(bd, 9, DIL_HEADS, DIL_HEAD_DIM)
    in_specs = [pl.BlockSpec((None, 9, DIL_HEADS, DIL_HEAD_DIM), lambda i: (i, 0, 0, 0))]
    cache_views = []
    for g, (win, dil) in enumerate(DIL_CONFIGS):
        depth = caches[g].shape[0]
        assert caches[g].shape[2] == win and win // dil == n_slots
        cache_views.append(caches[g].reshape(depth, bd, n_slots, dil, 2, DIL_HEADS, DIL_HEAD_DIM))
        in_specs.append(pl.BlockSpec((None, None, n_slots, None, 2, DIL_HEADS, DIL_HEAD_DIM),
                                     lambda i: (layer, i, 0, 0, 0, 0, 0)))
    bias_rows = jnp.stack([sb[:, n_slots:0:-1].T for sb in slot_bias], axis=0)
    bias_rows = jnp.broadcast_to(bias_rows[..., None], (3, n_slots, DIL_HEADS, LANES))
    bias_new = jnp.broadcast_to(jnp.stack([sb[:, 0] for sb in slot_bias], axis=0)[..., None], (3, DIL_HEADS, LANES))
    in_specs.append(pl.BlockSpec((3, n_slots, DIL_HEADS, LANES), lambda i: (0, 0, 0, 0)))
    in_specs.append(pl.BlockSpec((3, DIL_HEADS, LANES), lambda i: (0, 0, 0)))
    out = pl.pallas_call(
        _dil_step_kernel,
        grid=(bd,),
        in_specs=in_specs,
        out_specs=pl.BlockSpec((None, DIL_HEADS, DIL_HEAD_DIM), lambda i: (i, 0, 0)),
        out_shape=jax.ShapeDtypeStruct((bd, DIL_HEADS, DIL_HEAD_DIM), F32),
        compiler_params=_params("parallel"),
        name="dil_step",
    )(qkv, *cache_views, bias_rows, bias_new)
    return out.reshape(bd, DIL_WIDTH)


GLA_HEADS_PER_STEP = 2


def _gla_prompt_kernel(q_ref, k_ref, v_ref, r_ref, ga_ref, wa_ref, ba_ref, gn_ref, y_ref, s_out_ref, st_ref, o_scr, *, tq):
    t = pl.program_id(2)
    ck = GLA_CHUNK

    @pl.when(t == 0)
    def _():
        st_ref[...] = jnp.zeros_like(st_ref)

    la = jnp.dot(ga_ref[...].astype(BF16), wa_ref[...], preferred_element_type=F32) + ba_ref[...]
    log_a = _log_sigmoid(la) / GLA_TAU
    row = lax.broadcasted_iota(jnp.int32, (ck, ck), 0)
    col = lax.broadcasted_iota(jnp.int32, (ck, ck), 1)
    tril = row >= col
    tril_f = tril.astype(F32)
    nt = (((1,), (1,)), ((), ()))
    tn = (((0,), (0,)), ((), ()))
    for c in range(tq // ck):
        sl = slice(c * ck, (c + 1) * ck)
        for hh in range(GLA_HEADS_PER_STEP):
            ks = slice(hh * GLA_DK, (hh + 1) * GLA_DK)
            vs = slice(hh * GLA_DV, (hh + 1) * GLA_DV)
            b = jnp.dot(tril_f, log_a[sl, ks], precision=lax.Precision.HIGHEST, preferred_element_type=F32)
            b_last = b[ck - 1:ck, :]
            q = q_ref[sl, ks] * (GLA_DK ** -0.5)
            k = k_ref[sl, ks]
            v = v_ref[sl, vs].astype(BF16)
            q_t = (q * jnp.exp(b)).astype(BF16)
            k_t = (k * jnp.exp(-b)).astype(BF16)
            k_h = (k * jnp.exp(b_last - b)).astype(BF16)
            decay = jnp.exp(b_last)
            a = lax.dot_general(q_t, k_t, nt, preferred_element_type=F32)
            a = jnp.where(tril, a, 0.0)
            st = st_ref[hh]
            o = jnp.dot(a.astype(BF16), v, preferred_element_type=F32)
            o = o + lax.dot_general(q_t, st.astype(BF16), nt, preferred_element_type=F32)
            o_scr[sl, vs] = o
            st_ref[hh] = decay * st + lax.dot_general(v, k_h, tn, preferred_element_type=F32)

    for hh in range(GLA_HEADS_PER_STEP):
        vs = slice(hh * GLA_DV, (hh + 1) * GLA_DV)
        o = _rms_scale(o_scr[:, vs], gn_ref[:, vs])
        r = r_ref[:, vs]
        y_ref[:, vs] = (o * (r * _sigmoid(r))).astype(y_ref.dtype)

    @pl.when(t == pl.num_programs(2) - 1)
    def _():
        for hh in range(GLA_HEADS_PER_STEP):
            s_out_ref[hh] = st_ref[hh].T


def gla_prompt(z3, za3, w_a2p, b_a, gla_norm, *, tq=512):
    b, seq, _ = z3.shape
    hp = GLA_HEADS_PER_STEP
    kw, vw = hp * GLA_DK, hp * GLA_DV
    assert seq % tq == 0 and tq % GLA_CHUNK == 0 and GLA_HEADS % hp == 0
    return pl.pallas_call(
        functools.partial(_gla_prompt_kernel, tq=tq),
        grid=(b, GLA_HEADS // hp, seq // tq),
        in_specs=[
            pl.BlockSpec((None, tq, kw), lambda i, h, t: (i, t, COL_GQ // kw + h)),
            pl.BlockSpec((None, tq, kw), lambda i, h, t: (i, t, COL_GK // kw + h)),
            pl.BlockSpec((None, tq, vw), lambda i, h, t: (i, t, COL_GV // vw + h)),
            pl.BlockSpec((None, tq, vw), lambda i, h, t: (i, t, COL_GR // vw + h)),
            pl.BlockSpec((None, tq, LANES), lambda i, h, t: (i, t, 0)),
            pl.BlockSpec((LANES, kw), lambda i, h, t: (0, h)),
            pl.BlockSpec((1, kw), lambda i, h, t: (0, h)),
            pl.BlockSpec((1, vw), lambda i, h, t: (0, h)),
        ],
        out_specs=[
            pl.BlockSpec((None, tq, vw), lambda i, h, t: (i, t, h)),
            pl.BlockSpec((None, hp, GLA_DK, GLA_DV), lambda i, h, t: (i, h, 0, 0)),
        ],
        out_shape=[
            jax.ShapeDtypeStruct((b, seq, GLA_VAL_WIDTH), BF16),
            jax.ShapeDtypeStruct((b, GLA_HEADS, GLA_DK, GLA_DV), F32),
        ],
        scratch_shapes=[pltpu.VMEM((hp, GLA_DV, GLA_DK), F32), pltpu.VMEM((tq, vw), F32)],
        compiler_params=_params("parallel", "parallel", "arbitrary"),
        name="gla_prompt",
    )(z3, z3, z3, z3, za3, w_a2p, b_a.reshape(1, GLA_KEY_WIDTH), gla_norm.reshape(1, GLA_VAL_WIDTH))


def _gla_step_kernel(q_ref, k_ref, v_ref, r_ref, ga_ref, wa_ref, ba_ref, gn_ref, s_ref, y_ref, s_out_ref):
    ga8 = jnp.broadcast_to(ga_ref[...], (8, LANES)).astype(BF16)
    la = jnp.dot(ga8, wa_ref[...], preferred_element_type=F32)[0:1, :] + ba_ref[...]
    ea = jnp.exp(_log_sigmoid(la) / GLA_TAU)
    q = q_ref[...] * (GLA_DK ** -0.5)
    k = k_ref[...]
    eye = lax.broadcasted_iota(jnp.int32, (GLA_DK, GLA_DK), 0) == lax.broadcasted_iota(jnp.int32, (GLA_DK, GLA_DK), 1)

    def column(row):
        return jnp.sum(jnp.where(eye, jnp.broadcast_to(row, (GLA_DK, GLA_DK)), 0.0), axis=-1, keepdims=True)

    for h in range(GLA_HEADS):
        ks = slice(h * GLA_DK, (h + 1) * GLA_DK)
        vs = slice(h * GLA_DV, (h + 1) * GLA_DV)
        s_new = column(ea[:, ks]) * s_ref[h] + column(k[:, ks]) * v_ref[:, vs]
        s_out_ref[h] = s_new
        o = jnp.sum(column(q[:, ks]) * s_new, axis=0, keepdims=True)
        o = _rms_scale(o, gn_ref[:, vs])
        r = r_ref[:, vs]
        y_ref[:, vs] = o * (r * _sigmoid(r))


def gla_step(z_s, za_s, state, layer, w_a2p, b_a, gla_norm):
    bd = z_s.shape[0]
    zs3 = z_s.reshape(bd, 1, N_MAIN)
    y, s_new = pl.pallas_call(
        _gla_step_kernel,
        grid=(bd,),
        in_specs=[
            pl.BlockSpec((None, 1, GLA_KEY_WIDTH), lambda i: (i, 0, COL_GQ // GLA_KEY_WIDTH)),
            pl.BlockSpec((None, 1, GLA_KEY_WIDTH), lambda i: (i, 0, COL_GK // GLA_KEY_WIDTH)),
            pl.BlockSpec((None, 1, GLA_VAL_WIDTH), lambda i: (i, 0, COL_GV // GLA_VAL_WIDTH)),
            pl.BlockSpec((None, 1, GLA_VAL_WIDTH), lambda i: (i, 0, COL_GR // GLA_VAL_WIDTH)),
            pl.BlockSpec((None, 1, LANES), lambda i: (i, 0, 0)),
            pl.BlockSpec((LANES, GLA_KEY_WIDTH), lambda i: (0, 0)),
            pl.BlockSpec((1, GLA_KEY_WIDTH), lambda i: (0, 0)),
            pl.BlockSpec((1, GLA_VAL_WIDTH), lambda i: (0, 0)),
            pl.BlockSpec((None, None, GLA_HEADS, GLA_DK, GLA_DV), lambda i: (layer, i, 0, 0, 0)),
        ],
        out_specs=[
            pl.BlockSpec((None, 1, GLA_VAL_WIDTH), lambda i: (i, 0, 0)),
            pl.BlockSpec((None, GLA_HEADS, GLA_DK, GLA_DV), lambda i: (i, 0, 0, 0)),
        ],
        out_shape=[
            jax.ShapeDtypeStruct((bd, 1, GLA_VAL_WIDTH), F32),
            jax.ShapeDtypeStruct((bd, GLA_HEADS, GLA_DK, GLA_DV), F32),
        ],
        compiler_params=_params("parallel"),
        name="gla_step",
    )(zs3, zs3, zs3, zs3, za_s.reshape(bd, 1, LANES), w_a2p, b_a.reshape(1, GLA_KEY_WIDTH),
      gla_norm.reshape(1, GLA_VAL_WIDTH), state)
    return y.reshape(bd, GLA_VAL_WIDTH), s_new


def _xattn_kernel(q_ref, kv_ref, o_ref):
    scale = X_HEAD_DIM ** -0.5
    nt = (((1,), (1,)), ((), ()))
    for h in range(X_HEADS):
        hs = slice(h * X_HEAD_DIM, (h + 1) * X_HEAD_DIM)
        k = kv_ref[:, h * X_HEAD_DIM:(h + 1) * X_HEAD_DIM].astype(BF16)
        v = kv_ref[:, X_WIDTH + h * X_HEAD_DIM:X_WIDTH + (h + 1) * X_HEAD_DIM].astype(BF16)
        s = lax.dot_general(q_ref[:, hs], k, nt, preferred_element_type=F32) * scale
        e = jnp.exp(s - jnp.max(s, axis=-1, keepdims=True))
        l = jnp.sum(e, axis=-1, keepdims=True)
        o = jnp.dot(e.astype(BF16), v, preferred_element_type=F32) / l
        o_ref[:, hs] = o.astype(o_ref.dtype)


def xattn(q3, mem_kv, *, tt):
    b, t, _ = q3.shape
    tt = min(tt, t)
    assert t % tt == 0
    return pl.pallas_call(
        _xattn_kernel,
        grid=(b, t // tt),
        in_specs=[
            pl.BlockSpec((None, tt, X_WIDTH), lambda i, j: (i, j, 0)),
            pl.BlockSpec((None, MEM_LEN, 2 * X_WIDTH), lambda i, j: (i, 0, 0)),
        ],
        out_specs=pl.BlockSpec((None, tt, X_WIDTH), lambda i, j: (i, j, 0)),
        out_shape=jax.ShapeDtypeStruct((b, t, X_WIDTH), BF16),
        compiler_params=_params("parallel", "arbitrary"),
        name="xattn",
    )(q3, mem_kv)


def _xattn_step_kernel(q_ref, kv_ref, o_ref):
    q = q_ref[...]
    k = kv_ref[:, 0]
    v = kv_ref[:, 1]
    s = jnp.sum(k * q[None], axis=-1, keepdims=True) * (X_HEAD_DIM ** -0.5)
    p = jnp.exp(s - jnp.max(s, axis=0)[None])
    o_ref[...] = jnp.sum(p * v, axis=0) / jnp.sum(p, axis=0)


def xattn_step(q, mem_kv, layer):
    bd = q.shape[0]
    out = pl.pallas_call(
        _xattn_step_kernel,
        grid=(bd,),
        in_specs=[
            pl.BlockSpec((None, X_HEADS, X_HEAD_DIM), lambda i: (i, 0, 0)),
            pl.BlockSpec((None, None, MEM_LEN, 2, X_HEADS, X_HEAD_DIM), lambda i: (layer, i, 0, 0, 0, 0)),
        ],
        out_specs=pl.BlockSpec((None, X_HEADS, X_HEAD_DIM), lambda i: (i, 0, 0)),
        out_shape=jax.ShapeDtypeStruct((bd, X_HEADS, X_HEAD_DIM), F32),
        compiler_params=_params("parallel"),
        name="xattn_step",
    )(q.reshape(bd, X_HEADS, X_HEAD_DIM), mem_kv)
    return out.reshape(bd, X_WIDTH)


FFN_HALO = 16
FFN_TAIL = 8
FFN_ROWS = 256
MERGE_ROWS = 256


def _ffn_up_prompt_kernel(x_ref, xh_ref, g_ref, wg_ref, wv_ref, cwg_ref, cwv_ref, cbg_ref, cbv_ref,
                          act_ref, tg_ref, tv_ref, xn_ref, ug_ref, uv_ref, *, tm, seq_tiles, chunk):
    i = pl.program_id(0)

    @pl.when(pl.program_id(1) == 0)
    def _():
        _norm_rows(x_ref, g_ref, xn_ref, tm, chunk, dst_offset=FFN_HALO)
        halo = _rms_scale(xh_ref[...], g_ref[...])
        halo = jnp.where(i % seq_tiles == 0, 0.0, halo)
        xn_ref[0:FFN_HALO, :] = halo.astype(BF16)

    wg, wv = wg_ref[...], wv_ref[...]
    xh = xn_ref[0:FFN_HALO, :]
    ug_ref[0:FFN_HALO, :] = jnp.dot(xh, wg, preferred_element_type=F32)
    uv_ref[0:FFN_HALO, :] = jnp.dot(xh, wv, preferred_element_type=F32)

    def conv(u_ref, cw_ref, cb_ref, r0, rows):
        lo = FFN_HALO + r0
        c = cb_ref[...] + cw_ref[0:1, :] * u_ref[lo - 2:lo - 2 + rows, :]
        c = c + cw_ref[1:2, :] * u_ref[lo - 1:lo - 1 + rows, :]
        return c + cw_ref[2:3, :] * u_ref[lo:lo + rows, :]

    for r0, rows in _row_blocks(tm, FFN_ROWS):
        lo = FFN_HALO + r0
        xs = xn_ref[lo:lo + rows, :]
        ug_ref[lo:lo + rows, :] = jnp.dot(xs, wg, preferred_element_type=F32)
        uv_ref[lo:lo + rows, :] = jnp.dot(xs, wv, preferred_element_type=F32)
        gate = conv(ug_ref, cwg_ref, cbg_ref, r0, rows)
        val = conv(uv_ref, cwv_ref, cbv_ref, r0, rows)
        act_ref[r0:r0 + rows, :] = (_gelu_tanh(gate) * val).astype(act_ref.dtype)
    tg_ref[...] = ug_ref[FFN_HALO + tm - FFN_TAIL:FFN_HALO + tm, :]
    tv_ref[...] = uv_ref[FFN_HALO + tm - FFN_TAIL:FFN_HALO + tm, :]


def ffn_up_prompt(x, g, w_up, conv_w, conv_b, *, seq, tm, tn):
    m, k = x.shape
    assert seq % tm == 0 and D_FF % tn == 0 and tm % FFN_HALO == 0
    nj = D_FF // tn
    seq_tiles = seq // tm
    chunk = min(256, tm)
    halo_blocks = tm // FFN_HALO
    act, tail_g, tail_v = pl.pallas_call(
        functools.partial(_ffn_up_prompt_kernel, tm=tm, seq_tiles=seq_tiles, chunk=chunk),
        grid=(m // tm, nj),
        in_specs=[
            pl.BlockSpec((tm, k), lambda i, j: (i, 0)),
            pl.BlockSpec((FFN_HALO, k), lambda i, j: (jnp.maximum(i * halo_blocks - 1, 0), 0)),
            pl.BlockSpec((1, k), lambda i, j: (0, 0)),
            pl.BlockSpec((k, tn), lambda i, j: (0, j)),
            pl.BlockSpec((k, tn), lambda i, j: (0, j + nj)),
            pl.BlockSpec((3, tn), lambda i, j: (0, j)),
            pl.BlockSpec((3, tn), lambda i, j: (0, j + nj)),
            pl.BlockSpec((1, tn), lambda i, j: (0, j)),
            pl.BlockSpec((1, tn), lambda i, j: (0, j + nj)),
        ],
        out_specs=[
            pl.BlockSpec((tm, tn), lambda i, j: (i, j)),
            pl.BlockSpec((None, FFN_TAIL, tn), lambda i, j: (i, 0, j)),
            pl.BlockSpec((None, FFN_TAIL, tn), lambda i, j: (i, 0, j)),
        ],
        out_shape=[
            jax.ShapeDtypeStruct((m, D_FF), BF16),
            jax.ShapeDtypeStruct((m // tm, FFN_TAIL, D_FF), F32),
            jax.ShapeDtypeStruct((m // tm, FFN_TAIL, D_FF), F32),
        ],
        scratch_shapes=[
            pltpu.VMEM((tm + FFN_HALO, k), BF16),
            pltpu.VMEM((tm + FFN_HALO, tn), F32),
            pltpu.VMEM((tm + FFN_HALO, tn), F32),
        ],
        compiler_params=_params("parallel", "arbitrary"),
        name="ffn_up_prompt",
    )(x, x, g.reshape(1, k), w_up, w_up, conv_w, conv_w, conv_b.reshape(1, 2 * D_FF), conv_b.reshape(1, 2 * D_FF))
    return act, jnp.concatenate([tail_g, tail_v], axis=-1)


def _ffn_up_step_kernel(x_ref, g_ref, wg_ref, wv_ref, cwg_ref, cwv_ref, cbg_ref, cbv_ref, hg_ref, hv_ref,
                        act_ref, ng_ref, nv_ref, xn_ref, *, bd):
    @pl.when(pl.program_id(0) == 0)
    def _():
        _norm_rows(x_ref, g_ref, xn_ref, bd, bd)

    xn = xn_ref[...]
    ug = jnp.dot(xn, wg_ref[...], preferred_element_type=F32)
    uv = jnp.dot(xn, wv_ref[...], preferred_element_type=F32)

    def conv(u, h_ref, cw_ref, cb_ref):
        return cb_ref[...] + cw_ref[0:1, :] * h_ref[0] + cw_ref[1:2, :] * h_ref[1] + cw_ref[2:3, :] * u

    act_ref[...] = (_gelu_tanh(conv(ug, hg_ref, cwg_ref, cbg_ref)) * conv(uv, hv_ref, cwv_ref, cbv_ref)).astype(act_ref.dtype)
    ng_ref[0] = hg_ref[1]
    ng_ref[1] = ug
    nv_ref[0] = hv_ref[1]
    nv_ref[1] = uv


def ffn_up_step(x, g, w_up, conv_w, conv_b, hist_t, *, tn):
    bd, k = x.shape
    nj = D_FF // tn
    act, new_g, new_v = pl.pallas_call(
        functools.partial(_ffn_up_step_kernel, bd=bd),
        grid=(nj,),
        in_specs=[
            pl.BlockSpec((bd, k), lambda j: (0, 0)),
            pl.BlockSpec((1, k), lambda j: (0, 0)),
            pl.BlockSpec((k, tn), lambda j: (0, j)),
            pl.BlockSpec((k, tn), lambda j: (0, j + nj)),
            pl.BlockSpec((3, tn), lambda j: (0, j)),
            pl.BlockSpec((3, tn), lambda j: (0, j + nj)),
            pl.BlockSpec((1, tn), lambda j: (0, j)),
            pl.BlockSpec((1, tn), lambda j: (0, j + nj)),
            pl.BlockSpec((2, bd, tn), lambda j: (0, 0, j)),
            pl.BlockSpec((2, bd, tn), lambda j: (0, 0, j + nj)),
        ],
        out_specs=[
            pl.BlockSpec((bd, tn), lambda j: (0, j)),
            pl.BlockSpec((2, bd, tn), lambda j: (0, 0, j)),
            pl.BlockSpec((2, bd, tn), lambda j: (0, 0, j)),
        ],
        out_shape=[
            jax.ShapeDtypeStruct((bd, D_FF), BF16),
            jax.ShapeDtypeStruct((2, bd, D_FF), F32),
            jax.ShapeDtypeStruct((2, bd, D_FF), F32),
        ],
        scratch_shapes=[pltpu.VMEM((bd, k), BF16)],
        compiler_params=_params("arbitrary"),
        name="ffn_up_step",
    )(x, g.reshape(1, k), w_up, w_up, conv_w, conv_w, conv_b.reshape(1, 2 * D_FF), conv_b.reshape(1, 2 * D_FF),
      hist_t, hist_t)
    return act, jnp.concatenate([new_g, new_v], axis=-1)


TM = 1024
TN = 1024
TM_POST = 512
TK_POST = 2048
TN_MERGE = 512
TN_FFN = 512
XATTN_ROWS = 512


def _layer_weights(l, w_in, pool_w, gla_w_a2, w_br_pool, w_br_dil, w_br_gla, w_mix_out, w_xq, w_xkv, w_xo, w_up, w_down):
    w_in_l = w_in[l]
    ga_end = N_MAIN + GLA_RANK
    return dict(
        w_main=w_in_l[:, :N_MAIN].astype(BF16),
        w_gates=w_in_l[:, ga_end:].astype(BF16),
        w_ga=jnp.pad(w_in_l[:, N_MAIN:ga_end], ((0, 0), (0, LANES - GLA_RANK))).astype(BF16),
        w_a2p=jnp.pad(gla_w_a2[l], ((0, LANES - GLA_RANK), (0, 0))).astype(BF16),
        pool_w=pool_w[l].astype(BF16),
        w_br_pool=w_br_pool[l].astype(BF16),
        w_br_dil=w_br_dil[l].astype(BF16),
        w_br_gla=w_br_gla[l].astype(BF16),
        w_mix_out=w_mix_out[l].astype(BF16),
        w_xq=w_xq[l].astype(BF16),
        w_xkv=w_xkv[l].astype(BF16),
        w_xo=w_xo[l].astype(BF16),
        w_up=w_up[l].astype(BF16),
        w_down=w_down[l].astype(BF16),
    )


def kernel(x_prompt, x_sample, state_pool, cache_dil1_kv, cache_dil2_kv, cache_dil3_kv, state_gla, cache_mem_kv, state_ffn_conv, mem_prompt, rel_bias, norm_mix_pre, norm_mix_post, w_in, pool_w, pool_scale, gla_w_a2, gla_b_a, gla_norm, w_br_pool, w_br_dil, w_br_gla, w_mix_out, norm_x_pre, norm_x_post, norm_mem, w_xq, w_xkv, w_xo, norm_ffn_pre, norm_ffn_post, w_up, conv_w, conv_b, w_down):
    b, seq, d = x_prompt.shape
    bd = x_sample.shape[0]
    depth = w_in.shape[0]
    m = b * seq
    assert x_sample.shape[1] == 1 and d == D_MODEL and w_in.shape[2] == N_MAIN + GLA_RANK + N_GATES
    slot_bias = _slot_biases(rel_bias)
    band_bias = _band_bias(slot_bias)
    dil_caches = (cache_dil1_kv, cache_dil2_kv, cache_dil3_kv)

    xp = x_prompt.reshape(m, d)
    xs = x_sample.reshape(bd, d)
    pool_p, gla_p, mem_p, conv_p = [], [], [], []
    pool_s, gla_s, conv_s = [], [], []
    dil_p = [[] for _ in range(3)]
    dil_s = [[] for _ in range(3)]
    seq_tiles = seq // TM

    for l in range(depth):
        w = _layer_weights(l, w_in, pool_w, gla_w_a2, w_br_pool, w_br_dil, w_br_gla, w_mix_out, w_xq, w_xkv, w_xo, w_up, w_down)

        zm = norm_mm(xp, norm_mix_pre[l], w["w_main"], tm=TM, tn=TN)
        zg = norm_mm(xp, norm_mix_pre[l], w["w_gates"], tm=TM, tn=TN)
        za = norm_mm(xp, norm_mix_pre[l], w["w_ga"], tm=TM, tn=LANES)
        z3 = zm.reshape(b, seq, N_MAIN)
        y_pool = pool_prompt(z3, w["pool_w"], pool_scale[l]).reshape(m, POOL_WIDTH)
        y_dil = dil_prompt(z3, band_bias).reshape(m, DIL_WIDTH)
        y_gla, gla_new = gla_prompt(z3, za.reshape(b, seq, LANES), w["w_a2p"], gla_b_a[l], gla_norm[l])
        merged = branch_merge(y_pool, y_dil, y_gla.reshape(m, GLA_VAL_WIDTH), w["w_br_pool"], w["w_br_dil"], w["w_br_gla"],
                              zg, tm=TM, tn=TN_MERGE)
        xp = mm_post(merged, w["w_mix_out"], norm_mix_post[l], xp, tm=TM_POST, tk=TK_POST)
        mem_kv = norm_mm(mem_prompt.reshape(b * MEM_LEN, d), norm_mem[l], w["w_xkv"], tm=TM, tn=TN)
        q = norm_mm(xp, norm_x_pre[l], w["w_xq"], tm=TM, tn=TN, out_dtype=BF16)
        o = xattn(q.reshape(b, seq, X_WIDTH), mem_kv.reshape(b, MEM_LEN, 2 * X_WIDTH), tt=XATTN_ROWS)
        xp = mm_post(o.reshape(m, X_WIDTH), w["w_xo"], norm_x_post[l], xp, tm=TM_POST, tk=TK_POST)
        act, tails = ffn_up_prompt(xp, norm_ffn_pre[l], w["w_up"], conv_w[l], conv_b[l], seq=seq, tm=TM, tn=TN_FFN)
        xp = mm_post(act, w["w_down"], norm_ffn_post[l], xp, tm=TM_POST, tk=TK_POST)

        pool_p.append(z3[:, seq - POOL_HIST:, COL_POOL:COL_POOL + POOL_WIDTH])
        for g, (win, _) in enumerate(DIL_CONFIGS):
            keep = min(win, seq)
            kv = z3[:, seq - keep:, COL_DIL_K[g]:COL_DIL_K[g] + 2 * DIL_WIDTH]
            dil_p[g].append(kv.reshape(b, keep, 2, DIL_HEADS, DIL_HEAD_DIM))
        gla_p.append(gla_new)
        mem_p.append(mem_kv.reshape(b, MEM_LEN, 2, X_HEADS, X_HEAD_DIM))
        conv_p.append(tails.reshape(b, seq_tiles, FFN_TAIL, 2 * D_FF)[:, seq_tiles - 1, FFN_TAIL - 2:, :])

        zs = norm_mm(xs, norm_mix_pre[l], w["w_main"], tm=bd, tn=TN)
        zgs = norm_mm(xs, norm_mix_pre[l], w["w_gates"], tm=bd, tn=TN)
        zas = norm_mm(xs, norm_mix_pre[l], w["w_ga"], tm=bd, tn=LANES)
        y_pool_s, pool_new_t = pool_step(jnp.swapaxes(state_pool[l], 0, 1), zs, w["pool_w"], pool_scale[l])
        y_dil_s = dil_step(zs, dil_caches, l, slot_bias).astype(BF16)
        y_gla_s, gla_new_s = gla_step(zs, zas, state_gla, l, w["w_a2p"], gla_b_a[l], gla_norm[l])
        merged_s = branch_merge(y_pool_s, y_dil_s, y_gla_s.astype(BF16), w["w_br_pool"], w["w_br_dil"], w["w_br_gla"],
                                zgs, tm=bd, tn=TN_MERGE)
        xs = mm_post(merged_s, w["w_mix_out"], norm_mix_post[l], xs, tm=bd, tk=TK_POST)
        q_s = norm_mm(xs, norm_x_pre[l], w["w_xq"], tm=bd, tn=TN)
        o_s = xattn_step(q_s, cache_mem_kv, l).astype(BF16)
        xs = mm_post(o_s, w["w_xo"], norm_x_post[l], xs, tm=bd, tk=TK_POST)
        act_s, conv_new_t = ffn_up_step(xs, norm_ffn_pre[l], w["w_up"], conv_w[l], conv_b[l],
                                        jnp.swapaxes(state_ffn_conv[l], 0, 1), tn=TN_FFN)
        xs = mm_post(act_s, w["w_down"], norm_ffn_post[l], xs, tm=bd, tk=TK_POST)

        pool_s.append(jnp.swapaxes(pool_new_t, 0, 1))
        for g in range(3):
            kv = zs[:, COL_DIL_K[g]:COL_DIL_K[g] + 2 * DIL_WIDTH]
            dil_s[g].append(kv.reshape(bd, 1, 2, DIL_HEADS, DIL_HEAD_DIM))
        gla_s.append(gla_new_s)
        conv_s.append(jnp.swapaxes(conv_new_t, 0, 1))

    return (xp.reshape(b, seq, d), xs.reshape(bd, 1, d),
            jnp.stack(pool_p), jnp.stack(dil_p[0]), jnp.stack(dil_p[1]), jnp.stack(dil_p[2]), jnp.stack(gla_p), jnp.stack(mem_p), jnp.stack(conv_p),
            jnp.stack(pool_s), jnp.stack(dil_s[0]), jnp.stack(dil_s[1]), jnp.stack(dil_s[2]), jnp.stack(gla_s), jnp.stack(conv_s))
```

```python
import functools
import math

import jax
import jax.numpy as jnp
from jax import lax
from jax.experimental import pallas as pl
from jax.experimental.pallas import tpu as pltpu

F32 = jnp.float32
BF16 = jnp.bfloat16
EPS = 1e-6
NEG_INF = -1e30

VMEM_LIMIT_BYTES = 56 * 1024 * 1024
LANES = 128

D_MODEL = 2048
POOL_WINDOWS = (2, 4, 8, 16)
POOL_GROUP = 256
POOL_WIDTH = 1024
POOL_HIST = 15
DIL_CONFIGS = ((128, 1), (512, 4), (2048, 16))
DIL_HEADS = 8
DIL_HEAD_DIM = 128
DIL_WIDTH = 1024
DIL_BLOCK = 128
GLA_HEADS = 4
GLA_DK = 256
GLA_DV = 512
GLA_KEY_WIDTH = 1024
GLA_VAL_WIDTH = 2048
GLA_RANK = 16
GLA_TAU = 16.0
GLA_CHUNK = 64
REL_BUCKETS = 32
REL_MAX_DIST = 2048
MEM_LEN = 256
X_HEADS = 4
X_HEAD_DIM = 256
X_WIDTH = 1024
D_FF = 5632
PAST_LEN = 8192

COL_POOL = 0
COL_DIL_Q = tuple(1024 + 3072 * g for g in range(3))
COL_DIL_K = tuple(2048 + 3072 * g for g in range(3))
COL_DIL_V = tuple(3072 + 3072 * g for g in range(3))
COL_GQ = 10240
COL_GK = 11264
COL_GV = 12288
COL_GR = 14336
N_MAIN = 16384
N_GATES = 3 * D_MODEL
N_PROJ = N_MAIN + N_GATES


def _params(*semantics):
    return pltpu.CompilerParams(dimension_semantics=semantics, vmem_limit_bytes=VMEM_LIMIT_BYTES)


def _sigmoid(x):
    return 1.0 / (1.0 + jnp.exp(-x))


def _log_sigmoid(x):
    return jnp.minimum(x, 0.0) - jnp.log(1.0 + jnp.exp(-jnp.abs(x)))


def _gelu_tanh(x):
    return x * (0.5 * (1.0 + jnp.tanh(math.sqrt(2.0 / math.pi) * (x + 0.044715 * (x * x * x)))))


def _rms_scale(y, g):
    return y * lax.rsqrt(jnp.mean(y * y, axis=-1, keepdims=True) + EPS) * g


def _norm_rows(x_ref, g_ref, xn_ref, rows, chunk, dst_offset=0):
    g = g_ref[...]

    def body(c, carry):
        r0 = pl.multiple_of(c * chunk, chunk)
        xn_ref[pl.ds(dst_offset + r0, chunk), :] = _rms_scale(x_ref[pl.ds(r0, chunk), :], g).astype(BF16)
        return carry

    lax.fori_loop(0, rows // chunk, body, 0)


def _prep_w_in_kernel(w_ref, wx_ref, o_ref, ga_ref, *, n_plain, shift):
    j = pl.program_id(1)

    @pl.when(j < n_plain)
    def _():
        o_ref[...] = w_ref[...].astype(BF16)

    @pl.when(j >= n_plain)
    def _():
        tn = o_ref.shape[0]
        o_ref[0:tn - shift, :] = w_ref[shift:tn, :].astype(BF16)
        o_ref[tn - shift:tn, :] = wx_ref[...].astype(BF16)

    @pl.when(j == n_plain)
    def _():
        ga_ref[0:shift, :] = w_ref[0:shift, :].astype(BF16)
        ga_ref[shift:LANES, :] = jnp.zeros((LANES - shift, ga_ref.shape[1]), BF16)


def prep_w_in(w_in_t, *, tn):
    depth, _, k = w_in_t.shape
    n_plain = N_MAIN // tn
    nj = (N_MAIN + N_GATES) // tn
    return pl.pallas_call(
        functools.partial(_prep_w_in_kernel, n_plain=n_plain, shift=GLA_RANK),
        grid=(depth, nj),
        in_specs=[
            pl.BlockSpec((None, tn, k), lambda l, j: (l, j, 0)),
            pl.BlockSpec((None, GLA_RANK, k), lambda l, j: (l, jnp.maximum(j + 1, n_plain) * (tn // GLA_RANK), 0)),
        ],
        out_specs=[
            pl.BlockSpec((None, tn, k), lambda l, j: (l, j, 0)),
            pl.BlockSpec((None, LANES, k), lambda l, j: (l, 0, 0)),
        ],
        out_shape=[jax.ShapeDtypeStruct((depth, N_MAIN + N_GATES, k), BF16), jax.ShapeDtypeStruct((depth, LANES, k), BF16)],
        compiler_params=_params("parallel", "arbitrary"),
        name="prep_w_in",
    )(w_in_t, w_in_t)


def _in_proj_kernel(x_ref, g_ref, w_ref, o_ref, kv_ref, xn_ref, *, tm, chunk, kv_lo, kv_hi):
    j = pl.program_id(1)

    @pl.when(j == 0)
    def _():
        _norm_rows(x_ref, g_ref, xn_ref, tm, chunk)

    o_ref[...] = lax.dot_general(xn_ref[...], w_ref[...], (((1,), (1,)), ((), ())), preferred_element_type=F32)

    @pl.when(jnp.logical_and(j >= kv_lo, j < kv_hi))
    def _():
        kv_ref[...] = o_ref[...]


def in_proj(x, g, w_t, layer, *, tm, tn):
    m, k = x.shape
    n = w_t.shape[1]
    tm = min(tm, m)
    assert m % tm == 0 and n % tn == 0 and COL_DIL_K[2] % tn == 0
    chunk = min(256, tm)
    kv_lo = COL_DIL_K[2] // tn
    n_kv = 2 * DIL_WIDTH // tn
    return pl.pallas_call(
        functools.partial(_in_proj_kernel, tm=tm, chunk=chunk, kv_lo=kv_lo, kv_hi=kv_lo + n_kv),
        grid=(m // tm, n // tn),
        in_specs=[
            pl.BlockSpec((tm, k), lambda i, j: (i, 0)),
            pl.BlockSpec((1, k), lambda i, j: (0, 0)),
            pl.BlockSpec((None, tn, k), lambda i, j: (layer, j, 0)),
        ],
        out_specs=[
            pl.BlockSpec((tm, tn), lambda i, j: (i, j)),
            pl.BlockSpec((tm, tn), lambda i, j: (i, jnp.clip(j - kv_lo, 0, n_kv - 1))),
        ],
        out_shape=[jax.ShapeDtypeStruct((m, n), F32), jax.ShapeDtypeStruct((m, 2 * DIL_WIDTH), F32)],
        scratch_shapes=[pltpu.VMEM((tm, k), BF16)],
        compiler_params=_params("parallel", "arbitrary"),
        name="in_proj",
    )(x, g.reshape(1, k), w_t)


def _norm_mm_kernel(x_ref, g_ref, w_ref, o_ref, xn_ref, *, tm, chunk, w_rows_are_outputs):
    @pl.when(pl.program_id(1) == 0)
    def _():
        _norm_rows(x_ref, g_ref, xn_ref, tm, chunk)

    contract = (((1,), (1,)), ((), ())) if w_rows_are_outputs else (((1,), (0,)), ((), ()))
    o_ref[...] = lax.dot_general(xn_ref[...], w_ref[...], contract, preferred_element_type=F32).astype(o_ref.dtype)


def norm_mm(x, g, w, layer, *, tm, tn, out_dtype=F32, w_rows_are_outputs=False):
    m, k = x.shape
    n = w.shape[1] if w_rows_are_outputs else w.shape[2]
    tm = min(tm, m)
    tn = min(tn, n)
    assert m % tm == 0 and n % tn == 0
    chunk = min(256, tm)
    if w_rows_are_outputs:
        w_spec = pl.BlockSpec((None, tn, k), lambda i, j: (layer, j, 0))
    else:
        w_spec = pl.BlockSpec((None, k, tn), lambda i, j: (layer, 0, j))
    return pl.pallas_call(
        functools.partial(_norm_mm_kernel, tm=tm, chunk=chunk, w_rows_are_outputs=w_rows_are_outputs),
        grid=(m // tm, n // tn),
        in_specs=[
            pl.BlockSpec((tm, k), lambda i, j: (i, 0)),
            pl.BlockSpec((1, k), lambda i, j: (0, 0)),
            w_spec,
        ],
        out_specs=pl.BlockSpec((tm, tn), lambda i, j: (i, j)),
        out_shape=jax.ShapeDtypeStruct((m, n), out_dtype),
        scratch_shapes=[pltpu.VMEM((tm, k), BF16)],
        compiler_params=_params("parallel", "arbitrary"),
        name="norm_mm",
    )(x, g.reshape(1, k), w)


def _mm_post_kernel(a_ref, w_ref, g_ref, res_ref, o_ref, acc_ref, *, nk):
    kk = pl.program_id(1)
    part = jnp.dot(a_ref[...], w_ref[...], preferred_element_type=F32)
    if nk == 1:
        o_ref[...] = res_ref[...] + _rms_scale(part, g_ref[...])
        return

    @pl.when(kk == 0)
    def _():
        acc_ref[...] = part

    @pl.when(jnp.logical_and(kk > 0, kk < nk - 1))
    def _():
        acc_ref[...] += part

    @pl.when(kk == nk - 1)
    def _():
        o_ref[...] = res_ref[...] + _rms_scale(acc_ref[...] + part, g_ref[...])


def mm_post(a, w, layer, g, res, *, tm, tk):
    m, k = a.shape
    n = w.shape[2]
    tm = min(tm, m)
    tk = k if k <= tk else k // 4
    assert m % tm == 0 and k % tk == 0 and tk % LANES == 0
    return pl.pallas_call(
        functools.partial(_mm_post_kernel, nk=k // tk),
        grid=(m // tm, k // tk),
        in_specs=[
            pl.BlockSpec((tm, tk), lambda i, kk: (i, kk)),
            pl.BlockSpec((None, tk, n), lambda i, kk: (layer, kk, 0)),
            pl.BlockSpec((1, n), lambda i, kk: (0, 0)),
            pl.BlockSpec((tm, n), lambda i, kk: (i, 0)),
        ],
        out_specs=pl.BlockSpec((tm, n), lambda i, kk: (i, 0)),
        out_shape=jax.ShapeDtypeStruct((m, n), F32),
        scratch_shapes=[pltpu.VMEM((tm, n), F32)],
        compiler_params=_params("parallel", "arbitrary"),
        name="mm_post",
    )(a, w, g.reshape(1, n), res)


def _branch_merge_kernel(yp_ref, yd_ref, yg_ref, wp_ref, wd_ref, wg_ref, g0_ref, g1_ref, g2_ref, o_ref):
    bp = jnp.dot(yp_ref[...], wp_ref[...], preferred_element_type=F32)
    bd = jnp.dot(yd_ref[...], wd_ref[...], preferred_element_type=F32)
    bg = jnp.dot(yg_ref[...], wg_ref[...], preferred_element_type=F32)
    merged = _sigmoid(g0_ref[...]) * bp + _sigmoid(g1_ref[...]) * bd + _sigmoid(g2_ref[...]) * bg
    o_ref[...] = merged.astype(o_ref.dtype)


def branch_merge(y_pool, y_dil, y_gla, w_pool, w_dil, w_gla, layer, z_main, *, tm, tn):
    m = y_pool.shape[0]
    tm = min(tm, m)
    assert m % tm == 0 and D_MODEL % tn == 0
    gate_blk = [(N_MAIN + b * D_MODEL) // tn for b in range(3)]

    def gate_spec(b):
        return pl.BlockSpec((tm, tn), lambda i, j: (i, gate_blk[b] + j))

    return pl.pallas_call(
        _branch_merge_kernel,
        grid=(m // tm, D_MODEL // tn),
        in_specs=[
            pl.BlockSpec((tm, POOL_WIDTH), lambda i, j: (i, 0)),
            pl.BlockSpec((tm, DIL_WIDTH), lambda i, j: (i, 0)),
            pl.BlockSpec((tm, GLA_VAL_WIDTH), lambda i, j: (i, 0)),
            pl.BlockSpec((None, POOL_WIDTH, tn), lambda i, j: (layer, 0, j)),
            pl.BlockSpec((None, DIL_WIDTH, tn), lambda i, j: (layer, 0, j)),
            pl.BlockSpec((None, GLA_VAL_WIDTH, tn), lambda i, j: (layer, 0, j)),
            gate_spec(0),
            gate_spec(1),
            gate_spec(2),
        ],
        out_specs=pl.BlockSpec((tm, tn), lambda i, j: (i, j)),
        out_shape=jax.ShapeDtypeStruct((m, D_MODEL), BF16),
        compiler_params=_params("parallel", "arbitrary"),
        name="branch_merge",
    )(y_pool, y_dil, y_gla, w_pool, w_dil, w_gla, z_main, z_main, z_main)


def _pool_prompt_kernel(u_ref, w_ref, s_ref, o_ref, buf_a, buf_b, *, seq):
    pad = POOL_HIST + 1
    zeros = jnp.zeros((pad, POOL_GROUP), F32)
    t = lax.broadcasted_iota(jnp.int32, (seq, 1), 0)
    for g, win in enumerate(POOL_WINDOWS):
        cols = slice(g * POOL_GROUP, (g + 1) * POOL_GROUP)
        u = u_ref[:, cols]
        cur, nxt = buf_a, buf_b
        cur[0:pad, :] = zeros
        nxt[0:pad, :] = zeros
        cur[pad:pad + seq, :] = u
        k = 1
        while k < win:
            nxt[pad:pad + seq, :] = cur[pad:pad + seq, :] + cur[pad - k:pad - k + seq, :]
            cur, nxt = nxt, cur
            k *= 2
        cnt = jnp.minimum(win, t + 1).astype(F32)
        d = cur[pad:pad + seq, :] / cnt - u
        y = jnp.dot(d.astype(BF16), w_ref[g], preferred_element_type=F32) * s_ref[:, cols]
        o_ref[:, cols] = y.astype(o_ref.dtype)


def pool_prompt(z3, pool_w, layer, pool_scale):
    b, seq, _ = z3.shape
    return pl.pallas_call(
        functools.partial(_pool_prompt_kernel, seq=seq),
        grid=(b,),
        in_specs=[
            pl.BlockSpec((None, seq, POOL_WIDTH), lambda i: (i, 0, COL_POOL // POOL_WIDTH)),
            pl.BlockSpec((None, len(POOL_WINDOWS), POOL_GROUP, POOL_GROUP), lambda i: (layer, 0, 0, 0)),
            pl.BlockSpec((1, POOL_WIDTH), lambda i: (0, 0)),
        ],
        out_specs=pl.BlockSpec((None, seq, POOL_WIDTH), lambda i: (i, 0, 0)),
        out_shape=jax.ShapeDtypeStruct((b, seq, POOL_WIDTH), BF16),
        scratch_shapes=[pltpu.VMEM((seq + POOL_HIST + 1, POOL_GROUP), F32)] * 2,
        compiler_params=_params("parallel"),
        name="pool_prompt",
    )(z3, pool_w, pool_scale.reshape(1, POOL_WIDTH))


def _pool_step_kernel(hist_ref, u_ref, w_ref, s_ref, y_ref, new_ref):
    u = u_ref[...]
    for r in range(POOL_HIST - 1):
        new_ref[r] = hist_ref[r + 1]
    new_ref[POOL_HIST - 1] = u
    for g, win in enumerate(POOL_WINDOWS):
        cols = slice(g * POOL_GROUP, (g + 1) * POOL_GROUP)
        ug = u[:, cols]
        acc = ug
        for r in range(POOL_HIST - (win - 1), POOL_HIST):
            acc = acc + hist_ref[r, :, cols]
        d = acc / float(win) - ug
        y = jnp.dot(d.astype(BF16), w_ref[g], preferred_element_type=F32) * s_ref[:, cols]
        y_ref[:, cols] = y.astype(y_ref.dtype)


def pool_step(hist_t, z_s, pool_w, layer, pool_scale):
    bd = z_s.shape[0]
    return pl.pallas_call(
        _pool_step_kernel,
        grid=(1,),
        in_specs=[
            pl.BlockSpec((POOL_HIST, bd, POOL_WIDTH), lambda i: (0, 0, 0)),
            pl.BlockSpec((bd, POOL_WIDTH), lambda i: (0, COL_POOL // POOL_WIDTH)),
            pl.BlockSpec((None, len(POOL_WINDOWS), POOL_GROUP, POOL_GROUP), lambda i: (layer, 0, 0, 0)),
            pl.BlockSpec((1, POOL_WIDTH), lambda i: (0, 0)),
        ],
        out_specs=[
            pl.BlockSpec((bd, POOL_WIDTH), lambda i: (0, 0)),
            pl.BlockSpec((POOL_HIST, bd, POOL_WIDTH), lambda i: (0, 0, 0)),
        ],
        out_shape=[
            jax.ShapeDtypeStruct((bd, POOL_WIDTH), BF16),
            jax.ShapeDtypeStruct((POOL_HIST, bd, POOL_WIDTH), F32),
        ],
        compiler_params=_params("arbitrary"),
        name="pool_step",
    )(hist_t, z_s, pool_w, pool_scale.reshape(1, POOL_WIDTH))


def _rel_bucket(dist):
    max_exact = REL_BUCKETS // 2
    scaled = jnp.log(jnp.maximum(dist, 1).astype(F32) / max_exact) / math.log(REL_MAX_DIST / max_exact)
    large = jnp.minimum(max_exact + (scaled * (REL_BUCKETS - max_exact)).astype(jnp.int32), REL_BUCKETS - 1)
    return jnp.where(dist < max_exact, dist, large)


def _select_rows(table_t, index, n):
    onehot = (index.reshape(1, -1) == jnp.arange(n, dtype=jnp.int32)[:, None]).astype(F32)
    out = jnp.dot(table_t, onehot, precision=lax.Precision.HIGHEST, preferred_element_type=F32)
    return out.reshape((table_t.shape[0],) + index.shape)


def _slot_biases(rel_bias):
    out = []
    for g, (win, dil) in enumerate(DIL_CONFIGS):
        dist = jnp.arange(win // dil + 1, dtype=jnp.int32) * dil
        table_t = rel_bias[:, g * DIL_HEADS:(g + 1) * DIL_HEADS].T.astype(F32)
        out.append(_select_rows(table_t, _rel_bucket(dist), REL_BUCKETS))
    return out


def _band_bias(slot_bias):
    qi = jnp.arange(DIL_BLOCK, dtype=jnp.int32)[:, None] + DIL_BLOCK
    ki = jnp.arange(2 * DIL_BLOCK, dtype=jnp.int32)[None, :]
    rel = qi - ki
    out = []
    for g, (win, dil) in enumerate(DIL_CONFIGS):
        n_slots = win // dil
        ok = (rel >= 0) & (rel <= n_slots)
        b = _select_rows(slot_bias[g], jnp.clip(rel, 0, n_slots), n_slots + 1)
        out.append(jnp.where(ok[None], b, NEG_INF))
    return jnp.stack(out, axis=0)


def _dil_prompt_kernel(q0, k0, v0, q1, k1, v1, q2, k2, v2, bias_ref, o_ref, o_scr, lse_scr, *, seq):
    qs, ks, vs = (q0, q1, q2), (k0, k1, k2), (v0, v1, v2)
    scale = DIL_HEAD_DIM ** -0.5
    blk = DIL_BLOCK
    nt = (((1,), (1,)), ((), ()))

    def rows(ref, start, dil):
        if dil == 1:
            return ref[pl.ds(start, blk), :]
        return ref[pl.ds(start, blk, stride=dil), :]

    for g, (_, dil) in enumerate(DIL_CONFIGS):
        nb = seq // dil // blk
        for r in range(dil):
            for ub in range(nb):
                start = r + dil * ub * blk
                q = rows(qs[g], start, dil).astype(BF16)
                kc = rows(ks[g], start, dil).astype(BF16)
                vc = rows(vs[g], start, dil).astype(BF16)
                if ub == 0:
                    kk, vv, bias = kc, vc, bias_ref[g, :, blk:]
                else:
                    prev = start - dil * blk
                    kk = jnp.concatenate([rows(ks[g], prev, dil).astype(BF16), kc], axis=0)
                    vv = jnp.concatenate([rows(vs[g], prev, dil).astype(BF16), vc], axis=0)
                    bias = bias_ref[g]
                s = lax.dot_general(q, kk, nt, preferred_element_type=F32) * scale + bias
                m = jnp.max(s, axis=-1, keepdims=True)
                e = jnp.exp(s - m)
                l = jnp.sum(e, axis=-1, keepdims=True)
                o = jnp.dot(e.astype(BF16), vv, preferred_element_type=F32) / l
                lse = jnp.broadcast_to(m + jnp.log(l), (blk, LANES))
                if dil == 1:
                    o_scr[g, pl.ds(start, blk), :] = o
                    lse_scr[g, pl.ds(start, blk), :] = lse
                else:
                    o_scr[g, pl.ds(start, blk, stride=dil), :] = o
                    lse_scr[g, pl.ds(start, blk, stride=dil), :] = lse

    chunk = 256

    def combine(c, carry):
        sl = pl.ds(pl.multiple_of(c * chunk, chunk), chunk)
        l0, l1, l2 = lse_scr[0, sl, :], lse_scr[1, sl, :], lse_scr[2, sl, :]
        mx = jnp.maximum(jnp.maximum(l0, l1), l2)
        w0, w1, w2 = jnp.exp(l0 - mx), jnp.exp(l1 - mx), jnp.exp(l2 - mx)
        y = (w0 * o_scr[0, sl, :] + w1 * o_scr[1, sl, :] + w2 * o_scr[2, sl, :]) / (w0 + w1 + w2)
        o_ref[sl, :] = y.astype(o_ref.dtype)
        return carry

    lax.fori_loop(0, seq // chunk, combine, 0)


def dil_prompt(z3, band_bias):
    b, seq, _ = z3.shape
    assert seq % (DIL_BLOCK * 16) == 0

    def col_spec(col):
        blk0 = col // DIL_HEAD_DIM
        return pl.BlockSpec((None, seq, DIL_HEAD_DIM), lambda i, h: (i, 0, blk0 + h))

    in_specs = []
    for g in range(3):
        in_specs += [col_spec(COL_DIL_Q[g]), col_spec(COL_DIL_K[g]), col_spec(COL_DIL_V[g])]
    in_specs.append(pl.BlockSpec((3, None, DIL_BLOCK, 2 * DIL_BLOCK), lambda i, h: (0, h, 0, 0)))
    return pl.pallas_call(
        functools.partial(_dil_prompt_kernel, seq=seq),
        grid=(b, DIL_HEADS),
        in_specs=in_specs,
        out_specs=pl.BlockSpec((None, seq, DIL_HEAD_DIM), lambda i, h: (i, 0, h)),
        out_shape=jax.ShapeDtypeStruct((b, seq, DIL_WIDTH), BF16),
        scratch_shapes=[pltpu.VMEM((3, seq, DIL_HEAD_DIM), F32), pltpu.VMEM((3, seq, LANES), F32)],
        compiler_params=_params("parallel", "arbitrary"),
        name="dil_prompt",
    )(*([z3] * 9), band_bias)


def _dil_step_kernel(qkv_ref, c0, c1, c2, bias_ref, bias0_ref, o_ref):
    caches = (c0, c1, c2)
    scale = DIL_HEAD_DIM ** -0.5
    outs, lses = [], []
    for g in range(3):
        q, kn, vn = qkv_ref[3 * g], qkv_ref[3 * g + 1], qkv_ref[3 * g + 2]
        kc = caches[g][:, 0]
        vc = caches[g][:, 1]
        s = jnp.sum(kc * q[None], axis=-1, keepdims=True) * scale + bias_ref[g]
        s_new = jnp.sum(q * kn, axis=-1, keepdims=True) * scale + bias0_ref[g]
        m = jnp.maximum(jnp.max(s, axis=0), s_new)
        p = jnp.exp(s - m[None])
        p_new = jnp.exp(s_new - m)
        l = jnp.sum(p, axis=0) + p_new
        outs.append((jnp.sum(p * vc, axis=0) + p_new * vn) / l)
        lses.append(m + jnp.log(l))
    mx = jnp.maximum(jnp.maximum(lses[0], lses[1]), lses[2])
    w0, w1, w2 = jnp.exp(lses[0] - mx), jnp.exp(lses[1] - mx), jnp.exp(lses[2] - mx)
    o_ref[...] = (w0 * outs[0] + w1 * outs[1] + w2 * outs[2]) / (w0 + w1 + w2)


def dil_step(z_s, caches, layer, slot_bias):
    bd = z_s.shape[0]
    n_slots = DIL_CONFIGS[0][0] // DIL_CONFIGS[0][1]
    qkv = z_s[:, COL_DIL_Q[0]:COL_DIL_V[2] + DIL_WIDTH].reshape(bd, 9, DIL_HEADS, DIL_HEAD_DIM)
    in_specs = [pl.BlockSpec((None, 9, DIL_HEADS, DIL_HEAD_DIM), lambda i: (i, 0, 0, 0))]
    cache_views = []
    for g, (win, dil) in enumerate(DIL_CONFIGS):
        depth = caches[g].shape[0]
        assert caches[g].shape[2] == win and win // dil == n_slots
        cache_views.append(caches[g].reshape(depth, bd, n_slots, dil, 2, DIL_HEADS, DIL_HEAD_DIM))
        in_specs.append(pl.BlockSpec((None, None, n_slots, None, 2, DIL_HEADS, DIL_HEAD_DIM),
                                     lambda i: (layer, i, 0, 0, 0, 0, 0)))
    bias_rows = jnp.stack([sb[:, n_slots:0:-1].T for sb in slot_bias], axis=0)
    bias_rows = jnp.broadcast_to(bias_rows[..., None], (3, n_slots, DIL_HEADS, LANES))
    bias_new = jnp.broadcast_to(jnp.stack([sb[:, 0] for sb in slot_bias], axis=0)[..., None], (3, DIL_HEADS, LANES))
    in_specs.append(pl.BlockSpec((3, n_slots, DIL_HEADS, LANES), lambda i: (0, 0, 0, 0)))
    in_specs.append(pl.BlockSpec((3, DIL_HEADS, LANES), lambda i: (0, 0, 0)))
    out = pl.pallas_call(
        _dil_step_kernel,
        grid=(bd,),
        in_specs=in_specs,
        out_specs=pl.BlockSpec((None, DIL_HEADS, DIL_HEAD_DIM), lambda i: (i, 0, 0)),
        out_shape=jax.ShapeDtypeStruct((bd, DIL_HEADS, DIL_HEAD_DIM), F32),
        compiler_params=_params("parallel"),
        name="dil_step",
    )(qkv, *cache_views, bias_rows, bias_new)
    return out.reshape(bd, DIL_WIDTH)


GLA_HEADS_PER_STEP = 2


def _gla_prompt_kernel(q_ref, k_ref, v_ref, r_ref, ga_ref, wa_ref, ba_ref, gn_ref, y_ref, s_out_ref, st_ref, o_scr, *, tq):
    t = pl.program_id(2)
    ck = GLA_CHUNK

    @pl.when(t == 0)
    def _():
        st_ref[...] = jnp.zeros_like(st_ref)

    la = jnp.dot(ga_ref[...].astype(BF16), wa_ref[...], preferred_element_type=F32) + ba_ref[...]
    log_a = _log_sigmoid(la) / GLA_TAU
    row = lax.broadcasted_iota(jnp.int32, (ck, ck), 0)
    col = lax.broadcasted_iota(jnp.int32, (ck, ck), 1)
    tril = row >= col
    tril_f = tril.astype(F32)
    nt = (((1,), (1,)), ((), ()))
    tn = (((0,), (0,)), ((), ()))
    for c in range(tq // ck):
        sl = slice(c * ck, (c + 1) * ck)
        for hh in range(GLA_HEADS_PER_STEP):
            ks = slice(hh * GLA_DK, (hh + 1) * GLA_DK)
            vs = slice(hh * GLA_DV, (hh + 1) * GLA_DV)
            b = jnp.dot(tril_f, log_a[sl, ks], precision=lax.Precision.HIGHEST, preferred_element_type=F32)
            b_last = b[ck - 1:ck, :]
            q = q_ref[sl, ks] * (GLA_DK ** -0.5)
            k = k_ref[sl, ks]
            v = v_ref[sl, vs].astype(BF16)
            q_t = (q * jnp.exp(b)).astype(BF16)
            k_t = (k * jnp.exp(-b)).astype(BF16)
            k_h = (k * jnp.exp(b_last - b)).astype(BF16)
            decay = jnp.exp(b_last)
            a = lax.dot_general(q_t, k_t, nt, preferred_element_type=F32)
            a = jnp.where(tril, a, 0.0)
            st = st_ref[hh]
            o = jnp.dot(a.astype(BF16), v, preferred_element_type=F32)
            o = o + lax.dot_general(q_t, st.astype(BF16), nt, preferred_element_type=F32)
            o_scr[sl, vs] = o
            st_ref[hh] = decay * st + lax.dot_general(v, k_h, tn, preferred_element_type=F32)

    for hh in range(GLA_HEADS_PER_STEP):
        vs = slice(hh * GLA_DV, (hh + 1) * GLA_DV)
        o = _rms_scale(o_scr[:, vs], gn_ref[:, vs])
        r = r_ref[:, vs]
        y_ref[:, vs] = (o * (r * _sigmoid(r))).astype(y_ref.dtype)

    @pl.when(t == pl.num_programs(2) - 1)
    def _():
        for hh in range(GLA_HEADS_PER_STEP):
            s_out_ref[hh] = st_ref[hh].T


def gla_prompt(z3, za3, w_a2p, b_a, gla_norm, *, tq=512):
    b, seq, _ = z3.shape
    hp = GLA_HEADS_PER_STEP
    kw, vw = hp * GLA_DK, hp * GLA_DV
    assert seq % tq == 0 and tq % GLA_CHUNK == 0 and GLA_HEADS % hp == 0
    return pl.pallas_call(
        functools.partial(_gla_prompt_kernel, tq=tq),
        grid=(b, GLA_HEADS // hp, seq // tq),
        in_specs=[
            pl.BlockSpec((None, tq, kw), lambda i, h, t: (i, t, COL_GQ // kw + h)),
            pl.BlockSpec((None, tq, kw), lambda i, h, t: (i, t, COL_GK // kw + h)),
            pl.BlockSpec((None, tq, vw), lambda i, h, t: (i, t, COL_GV // vw + h)),
            pl.BlockSpec((None, tq, vw), lambda i, h, t: (i, t, COL_GR // vw + h)),
            pl.BlockSpec((None, tq, LANES), lambda i, h, t: (i, t, 0)),
            pl.BlockSpec((LANES, kw), lambda i, h, t: (0, h)),
            pl.BlockSpec((1, kw), lambda i, h, t: (0, h)),
            pl.BlockSpec((1, vw), lambda i, h, t: (0, h)),
        ],
        out_specs=[
            pl.BlockSpec((None, tq, vw), lambda i, h, t: (i, t, h)),
            pl.BlockSpec((None, hp, GLA_DK, GLA_DV), lambda i, h, t: (i, h, 0, 0)),
        ],
        out_shape=[
            jax.ShapeDtypeStruct((b, seq, GLA_VAL_WIDTH), BF16),
            jax.ShapeDtypeStruct((b, GLA_HEADS, GLA_DK, GLA_DV), F32),
        ],
        scratch_shapes=[pltpu.VMEM((hp, GLA_DV, GLA_DK), F32), pltpu.VMEM((tq, vw), F32)],
        compiler_params=_params("parallel", "parallel", "arbitrary"),
        name="gla_prompt",
    )(z3, z3, z3, z3, za3, w_a2p, b_a.reshape(1, GLA_KEY_WIDTH), gla_norm.reshape(1, GLA_VAL_WIDTH))


def _gla_step_kernel(q_ref, k_ref, v_ref, r_ref, ga_ref, wa_ref, ba_ref, gn_ref, s_ref, y_ref, s_out_ref):
    ga8 = jnp.broadcast_to(ga_ref[...], (8, LANES)).astype(BF16)
    la = jnp.dot(ga8, wa_ref[...], preferred_element_type=F32)[0:1, :] + ba_ref[...]
    ea = jnp.exp(_log_sigmoid(la) / GLA_TAU)
    q = q_ref[...] * (GLA_DK ** -0.5)
    k = k_ref[...]
    eye = lax.broadcasted_iota(jnp.int32, (GLA_DK, GLA_DK), 0) == lax.broadcasted_iota(jnp.int32, (GLA_DK, GLA_DK), 1)

    def column(row):
        return jnp.sum(jnp.where(eye, jnp.broadcast_to(row, (GLA_DK, GLA_DK)), 0.0), axis=-1, keepdims=True)

    for h in range(GLA_HEADS):
        ks = slice(h * GLA_DK, (h + 1) * GLA_DK)
        vs = slice(h * GLA_DV, (h + 1) * GLA_DV)
        s_new = column(ea[:, ks]) * s_ref[h] + column(k[:, ks]) * v_ref[:, vs]
        s_out_ref[h] = s_new
        o = jnp.sum(column(q[:, ks]) * s_new, axis=0, keepdims=True)
        o = _rms_scale(o, gn_ref[:, vs])
        r = r_ref[:, vs]
        y_ref[:, vs] = o * (r * _sigmoid(r))


def gla_step(z_s, za_s, state, layer, w_a2p, b_a, gla_norm):
    bd = z_s.shape[0]
    zs3 = z_s.reshape(bd, 1, N_PROJ)
    y, s_new = pl.pallas_call(
        _gla_step_kernel,
        grid=(bd,),
        in_specs=[
            pl.BlockSpec((None, 1, GLA_KEY_WIDTH), lambda i: (i, 0, COL_GQ // GLA_KEY_WIDTH)),
            pl.BlockSpec((None, 1, GLA_KEY_WIDTH), lambda i: (i, 0, COL_GK // GLA_KEY_WIDTH)),
            pl.BlockSpec((None, 1, GLA_VAL_WIDTH), lambda i: (i, 0, COL_GV // GLA_VAL_WIDTH)),
            pl.BlockSpec((None, 1, GLA_VAL_WIDTH), lambda i: (i, 0, COL_GR // GLA_VAL_WIDTH)),
            pl.BlockSpec((None, 1, LANES), lambda i: (i, 0, 0)),
            pl.BlockSpec((LANES, GLA_KEY_WIDTH), lambda i: (0, 0)),
            pl.BlockSpec((1, GLA_KEY_WIDTH), lambda i: (0, 0)),
            pl.BlockSpec((1, GLA_VAL_WIDTH), lambda i: (0, 0)),
            pl.BlockSpec((None, None, GLA_HEADS, GLA_DK, GLA_DV), lambda i: (layer, i, 0, 0, 0)),
        ],
        out_specs=[
            pl.BlockSpec((None, 1, GLA_VAL_WIDTH), lambda i: (i, 0, 0)),
            pl.BlockSpec((None, GLA_HEADS, GLA_DK, GLA_DV), lambda i: (i, 0, 0, 0)),
        ],
        out_shape=[
            jax.ShapeDtypeStruct((bd, 1, GLA_VAL_WIDTH), F32),
            jax.ShapeDtypeStruct((bd, GLA_HEADS, GLA_DK, GLA_DV), F32),
        ],
        compiler_params=_params("parallel"),
        name="gla_step",
    )(zs3, zs3, zs3, zs3, za_s.reshape(bd, 1, LANES), w_a2p, b_a.reshape(1, GLA_KEY_WIDTH),
      gla_norm.reshape(1, GLA_VAL_WIDTH), state)
    return y.reshape(bd, GLA_VAL_WIDTH), s_new


def _xattn_kernel(q_ref, kv_ref, o_ref):
    scale = X_HEAD_DIM ** -0.5
    nt = (((1,), (1,)), ((), ()))
    for h in range(X_HEADS):
        hs = slice(h * X_HEAD_DIM, (h + 1) * X_HEAD_DIM)
        k = kv_ref[:, h * X_HEAD_DIM:(h + 1) * X_HEAD_DIM].astype(BF16)
        v = kv_ref[:, X_WIDTH + h * X_HEAD_DIM:X_WIDTH + (h + 1) * X_HEAD_DIM].astype(BF16)
        s = lax.dot_general(q_ref[:, hs], k, nt, preferred_element_type=F32) * scale
        e = jnp.exp(s - jnp.max(s, axis=-1, keepdims=True))
        l = jnp.sum(e, axis=-1, keepdims=True)
        o = jnp.dot(e.astype(BF16), v, preferred_element_type=F32) / l
        o_ref[:, hs] = o.astype(o_ref.dtype)


def xattn(q3, mem_kv, *, tt):
    b, t, _ = q3.shape
    tt = min(tt, t)
    assert t % tt == 0
    return pl.pallas_call(
        _xattn_kernel,
        grid=(b, t // tt),
        in_specs=[
            pl.BlockSpec((None, tt, X_WIDTH), lambda i, j: (i, j, 0)),
            pl.BlockSpec((None, MEM_LEN, 2 * X_WIDTH), lambda i, j: (i, 0, 0)),
        ],
        out_specs=pl.BlockSpec((None, tt, X_WIDTH), lambda i, j: (i, j, 0)),
        out_shape=jax.ShapeDtypeStruct((b, t, X_WIDTH), BF16),
        compiler_params=_params("parallel", "arbitrary"),
        name="xattn",
    )(q3, mem_kv)


def _xattn_step_kernel(q_ref, kv_ref, o_ref):
    q = q_ref[...]
    k = kv_ref[:, 0]
    v = kv_ref[:, 1]
    s = jnp.sum(k * q[None], axis=-1, keepdims=True) * (X_HEAD_DIM ** -0.5)
    p = jnp.exp(s - jnp.max(s, axis=0)[None])
    o_ref[...] = jnp.sum(p * v, axis=0) / jnp.sum(p, axis=0)


def xattn_step(q, mem_kv, layer):
    bd = q.shape[0]
    out = pl.pallas_call(
        _xattn_step_kernel,
        grid=(bd,),
        in_specs=[
            pl.BlockSpec((None, X_HEADS, X_HEAD_DIM), lambda i: (i, 0, 0)),
            pl.BlockSpec((None, None, MEM_LEN, 2, X_HEADS, X_HEAD_DIM), lambda i: (layer, i, 0, 0, 0, 0)),
        ],
        out_specs=pl.BlockSpec((None, X_HEADS, X_HEAD_DIM), lambda i: (i, 0, 0)),
        out_shape=jax.ShapeDtypeStruct((bd, X_HEADS, X_HEAD_DIM), F32),
        compiler_params=_params("parallel"),
        name="xattn_step",
    )(q.reshape(bd, X_HEADS, X_HEAD_DIM), mem_kv)
    return out.reshape(bd, X_WIDTH)


FFN_HALO = 16
FFN_TAIL = 8


def _ffn_up_prompt_kernel(x_ref, xh_ref, g_ref, wg_ref, wv_ref, cwg_ref, cwv_ref, cbg_ref, cbv_ref,
                          act_ref, tg_ref, tv_ref, xn_ref, ug_ref, uv_ref, *, tm, seq_tiles, chunk):
    i = pl.program_id(0)

    @pl.when(pl.program_id(1) == 0)
    def _():
        _norm_rows(x_ref, g_ref, xn_ref, tm, chunk, dst_offset=FFN_HALO)
        halo = _rms_scale(xh_ref[...], g_ref[...])
        halo = jnp.where(i % seq_tiles == 0, 0.0, halo)
        xn_ref[0:FFN_HALO, :] = halo.astype(BF16)

    xn = xn_ref[...]
    ug_ref[...] = jnp.dot(xn, wg_ref[...], preferred_element_type=F32)
    uv_ref[...] = jnp.dot(xn, wv_ref[...], preferred_element_type=F32)

    def conv(u_ref, cw_ref, cb_ref):
        c = cb_ref[...] + cw_ref[0:1, :] * u_ref[FFN_HALO - 2:FFN_HALO - 2 + tm, :]
        c = c + cw_ref[1:2, :] * u_ref[FFN_HALO - 1:FFN_HALO - 1 + tm, :]
        return c + cw_ref[2:3, :] * u_ref[FFN_HALO:FFN_HALO + tm, :]

    act_ref[...] = (_gelu_tanh(conv(ug_ref, cwg_ref, cbg_ref)) * conv(uv_ref, cwv_ref, cbv_ref)).astype(act_ref.dtype)
    tg_ref[...] = ug_ref[FFN_HALO + tm - FFN_TAIL:FFN_HALO + tm, :]
    tv_ref[...] = uv_ref[FFN_HALO + tm - FFN_TAIL:FFN_HALO + tm, :]


def ffn_up_prompt(x, g, w_up, conv_w, conv_b3, layer, *, seq, tm, tn):
    m, k = x.shape
    assert seq % tm == 0 and D_FF % tn == 0 and tm % FFN_HALO == 0
    nj = D_FF // tn
    seq_tiles = seq // tm
    chunk = min(256, tm)
    halo_blocks = tm // FFN_HALO
    act, tail_g, tail_v = pl.pallas_call(
        functools.partial(_ffn_up_prompt_kernel, tm=tm, seq_tiles=seq_tiles, chunk=chunk),
        grid=(m // tm, nj),
        in_specs=[
            pl.BlockSpec((tm, k), lambda i, j: (i, 0)),
            pl.BlockSpec((FFN_HALO, k), lambda i, j: (jnp.maximum(i * halo_blocks - 1, 0), 0)),
            pl.BlockSpec((1, k), lambda i, j: (0, 0)),
            pl.BlockSpec((None, k, tn), lambda i, j: (layer, 0, j)),
            pl.BlockSpec((None, k, tn), lambda i, j: (layer, 0, j + nj)),
            pl.BlockSpec((None, 3, tn), lambda i, j: (layer, 0, j)),
            pl.BlockSpec((None, 3, tn), lambda i, j: (layer, 0, j + nj)),
            pl.BlockSpec((None, 1, tn), lambda i, j: (layer, 0, j)),
            pl.BlockSpec((None, 1, tn), lambda i, j: (layer, 0, j + nj)),
        ],
        out_specs=[
            pl.BlockSpec((tm, tn), lambda i, j: (i, j)),
            pl.BlockSpec((None, FFN_TAIL, tn), lambda i, j: (i, 0, j)),
            pl.BlockSpec((None, FFN_TAIL, tn), lambda i, j: (i, 0, j)),
        ],
        out_shape=[
            jax.ShapeDtypeStruct((m, D_FF), BF16),
            jax.ShapeDtypeStruct((m // tm, FFN_TAIL, D_FF), F32),
            jax.ShapeDtypeStruct((m // tm, FFN_TAIL, D_FF), F32),
        ],
        scratch_shapes=[
            pltpu.VMEM((tm + FFN_HALO, k), BF16),
            pltpu.VMEM((tm + FFN_HALO, tn), F32),
            pltpu.VMEM((tm + FFN_HALO, tn), F32),
        ],
        compiler_params=_params("parallel", "arbitrary"),
        name="ffn_up_prompt",
    )(x, x, g.reshape(1, k), w_up, w_up, conv_w, conv_w, conv_b3, conv_b3)
    return act, jnp.concatenate([tail_g, tail_v], axis=-1)


def _ffn_up_step_kernel(x_ref, g_ref, wg_ref, wv_ref, cwg_ref, cwv_ref, cbg_ref, cbv_ref, hg_ref, hv_ref,
                        act_ref, ng_ref, nv_ref, xn_ref, *, bd):
    @pl.when(pl.program_id(0) == 0)
    def _():
        _norm_rows(x_ref, g_ref, xn_ref, bd, bd)

    xn = xn_ref[...]
    ug = jnp.dot(xn, wg_ref[...], preferred_element_type=F32)
    uv = jnp.dot(xn, wv_ref[...], preferred_element_type=F32)

    def conv(u, h_ref, cw_ref, cb_ref):
        return cb_ref[...] + cw_ref[0:1, :] * h_ref[0] + cw_ref[1:2, :] * h_ref[1] + cw_ref[2:3, :] * u

    act_ref[...] = (_gelu_tanh(conv(ug, hg_ref, cwg_ref, cbg_ref)) * conv(uv, hv_ref, cwv_ref, cbv_ref)).astype(act_ref.dtype)
    ng_ref[0] = hg_ref[1]
    ng_ref[1] = ug
    nv_ref[0] = hv_ref[1]
    nv_ref[1] = uv


def ffn_up_step(x, g, w_up, conv_w, conv_b3, layer, hist_t, *, tn):
    bd, k = x.shape
    nj = D_FF // tn
    act, new_g, new_v = pl.pallas_call(
        functools.partial(_ffn_up_step_kernel, bd=bd),
        grid=(nj,),
        in_specs=[
            pl.BlockSpec((bd, k), lambda j: (0, 0)),
            pl.BlockSpec((1, k), lambda j: (0, 0)),
            pl.BlockSpec((None, k, tn), lambda j: (layer, 0, j)),
            pl.BlockSpec((None, k, tn), lambda j: (layer, 0, j + nj)),
            pl.BlockSpec((None, 3, tn), lambda j: (layer, 0, j)),
            pl.BlockSpec((None, 3, tn), lambda j: (layer, 0, j + nj)),
            pl.BlockSpec((None, 1, tn), lambda j: (layer, 0, j)),
            pl.BlockSpec((None, 1, tn), lambda j: (layer, 0, j + nj)),
            pl.BlockSpec((2, bd, tn), lambda j: (0, 0, j)),
            pl.BlockSpec((2, bd, tn), lambda j: (0, 0, j + nj)),
        ],
        out_specs=[
            pl.BlockSpec((bd, tn), lambda j: (0, j)),
            pl.BlockSpec((2, bd, tn), lambda j: (0, 0, j)),
            pl.BlockSpec((2, bd, tn), lambda j: (0, 0, j)),
        ],
        out_shape=[
            jax.ShapeDtypeStruct((bd, D_FF), BF16),
            jax.ShapeDtypeStruct((2, bd, D_FF), F32),
            jax.ShapeDtypeStruct((2, bd, D_FF), F32),
        ],
        scratch_shapes=[pltpu.VMEM((bd, k), BF16)],
        compiler_params=_params("arbitrary"),
        name="ffn_up_step",
    )(x, g.reshape(1, k), w_up, w_up, conv_w, conv_w, conv_b3, conv_b3, hist_t, hist_t)
    return act, jnp.concatenate([new_g, new_v], axis=-1)


TM = 1024
TN = 1024
TM_POST = 512
TK_POST = 2048
TN_MERGE = 512
TN_FFN = 512
XATTN_ROWS = 512


def _bf16_weights(w_in, pool_w, gla_w_a2, w_br_pool, w_br_dil, w_br_gla, w_mix_out, w_xq, w_xkv, w_xo, w_up, w_down):
    w_in_t, w_ga_t = prep_w_in(jnp.swapaxes(w_in, 1, 2), tn=TN)
    return dict(
        w_in_t=w_in_t,
        w_ga_t=w_ga_t,
        w_a2p=jnp.pad(gla_w_a2, ((0, 0), (0, LANES - GLA_RANK), (0, 0))).astype(BF16),
        pool_w=pool_w.astype(BF16),
        w_br_pool=w_br_pool.astype(BF16),
        w_br_dil=w_br_dil.astype(BF16),
        w_br_gla=w_br_gla.astype(BF16),
        w_mix_out=w_mix_out.astype(BF16),
        w_xq=w_xq.astype(BF16),
        w_xkv=w_xkv.astype(BF16),
        w_xo=w_xo.astype(BF16),
        w_up=w_up.astype(BF16),
        w_down=w_down.astype(BF16),
    )


def kernel(x_prompt, x_sample, state_pool, cache_dil1_kv, cache_dil2_kv, cache_dil3_kv, state_gla, cache_mem_kv, state_ffn_conv, mem_prompt, rel_bias, norm_mix_pre, norm_mix_post, w_in, pool_w, pool_scale, gla_w_a2, gla_b_a, gla_norm, w_br_pool, w_br_dil, w_br_gla, w_mix_out, norm_x_pre, norm_x_post, norm_mem, w_xq, w_xkv, w_xo, norm_ffn_pre, norm_ffn_post, w_up, conv_w, conv_b, w_down):
    b, seq, d = x_prompt.shape
    bd = x_sample.shape[0]
    depth = w_in.shape[0]
    m = b * seq
    assert x_sample.shape[1] == 1 and d == D_MODEL and w_in.shape[2] == N_MAIN + GLA_RANK + N_GATES
    slot_bias = _slot_biases(rel_bias)
    band_bias = _band_bias(slot_bias)
    dil_caches = (cache_dil1_kv, cache_dil2_kv, cache_dil3_kv)

    xp = x_prompt.reshape(m, d)
    xs = x_sample.reshape(bd, d)
    pool_p, gla_p, mem_p, conv_p = [], [], [], []
    pool_s, gla_s, conv_s = [], [], []
    dil_p = [[] for _ in range(3)]
    dil_s = [[] for _ in range(3)]
    seq_tiles = seq // TM

    w = _bf16_weights(w_in, pool_w, gla_w_a2, w_br_pool, w_br_dil, w_br_gla, w_mix_out, w_xq, w_xkv, w_xo, w_up, w_down)
    conv_b3 = conv_b.reshape(depth, 1, 2 * D_FF)
    mem_rows = mem_prompt.reshape(b * MEM_LEN, d)
    for l in range(depth):
        w_a2p = w["w_a2p"][l]

        zm, zkv_wide = in_proj(xp, norm_mix_pre[l], w["w_in_t"], l, tm=TM, tn=TN)
        za = norm_mm(xp, norm_mix_pre[l], w["w_ga_t"], l, tm=TM, tn=LANES, w_rows_are_outputs=True)
        z3 = zm.reshape(b, seq, N_PROJ)
        y_pool = pool_prompt(z3, w["pool_w"], l, pool_scale[l]).reshape(m, POOL_WIDTH)
        y_dil = dil_prompt(z3, band_bias).reshape(m, DIL_WIDTH)
        y_gla, gla_new = gla_prompt(z3, za.reshape(b, seq, LANES), w_a2p, gla_b_a[l], gla_norm[l])
        merged = branch_merge(y_pool, y_dil, y_gla.reshape(m, GLA_VAL_WIDTH), w["w_br_pool"], w["w_br_dil"], w["w_br_gla"], l,
                              zm, tm=TM, tn=TN_MERGE)
        xp = mm_post(merged, w["w_mix_out"], l, norm_mix_post[l], xp, tm=TM_POST, tk=TK_POST)
        mem_kv = norm_mm(mem_rows, norm_mem[l], w["w_xkv"], l, tm=TM, tn=TN)
        q = norm_mm(xp, norm_x_pre[l], w["w_xq"], l, tm=TM, tn=TN, out_dtype=BF16)
        o = xattn(q.reshape(b, seq, X_WIDTH), mem_kv.reshape(b, MEM_LEN, 2 * X_WIDTH), tt=XATTN_ROWS)
        xp = mm_post(o.reshape(m, X_WIDTH), w["w_xo"], l, norm_x_post[l], xp, tm=TM_POST, tk=TK_POST)
        act, tails = ffn_up_prompt(xp, norm_ffn_pre[l], w["w_up"], conv_w, conv_b3, l, seq=seq, tm=TM, tn=TN_FFN)
        xp = mm_post(act, w["w_down"], l, norm_ffn_post[l], xp, tm=TM_POST, tk=TK_POST)

        pool_p.append(z3[:, seq - POOL_HIST:, COL_POOL:COL_POOL + POOL_WIDTH])
        for g, (win, _) in enumerate(DIL_CONFIGS):
            keep = min(win, seq)
            if g == 2:
                kv = zkv_wide.reshape(b, seq, 2 * DIL_WIDTH)[:, seq - keep:]
            else:
                kv = z3[:, seq - keep:, COL_DIL_K[g]:COL_DIL_K[g] + 2 * DIL_WIDTH]
            dil_p[g].append(kv.reshape(b, keep, 2, DIL_HEADS, DIL_HEAD_DIM))
        gla_p.append(gla_new)
        mem_p.append(mem_kv.reshape(b, MEM_LEN, 2, X_HEADS, X_HEAD_DIM))
        conv_p.append(tails.reshape(b, seq_tiles, FFN_TAIL, 2 * D_FF)[:, seq_tiles - 1, FFN_TAIL - 2:, :])

        zs, _ = in_proj(xs, norm_mix_pre[l], w["w_in_t"], l, tm=bd, tn=TN)
        zas = norm_mm(xs, norm_mix_pre[l], w["w_ga_t"], l, tm=bd, tn=LANES, w_rows_are_outputs=True)
        y_pool_s, pool_new_t = pool_step(jnp.swapaxes(state_pool[l], 0, 1), zs, w["pool_w"], l, pool_scale[l])
        y_dil_s = dil_step(zs, dil_caches, l, slot_bias).astype(BF16)
        y_gla_s, gla_new_s = gla_step(zs, zas, state_gla, l, w_a2p, gla_b_a[l], gla_norm[l])
        merged_s = branch_merge(y_pool_s, y_dil_s, y_gla_s.astype(BF16), w["w_br_pool"], w["w_br_dil"], w["w_br_gla"], l,
                                zs, tm=bd, tn=TN_MERGE)
        xs = mm_post(merged_s, w["w_mix_out"], l, norm_mix_post[l], xs, tm=bd, tk=TK_POST)
        q_s = norm_mm(xs, norm_x_pre[l], w["w_xq"], l, tm=bd, tn=TN)
        o_s = xattn_step(q_s, cache_mem_kv, l).astype(BF16)
        xs = mm_post(o_s, w["w_xo"], l, norm_x_post[l], xs, tm=bd, tk=TK_POST)
        act_s, conv_new_t = ffn_up_step(xs, norm_ffn_pre[l], w["w_up"], conv_w, conv_b3, l,
                                        jnp.swapaxes(state_ffn_conv[l], 0, 1), tn=TN_FFN)
        xs = mm_post(act_s, w["w_down"], l, norm_ffn_post[l], xs, tm=bd, tk=TK_POST)

        pool_s.append(jnp.swapaxes(pool_new_t, 0, 1))
        for g in range(3):
            kv = zs[:, COL_DIL_K[g]:COL_DIL_K[g] + 2 * DIL_WIDTH]
            dil_s[g].append(kv.reshape(bd, 1, 2, DIL_HEADS, DIL_HEAD_DIM))
        gla_s.append(gla_new_s)
        conv_s.append(jnp.swapaxes(conv_new_t, 0, 1))

    return (xp.reshape(b, seq, d), xs.reshape(bd, 1, d),
            jnp.stack(pool_p), jnp.stack(dil_p[0]), jnp.stack(dil_p[1]), jnp.stack(dil_p[2]), jnp.stack(gla_p), jnp.stack(mem_p), jnp.stack(conv_p),
            jnp.stack(pool_s), jnp.stack(dil_s[0]), jnp.stack(dil_s[1]), jnp.stack(dil_s[2]), jnp.stack(gla_s), jnp.stack(conv_s))
```

```python
import functools
import math

import jax
import jax.numpy as jnp
from jax import lax
from jax.experimental import pallas as pl
from jax.experimental.pallas import tpu as pltpu

F32 = jnp.float32
BF16 = jnp.bfloat16
EPS = 1e-6
NEG_INF = -1e30

VMEM_LIMIT_BYTES = 56 * 1024 * 1024
LANES = 128

D_MODEL = 2048
POOL_WINDOWS = (2, 4, 8, 16)
POOL_GROUP = 256
POOL_WIDTH = 1024
POOL_HIST = 15
DIL_CONFIGS = ((128, 1), (512, 4), (2048, 16))
DIL_HEADS = 8
DIL_HEAD_DIM = 128
DIL_WIDTH = 1024
DIL_BLOCK = 128
GLA_HEADS = 4
GLA_DK = 256
GLA_DV = 512
GLA_KEY_WIDTH = 1024
GLA_VAL_WIDTH = 2048
GLA_RANK = 16
GLA_TAU = 16.0
GLA_CHUNK = 64
REL_BUCKETS = 32
REL_MAX_DIST = 2048
MEM_LEN = 256
X_HEADS = 4
X_HEAD_DIM = 256
X_WIDTH = 1024
D_FF = 5632
PAST_LEN = 8192

COL_POOL = 0
COL_DIL_Q = tuple(1024 + 3072 * g for g in range(3))
COL_DIL_K = tuple(2048 + 3072 * g for g in range(3))
COL_DIL_V = tuple(3072 + 3072 * g for g in range(3))
COL_GQ = 10240
COL_GK = 11264
COL_GV = 12288
COL_GR = 14336
N_MAIN = 16384
N_GATES = 3 * D_MODEL
N_PROJ = N_MAIN + N_GATES


def _params(*semantics):
    return pltpu.CompilerParams(dimension_semantics=semantics, vmem_limit_bytes=VMEM_LIMIT_BYTES)


def _sigmoid(x):
    return 1.0 / (1.0 + jnp.exp(-x))


def _log_sigmoid(x):
    return jnp.minimum(x, 0.0) - jnp.log(1.0 + jnp.exp(-jnp.abs(x)))


def _gelu_tanh(x):
    return x * (0.5 * (1.0 + jnp.tanh(math.sqrt(2.0 / math.pi) * (x + 0.044715 * (x * x * x)))))


def _rms_scale(y, g):
    return y * lax.rsqrt(jnp.mean(y * y, axis=-1, keepdims=True) + EPS) * g


def _norm_rows(x_ref, g_ref, xn_ref, rows, chunk, dst_offset=0):
    g = g_ref[...]

    def body(c, carry):
        r0 = pl.multiple_of(c * chunk, chunk)
        xn_ref[pl.ds(dst_offset + r0, chunk), :] = _rms_scale(x_ref[pl.ds(r0, chunk), :], g).astype(BF16)
        return carry

    lax.fori_loop(0, rows // chunk, body, 0)


def _prep_w_in_kernel(w_ref, wx_ref, o_ref, ga_ref, *, n_plain, shift):
    j = pl.program_id(1)

    @pl.when(j < n_plain)
    def _():
        o_ref[...] = w_ref[...].astype(BF16)

    @pl.when(j >= n_plain)
    def _():
        tn = o_ref.shape[0]
        o_ref[0:tn - shift, :] = w_ref[shift:tn, :].astype(BF16)
        o_ref[tn - shift:tn, :] = wx_ref[...].astype(BF16)

    @pl.when(j == n_plain)
    def _():
        ga_ref[0:shift, :] = w_ref[0:shift, :].astype(BF16)
        ga_ref[shift:LANES, :] = jnp.zeros((LANES - shift, ga_ref.shape[1]), BF16)


def prep_w_in(w_in_t, *, tn):
    depth, _, k = w_in_t.shape
    n_plain = N_MAIN // tn
    nj = (N_MAIN + N_GATES) // tn
    return pl.pallas_call(
        functools.partial(_prep_w_in_kernel, n_plain=n_plain, shift=GLA_RANK),
        grid=(depth, nj),
        in_specs=[
            pl.BlockSpec((None, tn, k), lambda l, j: (l, j, 0)),
            pl.BlockSpec((None, GLA_RANK, k), lambda l, j: (l, jnp.maximum(j + 1, n_plain) * (tn // GLA_RANK), 0)),
        ],
        out_specs=[
            pl.BlockSpec((None, tn, k), lambda l, j: (l, j, 0)),
            pl.BlockSpec((None, LANES, k), lambda l, j: (l, 0, 0)),
        ],
        out_shape=[jax.ShapeDtypeStruct((depth, N_MAIN + N_GATES, k), BF16), jax.ShapeDtypeStruct((depth, LANES, k), BF16)],
        compiler_params=_params("parallel", "arbitrary"),
        name="prep_w_in",
    )(w_in_t, w_in_t)


def _in_proj_kernel(x_ref, g_ref, w_ref, o_ref, kv_ref, xn_ref, *, tm, chunk, kv_lo, kv_hi):
    j = pl.program_id(1)

    @pl.when(j == 0)
    def _():
        _norm_rows(x_ref, g_ref, xn_ref, tm, chunk)

    o_ref[...] = lax.dot_general(xn_ref[...], w_ref[...], (((1,), (1,)), ((), ())), preferred_element_type=F32)

    @pl.when(jnp.logical_and(j >= kv_lo, j < kv_hi))
    def _():
        kv_ref[...] = o_ref[...]


def in_proj(x, g, w_t, layer, *, tm, tn):
    m, k = x.shape
    n = w_t.shape[1]
    tm = min(tm, m)
    assert m % tm == 0 and n % tn == 0 and COL_DIL_K[2] % tn == 0
    chunk = min(256, tm)
    kv_lo = COL_DIL_K[2] // tn
    n_kv = 2 * DIL_WIDTH // tn
    return pl.pallas_call(
        functools.partial(_in_proj_kernel, tm=tm, chunk=chunk, kv_lo=kv_lo, kv_hi=kv_lo + n_kv),
        grid=(m // tm, n // tn),
        in_specs=[
            pl.BlockSpec((tm, k), lambda i, j: (i, 0)),
            pl.BlockSpec((1, k), lambda i, j: (0, 0)),
            pl.BlockSpec((None, tn, k), lambda i, j: (layer, j, 0)),
        ],
        out_specs=[
            pl.BlockSpec((tm, tn), lambda i, j: (i, j)),
            pl.BlockSpec((tm, tn), lambda i, j: (i, jnp.clip(j - kv_lo, 0, n_kv - 1))),
        ],
        out_shape=[jax.ShapeDtypeStruct((m, n), F32), jax.ShapeDtypeStruct((m, 2 * DIL_WIDTH), F32)],
        scratch_shapes=[pltpu.VMEM((tm, k), BF16)],
        compiler_params=_params("parallel", "arbitrary"),
        name="in_proj",
    )(x, g.reshape(1, k), w_t)


def _norm_mm_kernel(x_ref, g_ref, w_ref, o_ref, xn_ref, *, tm, chunk, w_rows_are_outputs):
    @pl.when(pl.program_id(1) == 0)
    def _():
        _norm_rows(x_ref, g_ref, xn_ref, tm, chunk)

    contract = (((1,), (1,)), ((), ())) if w_rows_are_outputs else (((1,), (0,)), ((), ()))
    o_ref[...] = lax.dot_general(xn_ref[...], w_ref[...], contract, preferred_element_type=F32).astype(o_ref.dtype)


def norm_mm(x, g, w, layer, *, tm, tn, out_dtype=F32, w_rows_are_outputs=False):
    m, k = x.shape
    n = w.shape[1] if w_rows_are_outputs else w.shape[2]
    tm = min(tm, m)
    tn = min(tn, n)
    assert m % tm == 0 and n % tn == 0
    chunk = min(256, tm)
    if w_rows_are_outputs:
        w_spec = pl.BlockSpec((None, tn, k), lambda i, j: (layer, j, 0))
    else:
        w_spec = pl.BlockSpec((None, k, tn), lambda i, j: (layer, 0, j))
    return pl.pallas_call(
        functools.partial(_norm_mm_kernel, tm=tm, chunk=chunk, w_rows_are_outputs=w_rows_are_outputs),
        grid=(m // tm, n // tn),
        in_specs=[
            pl.BlockSpec((tm, k), lambda i, j: (i, 0)),
            pl.BlockSpec((1, k), lambda i, j: (0, 0)),
            w_spec,
        ],
        out_specs=pl.BlockSpec((tm, tn), lambda i, j: (i, j)),
        out_shape=jax.ShapeDtypeStruct((m, n), out_dtype),
        scratch_shapes=[pltpu.VMEM((tm, k), BF16)],
        compiler_params=_params("parallel", "arbitrary"),
        name="norm_mm",
    )(x, g.reshape(1, k), w)


def _mm_post_kernel(a_ref, w_ref, g_ref, res_ref, o_ref, acc_ref, *, nk):
    kk = pl.program_id(1)
    part = jnp.dot(a_ref[...], w_ref[...], preferred_element_type=F32)
    if nk == 1:
        o_ref[...] = res_ref[...] + _rms_scale(part, g_ref[...])
        return

    @pl.when(kk == 0)
    def _():
        acc_ref[...] = part

    @pl.when(jnp.logical_and(kk > 0, kk < nk - 1))
    def _():
        acc_ref[...] += part

    @pl.when(kk == nk - 1)
    def _():
        o_ref[...] = res_ref[...] + _rms_scale(acc_ref[...] + part, g_ref[...])


def mm_post(a, w, layer, g, res, *, tm, tk):
    m, k = a.shape
    n = w.shape[2]
    tm = min(tm, m)
    tk = k if k <= tk else k // 2
    assert m % tm == 0 and k % tk == 0 and tk % LANES == 0
    return pl.pallas_call(
        functools.partial(_mm_post_kernel, nk=k // tk),
        grid=(m // tm, k // tk),
        in_specs=[
            pl.BlockSpec((tm, tk), lambda i, kk: (i, kk)),
            pl.BlockSpec((None, tk, n), lambda i, kk: (layer, kk, 0)),
            pl.BlockSpec((1, n), lambda i, kk: (0, 0)),
            pl.BlockSpec((tm, n), lambda i, kk: (i, 0)),
        ],
        out_specs=pl.BlockSpec((tm, n), lambda i, kk: (i, 0)),
        out_shape=jax.ShapeDtypeStruct((m, n), F32),
        scratch_shapes=[pltpu.VMEM((tm, n), F32)],
        compiler_params=_params("parallel", "arbitrary"),
        name="mm_post",
    )(a, w, g.reshape(1, n), res)


def _branch_merge_kernel(yp_ref, yd_ref, yg_ref, wp_ref, wd_ref, wg_ref, g0_ref, g1_ref, g2_ref, o_ref):
    bp = jnp.dot(yp_ref[...], wp_ref[...], preferred_element_type=F32)
    bd = jnp.dot(yd_ref[...], wd_ref[...], preferred_element_type=F32)
    bg = jnp.dot(yg_ref[...], wg_ref[...], preferred_element_type=F32)
    merged = _sigmoid(g0_ref[...]) * bp + _sigmoid(g1_ref[...]) * bd + _sigmoid(g2_ref[...]) * bg
    o_ref[...] = merged.astype(o_ref.dtype)


def branch_merge(y_pool, y_dil, y_gla, w_pool, w_dil, w_gla, layer, z_main, *, tm, tn):
    m = y_pool.shape[0]
    tm = min(tm, m)
    assert m % tm == 0 and D_MODEL % tn == 0
    gate_blk = [(N_MAIN + b * D_MODEL) // tn for b in range(3)]

    def gate_spec(b):
        return pl.BlockSpec((tm, tn), lambda i, j: (i, gate_blk[b] + j))

    return pl.pallas_call(
        _branch_merge_kernel,
        grid=(m // tm, D_MODEL // tn),
        in_specs=[
            pl.BlockSpec((tm, POOL_WIDTH), lambda i, j: (i, 0)),
            pl.BlockSpec((tm, DIL_WIDTH), lambda i, j: (i, 0)),
            pl.BlockSpec((tm, GLA_VAL_WIDTH), lambda i, j: (i, 0)),
            pl.BlockSpec((None, POOL_WIDTH, tn), lambda i, j: (layer, 0, j)),
            pl.BlockSpec((None, DIL_WIDTH, tn), lambda i, j: (layer, 0, j)),
            pl.BlockSpec((None, GLA_VAL_WIDTH, tn), lambda i, j: (layer, 0, j)),
            gate_spec(0),
            gate_spec(1),
            gate_spec(2),
        ],
        out_specs=pl.BlockSpec((tm, tn), lambda i, j: (i, j)),
        out_shape=jax.ShapeDtypeStruct((m, D_MODEL), BF16),
        compiler_params=_params("parallel", "arbitrary"),
        name="branch_merge",
    )(y_pool, y_dil, y_gla, w_pool, w_dil, w_gla, z_main, z_main, z_main)


def _pool_prompt_kernel(u_ref, w_ref, s_ref, o_ref, buf_a, buf_b, *, seq):
    pad = POOL_HIST + 1
    zeros = jnp.zeros((pad, POOL_GROUP), F32)
    t = lax.broadcasted_iota(jnp.int32, (seq, 1), 0)
    for g, win in enumerate(POOL_WINDOWS):
        cols = slice(g * POOL_GROUP, (g + 1) * POOL_GROUP)
        u = u_ref[:, cols]
        cur, nxt = buf_a, buf_b
        cur[0:pad, :] = zeros
        nxt[0:pad, :] = zeros
        cur[pad:pad + seq, :] = u
        k = 1
        while k < win:
            nxt[pad:pad + seq, :] = cur[pad:pad + seq, :] + cur[pad - k:pad - k + seq, :]
            cur, nxt = nxt, cur
            k *= 2
        cnt = jnp.minimum(win, t + 1).astype(F32)
        d = cur[pad:pad + seq, :] / cnt - u
        y = jnp.dot(d.astype(BF16), w_ref[g], preferred_element_type=F32) * s_ref[:, cols]
        o_ref[:, cols] = y.astype(o_ref.dtype)


def pool_prompt(z3, pool_w, layer, pool_scale):
    b, seq, _ = z3.shape
    return pl.pallas_call(
        functools.partial(_pool_prompt_kernel, seq=seq),
        grid=(b,),
        in_specs=[
            pl.BlockSpec((None, seq, POOL_WIDTH), lambda i: (i, 0, COL_POOL // POOL_WIDTH)),
            pl.BlockSpec((None, len(POOL_WINDOWS), POOL_GROUP, POOL_GROUP), lambda i: (layer, 0, 0, 0)),
            pl.BlockSpec((1, POOL_WIDTH), lambda i: (0, 0)),
        ],
        out_specs=pl.BlockSpec((None, seq, POOL_WIDTH), lambda i: (i, 0, 0)),
        out_shape=jax.ShapeDtypeStruct((b, seq, POOL_WIDTH), BF16),
        scratch_shapes=[pltpu.VMEM((seq + POOL_HIST + 1, POOL_GROUP), F32)] * 2,
        compiler_params=_params("parallel"),
        name="pool_prompt",
    )(z3, pool_w, pool_scale.reshape(1, POOL_WIDTH))


def _pool_step_kernel(hist_ref, u_ref, w_ref, s_ref, y_ref, new_ref):
    u = u_ref[...]
    for r in range(POOL_HIST - 1):
        new_ref[r] = hist_ref[r + 1]
    new_ref[POOL_HIST - 1] = u
    for g, win in enumerate(POOL_WINDOWS):
        cols = slice(g * POOL_GROUP, (g + 1) * POOL_GROUP)
        ug = u[:, cols]
        acc = ug
        for r in range(POOL_HIST - (win - 1), POOL_HIST):
            acc = acc + hist_ref[r, :, cols]
        d = acc / float(win) - ug
        y = jnp.dot(d.astype(BF16), w_ref[g], preferred_element_type=F32) * s_ref[:, cols]
        y_ref[:, cols] = y.astype(y_ref.dtype)


def pool_step(hist_t, z_s, pool_w, layer, pool_scale):
    bd = z_s.shape[0]
    return pl.pallas_call(
        _pool_step_kernel,
        grid=(1,),
        in_specs=[
            pl.BlockSpec((POOL_HIST, bd, POOL_WIDTH), lambda i: (0, 0, 0)),
            pl.BlockSpec((bd, POOL_WIDTH), lambda i: (0, COL_POOL // POOL_WIDTH)),
            pl.BlockSpec((None, len(POOL_WINDOWS), POOL_GROUP, POOL_GROUP), lambda i: (layer, 0, 0, 0)),
            pl.BlockSpec((1, POOL_WIDTH), lambda i: (0, 0)),
        ],
        out_specs=[
            pl.BlockSpec((bd, POOL_WIDTH), lambda i: (0, 0)),
            pl.BlockSpec((POOL_HIST, bd, POOL_WIDTH), lambda i: (0, 0, 0)),
        ],
        out_shape=[
            jax.ShapeDtypeStruct((bd, POOL_WIDTH), BF16),
            jax.ShapeDtypeStruct((POOL_HIST, bd, POOL_WIDTH), F32),
        ],
        compiler_params=_params("arbitrary"),
        name="pool_step",
    )(hist_t, z_s, pool_w, pool_scale.reshape(1, POOL_WIDTH))


def _rel_bucket(dist):
    max_exact = REL_BUCKETS // 2
    scaled = jnp.log(jnp.maximum(dist, 1).astype(F32) / max_exact) / math.log(REL_MAX_DIST / max_exact)
    large = jnp.minimum(max_exact + (scaled * (REL_BUCKETS - max_exact)).astype(jnp.int32), REL_BUCKETS - 1)
    return jnp.where(dist < max_exact, dist, large)


def _select_rows(table_t, index, n):
    onehot = (index.reshape(1, -1) == jnp.arange(n, dtype=jnp.int32)[:, None]).astype(F32)
    out = jnp.dot(table_t, onehot, precision=lax.Precision.HIGHEST, preferred_element_type=F32)
    return out.reshape((table_t.shape[0],) + index.shape)


def _slot_biases(rel_bias):
    out = []
    for g, (win, dil) in enumerate(DIL_CONFIGS):
        dist = jnp.arange(win // dil + 1, dtype=jnp.int32) * dil
        table_t = rel_bias[:, g * DIL_HEADS:(g + 1) * DIL_HEADS].T.astype(F32)
        out.append(_select_rows(table_t, _rel_bucket(dist), REL_BUCKETS))
    return out


def _band_bias(slot_bias):
    qi = jnp.arange(DIL_BLOCK, dtype=jnp.int32)[:, None] + DIL_BLOCK
    ki = jnp.arange(2 * DIL_BLOCK, dtype=jnp.int32)[None, :]
    rel = qi - ki
    out = []
    for g, (win, dil) in enumerate(DIL_CONFIGS):
        n_slots = win // dil
        ok = (rel >= 0) & (rel <= n_slots)
        b = _select_rows(slot_bias[g], jnp.clip(rel, 0, n_slots), n_slots + 1)
        out.append(jnp.where(ok[None], b, NEG_INF))
    return jnp.stack(out, axis=0)


def _dil_prompt_kernel(q0, k0, v0, q1, k1, v1, q2, k2, v2, bias_ref, o_ref, o_scr, lse_scr, *, seq):
    qs, ks, vs = (q0, q1, q2), (k0, k1, k2), (v0, v1, v2)
    scale = DIL_HEAD_DIM ** -0.5
    blk = DIL_BLOCK
    nt = (((1,), (1,)), ((), ()))

    def rows(ref, start, dil):
        if dil == 1:
            return ref[pl.ds(start, blk), :]
        return ref[pl.ds(start, blk, stride=dil), :]

    for g, (_, dil) in enumerate(DIL_CONFIGS):
        nb = seq // dil // blk
        for r in range(dil):
            for ub in range(nb):
                start = r + dil * ub * blk
                q = rows(qs[g], start, dil).astype(BF16)
                kc = rows(ks[g], start, dil).astype(BF16)
                vc = rows(vs[g], start, dil).astype(BF16)
                if ub == 0:
                    kk, vv, bias = kc, vc, bias_ref[g, :, blk:]
                else:
                    prev = start - dil * blk
                    kk = jnp.concatenate([rows(ks[g], prev, dil).astype(BF16), kc], axis=0)
                    vv = jnp.concatenate([rows(vs[g], prev, dil).astype(BF16), vc], axis=0)
                    bias = bias_ref[g]
                s = lax.dot_general(q, kk, nt, preferred_element_type=F32) * scale + bias
                m = jnp.max(s, axis=-1, keepdims=True)
                e = jnp.exp(s - m)
                l = jnp.sum(e, axis=-1, keepdims=True)
                o = jnp.dot(e.astype(BF16), vv, preferred_element_type=F32) / l
                lse = jnp.broadcast_to(m + jnp.log(l), (blk, LANES))
                if dil == 1:
                    o_scr[g, pl.ds(start, blk), :] = o
                    lse_scr[g, pl.ds(start, blk), :] = lse
                else:
                    o_scr[g, pl.ds(start, blk, stride=dil), :] = o
                    lse_scr[g, pl.ds(start, blk, stride=dil), :] = lse

    chunk = 256

    def combine(c, carry):
        sl = pl.ds(pl.multiple_of(c * chunk, chunk), chunk)
        l0, l1, l2 = lse_scr[0, sl, :], lse_scr[1, sl, :], lse_scr[2, sl, :]
        mx = jnp.maximum(jnp.maximum(l0, l1), l2)
        w0, w1, w2 = jnp.exp(l0 - mx), jnp.exp(l1 - mx), jnp.exp(l2 - mx)
        y = (w0 * o_scr[0, sl, :] + w1 * o_scr[1, sl, :] + w2 * o_scr[2, sl, :]) / (w0 + w1 + w2)
        o_ref[sl, :] = y.astype(o_ref.dtype)
        return carry

    lax.fori_loop(0, seq // chunk, combine, 0)


def dil_prompt(z3, band_bias):
    b, seq, _ = z3.shape
    assert seq % (DIL_BLOCK * 16) == 0

    def col_spec(col):
        blk0 = col // DIL_HEAD_DIM
        return pl.BlockSpec((None, seq, DIL_HEAD_DIM), lambda i, h: (i, 0, blk0 + h))

    in_specs = []
    for g in range(3):
        in_specs += [col_spec(COL_DIL_Q[g]), col_spec(COL_DIL_K[g]), col_spec(COL_DIL_V[g])]
    in_specs.append(pl.BlockSpec((3, None, DIL_BLOCK, 2 * DIL_BLOCK), lambda i, h: (0, h, 0, 0)))
    return pl.pallas_call(
        functools.partial(_dil_prompt_kernel, seq=seq),
        grid=(b, DIL_HEADS),
        in_specs=in_specs,
        out_specs=pl.BlockSpec((None, seq, DIL_HEAD_DIM), lambda i, h: (i, 0, h)),
        out_shape=jax.ShapeDtypeStruct((b, seq, DIL_WIDTH), BF16),
        scratch_shapes=[pltpu.VMEM((3, seq, DIL_HEAD_DIM), F32), pltpu.VMEM((3, seq, LANES), F32)],
        compiler_params=_params("parallel", "arbitrary"),
        name="dil_prompt",
    )(*([z3] * 9), band_bias)


def _dil_step_kernel(qkv_ref, c0, c1, c2, bias_ref, bias0_ref, o_ref):
    caches = (c0, c1, c2)
    scale = DIL_HEAD_DIM ** -0.5
    outs, lses = [], []
    for g in range(3):
        q, kn, vn = qkv_ref[3 * g], qkv_ref[3 * g + 1], qkv_ref[3 * g + 2]
        kc = caches[g][:, 0]
        vc = caches[g][:, 1]
        s = jnp.sum(kc * q[None], axis=-1, keepdims=True) * scale + bias_ref[g]
        s_new = jnp.sum(q * kn, axis=-1, keepdims=True) * scale + bias0_ref[g]
        m = jnp.maximum(jnp.max(s, axis=0), s_new)
        p = jnp.exp(s - m[None])
        p_new = jnp.exp(s_new - m)
        l = jnp.sum(p, axis=0) + p_new
        outs.append((jnp.sum(p * vc, axis=0) + p_new * vn) / l)
        lses.append(m + jnp.log(l))
    mx = jnp.maximum(jnp.maximum(lses[0], lses[1]), lses[2])
    w0, w1, w2 = jnp.exp(lses[0] - mx), jnp.exp(lses[1] - mx), jnp.exp(lses[2] - mx)
    o_ref[...] = (w0 * outs[0] + w1 * outs[1] + w2 * outs[2]) / (w0 + w1 + w2)


def dil_step(z_s, caches, layer, slot_bias):
    bd = z_s.shape[0]
    n_slots = DIL_CONFIGS[0][0] // DIL_CONFIGS[0][1]
    qkv = z_s[:, COL_DIL_Q[0]:COL_DIL_V[2] + DIL_WIDTH].reshape(bd, 9, DIL_HEADS, DIL_HEAD_DIM)
    in_specs = [pl.BlockSpec((None, 9, DIL_HEADS, DIL_HEAD_DIM), lambda i: (i, 0, 0, 0))]
    cache_views = []
    for g, (win, dil) in enumerate(DIL_CONFIGS):
        depth = caches[g].shape[0]
        assert caches[g].shape[2] == win and win // dil == n_slots
        cache_views.append(caches[g].reshape(depth, bd, n_slots, dil, 2, DIL_HEADS, DIL_HEAD_DIM))
        in_specs.append(pl.BlockSpec((None, None, n_slots, None, 2, DIL_HEADS, DIL_HEAD_DIM),
                                     lambda i: (layer, i, 0, 0, 0, 0, 0)))
    bias_rows = jnp.stack([sb[:, n_slots:0:-1].T for sb in slot_bias], axis=0)
    bias_rows = jnp.broadcast_to(bias_rows[..., None], (3, n_slots, DIL_HEADS, LANES))
    bias_new = jnp.broadcast_to(jnp.stack([sb[:, 0] for sb in slot_bias], axis=0)[..., None], (3, DIL_HEADS, LANES))
    in_specs.append(pl.BlockSpec((3, n_slots, DIL_HEADS, LANES), lambda i: (0, 0, 0, 0)))
    in_specs.append(pl.BlockSpec((3, DIL_HEADS, LANES), lambda i: (0, 0, 0)))
    out = pl.pallas_call(
        _dil_step_kernel,
        grid=(bd,),
        in_specs=in_specs,
        out_specs=pl.BlockSpec((None, DIL_HEADS, DIL_HEAD_DIM), lambda i: (i, 0, 0)),
        out_shape=jax.ShapeDtypeStruct((bd, DIL_HEADS, DIL_HEAD_DIM), F32),
        compiler_params=_params("parallel"),
        name="dil_step",
    )(qkv, *cache_views, bias_rows, bias_new)
    return out.reshape(bd, DIL_WIDTH)


GLA_HEADS_PER_STEP = 2


def _gla_prompt_kernel(q_ref, k_ref, v_ref, r_ref, ga_ref, wa_ref, ba_ref, gn_ref, y_ref, s_out_ref, st_ref, o_scr, *, tq):
    t = pl.program_id(2)
    ck = GLA_CHUNK

    @pl.when(t == 0)
    def _():
        st_ref[...] = jnp.zeros_like(st_ref)

    la = jnp.dot(ga_ref[...].astype(BF16), wa_ref[...], preferred_element_type=F32) + ba_ref[...]
    log_a = _log_sigmoid(la) / GLA_TAU
    row = lax.broadcasted_iota(jnp.int32, (ck, ck), 0)
    col = lax.broadcasted_iota(jnp.int32, (ck, ck), 1)
    tril = row >= col
    tril_b = tril.astype(BF16)
    log_a_hi = log_a.astype(BF16)
    log_a_lo = (log_a - log_a_hi.astype(F32)).astype(BF16)
    nt = (((1,), (1,)), ((), ()))
    tn = (((0,), (0,)), ((), ()))
    for c in range(tq // ck):
        sl = slice(c * ck, (c + 1) * ck)
        for hh in range(GLA_HEADS_PER_STEP):
            ks = slice(hh * GLA_DK, (hh + 1) * GLA_DK)
            vs = slice(hh * GLA_DV, (hh + 1) * GLA_DV)
            b = (jnp.dot(tril_b, log_a_hi[sl, ks], preferred_element_type=F32)
                 + jnp.dot(tril_b, log_a_lo[sl, ks], preferred_element_type=F32))
            b_last = b[ck - 1:ck, :]
            q = q_ref[sl, ks] * (GLA_DK ** -0.5)
            k = k_ref[sl, ks]
            v = v_ref[sl, vs].astype(BF16)
            q_t = (q * jnp.exp(b)).astype(BF16)
            k_t = (k * jnp.exp(-b)).astype(BF16)
            k_h = (k * jnp.exp(b_last - b)).astype(BF16)
            decay = jnp.exp(b_last)
            a = lax.dot_general(q_t, k_t, nt, preferred_element_type=F32)
            a = jnp.where(tril, a, 0.0)
            st = st_ref[hh]
            o = jnp.dot(a.astype(BF16), v, preferred_element_type=F32)
            o = o + lax.dot_general(q_t, st.astype(BF16), nt, preferred_element_type=F32)
            o_scr[sl, vs] = o
            st_ref[hh] = decay * st + lax.dot_general(v, k_h, tn, preferred_element_type=F32)

    for hh in range(GLA_HEADS_PER_STEP):
        vs = slice(hh * GLA_DV, (hh + 1) * GLA_DV)
        o = _rms_scale(o_scr[:, vs], gn_ref[:, vs])
        r = r_ref[:, vs]
        y_ref[:, vs] = (o * (r * _sigmoid(r))).astype(y_ref.dtype)

    @pl.when(t == pl.num_programs(2) - 1)
    def _():
        for hh in range(GLA_HEADS_PER_STEP):
            s_out_ref[hh] = st_ref[hh].T


def gla_prompt(z3, za3, w_a2p, b_a, gla_norm, *, tq=1024):
    b, seq, _ = z3.shape
    hp = GLA_HEADS_PER_STEP
    kw, vw = hp * GLA_DK, hp * GLA_DV
    assert seq % tq == 0 and tq % GLA_CHUNK == 0 and GLA_HEADS % hp == 0
    return pl.pallas_call(
        functools.partial(_gla_prompt_kernel, tq=tq),
        grid=(b, GLA_HEADS // hp, seq // tq),
        in_specs=[
            pl.BlockSpec((None, tq, kw), lambda i, h, t: (i, t, COL_GQ // kw + h)),
            pl.BlockSpec((None, tq, kw), lambda i, h, t: (i, t, COL_GK // kw + h)),
            pl.BlockSpec((None, tq, vw), lambda i, h, t: (i, t, COL_GV // vw + h)),
            pl.BlockSpec((None, tq, vw), lambda i, h, t: (i, t, COL_GR // vw + h)),
            pl.BlockSpec((None, tq, LANES), lambda i, h, t: (i, t, 0)),
            pl.BlockSpec((LANES, kw), lambda i, h, t: (0, h)),
            pl.BlockSpec((1, kw), lambda i, h, t: (0, h)),
            pl.BlockSpec((1, vw), lambda i, h, t: (0, h)),
        ],
        out_specs=[
            pl.BlockSpec((None, tq, vw), lambda i, h, t: (i, t, h)),
            pl.BlockSpec((None, hp, GLA_DK, GLA_DV), lambda i, h, t: (i, h, 0, 0)),
        ],
        out_shape=[
            jax.ShapeDtypeStruct((b, seq, GLA_VAL_WIDTH), BF16),
            jax.ShapeDtypeStruct((b, GLA_HEADS, GLA_DK, GLA_DV), F32),
        ],
        scratch_shapes=[pltpu.VMEM((hp, GLA_DV, GLA_DK), F32), pltpu.VMEM((tq, vw), F32)],
        compiler_params=_params("parallel", "parallel", "arbitrary"),
        name="gla_prompt",
    )(z3, z3, z3, z3, za3, w_a2p, b_a.reshape(1, GLA_KEY_WIDTH), gla_norm.reshape(1, GLA_VAL_WIDTH))


def _gla_step_kernel(q_ref, k_ref, v_ref, r_ref, ga_ref, wa_ref, ba_ref, gn_ref, s_ref, y_ref, s_out_ref):
    ga8 = jnp.broadcast_to(ga_ref[...], (8, LANES)).astype(BF16)
    la = jnp.dot(ga8, wa_ref[...], preferred_element_type=F32)[0:1, :] + ba_ref[...]
    ea = jnp.exp(_log_sigmoid(la) / GLA_TAU)
    q = q_ref[...] * (GLA_DK ** -0.5)
    k = k_ref[...]
    eye = lax.broadcasted_iota(jnp.int32, (GLA_DK, GLA_DK), 0) == lax.broadcasted_iota(jnp.int32, (GLA_DK, GLA_DK), 1)

    def column(row):
        return jnp.sum(jnp.where(eye, jnp.broadcast_to(row, (GLA_DK, GLA_DK)), 0.0), axis=-1, keepdims=True)

    for h in range(GLA_HEADS):
        ks = slice(h * GLA_DK, (h + 1) * GLA_DK)
        vs = slice(h * GLA_DV, (h + 1) * GLA_DV)
        s_new = column(ea[:, ks]) * s_ref[h] + column(k[:, ks]) * v_ref[:, vs]
        s_out_ref[h] = s_new
        o = jnp.sum(column(q[:, ks]) * s_new, axis=0, keepdims=True)
        o = _rms_scale(o, gn_ref[:, vs])
        r = r_ref[:, vs]
        y_ref[:, vs] = o * (r * _sigmoid(r))


def gla_step(z_s, za_s, state, layer, w_a2p, b_a, gla_norm):
    bd = z_s.shape[0]
    zs3 = z_s.reshape(bd, 1, N_PROJ)
    y, s_new = pl.pallas_call(
        _gla_step_kernel,
        grid=(bd,),
        in_specs=[
            pl.BlockSpec((None, 1, GLA_KEY_WIDTH), lambda i: (i, 0, COL_GQ // GLA_KEY_WIDTH)),
            pl.BlockSpec((None, 1, GLA_KEY_WIDTH), lambda i: (i, 0, COL_GK // GLA_KEY_WIDTH)),
            pl.BlockSpec((None, 1, GLA_VAL_WIDTH), lambda i: (i, 0, COL_GV // GLA_VAL_WIDTH)),
            pl.BlockSpec((None, 1, GLA_VAL_WIDTH), lambda i: (i, 0, COL_GR // GLA_VAL_WIDTH)),
            pl.BlockSpec((None, 1, LANES), lambda i: (i, 0, 0)),
            pl.BlockSpec((LANES, GLA_KEY_WIDTH), lambda i: (0, 0)),
            pl.BlockSpec((1, GLA_KEY_WIDTH), lambda i: (0, 0)),
            pl.BlockSpec((1, GLA_VAL_WIDTH), lambda i: (0, 0)),
            pl.BlockSpec((None, None, GLA_HEADS, GLA_DK, GLA_DV), lambda i: (layer, i, 0, 0, 0)),
        ],
        out_specs=[
            pl.BlockSpec((None, 1, GLA_VAL_WIDTH), lambda i: (i, 0, 0)),
            pl.BlockSpec((None, GLA_HEADS, GLA_DK, GLA_DV), lambda i: (i, 0, 0, 0)),
        ],
        out_shape=[
            jax.ShapeDtypeStruct((bd, 1, GLA_VAL_WIDTH), F32),
            jax.ShapeDtypeStruct((bd, GLA_HEADS, GLA_DK, GLA_DV), F32),
        ],
        compiler_params=_params("parallel"),
        name="gla_step",
    )(zs3, zs3, zs3, zs3, za_s.reshape(bd, 1, LANES), w_a2p, b_a.reshape(1, GLA_KEY_WIDTH),
      gla_norm.reshape(1, GLA_VAL_WIDTH), state)
    return y.reshape(bd, GLA_VAL_WIDTH), s_new


def _xattn_kernel(q_ref, kv_ref, o_ref):
    scale = X_HEAD_DIM ** -0.5
    nt = (((1,), (1,)), ((), ()))
    for h in range(X_HEADS):
        hs = slice(h * X_HEAD_DIM, (h + 1) * X_HEAD_DIM)
        k = kv_ref[:, h * X_HEAD_DIM:(h + 1) * X_HEAD_DIM].astype(BF16)
        v = kv_ref[:, X_WIDTH + h * X_HEAD_DIM:X_WIDTH + (h + 1) * X_HEAD_DIM].astype(BF16)
        s = lax.dot_general(q_ref[:, hs], k, nt, preferred_element_type=F32) * scale
        e = jnp.exp(s - jnp.max(s, axis=-1, keepdims=True))
        l = jnp.sum(e, axis=-1, keepdims=True)
        o = jnp.dot(e.astype(BF16), v, preferred_element_type=F32) / l
        o_ref[:, hs] = o.astype(o_ref.dtype)


def xattn(q3, mem_kv, *, tt):
    b, t, _ = q3.shape
    tt = min(tt, t)
    assert t % tt == 0
    return pl.pallas_call(
        _xattn_kernel,
        grid=(b, t // tt),
        in_specs=[
            pl.BlockSpec((None, tt, X_WIDTH), lambda i, j: (i, j, 0)),
            pl.BlockSpec((None, MEM_LEN, 2 * X_WIDTH), lambda i, j: (i, 0, 0)),
        ],
        out_specs=pl.BlockSpec((None, tt, X_WIDTH), lambda i, j: (i, j, 0)),
        out_shape=jax.ShapeDtypeStruct((b, t, X_WIDTH), BF16),
        compiler_params=_params("parallel", "arbitrary"),
        name="xattn",
    )(q3, mem_kv)


def _xattn_step_kernel(q_ref, kv_ref, o_ref):
    q = q_ref[...]
    k = kv_ref[:, 0]
    v = kv_ref[:, 1]
    s = jnp.sum(k * q[None], axis=-1, keepdims=True) * (X_HEAD_DIM ** -0.5)
    p = jnp.exp(s - jnp.max(s, axis=0)[None])
    o_ref[...] = jnp.sum(p * v, axis=0) / jnp.sum(p, axis=0)


def xattn_step(q, mem_kv, layer):
    bd = q.shape[0]
    out = pl.pallas_call(
        _xattn_step_kernel,
        grid=(bd,),
        in_specs=[
            pl.BlockSpec((None, X_HEADS, X_HEAD_DIM), lambda i: (i, 0, 0)),
            pl.BlockSpec((None, None, MEM_LEN, 2, X_HEADS, X_HEAD_DIM), lambda i: (layer, i, 0, 0, 0, 0)),
        ],
        out_specs=pl.BlockSpec((None, X_HEADS, X_HEAD_DIM), lambda i: (i, 0, 0)),
        out_shape=jax.ShapeDtypeStruct((bd, X_HEADS, X_HEAD_DIM), F32),
        compiler_params=_params("parallel"),
        name="xattn_step",
    )(q.reshape(bd, X_HEADS, X_HEAD_DIM), mem_kv)
    return out.reshape(bd, X_WIDTH)


FFN_HALO = 16
FFN_TAIL = 8


def _ffn_up_prompt_kernel(x_ref, xh_ref, g_ref, wg_ref, wv_ref, cwg_ref, cwv_ref, cbg_ref, cbv_ref,
                          act_ref, tg_ref, tv_ref, xn_ref, ug_ref, uv_ref, *, tm, seq_tiles, chunk):
    i = pl.program_id(0)

    @pl.when(pl.program_id(1) == 0)
    def _():
        _norm_rows(x_ref, g_ref, xn_ref, tm, chunk, dst_offset=FFN_HALO)
        halo = _rms_scale(xh_ref[...], g_ref[...])
        halo = jnp.where(i % seq_tiles == 0, 0.0, halo)
        xn_ref[0:FFN_HALO, :] = halo.astype(BF16)

    xn = xn_ref[...]
    ug_ref[...] = jnp.dot(xn, wg_ref[...], preferred_element_type=F32)
    uv_ref[...] = jnp.dot(xn, wv_ref[...], preferred_element_type=F32)

    def conv(u_ref, cw_ref, cb_ref):
        lead = FFN_TAIL
        u = u_ref[FFN_HALO - lead:FFN_HALO + tm, :]
        c = cb_ref[...] + cw_ref[0:1, :] * pltpu.roll(u, 2, 0) + cw_ref[1:2, :] * pltpu.roll(u, 1, 0) + cw_ref[2:3, :] * u
        return c[lead:lead + tm]

    act_ref[...] = (_gelu_tanh(conv(ug_ref, cwg_ref, cbg_ref)) * conv(uv_ref, cwv_ref, cbv_ref)).astype(act_ref.dtype)
    tg_ref[...] = ug_ref[FFN_HALO + tm - FFN_TAIL:FFN_HALO + tm, :]
    tv_ref[...] = uv_ref[FFN_HALO + tm - FFN_TAIL:FFN_HALO + tm, :]


def ffn_up_prompt(x, g, w_up, conv_w, conv_b3, layer, *, seq, tm, tn):
    m, k = x.shape
    assert seq % tm == 0 and D_FF % tn == 0 and tm % FFN_HALO == 0
    nj = D_FF // tn
    seq_tiles = seq // tm
    chunk = min(256, tm)
    halo_blocks = tm // FFN_HALO
    act, tail_g, tail_v = pl.pallas_call(
        functools.partial(_ffn_up_prompt_kernel, tm=tm, seq_tiles=seq_tiles, chunk=chunk),
        grid=(m // tm, nj),
        in_specs=[
            pl.BlockSpec((tm, k), lambda i, j: (i, 0)),
            pl.BlockSpec((FFN_HALO, k), lambda i, j: (jnp.maximum(i * halo_blocks - 1, 0), 0)),
            pl.BlockSpec((1, k), lambda i, j: (0, 0)),
            pl.BlockSpec((None, k, tn), lambda i, j: (layer, 0, j)),
            pl.BlockSpec((None, k, tn), lambda i, j: (layer, 0, j + nj)),
            pl.BlockSpec((None, 3, tn), lambda i, j: (layer, 0, j)),
            pl.BlockSpec((None, 3, tn), lambda i, j: (layer, 0, j + nj)),
            pl.BlockSpec((None, 1, tn), lambda i, j: (layer, 0, j)),
            pl.BlockSpec((None, 1, tn), lambda i, j: (layer, 0, j + nj)),
        ],
        out_specs=[
            pl.BlockSpec((tm, tn), lambda i, j: (i, j)),
            pl.BlockSpec((None, FFN_TAIL, tn), lambda i, j: (i, 0, j)),
            pl.BlockSpec((None, FFN_TAIL, tn), lambda i, j: (i, 0, j)),
        ],
        out_shape=[
            jax.ShapeDtypeStruct((m, D_FF), BF16),
            jax.ShapeDtypeStruct((m // tm, FFN_TAIL, D_FF), F32),
            jax.ShapeDtypeStruct((m // tm, FFN_TAIL, D_FF), F32),
        ],
        scratch_shapes=[
            pltpu.VMEM((tm + FFN_HALO, k), BF16),
            pltpu.VMEM((tm + FFN_HALO, tn), F32),
            pltpu.VMEM((tm + FFN_HALO, tn), F32),
        ],
        compiler_params=_params("parallel", "arbitrary"),
        name="ffn_up_prompt",
    )(x, x, g.reshape(1, k), w_up, w_up, conv_w, conv_w, conv_b3, conv_b3)
    return act, jnp.concatenate([tail_g, tail_v], axis=-1)


def _ffn_up_step_kernel(x_ref, g_ref, wg_ref, wv_ref, cwg_ref, cwv_ref, cbg_ref, cbv_ref, hg_ref, hv_ref,
                        act_ref, ng_ref, nv_ref, xn_ref, *, bd):
    @pl.when(pl.program_id(0) == 0)
    def _():
        _norm_rows(x_ref, g_ref, xn_ref, bd, bd)

    xn = xn_ref[...]
    ug = jnp.dot(xn, wg_ref[...], preferred_element_type=F32)
    uv = jnp.dot(xn, wv_ref[...], preferred_element_type=F32)

    def conv(u, h_ref, cw_ref, cb_ref):
        return cb_ref[...] + cw_ref[0:1, :] * h_ref[0] + cw_ref[1:2, :] * h_ref[1] + cw_ref[2:3, :] * u

    act_ref[...] = (_gelu_tanh(conv(ug, hg_ref, cwg_ref, cbg_ref)) * conv(uv, hv_ref, cwv_ref, cbv_ref)).astype(act_ref.dtype)
    ng_ref[0] = hg_ref[1]
    ng_ref[1] = ug
    nv_ref[0] = hv_ref[1]
    nv_ref[1] = uv


def ffn_up_step(x, g, w_up, conv_w, conv_b3, layer, hist_t, *, tn):
    bd, k = x.shape
    nj = D_FF // tn
    act, new_g, new_v = pl.pallas_call(
        functools.partial(_ffn_up_step_kernel, bd=bd),
        grid=(nj,),
        in_specs=[
            pl.BlockSpec((bd, k), lambda j: (0, 0)),
            pl.BlockSpec((1, k), lambda j: (0, 0)),
            pl.BlockSpec((None, k, tn), lambda j: (layer, 0, j)),
            pl.BlockSpec((None, k, tn), lambda j: (layer, 0, j + nj)),
            pl.BlockSpec((None, 3, tn), lambda j: (layer, 0, j)),
            pl.BlockSpec((None, 3, tn), lambda j: (layer, 0, j + nj)),
            pl.BlockSpec((None, 1, tn), lambda j: (layer, 0, j)),
            pl.BlockSpec((None, 1, tn), lambda j: (layer, 0, j + nj)),
            pl.BlockSpec((2, bd, tn), lambda j: (0, 0, j)),
            pl.BlockSpec((2, bd, tn), lambda j: (0, 0, j + nj)),
        ],
        out_specs=[
            pl.BlockSpec((bd, tn), lambda j: (0, j)),
            pl.BlockSpec((2, bd, tn), lambda j: (0, 0, j)),
            pl.BlockSpec((2, bd, tn), lambda j: (0, 0, j)),
        ],
        out_shape=[
            jax.ShapeDtypeStruct((bd, D_FF), BF16),
            jax.ShapeDtypeStruct((2, bd, D_FF), F32),
            jax.ShapeDtypeStruct((2, bd, D_FF), F32),
        ],
        scratch_shapes=[pltpu.VMEM((bd, k), BF16)],
        compiler_params=_params("arbitrary"),
        name="ffn_up_step",
    )(x, g.reshape(1, k), w_up, w_up, conv_w, conv_w, conv_b3, conv_b3, hist_t, hist_t)
    return act, jnp.concatenate([new_g, new_v], axis=-1)


TM = 1024
TN = 1024
TM_POST = 512
TK_POST = 2048
TN_MERGE = 512
TN_FFN = 512
XATTN_ROWS = 512


def _bf16_weights(w_in, pool_w, gla_w_a2, w_br_pool, w_br_dil, w_br_gla, w_mix_out, w_xq, w_xkv, w_xo, w_up, w_down):
    w_in_t, w_ga_t = prep_w_in(jnp.swapaxes(w_in, 1, 2), tn=TN)
    return dict(
        w_in_t=w_in_t,
        w_ga_t=w_ga_t,
        w_a2p=jnp.pad(gla_w_a2, ((0, 0), (0, LANES - GLA_RANK), (0, 0))).astype(BF16),
        pool_w=pool_w.astype(BF16),
        w_br_pool=w_br_pool.astype(BF16),
        w_br_dil=w_br_dil.astype(BF16),
        w_br_gla=w_br_gla.astype(BF16),
        w_mix_out=w_mix_out.astype(BF16),
        w_xq=w_xq.astype(BF16),
        w_xkv=w_xkv.astype(BF16),
        w_xo=w_xo.astype(BF16),
        w_up=w_up.astype(BF16),
        w_down=w_down.astype(BF16),
    )


def kernel(x_prompt, x_sample, state_pool, cache_dil1_kv, cache_dil2_kv, cache_dil3_kv, state_gla, cache_mem_kv, state_ffn_conv, mem_prompt, rel_bias, norm_mix_pre, norm_mix_post, w_in, pool_w, pool_scale, gla_w_a2, gla_b_a, gla_norm, w_br_pool, w_br_dil, w_br_gla, w_mix_out, norm_x_pre, norm_x_post, norm_mem, w_xq, w_xkv, w_xo, norm_ffn_pre, norm_ffn_post, w_up, conv_w, conv_b, w_down):
    b, seq, d = x_prompt.shape
    bd = x_sample.shape[0]
    depth = w_in.shape[0]
    m = b * seq
    assert x_sample.shape[1] == 1 and d == D_MODEL and w_in.shape[2] == N_MAIN + GLA_RANK + N_GATES
    slot_bias = _slot_biases(rel_bias)
    band_bias = _band_bias(slot_bias)
    dil_caches = (cache_dil1_kv, cache_dil2_kv, cache_dil3_kv)

    xp = x_prompt.reshape(m, d)
    xs = x_sample.reshape(bd, d)
    pool_p, gla_p, mem_p, conv_p = [], [], [], []
    pool_s, gla_s, conv_s = [], [], []
    dil_p = [[] for _ in range(3)]
    dil_s = [[] for _ in range(3)]
    seq_tiles = seq // TM

    w = _bf16_weights(w_in, pool_w, gla_w_a2, w_br_pool, w_br_dil, w_br_gla, w_mix_out, w_xq, w_xkv, w_xo, w_up, w_down)
    conv_b3 = conv_b.reshape(depth, 1, 2 * D_FF)
    mem_rows = mem_prompt.reshape(b * MEM_LEN, d)
    for l in range(depth):
        w_a2p = w["w_a2p"][l]

        zm, zkv_wide = in_proj(xp, norm_mix_pre[l], w["w_in_t"], l, tm=TM, tn=TN)
        za = norm_mm(xp, norm_mix_pre[l], w["w_ga_t"], l, tm=TM, tn=LANES, w_rows_are_outputs=True)
        z3 = zm.reshape(b, seq, N_PROJ)
        y_pool = pool_prompt(z3, w["pool_w"], l, pool_scale[l]).reshape(m, POOL_WIDTH)
        y_dil = dil_prompt(z3, band_bias).reshape(m, DIL_WIDTH)
        y_gla, gla_new = gla_prompt(z3, za.reshape(b, seq, LANES), w_a2p, gla_b_a[l], gla_norm[l])
        merged = branch_merge(y_pool, y_dil, y_gla.reshape(m, GLA_VAL_WIDTH), w["w_br_pool"], w["w_br_dil"], w["w_br_gla"], l,
                              zm, tm=TM, tn=TN_MERGE)
        xp = mm_post(merged, w["w_mix_out"], l, norm_mix_post[l], xp, tm=TM_POST, tk=TK_POST)
        mem_kv = norm_mm(mem_rows, norm_mem[l], w["w_xkv"], l, tm=TM, tn=TN)
        q = norm_mm(xp, norm_x_pre[l], w["w_xq"], l, tm=TM, tn=TN, out_dtype=BF16)
        o = xattn(q.reshape(b, seq, X_WIDTH), mem_kv.reshape(b, MEM_LEN, 2 * X_WIDTH), tt=XATTN_ROWS)
        xp = mm_post(o.reshape(m, X_WIDTH), w["w_xo"], l, norm_x_post[l], xp, tm=TM_POST, tk=TK_POST)
        act, tails = ffn_up_prompt(xp, norm_ffn_pre[l], w["w_up"], conv_w, conv_b3, l, seq=seq, tm=TM, tn=TN_FFN)
        xp = mm_post(act, w["w_down"], l, norm_ffn_post[l], xp, tm=TM_POST, tk=TK_POST)

        pool_p.append(z3[:, seq - POOL_HIST:, COL_POOL:COL_POOL + POOL_WIDTH])
        for g, (win, _) in enumerate(DIL_CONFIGS):
            keep = min(win, seq)
            if g == 2:
                kv = zkv_wide.reshape(b, seq, 2 * DIL_WIDTH)[:, seq - keep:]
            else:
                kv = z3[:, seq - keep:, COL_DIL_K[g]:COL_DIL_K[g] + 2 * DIL_WIDTH]
            dil_p[g].append(kv.reshape(b, keep, 2, DIL_HEADS, DIL_HEAD_DIM))
        gla_p.append(gla_new)
        mem_p.append(mem_kv.reshape(b, MEM_LEN, 2, X_HEADS, X_HEAD_DIM))
        conv_p.append(tails.reshape(b, seq_tiles, FFN_TAIL, 2 * D_FF)[:, seq_tiles - 1, FFN_TAIL - 2:, :])

        zs, _ = in_proj(xs, norm_mix_pre[l], w["w_in_t"], l, tm=bd, tn=TN)
        zas = norm_mm(xs, norm_mix_pre[l], w["w_ga_t"], l, tm=bd, tn=LANES, w_rows_are_outputs=True)
        y_pool_s, pool_new_t = pool_step(jnp.swapaxes(state_pool[l], 0, 1), zs, w["pool_w"], l, pool_scale[l])
        y_dil_s = dil_step(zs, dil_caches, l, slot_bias).astype(BF16)
        y_gla_s, gla_new_s = gla_step(zs, zas, state_gla, l, w_a2p, gla_b_a[l], gla_norm[l])
        merged_s = branch_merge(y_pool_s, y_dil_s, y_gla_s.astype(BF16), w["w_br_pool"], w["w_br_dil"], w["w_br_gla"], l,
                                zs, tm=bd, tn=TN_MERGE)
        xs = mm_post(merged_s, w["w_mix_out"], l, norm_mix_post[l], xs, tm=bd, tk=TK_POST)
        q_s = norm_mm(xs, norm_x_pre[l], w["w_xq"], l, tm=bd, tn=TN)
        o_s = xattn_step(q_s, cache_mem_kv, l).astype(BF16)
        xs = mm_post(o_s, w["w_xo"], l, norm_x_post[l], xs, tm=bd, tk=TK_POST)
        act_s, conv_new_t = ffn_up_step(xs, norm_ffn_pre[l], w["w_up"], conv_w, conv_b3, l,
                                        jnp.swapaxes(state_ffn_conv[l], 0, 1), tn=TN_FFN)
        xs = mm_post(act_s, w["w_down"], l, norm_ffn_post[l], xs, tm=bd, tk=TK_POST)

        pool_s.append(jnp.swapaxes(pool_new_t, 0, 1))
        for g in range(3):
            kv = zs[:, COL_DIL_K[g]:COL_DIL_K[g] + 2 * DIL_WIDTH]
            dil_s[g].append(kv.reshape(bd, 1, 2, DIL_HEADS, DIL_HEAD_DIM))
        gla_s.append(gla_new_s)
        conv_s.append(jnp.swapaxes(conv_new_t, 0, 1))

    return (xp.reshape(b, seq, d), xs.reshape(bd, 1, d),
            jnp.stack(pool_p), jnp.stack(dil_p[0]), jnp.stack(dil_p[1]), jnp.stack(dil_p[2]), jnp.stack(gla_p), jnp.stack(mem_p), jnp.stack(conv_p),
            jnp.stack(pool_s), jnp.stack(dil_s[0]), jnp.stack(dil_s[1]), jnp.stack(dil_s[2]), jnp.stack(gla_s), jnp.stack(conv_s))
```

```python
import functools
import math

import jax
import jax.numpy as jnp
from jax import lax
from jax.experimental import pallas as pl
from jax.experimental.pallas import tpu as pltpu

F32 = jnp.float32
BF16 = jnp.bfloat16
EPS = 1e-6
NEG_INF = -1e30

VMEM_LIMIT_BYTES = 56 * 1024 * 1024
LANES = 128

D_MODEL = 2048
POOL_WINDOWS = (2, 4, 8, 16)
POOL_GROUP = 256
POOL_WIDTH = 1024
POOL_HIST = 15
DIL_CONFIGS = ((128, 1), (512, 4), (2048, 16))
DIL_HEADS = 8
DIL_HEAD_DIM = 128
DIL_WIDTH = 1024
DIL_BLOCK = 128
GLA_HEADS = 4
GLA_DK = 256
GLA_DV = 512
GLA_KEY_WIDTH = 1024
GLA_VAL_WIDTH = 2048
GLA_RANK = 16
GLA_TAU = 16.0
GLA_CHUNK = 64
REL_BUCKETS = 32
REL_MAX_DIST = 2048
MEM_LEN = 256
X_HEADS = 4
X_HEAD_DIM = 256
X_WIDTH = 1024
D_FF = 5632
PAST_LEN = 8192

COL_POOL = 0
COL_DIL_Q = tuple(1024 + 3072 * g for g in range(3))
COL_DIL_K = tuple(2048 + 3072 * g for g in range(3))
COL_DIL_V = tuple(3072 + 3072 * g for g in range(3))
COL_GQ = 10240
COL_GK = 11264
COL_GV = 12288
COL_GR = 14336
N_MAIN = 16384
N_GATES = 3 * D_MODEL
N_PROJ = N_MAIN + N_GATES


def _params(*semantics):
    return pltpu.CompilerParams(dimension_semantics=semantics, vmem_limit_bytes=VMEM_LIMIT_BYTES)


def _sigmoid(x):
    return 0.5 * jnp.tanh(0.5 * x) + 0.5


def _log_sigmoid(x):
    return jnp.minimum(x, 0.0) - jnp.log(1.0 + jnp.exp(-jnp.abs(x)))


def _gelu_tanh(x):
    return x * (0.5 * (1.0 + jnp.tanh(math.sqrt(2.0 / math.pi) * (x + 0.044715 * (x * x * x)))))


def _rms_scale(y, g):
    return y * lax.rsqrt(jnp.mean(y * y, axis=-1, keepdims=True) + EPS) * g


def _norm_rows(x_ref, g_ref, xn_ref, rows, chunk, dst_offset=0):
    g = g_ref[...]

    def body(c, carry):
        r0 = pl.multiple_of(c * chunk, chunk)
        xn_ref[pl.ds(dst_offset + r0, chunk), :] = _rms_scale(x_ref[pl.ds(r0, chunk), :], g).astype(BF16)
        return carry

    lax.fori_loop(0, rows // chunk, body, 0)


def _prep_w_in_kernel(w_ref, wx_ref, o_ref, ga_ref, *, n_plain, shift):
    j = pl.program_id(1)

    @pl.when(j < n_plain)
    def _():
        o_ref[...] = w_ref[...].astype(BF16)

    @pl.when(j >= n_plain)
    def _():
        tn = o_ref.shape[0]
        o_ref[0:tn - shift, :] = w_ref[shift:tn, :].astype(BF16)
        o_ref[tn - shift:tn, :] = wx_ref[...].astype(BF16)

    @pl.when(j == n_plain)
    def _():
        ga_ref[0:shift, :] = w_ref[0:shift, :].astype(BF16)
        ga_ref[shift:LANES, :] = jnp.zeros((LANES - shift, ga_ref.shape[1]), BF16)


def prep_w_in(w_in_t, *, tn):
    depth, _, k = w_in_t.shape
    n_plain = N_MAIN // tn
    nj = (N_MAIN + N_GATES) // tn
    return pl.pallas_call(
        functools.partial(_prep_w_in_kernel, n_plain=n_plain, shift=GLA_RANK),
        grid=(depth, nj),
        in_specs=[
            pl.BlockSpec((None, tn, k), lambda l, j: (l, j, 0)),
            pl.BlockSpec((None, GLA_RANK, k), lambda l, j: (l, jnp.maximum(j + 1, n_plain) * (tn // GLA_RANK), 0)),
        ],
        out_specs=[
            pl.BlockSpec((None, tn, k), lambda l, j: (l, j, 0)),
            pl.BlockSpec((None, LANES, k), lambda l, j: (l, 0, 0)),
        ],
        out_shape=[jax.ShapeDtypeStruct((depth, N_MAIN + N_GATES, k), BF16), jax.ShapeDtypeStruct((depth, LANES, k), BF16)],
        compiler_params=_params("parallel", "arbitrary"),
        name="prep_w_in",
    )(w_in_t, w_in_t)


def _in_proj_kernel(x_ref, xs_ref, g_ref, w_ref, wga_ref, o_ref, kv_ref, ga_ref, os_ref, gas_ref, xn_ref, xsn_ref,
                    *, tm, chunk, kv_lo, kv_hi):
    i = pl.program_id(0)
    j = pl.program_id(1)
    nt = (((1,), (1,)), ((), ()))

    @pl.when(j == 0)
    def _():
        _norm_rows(x_ref, g_ref, xn_ref, tm, chunk)
        ga_ref[...] = lax.dot_general(xn_ref[...], wga_ref[...], nt, preferred_element_type=F32)

    o_ref[...] = lax.dot_general(xn_ref[...], w_ref[...], nt, preferred_element_type=F32)

    @pl.when(jnp.logical_and(j >= kv_lo, j < kv_hi))
    def _():
        kv_ref[...] = o_ref[...]

    @pl.when(i == 0)
    def _():
        @pl.when(j == 0)
        def _():
            xsn_ref[...] = _rms_scale(xs_ref[...], g_ref[...]).astype(BF16)
            gas_ref[...] = lax.dot_general(xsn_ref[...], wga_ref[...], nt, preferred_element_type=F32)

        os_ref[...] = lax.dot_general(xsn_ref[...], w_ref[...], nt, preferred_element_type=F32)


def in_proj(x, xs, g, w_t, w_ga_t, layer, *, tm, tn):
    m, k = x.shape
    ms = xs.shape[0]
    n = w_t.shape[1]
    assert m % tm == 0 and n % tn == 0 and COL_DIL_K[2] % tn == 0
    chunk = min(256, tm)
    nj = n // tn
    kv_lo = COL_DIL_K[2] // tn
    n_kv = 2 * DIL_WIDTH // tn
    return pl.pallas_call(
        functools.partial(_in_proj_kernel, tm=tm, chunk=chunk, kv_lo=kv_lo, kv_hi=kv_lo + n_kv),
        grid=(m // tm, nj),
        in_specs=[
            pl.BlockSpec((tm, k), lambda i, j: (i, 0)),
            pl.BlockSpec((ms, k), lambda i, j: (0, 0)),
            pl.BlockSpec((1, k), lambda i, j: (0, 0)),
            pl.BlockSpec((None, tn, k), lambda i, j: (layer, j, 0)),
            pl.BlockSpec((None, LANES, k), lambda i, j: (layer, 0, 0)),
        ],
        out_specs=[
            pl.BlockSpec((tm, tn), lambda i, j: (i, j)),
            pl.BlockSpec((tm, tn), lambda i, j: (i, jnp.clip(j - kv_lo, 0, n_kv - 1))),
            pl.BlockSpec((tm, LANES), lambda i, j: (i, 0)),
            pl.BlockSpec((ms, tn), lambda i, j: (0, jnp.where(i == 0, j, nj - 1))),
            pl.BlockSpec((ms, LANES), lambda i, j: (0, 0)),
        ],
        out_shape=[
            jax.ShapeDtypeStruct((m, n), F32),
            jax.ShapeDtypeStruct((m, 2 * DIL_WIDTH), F32),
            jax.ShapeDtypeStruct((m, LANES), F32),
            jax.ShapeDtypeStruct((ms, n), F32),
            jax.ShapeDtypeStruct((ms, LANES), F32),
        ],
        scratch_shapes=[pltpu.VMEM((tm, k), BF16), pltpu.VMEM((ms, k), BF16)],
        compiler_params=_params("arbitrary", "arbitrary"),
        name="in_proj",
    )(x, xs, g.reshape(1, k), w_t, w_ga_t)


def _norm_mm_kernel(x_ref, g_ref, w_ref, o_ref, xn_ref, *, tm, chunk):
    @pl.when(pl.program_id(1) == 0)
    def _():
        _norm_rows(x_ref, g_ref, xn_ref, tm, chunk)

    o_ref[...] = jnp.dot(xn_ref[...], w_ref[...], preferred_element_type=F32).astype(o_ref.dtype)


def norm_mm(x, g, w, layer, *, tm, tn, out_dtype=F32):
    m, k = x.shape
    n = w.shape[2]
    tm = min(tm, m)
    tn = min(tn, n)
    assert m % tm == 0 and n % tn == 0
    chunk = min(256, tm)
    return pl.pallas_call(
        functools.partial(_norm_mm_kernel, tm=tm, chunk=chunk),
        grid=(m // tm, n // tn),
        in_specs=[
            pl.BlockSpec((tm, k), lambda i, j: (i, 0)),
            pl.BlockSpec((1, k), lambda i, j: (0, 0)),
            pl.BlockSpec((None, k, tn), lambda i, j: (layer, 0, j)),
        ],
        out_specs=pl.BlockSpec((tm, tn), lambda i, j: (i, j)),
        out_shape=jax.ShapeDtypeStruct((m, n), out_dtype),
        scratch_shapes=[pltpu.VMEM((tm, k), BF16)],
        compiler_params=_params("parallel", "arbitrary"),
        name="norm_mm",
    )(x, g.reshape(1, k), w)


def _mm_post_kernel(a_ref, w_ref, g_ref, res_ref, o_ref, acc_ref, *, nk):
    kk = pl.program_id(1)
    part = jnp.dot(a_ref[...], w_ref[...], preferred_element_type=F32)
    if nk == 1:
        o_ref[...] = res_ref[...] + _rms_scale(part, g_ref[...])
        return

    @pl.when(kk == 0)
    def _():
        acc_ref[...] = part

    @pl.when(jnp.logical_and(kk > 0, kk < nk - 1))
    def _():
        acc_ref[...] += part

    @pl.when(kk == nk - 1)
    def _():
        o_ref[...] = res_ref[...] + _rms_scale(acc_ref[...] + part, g_ref[...])


def mm_post(a, w, layer, g, res, *, tm, tk):
    m, k = a.shape
    n = w.shape[2]
    tm = min(tm, m)
    tk = k if k <= tk else k // 2
    assert m % tm == 0 and k % tk == 0 and tk % LANES == 0
    return pl.pallas_call(
        functools.partial(_mm_post_kernel, nk=k // tk),
        grid=(m // tm, k // tk),
        in_specs=[
            pl.BlockSpec((tm, tk), lambda i, kk: (i, kk)),
            pl.BlockSpec((None, tk, n), lambda i, kk: (layer, kk, 0)),
            pl.BlockSpec((1, n), lambda i, kk: (0, 0)),
            pl.BlockSpec((tm, n), lambda i, kk: (i, 0)),
        ],
        out_specs=pl.BlockSpec((tm, n), lambda i, kk: (i, 0)),
        out_shape=jax.ShapeDtypeStruct((m, n), F32),
        scratch_shapes=[pltpu.VMEM((tm, n), F32)],
        compiler_params=_params("parallel", "arbitrary"),
        name="mm_post",
    )(a, w, g.reshape(1, n), res)


def _branch_merge_kernel(yp_ref, yd_ref, yg_ref, wp_ref, wd_ref, wg_ref, g0_ref, g1_ref, g2_ref, o_ref):
    bp = jnp.dot(yp_ref[...], wp_ref[...], preferred_element_type=F32)
    bd = jnp.dot(yd_ref[...], wd_ref[...], preferred_element_type=F32)
    bg = jnp.dot(yg_ref[...], wg_ref[...], preferred_element_type=F32)
    merged = _sigmoid(g0_ref[...]) * bp + _sigmoid(g1_ref[...]) * bd + _sigmoid(g2_ref[...]) * bg
    o_ref[...] = merged.astype(o_ref.dtype)


def branch_merge(y_pool, y_dil, y_gla, w_pool, w_dil, w_gla, layer, z_main, *, tm, tn):
    m = y_pool.shape[0]
    tm = min(tm, m)
    assert m % tm == 0 and D_MODEL % tn == 0
    gate_blk = [(N_MAIN + b * D_MODEL) // tn for b in range(3)]

    def gate_spec(b):
        return pl.BlockSpec((tm, tn), lambda i, j: (i, gate_blk[b] + j))

    return pl.pallas_call(
        _branch_merge_kernel,
        grid=(m // tm, D_MODEL // tn),
        in_specs=[
            pl.BlockSpec((tm, POOL_WIDTH), lambda i, j: (i, 0)),
            pl.BlockSpec((tm, DIL_WIDTH), lambda i, j: (i, 0)),
            pl.BlockSpec((tm, GLA_VAL_WIDTH), lambda i, j: (i, 0)),
            pl.BlockSpec((None, POOL_WIDTH, tn), lambda i, j: (layer, 0, j)),
            pl.BlockSpec((None, DIL_WIDTH, tn), lambda i, j: (layer, 0, j)),
            pl.BlockSpec((None, GLA_VAL_WIDTH, tn), lambda i, j: (layer, 0, j)),
            gate_spec(0),
            gate_spec(1),
            gate_spec(2),
        ],
        out_specs=pl.BlockSpec((tm, tn), lambda i, j: (i, j)),
        out_shape=jax.ShapeDtypeStruct((m, D_MODEL), BF16),
        compiler_params=_params("parallel", "arbitrary"),
        name="branch_merge",
    )(y_pool, y_dil, y_gla, w_pool, w_dil, w_gla, z_main, z_main, z_main)


def _pool_prompt_kernel(u_ref, w_ref, s_ref, o_ref, buf_a, buf_b, *, seq):
    pad = POOL_HIST + 1
    zeros = jnp.zeros((pad, POOL_GROUP), F32)
    t = lax.broadcasted_iota(jnp.int32, (seq, 1), 0)
    for g, win in enumerate(POOL_WINDOWS):
        cols = slice(g * POOL_GROUP, (g + 1) * POOL_GROUP)
        u = u_ref[:, cols]
        cur, nxt = buf_a, buf_b
        cur[0:pad, :] = zeros
        nxt[0:pad, :] = zeros
        cur[pad:pad + seq, :] = u
        k = 1
        while k < win:
            nxt[pad:pad + seq, :] = cur[pad:pad + seq, :] + cur[pad - k:pad - k + seq, :]
            cur, nxt = nxt, cur
            k *= 2
        cnt = jnp.minimum(win, t + 1).astype(F32)
        d = cur[pad:pad + seq, :] / cnt - u
        y = jnp.dot(d.astype(BF16), w_ref[g], preferred_element_type=F32) * s_ref[:, cols]
        o_ref[:, cols] = y.astype(o_ref.dtype)


def pool_prompt(z3, pool_w, layer, pool_scale):
    b, seq, _ = z3.shape
    return pl.pallas_call(
        functools.partial(_pool_prompt_kernel, seq=seq),
        grid=(b,),
        in_specs=[
            pl.BlockSpec((None, seq, POOL_WIDTH), lambda i: (i, 0, COL_POOL // POOL_WIDTH)),
            pl.BlockSpec((None, len(POOL_WINDOWS), POOL_GROUP, POOL_GROUP), lambda i: (layer, 0, 0, 0)),
            pl.BlockSpec((1, POOL_WIDTH), lambda i: (0, 0)),
        ],
        out_specs=pl.BlockSpec((None, seq, POOL_WIDTH), lambda i: (i, 0, 0)),
        out_shape=jax.ShapeDtypeStruct((b, seq, POOL_WIDTH), BF16),
        scratch_shapes=[pltpu.VMEM((seq + POOL_HIST + 1, POOL_GROUP), F32)] * 2,
        compiler_params=_params("parallel"),
        name="pool_prompt",
    )(z3, pool_w, pool_scale.reshape(1, POOL_WIDTH))


def _pool_step_kernel(hist_ref, u_ref, w_ref, s_ref, y_ref, new_ref):
    u = u_ref[...]
    for r in range(POOL_HIST - 1):
        new_ref[r] = hist_ref[r + 1]
    new_ref[POOL_HIST - 1] = u
    for g, win in enumerate(POOL_WINDOWS):
        cols = slice(g * POOL_GROUP, (g + 1) * POOL_GROUP)
        ug = u[:, cols]
        acc = ug
        for r in range(POOL_HIST - (win - 1), POOL_HIST):
            acc = acc + hist_ref[r, :, cols]
        d = acc / float(win) - ug
        y = jnp.dot(d.astype(BF16), w_ref[g], preferred_element_type=F32) * s_ref[:, cols]
        y_ref[:, cols] = y.astype(y_ref.dtype)


def pool_step(hist_t, z_s, pool_w, layer, pool_scale):
    bd = z_s.shape[0]
    return pl.pallas_call(
        _pool_step_kernel,
        grid=(1,),
        in_specs=[
            pl.BlockSpec((POOL_HIST, bd, POOL_WIDTH), lambda i: (0, 0, 0)),
            pl.BlockSpec((bd, POOL_WIDTH), lambda i: (0, COL_POOL // POOL_WIDTH)),
            pl.BlockSpec((None, len(POOL_WINDOWS), POOL_GROUP, POOL_GROUP), lambda i: (layer, 0, 0, 0)),
            pl.BlockSpec((1, POOL_WIDTH), lambda i: (0, 0)),
        ],
        out_specs=[
            pl.BlockSpec((bd, POOL_WIDTH), lambda i: (0, 0)),
            pl.BlockSpec((POOL_HIST, bd, POOL_WIDTH), lambda i: (0, 0, 0)),
        ],
        out_shape=[
            jax.ShapeDtypeStruct((bd, POOL_WIDTH), BF16),
            jax.ShapeDtypeStruct((POOL_HIST, bd, POOL_WIDTH), F32),
        ],
        compiler_params=_params("arbitrary"),
        name="pool_step",
    )(hist_t, z_s, pool_w, pool_scale.reshape(1, POOL_WIDTH))


def _rel_bucket(dist):
    max_exact = REL_BUCKETS // 2
    scaled = jnp.log(jnp.maximum(dist, 1).astype(F32) / max_exact) / math.log(REL_MAX_DIST / max_exact)
    large = jnp.minimum(max_exact + (scaled * (REL_BUCKETS - max_exact)).astype(jnp.int32), REL_BUCKETS - 1)
    return jnp.where(dist < max_exact, dist, large)


def _select_rows(table_t, index, n):
    onehot = (index.reshape(1, -1) == jnp.arange(n, dtype=jnp.int32)[:, None]).astype(F32)
    out = jnp.dot(table_t, onehot, precision=lax.Precision.HIGHEST, preferred_element_type=F32)
    return out.reshape((table_t.shape[0],) + index.shape)


def _slot_biases(rel_bias):
    out = []
    for g, (win, dil) in enumerate(DIL_CONFIGS):
        dist = jnp.arange(win // dil + 1, dtype=jnp.int32) * dil
        table_t = rel_bias[:, g * DIL_HEADS:(g + 1) * DIL_HEADS].T.astype(F32)
        out.append(_select_rows(table_t, _rel_bucket(dist), REL_BUCKETS))
    return out


def _band_bias(slot_bias):
    qi = jnp.arange(DIL_BLOCK, dtype=jnp.int32)[:, None] + DIL_BLOCK
    ki = jnp.arange(2 * DIL_BLOCK, dtype=jnp.int32)[None, :]
    rel = qi - ki
    out = []
    for g, (win, dil) in enumerate(DIL_CONFIGS):
        n_slots = win // dil
        ok = (rel >= 0) & (rel <= n_slots)
        b = _select_rows(slot_bias[g], jnp.clip(rel, 0, n_slots), n_slots + 1)
        out.append(jnp.where(ok[None], b, NEG_INF))
    return jnp.stack(out, axis=0)


def _dil_prompt_kernel(q0, k0, v0, q1, k1, v1, q2, k2, v2, bias_ref, o_ref, o_scr, lse_scr, *, seq):
    qs, ks, vs = (q0, q1, q2), (k0, k1, k2), (v0, v1, v2)
    scale = DIL_HEAD_DIM ** -0.5
    blk = DIL_BLOCK
    nt = (((1,), (1,)), ((), ()))

    def rows(ref, start, dil):
        if dil == 1:
            return ref[pl.ds(start, blk), :]
        return ref[pl.ds(start, blk, stride=dil), :]

    for g, (_, dil) in enumerate(DIL_CONFIGS):
        nb = seq // dil // blk
        for r in range(dil):
            for ub in range(nb):
                start = r + dil * ub * blk
                q = rows(qs[g], start, dil).astype(BF16)
                kc = rows(ks[g], start, dil).astype(BF16)
                vc = rows(vs[g], start, dil).astype(BF16)
                if ub == 0:
                    kk, vv, bias = kc, vc, bias_ref[g, :, blk:]
                else:
                    prev = start - dil * blk
                    kk = jnp.concatenate([rows(ks[g], prev, dil).astype(BF16), kc], axis=0)
                    vv = jnp.concatenate([rows(vs[g], prev, dil).astype(BF16), vc], axis=0)
                    bias = bias_ref[g]
                s = lax.dot_general(q, kk, nt, preferred_element_type=F32) * scale + bias
                m = jnp.max(s, axis=-1, keepdims=True)
                e = jnp.exp(s - m)
                l = jnp.sum(e, axis=-1, keepdims=True)
                o = jnp.dot(e.astype(BF16), vv, preferred_element_type=F32) / l
                lse = jnp.broadcast_to(m + jnp.log(l), (blk, LANES))
                if dil == 1:
                    o_scr[g, pl.ds(start, blk), :] = o
                    lse_scr[g, pl.ds(start, blk), :] = lse
                else:
                    o_scr[g, pl.ds(start, blk, stride=dil), :] = o
                    lse_scr[g, pl.ds(start, blk, stride=dil), :] = lse

    chunk = 256

    def combine(c, carry):
        sl = pl.ds(pl.multiple_of(c * chunk, chunk), chunk)
        l0, l1, l2 = lse_scr[0, sl, :], lse_scr[1, sl, :], lse_scr[2, sl, :]
        mx = jnp.maximum(jnp.maximum(l0, l1), l2)
        w0, w1, w2 = jnp.exp(l0 - mx), jnp.exp(l1 - mx), jnp.exp(l2 - mx)
        y = (w0 * o_scr[0, sl, :] + w1 * o_scr[1, sl, :] + w2 * o_scr[2, sl, :]) / (w0 + w1 + w2)
        o_ref[sl, :] = y.astype(o_ref.dtype)
        return carry

    lax.fori_loop(0, seq // chunk, combine, 0)


def dil_prompt(z3, band_bias):
    b, seq, _ = z3.shape
    assert seq % (DIL_BLOCK * 16) == 0

    def col_spec(col):
        blk0 = col // DIL_HEAD_DIM
        return pl.BlockSpec((None, seq, DIL_HEAD_DIM), lambda i, h: (i, 0, blk0 + h))

    in_specs = []
    for g in range(3):
        in_specs += [col_spec(COL_DIL_Q[g]), col_spec(COL_DIL_K[g]), col_spec(COL_DIL_V[g])]
    in_specs.append(pl.BlockSpec((3, None, DIL_BLOCK, 2 * DIL_BLOCK), lambda i, h: (0, h, 0, 0)))
    return pl.pallas_call(
        functools.partial(_dil_prompt_kernel, seq=seq),
        grid=(b, DIL_HEADS),
        in_specs=in_specs,
        out_specs=pl.BlockSpec((None, seq, DIL_HEAD_DIM), lambda i, h: (i, 0, h)),
        out_shape=jax.ShapeDtypeStruct((b, seq, DIL_WIDTH), BF16),
        scratch_shapes=[pltpu.VMEM((3, seq, DIL_HEAD_DIM), F32), pltpu.VMEM((3, seq, LANES), F32)],
        compiler_params=_params("parallel", "arbitrary"),
        name="dil_prompt",
    )(*([z3] * 9), band_bias)


def _dil_step_kernel(qkv_ref, c0, c1, c2, bias_ref, bias0_ref, o_ref):
    caches = (c0, c1, c2)
    scale = DIL_HEAD_DIM ** -0.5
    outs, lses = [], []
    for g in range(3):
        q, kn, vn = qkv_ref[3 * g], qkv_ref[3 * g + 1], qkv_ref[3 * g + 2]
        kc = caches[g][:, 0]
        vc = caches[g][:, 1]
        s = jnp.sum(kc * q[None], axis=-1, keepdims=True) * scale + bias_ref[g]
        s_new = jnp.sum(q * kn, axis=-1, keepdims=True) * scale + bias0_ref[g]
        m = jnp.maximum(jnp.max(s, axis=0), s_new)
        p = jnp.exp(s - m[None])
        p_new = jnp.exp(s_new - m)
        l = jnp.sum(p, axis=0) + p_new
        outs.append((jnp.sum(p * vc, axis=0) + p_new * vn) / l)
        lses.append(m + jnp.log(l))
    mx = jnp.maximum(jnp.maximum(lses[0], lses[1]), lses[2])
    w0, w1, w2 = jnp.exp(lses[0] - mx), jnp.exp(lses[1] - mx), jnp.exp(lses[2] - mx)
    o_ref[...] = (w0 * outs[0] + w1 * outs[1] + w2 * outs[2]) / (w0 + w1 + w2)


def dil_step(z_s, caches, layer, slot_bias):
    bd = z_s.shape[0]
    n_slots = DIL_CONFIGS[0][0] // DIL_CONFIGS[0][1]
    qkv = z_s[:, COL_DIL_Q[0]:COL_DIL_V[2] + DIL_WIDTH].reshape(bd, 9, DIL_HEADS, DIL_HEAD_DIM)
    in_specs = [pl.BlockSpec((None, 9, DIL_HEADS, DIL_HEAD_DIM), lambda i: (i, 0, 0, 0))]
    cache_views = []
    for g, (win, dil) in enumerate(DIL_CONFIGS):
        depth = caches[g].shape[0]
        assert caches[g].shape[2] == win and win // dil == n_slots
        cache_views.append(caches[g].reshape(depth, bd, n_slots, dil, 2, DIL_HEADS, DIL_HEAD_DIM))
        in_specs.append(pl.BlockSpec((None, None, n_slots, None, 2, DIL_HEADS, DIL_HEAD_DIM),
                                     lambda i: (layer, i, 0, 0, 0, 0, 0)))
    bias_rows = jnp.stack([sb[:, n_slots:0:-1].T for sb in slot_bias], axis=0)
    bias_rows = jnp.broadcast_to(bias_rows[..., None], (3, n_slots, DIL_HEADS, LANES))
    bias_new = jnp.broadcast_to(jnp.stack([sb[:, 0] for sb in slot_bias], axis=0)[..., None], (3, DIL_HEADS, LANES))
    in_specs.append(pl.BlockSpec((3, n_slots, DIL_HEADS, LANES), lambda i: (0, 0, 0, 0)))
    in_specs.append(pl.BlockSpec((3, DIL_HEADS, LANES), lambda i: (0, 0, 0)))
    out = pl.pallas_call(
        _dil_step_kernel,
        grid=(bd,),
        in_specs=in_specs,
        out_specs=pl.BlockSpec((None, DIL_HEADS, DIL_HEAD_DIM), lambda i: (i, 0, 0)),
        out_shape=jax.ShapeDtypeStruct((bd, DIL_HEADS, DIL_HEAD_DIM), F32),
        compiler_params=_params("parallel"),
        name="dil_step",
    )(qkv, *cache_views, bias_rows, bias_new)
    return out.reshape(bd, DIL_WIDTH)


GLA_HEADS_PER_STEP = 2


def _gla_prompt_kernel(q_ref, k_ref, v_ref, r_ref, ga_ref, wa_ref, ba_ref, gn_ref, y_ref, s_out_ref, st_ref, o_scr, *, tq):
    t = pl.program_id(2)
    ck = GLA_CHUNK

    @pl.when(t == 0)
    def _():
        st_ref[...] = jnp.zeros_like(st_ref)

    la = jnp.dot(ga_ref[...].astype(BF16), wa_ref[...], preferred_element_type=F32) + ba_ref[...]
    log_a = _log_sigmoid(la) / GLA_TAU
    row = lax.broadcasted_iota(jnp.int32, (ck, ck), 0)
    col = lax.broadcasted_iota(jnp.int32, (ck, ck), 1)
    tril = row >= col
    tril_b = tril.astype(BF16)
    log_a_hi = log_a.astype(BF16)
    log_a_lo = (log_a - log_a_hi.astype(F32)).astype(BF16)
    nt = (((1,), (1,)), ((), ()))
    tn = (((0,), (0,)), ((), ()))
    for c in range(tq // ck):
        sl = slice(c * ck, (c + 1) * ck)
        for hh in range(GLA_HEADS_PER_STEP):
            ks = slice(hh * GLA_DK, (hh + 1) * GLA_DK)
            vs = slice(hh * GLA_DV, (hh + 1) * GLA_DV)
            b = (jnp.dot(tril_b, log_a_hi[sl, ks], preferred_element_type=F32)
                 + jnp.dot(tril_b, log_a_lo[sl, ks], preferred_element_type=F32))
            b_last = b[ck - 1:ck, :]
            q = q_ref[sl, ks] * (GLA_DK ** -0.5)
            k = k_ref[sl, ks]
            v = v_ref[sl, vs].astype(BF16)
            q_t = (q * jnp.exp(b)).astype(BF16)
            k_t = (k * jnp.exp(-b)).astype(BF16)
            k_h = (k * jnp.exp(b_last - b)).astype(BF16)
            decay = jnp.exp(b_last)
            a = lax.dot_general(q_t, k_t, nt, preferred_element_type=F32)
            a = jnp.where(tril, a, 0.0)
            st = st_ref[hh]
            o = jnp.dot(a.astype(BF16), v, preferred_element_type=F32)
            o = o + lax.dot_general(q_t, st.astype(BF16), nt, preferred_element_type=F32)
            o_scr[sl, vs] = o
            st_ref[hh] = decay * st + lax.dot_general(v, k_h, tn, preferred_element_type=F32)

    for hh in range(GLA_HEADS_PER_STEP):
        vs = slice(hh * GLA_DV, (hh + 1) * GLA_DV)
        o = _rms_scale(o_scr[:, vs], gn_ref[:, vs])
        r = r_ref[:, vs]
        y_ref[:, vs] = (o * (r * _sigmoid(r))).astype(y_ref.dtype)

    @pl.when(t == pl.num_programs(2) - 1)
    def _():
        for hh in range(GLA_HEADS_PER_STEP):
            s_out_ref[hh] = st_ref[hh].T


def gla_prompt(z3, za3, w_a2p, b_a, gla_norm, *, tq=1024):
    b, seq, _ = z3.shape
    hp = GLA_HEADS_PER_STEP
    kw, vw = hp * GLA_DK, hp * GLA_DV
    assert seq % tq == 0 and tq % GLA_CHUNK == 0 and GLA_HEADS % hp == 0
    return pl.pallas_call(
        functools.partial(_gla_prompt_kernel, tq=tq),
        grid=(b, GLA_HEADS // hp, seq // tq),
        in_specs=[
            pl.BlockSpec((None, tq, kw), lambda i, h, t: (i, t, COL_GQ // kw + h)),
            pl.BlockSpec((None, tq, kw), lambda i, h, t: (i, t, COL_GK // kw + h)),
            pl.BlockSpec((None, tq, vw), lambda i, h, t: (i, t, COL_GV // vw + h)),
            pl.BlockSpec((None, tq, vw), lambda i, h, t: (i, t, COL_GR // vw + h)),
            pl.BlockSpec((None, tq, LANES), lambda i, h, t: (i, t, 0)),
            pl.BlockSpec((LANES, kw), lambda i, h, t: (0, h)),
            pl.BlockSpec((1, kw), lambda i, h, t: (0, h)),
            pl.BlockSpec((1, vw), lambda i, h, t: (0, h)),
        ],
        out_specs=[
            pl.BlockSpec((None, tq, vw), lambda i, h, t: (i, t, h)),
            pl.BlockSpec((None, hp, GLA_DK, GLA_DV), lambda i, h, t: (i, h, 0, 0)),
        ],
        out_shape=[
            jax.ShapeDtypeStruct((b, seq, GLA_VAL_WIDTH), BF16),
            jax.ShapeDtypeStruct((b, GLA_HEADS, GLA_DK, GLA_DV), F32),
        ],
        scratch_shapes=[pltpu.VMEM((hp, GLA_DV, GLA_DK), F32), pltpu.VMEM((tq, vw), F32)],
        compiler_params=_params("parallel", "parallel", "arbitrary"),
        name="gla_prompt",
    )(z3, z3, z3, z3, za3, w_a2p, b_a.reshape(1, GLA_KEY_WIDTH), gla_norm.reshape(1, GLA_VAL_WIDTH))


def _gla_step_kernel(q_ref, k_ref, v_ref, r_ref, ga_ref, wa_ref, ba_ref, gn_ref, s_ref, y_ref, s_out_ref):
    ga8 = jnp.broadcast_to(ga_ref[...], (8, LANES)).astype(BF16)
    la = jnp.dot(ga8, wa_ref[...], preferred_element_type=F32)[0:1, :] + ba_ref[...]
    ea = jnp.exp(_log_sigmoid(la) / GLA_TAU)
    q = q_ref[...] * (GLA_DK ** -0.5)
    k = k_ref[...]
    eye = lax.broadcasted_iota(jnp.int32, (GLA_DK, GLA_DK), 0) == lax.broadcasted_iota(jnp.int32, (GLA_DK, GLA_DK), 1)

    def column(row):
        return jnp.sum(jnp.where(eye, jnp.broadcast_to(row, (GLA_DK, GLA_DK)), 0.0), axis=-1, keepdims=True)

    for h in range(GLA_HEADS):
        ks = slice(h * GLA_DK, (h + 1) * GLA_DK)
        vs = slice(h * GLA_DV, (h + 1) * GLA_DV)
        s_new = column(ea[:, ks]) * s_ref[h] + column(k[:, ks]) * v_ref[:, vs]
        s_out_ref[h] = s_new
        o = jnp.sum(column(q[:, ks]) * s_new, axis=0, keepdims=True)
        o = _rms_scale(o, gn_ref[:, vs])
        r = r_ref[:, vs]
        y_ref[:, vs] = o * (r * _sigmoid(r))


def gla_step(z_s, za_s, state, layer, w_a2p, b_a, gla_norm):
    bd = z_s.shape[0]
    zs3 = z_s.reshape(bd, 1, N_PROJ)
    y, s_new = pl.pallas_call(
        _gla_step_kernel,
        grid=(bd,),
        in_specs=[
            pl.BlockSpec((None, 1, GLA_KEY_WIDTH), lambda i: (i, 0, COL_GQ // GLA_KEY_WIDTH)),
            pl.BlockSpec((None, 1, GLA_KEY_WIDTH), lambda i: (i, 0, COL_GK // GLA_KEY_WIDTH)),
            pl.BlockSpec((None, 1, GLA_VAL_WIDTH), lambda i: (i, 0, COL_GV // GLA_VAL_WIDTH)),
            pl.BlockSpec((None, 1, GLA_VAL_WIDTH), lambda i: (i, 0, COL_GR // GLA_VAL_WIDTH)),
            pl.BlockSpec((None, 1, LANES), lambda i: (i, 0, 0)),
            pl.BlockSpec((LANES, GLA_KEY_WIDTH), lambda i: (0, 0)),
            pl.BlockSpec((1, GLA_KEY_WIDTH), lambda i: (0, 0)),
            pl.BlockSpec((1, GLA_VAL_WIDTH), lambda i: (0, 0)),
            pl.BlockSpec((None, None, GLA_HEADS, GLA_DK, GLA_DV), lambda i: (layer, i, 0, 0, 0)),
        ],
        out_specs=[
            pl.BlockSpec((None, 1, GLA_VAL_WIDTH), lambda i: (i, 0, 0)),
            pl.BlockSpec((None, GLA_HEADS, GLA_DK, GLA_DV), lambda i: (i, 0, 0, 0)),
        ],
        out_shape=[
            jax.ShapeDtypeStruct((bd, 1, GLA_VAL_WIDTH), F32),
            jax.ShapeDtypeStruct((bd, GLA_HEADS, GLA_DK, GLA_DV), F32),
        ],
        compiler_params=_params("parallel"),
        name="gla_step",
    )(zs3, zs3, zs3, zs3, za_s.reshape(bd, 1, LANES), w_a2p, b_a.reshape(1, GLA_KEY_WIDTH),
      gla_norm.reshape(1, GLA_VAL_WIDTH), state)
    return y.reshape(bd, GLA_VAL_WIDTH), s_new


def _xattn_kernel(q_ref, kv_ref, o_ref):
    scale = X_HEAD_DIM ** -0.5
    nt = (((1,), (1,)), ((), ()))
    for h in range(X_HEADS):
        hs = slice(h * X_HEAD_DIM, (h + 1) * X_HEAD_DIM)
        k = kv_ref[:, h * X_HEAD_DIM:(h + 1) * X_HEAD_DIM].astype(BF16)
        v = kv_ref[:, X_WIDTH + h * X_HEAD_DIM:X_WIDTH + (h + 1) * X_HEAD_DIM].astype(BF16)
        s = lax.dot_general(q_ref[:, hs], k, nt, preferred_element_type=F32) * scale
        e = jnp.exp(s - jnp.max(s, axis=-1, keepdims=True))
        l = jnp.sum(e, axis=-1, keepdims=True)
        o = jnp.dot(e.astype(BF16), v, preferred_element_type=F32) / l
        o_ref[:, hs] = o.astype(o_ref.dtype)


def xattn(q3, mem_kv, *, tt):
    b, t, _ = q3.shape
    tt = min(tt, t)
    assert t % tt == 0
    return pl.pallas_call(
        _xattn_kernel,
        grid=(b, t // tt),
        in_specs=[
            pl.BlockSpec((None, tt, X_WIDTH), lambda i, j: (i, j, 0)),
            pl.BlockSpec((None, MEM_LEN, 2 * X_WIDTH), lambda i, j: (i, 0, 0)),
        ],
        out_specs=pl.BlockSpec((None, tt, X_WIDTH), lambda i, j: (i, j, 0)),
        out_shape=jax.ShapeDtypeStruct((b, t, X_WIDTH), BF16),
        compiler_params=_params("parallel", "arbitrary"),
        name="xattn",
    )(q3, mem_kv)


def _xattn_step_kernel(q_ref, kv_ref, o_ref):
    q = q_ref[...]
    k = kv_ref[:, 0]
    v = kv_ref[:, 1]
    s = jnp.sum(k * q[None], axis=-1, keepdims=True) * (X_HEAD_DIM ** -0.5)
    p = jnp.exp(s - jnp.max(s, axis=0)[None])
    o_ref[...] = jnp.sum(p * v, axis=0) / jnp.sum(p, axis=0)


def xattn_step(q, mem_kv, layer):
    bd = q.shape[0]
    out = pl.pallas_call(
        _xattn_step_kernel,
        grid=(bd,),
        in_specs=[
            pl.BlockSpec((None, X_HEADS, X_HEAD_DIM), lambda i: (i, 0, 0)),
            pl.BlockSpec((None, None, MEM_LEN, 2, X_HEADS, X_HEAD_DIM), lambda i: (layer, i, 0, 0, 0, 0)),
        ],
        out_specs=pl.BlockSpec((None, X_HEADS, X_HEAD_DIM), lambda i: (i, 0, 0)),
        out_shape=jax.ShapeDtypeStruct((bd, X_HEADS, X_HEAD_DIM), F32),
        compiler_params=_params("parallel"),
        name="xattn_step",
    )(q.reshape(bd, X_HEADS, X_HEAD_DIM), mem_kv)
    return out.reshape(bd, X_WIDTH)


FFN_HALO = 16
FFN_TAIL = 8


def _ffn_up_prompt_kernel(x_ref, xh_ref, g_ref, wg_ref, wv_ref, cwg_ref, cwv_ref, cbg_ref, cbv_ref,
                          act_ref, tg_ref, tv_ref, xn_ref, ug_ref, uv_ref, *, tm, seq_tiles, chunk):
    i = pl.program_id(0)

    @pl.when(pl.program_id(1) == 0)
    def _():
        _norm_rows(x_ref, g_ref, xn_ref, tm, chunk, dst_offset=FFN_HALO)
        halo = _rms_scale(xh_ref[...], g_ref[...])
        halo = jnp.where(i % seq_tiles == 0, 0.0, halo)
        xn_ref[0:FFN_HALO, :] = halo.astype(BF16)

    xn = xn_ref[...]
    ug_ref[...] = jnp.dot(xn, wg_ref[...], preferred_element_type=F32)
    uv_ref[...] = jnp.dot(xn, wv_ref[...], preferred_element_type=F32)

    def conv(u_ref, cw_ref, cb_ref):
        lead = FFN_TAIL
        u = u_ref[FFN_HALO - lead:FFN_HALO + tm, :]
        c = cb_ref[...] + cw_ref[0:1, :] * pltpu.roll(u, 2, 0) + cw_ref[1:2, :] * pltpu.roll(u, 1, 0) + cw_ref[2:3, :] * u
        return c[lead:lead + tm]

    act_ref[...] = (_gelu_tanh(conv(ug_ref, cwg_ref, cbg_ref)) * conv(uv_ref, cwv_ref, cbv_ref)).astype(act_ref.dtype)
    tg_ref[...] = ug_ref[FFN_HALO + tm - FFN_TAIL:FFN_HALO + tm, :]
    tv_ref[...] = uv_ref[FFN_HALO + tm - FFN_TAIL:FFN_HALO + tm, :]


def ffn_up_prompt(x, g, w_up, conv_w, conv_b3, layer, *, seq, tm, tn):
    m, k = x.shape
    assert seq % tm == 0 and D_FF % tn == 0 and tm % FFN_HALO == 0
    nj = D_FF // tn
    seq_tiles = seq // tm
    chunk = min(256, tm)
    halo_blocks = tm // FFN_HALO
    act, tail_g, tail_v = pl.pallas_call(
        functools.partial(_ffn_up_prompt_kernel, tm=tm, seq_tiles=seq_tiles, chunk=chunk),
        grid=(m // tm, nj),
        in_specs=[
            pl.BlockSpec((tm, k), lambda i, j: (i, 0)),
            pl.BlockSpec((FFN_HALO, k), lambda i, j: (jnp.maximum(i * halo_blocks - 1, 0), 0)),
            pl.BlockSpec((1, k), lambda i, j: (0, 0)),
            pl.BlockSpec((None, k, tn), lambda i, j: (layer, 0, j)),
            pl.BlockSpec((None, k, tn), lambda i, j: (layer, 0, j + nj)),
            pl.BlockSpec((None, 3, tn), lambda i, j: (layer, 0, j)),
            pl.BlockSpec((None, 3, tn), lambda i, j: (layer, 0, j + nj)),
            pl.BlockSpec((None, 1, tn), lambda i, j: (layer, 0, j)),
            pl.BlockSpec((None, 1, tn), lambda i, j: (layer, 0, j + nj)),
        ],
        out_specs=[
            pl.BlockSpec((tm, tn), lambda i, j: (i, j)),
            pl.BlockSpec((None, FFN_TAIL, tn), lambda i, j: (i, 0, j)),
            pl.BlockSpec((None, FFN_TAIL, tn), lambda i, j: (i, 0, j)),
        ],
        out_shape=[
            jax.ShapeDtypeStruct((m, D_FF), BF16),
            jax.ShapeDtypeStruct((m // tm, FFN_TAIL, D_FF), F32),
            jax.ShapeDtypeStruct((m // tm, FFN_TAIL, D_FF), F32),
        ],
        scratch_shapes=[
            pltpu.VMEM((tm + FFN_HALO, k), BF16),
            pltpu.VMEM((tm + FFN_HALO, tn), F32),
            pltpu.VMEM((tm + FFN_HALO, tn), F32),
        ],
        compiler_params=_params("parallel", "arbitrary"),
        name="ffn_up_prompt",
    )(x, x, g.reshape(1, k), w_up, w_up, conv_w, conv_w, conv_b3, conv_b3)
    return act, jnp.concatenate([tail_g, tail_v], axis=-1)


def _ffn_up_step_kernel(x_ref, g_ref, wg_ref, wv_ref, cwg_ref, cwv_ref, cbg_ref, cbv_ref, hg_ref, hv_ref,
                        act_ref, ng_ref, nv_ref, xn_ref, *, bd):
    @pl.when(pl.program_id(0) == 0)
    def _():
        _norm_rows(x_ref, g_ref, xn_ref, bd, bd)

    xn = xn_ref[...]
    ug = jnp.dot(xn, wg_ref[...], preferred_element_type=F32)
    uv = jnp.dot(xn, wv_ref[...], preferred_element_type=F32)

    def conv(u, h_ref, cw_ref, cb_ref):
        return cb_ref[...] + cw_ref[0:1, :] * h_ref[0] + cw_ref[1:2, :] * h_ref[1] + cw_ref[2:3, :] * u

    act_ref[...] = (_gelu_tanh(conv(ug, hg_ref, cwg_ref, cbg_ref)) * conv(uv, hv_ref, cwv_ref, cbv_ref)).astype(act_ref.dtype)
    ng_ref[0] = hg_ref[1]
    ng_ref[1] = ug
    nv_ref[0] = hv_ref[1]
    nv_ref[1] = uv


def ffn_up_step(x, g, w_up, conv_w, conv_b3, layer, hist_t, *, tn):
    bd, k = x.shape
    nj = D_FF // tn
    act, new_g, new_v = pl.pallas_call(
        functools.partial(_ffn_up_step_kernel, bd=bd),
        grid=(nj,),
        in_specs=[
            pl.BlockSpec((bd, k), lambda j: (0, 0)),
            pl.BlockSpec((1, k), lambda j: (0, 0)),
            pl.BlockSpec((None, k, tn), lambda j: (layer, 0, j)),
            pl.BlockSpec((None, k, tn), lambda j: (layer, 0, j + nj)),
            pl.BlockSpec((None, 3, tn), lambda j: (layer, 0, j)),
            pl.BlockSpec((None, 3, tn), lambda j: (layer, 0, j + nj)),
            pl.BlockSpec((None, 1, tn), lambda j: (layer, 0, j)),
            pl.BlockSpec((None, 1, tn), lambda j: (layer, 0, j + nj)),
            pl.BlockSpec((2, bd, tn), lambda j: (0, 0, j)),
            pl.BlockSpec((2, bd, tn), lambda j: (0, 0, j + nj)),
        ],
        out_specs=[
            pl.BlockSpec((bd, tn), lambda j: (0, j)),
            pl.BlockSpec((2, bd, tn), lambda j: (0, 0, j)),
            pl.BlockSpec((2, bd, tn), lambda j: (0, 0, j)),
        ],
        out_shape=[
            jax.ShapeDtypeStruct((bd, D_FF), BF16),
            jax.ShapeDtypeStruct((2, bd, D_FF), F32),
            jax.ShapeDtypeStruct((2, bd, D_FF), F32),
        ],
        scratch_shapes=[pltpu.VMEM((bd, k), BF16)],
        compiler_params=_params("arbitrary"),
        name="ffn_up_step",
    )(x, g.reshape(1, k), w_up, w_up, conv_w, conv_w, conv_b3, conv_b3, hist_t, hist_t)
    return act, jnp.concatenate([new_g, new_v], axis=-1)


TM = 1024
TN = 1024
TM_POST = 512
TK_POST = 2048
TN_MERGE = 512
TN_FFN = 512
XATTN_ROWS = 512


def _bf16_weights(w_in, pool_w, gla_w_a2, w_br_pool, w_br_dil, w_br_gla, w_mix_out, w_xq, w_xkv, w_xo, w_up, w_down):
    w_in_t, w_ga_t = prep_w_in(jnp.swapaxes(w_in, 1, 2), tn=TN)
    return dict(
        w_in_t=w_in_t,
        w_ga_t=w_ga_t,
        w_a2p=jnp.pad(gla_w_a2, ((0, 0), (0, LANES - GLA_RANK), (0, 0))).astype(BF16),
        pool_w=pool_w.astype(BF16),
        w_br_pool=w_br_pool.astype(BF16),
        w_br_dil=w_br_dil.astype(BF16),
        w_br_gla=w_br_gla.astype(BF16),
        w_mix_out=w_mix_out.astype(BF16),
        w_xq=w_xq.astype(BF16),
        w_xkv=w_xkv.astype(BF16),
        w_xo=w_xo.astype(BF16),
        w_up=w_up.astype(BF16),
        w_down=w_down.astype(BF16),
    )


def kernel(x_prompt, x_sample, state_pool, cache_dil1_kv, cache_dil2_kv, cache_dil3_kv, state_gla, cache_mem_kv, state_ffn_conv, mem_prompt, rel_bias, norm_mix_pre, norm_mix_post, w_in, pool_w, pool_scale, gla_w_a2, gla_b_a, gla_norm, w_br_pool, w_br_dil, w_br_gla, w_mix_out, norm_x_pre, norm_x_post, norm_mem, w_xq, w_xkv, w_xo, norm_ffn_pre, norm_ffn_post, w_up, conv_w, conv_b, w_down):
    b, seq, d = x_prompt.shape
    bd = x_sample.shape[0]
    depth = w_in.shape[0]
    m = b * seq
    assert x_sample.shape[1] == 1 and d == D_MODEL and w_in.shape[2] == N_MAIN + GLA_RANK + N_GATES
    slot_bias = _slot_biases(rel_bias)
    band_bias = _band_bias(slot_bias)
    dil_caches = (cache_dil1_kv, cache_dil2_kv, cache_dil3_kv)

    xp = x_prompt.reshape(m, d)
    xs = x_sample.reshape(bd, d)
    pool_p, gla_p, mem_p, conv_p = [], [], [], []
    pool_s, gla_s, conv_s = [], [], []
    dil_p = [[] for _ in range(3)]
    dil_s = [[] for _ in range(3)]
    seq_tiles = seq // TM

    w = _bf16_weights(w_in, pool_w, gla_w_a2, w_br_pool, w_br_dil, w_br_gla, w_mix_out, w_xq, w_xkv, w_xo, w_up, w_down)
    conv_b3 = conv_b.reshape(depth, 1, 2 * D_FF)
    mem_rows = mem_prompt.reshape(b * MEM_LEN, d)
    for l in range(depth):
        w_a2p = w["w_a2p"][l]

        zm, zkv_wide, za, zs, zas = in_proj(xp, xs, norm_mix_pre[l], w["w_in_t"], w["w_ga_t"], l, tm=TM, tn=TN)
        z3 = zm.reshape(b, seq, N_PROJ)
        y_pool = pool_prompt(z3, w["pool_w"], l, pool_scale[l]).reshape(m, POOL_WIDTH)
        y_dil = dil_prompt(z3, band_bias).reshape(m, DIL_WIDTH)
        y_gla, gla_new = gla_prompt(z3, za.reshape(b, seq, LANES), w_a2p, gla_b_a[l], gla_norm[l])
        merged = branch_merge(y_pool, y_dil, y_gla.reshape(m, GLA_VAL_WIDTH), w["w_br_pool"], w["w_br_dil"], w["w_br_gla"], l,
                              zm, tm=TM, tn=TN_MERGE)
        xp = mm_post(merged, w["w_mix_out"], l, norm_mix_post[l], xp, tm=TM_POST, tk=TK_POST)
        mem_kv = norm_mm(mem_rows, norm_mem[l], w["w_xkv"], l, tm=TM, tn=TN)
        q = norm_mm(xp, norm_x_pre[l], w["w_xq"], l, tm=TM, tn=TN, out_dtype=BF16)
        o = xattn(q.reshape(b, seq, X_WIDTH), mem_kv.reshape(b, MEM_LEN, 2 * X_WIDTH), tt=XATTN_ROWS)
        xp = mm_post(o.reshape(m, X_WIDTH), w["w_xo"], l, norm_x_post[l], xp, tm=TM_POST, tk=TK_POST)
        act, tails = ffn_up_prompt(xp, norm_ffn_pre[l], w["w_up"], conv_w, conv_b3, l, seq=seq, tm=TM, tn=TN_FFN)
        xp = mm_post(act, w["w_down"], l, norm_ffn_post[l], xp, tm=TM_POST, tk=TK_POST)

        pool_p.append(z3[:, seq - POOL_HIST:, COL_POOL:COL_POOL + POOL_WIDTH])
        for g, (win, _) in enumerate(DIL_CONFIGS):
            keep = min(win, seq)
            if g == 2:
                kv = zkv_wide.reshape(b, seq, 2 * DIL_WIDTH)[:, seq - keep:]
            else:
                kv = z3[:, seq - keep:, COL_DIL_K[g]:COL_DIL_K[g] + 2 * DIL_WIDTH]
            dil_p[g].append(kv.reshape(b, keep, 2, DIL_HEADS, DIL_HEAD_DIM))
        gla_p.append(gla_new)
        mem_p.append(mem_kv.reshape(b, MEM_LEN, 2, X_HEADS, X_HEAD_DIM))
        conv_p.append(tails.reshape(b, seq_tiles, FFN_TAIL, 2 * D_FF)[:, seq_tiles - 1, FFN_TAIL - 2:, :])

        y_pool_s, pool_new_t = pool_step(jnp.swapaxes(state_pool[l], 0, 1), zs, w["pool_w"], l, pool_scale[l])
        y_dil_s = dil_step(zs, dil_caches, l, slot_bias).astype(BF16)
        y_gla_s, gla_new_s = gla_step(zs, zas, state_gla, l, w_a2p, gla_b_a[l], gla_norm[l])
        merged_s = branch_merge(y_pool_s, y_dil_s, y_gla_s.astype(BF16), w["w_br_pool"], w["w_br_dil"], w["w_br_gla"], l,
                                zs, tm=bd, tn=TN_MERGE)
        xs = mm_post(merged_s, w["w_mix_out"], l, norm_mix_post[l], xs, tm=bd, tk=TK_POST)
        q_s = norm_mm(xs, norm_x_pre[l], w["w_xq"], l, tm=bd, tn=TN)
        o_s = xattn_step(q_s, cache_mem_kv, l).astype(BF16)
        xs = mm_post(o_s, w["w_xo"], l, norm_x_post[l], xs, tm=bd, tk=TK_POST)
        act_s, conv_new_t = ffn_up_step(xs, norm_ffn_pre[l], w["w_up"], conv_w, conv_b3, l,
                                        jnp.swapaxes(state_ffn_conv[l], 0, 1), tn=TN_FFN)
        xs = mm_post(act_s, w["w_down"], l, norm_ffn_post[l], xs, tm=bd, tk=TK_POST)

        pool_s.append(jnp.swapaxes(pool_new_t, 0, 1))
        for g in range(3):
            kv = zs[:, COL_DIL_K[g]:COL_DIL_K[g] + 2 * DIL_WIDTH]
            dil_s[g].append(kv.reshape(bd, 1, 2, DIL_HEADS, DIL_HEAD_DIM))
        gla_s.append(gla_new_s)
        conv_s.append(jnp.swapaxes(conv_new_t, 0, 1))

    return (xp.reshape(b, seq, d), xs.reshape(bd, 1, d),
            jnp.stack(pool_p), jnp.stack(dil_p[0]), jnp.stack(dil_p[1]), jnp.stack(dil_p[2]), jnp.stack(gla_p), jnp.stack(mem_p), jnp.stack(conv_p),
            jnp.stack(pool_s), jnp.stack(dil_s[0]), jnp.stack(dil_s[1]), jnp.stack(dil_s[2]), jnp.stack(gla_s), jnp.stack(conv_s))
```

```python
import functools
import math

import jax
import jax.numpy as jnp
from jax import lax
from jax.experimental import pallas as pl
from jax.experimental.pallas import tpu as pltpu

F32 = jnp.float32
BF16 = jnp.bfloat16
EPS = 1e-6
NEG_INF = -1e30

VMEM_LIMIT_BYTES = 56 * 1024 * 1024
LANES = 128

D_MODEL = 2048
POOL_WINDOWS = (2, 4, 8, 16)
POOL_GROUP = 256
POOL_WIDTH = 1024
POOL_HIST = 15
DIL_CONFIGS = ((128, 1), (512, 4), (2048, 16))
DIL_HEADS = 8
DIL_HEAD_DIM = 128
DIL_WIDTH = 1024
DIL_BLOCK = 128
GLA_HEADS = 4
GLA_DK = 256
GLA_DV = 512
GLA_KEY_WIDTH = 1024
GLA_VAL_WIDTH = 2048
GLA_RANK = 16
GLA_TAU = 16.0
GLA_CHUNK = 64
REL_BUCKETS = 32
REL_MAX_DIST = 2048
MEM_LEN = 256
X_HEADS = 4
X_HEAD_DIM = 256
X_WIDTH = 1024
D_FF = 5632
PAST_LEN = 8192

COL_POOL = 0
COL_DIL_Q = tuple(1024 + 3072 * g for g in range(3))
COL_DIL_K = tuple(2048 + 3072 * g for g in range(3))
COL_DIL_V = tuple(3072 + 3072 * g for g in range(3))
COL_GQ = 10240
COL_GK = 11264
COL_GV = 12288
COL_GR = 14336
N_MAIN = 16384
N_GATES = 3 * D_MODEL
N_PROJ = N_MAIN + N_GATES


def _params(*semantics):
    return pltpu.CompilerParams(dimension_semantics=semantics, vmem_limit_bytes=VMEM_LIMIT_BYTES)


def _sigmoid(x):
    return 0.5 * jnp.tanh(0.5 * x) + 0.5


def _log_sigmoid(x):
    return jnp.minimum(x, 0.0) - jnp.log(1.0 + jnp.exp(-jnp.abs(x)))


def _gelu_tanh(x):
    return x * (0.5 * (1.0 + jnp.tanh(math.sqrt(2.0 / math.pi) * (x + 0.044715 * (x * x * x)))))


def _rms_scale(y, g):
    return y * lax.rsqrt(jnp.mean(y * y, axis=-1, keepdims=True) + EPS) * g


def _norm_rows(x_ref, g_ref, xn_ref, rows, chunk, dst_offset=0):
    g = g_ref[...]

    def body(c, carry):
        r0 = pl.multiple_of(c * chunk, chunk)
        xn_ref[pl.ds(dst_offset + r0, chunk), :] = _rms_scale(x_ref[pl.ds(r0, chunk), :], g).astype(BF16)
        return carry

    lax.fori_loop(0, rows // chunk, body, 0)


def _prep_w_in_kernel(w_ref, wx_ref, o_ref, ga_ref, *, n_plain, shift):
    j = pl.program_id(1)

    @pl.when(j < n_plain)
    def _():
        o_ref[...] = w_ref[...].astype(BF16)

    @pl.when(j >= n_plain)
    def _():
        tn = o_ref.shape[0]
        o_ref[0:tn - shift, :] = w_ref[shift:tn, :].astype(BF16)
        o_ref[tn - shift:tn, :] = wx_ref[...].astype(BF16)

    @pl.when(j == n_plain)
    def _():
        ga_ref[0:shift, :] = w_ref[0:shift, :].astype(BF16)
        ga_ref[shift:LANES, :] = jnp.zeros((LANES - shift, ga_ref.shape[1]), BF16)


def prep_w_in(w_in_t, *, tn):
    depth, _, k = w_in_t.shape
    n_plain = N_MAIN // tn
    nj = (N_MAIN + N_GATES) // tn
    return pl.pallas_call(
        functools.partial(_prep_w_in_kernel, n_plain=n_plain, shift=GLA_RANK),
        grid=(depth, nj),
        in_specs=[
            pl.BlockSpec((None, tn, k), lambda l, j: (l, j, 0)),
            pl.BlockSpec((None, GLA_RANK, k), lambda l, j: (l, jnp.maximum(j + 1, n_plain) * (tn // GLA_RANK), 0)),
        ],
        out_specs=[
            pl.BlockSpec((None, tn, k), lambda l, j: (l, j, 0)),
            pl.BlockSpec((None, LANES, k), lambda l, j: (l, 0, 0)),
        ],
        out_shape=[jax.ShapeDtypeStruct((depth, N_MAIN + N_GATES, k), BF16), jax.ShapeDtypeStruct((depth, LANES, k), BF16)],
        compiler_params=_params("parallel", "arbitrary"),
        name="prep_w_in",
    )(w_in_t, w_in_t)


def _in_proj_kernel(x_ref, xs_ref, g_ref, w_ref, wga_ref, o_ref, kv_ref, ga_ref, os_ref, gas_ref, xn_ref, xsn_ref,
                    *, tm, chunk, kv_lo, kv_hi):
    i = pl.program_id(0)
    j = pl.program_id(1)
    nt = (((1,), (1,)), ((), ()))

    @pl.when(j == 0)
    def _():
        _norm_rows(x_ref, g_ref, xn_ref, tm, chunk)
        ga_ref[...] = lax.dot_general(xn_ref[...], wga_ref[...], nt, preferred_element_type=F32)

    o_ref[...] = lax.dot_general(xn_ref[...], w_ref[...], nt, preferred_element_type=F32)

    @pl.when(jnp.logical_and(j >= kv_lo, j < kv_hi))
    def _():
        kv_ref[...] = o_ref[...]

    @pl.when(i == 0)
    def _():
        @pl.when(j == 0)
        def _():
            xsn_ref[...] = _rms_scale(xs_ref[...], g_ref[...]).astype(BF16)
            gas_ref[...] = lax.dot_general(xsn_ref[...], wga_ref[...], nt, preferred_element_type=F32)

        os_ref[...] = lax.dot_general(xsn_ref[...], w_ref[...], nt, preferred_element_type=F32)


def in_proj(x, xs, g, w_t, w_ga_t, layer, *, tm, tn):
    m, k = x.shape
    ms = xs.shape[0]
    n = w_t.shape[1]
    assert m % tm == 0 and n % tn == 0 and COL_DIL_K[2] % tn == 0
    chunk = min(256, tm)
    nj = n // tn
    kv_lo = COL_DIL_K[2] // tn
    n_kv = 2 * DIL_WIDTH // tn
    return pl.pallas_call(
        functools.partial(_in_proj_kernel, tm=tm, chunk=chunk, kv_lo=kv_lo, kv_hi=kv_lo + n_kv),
        grid=(m // tm, nj),
        in_specs=[
            pl.BlockSpec((tm, k), lambda i, j: (i, 0)),
            pl.BlockSpec((ms, k), lambda i, j: (0, 0)),
            pl.BlockSpec((1, k), lambda i, j: (0, 0)),
            pl.BlockSpec((None, tn, k), lambda i, j: (layer, j, 0)),
            pl.BlockSpec((None, LANES, k), lambda i, j: (layer, 0, 0)),
        ],
        out_specs=[
            pl.BlockSpec((tm, tn), lambda i, j: (i, j)),
            pl.BlockSpec((tm, tn), lambda i, j: (i, jnp.clip(j - kv_lo, 0, n_kv - 1))),
            pl.BlockSpec((tm, LANES), lambda i, j: (i, 0)),
            pl.BlockSpec((ms, tn), lambda i, j: (0, jnp.where(i == 0, j, nj - 1))),
            pl.BlockSpec((ms, LANES), lambda i, j: (0, 0)),
        ],
        out_shape=[
            jax.ShapeDtypeStruct((m, n), F32),
            jax.ShapeDtypeStruct((m, 2 * DIL_WIDTH), F32),
            jax.ShapeDtypeStruct((m, LANES), F32),
            jax.ShapeDtypeStruct((ms, n), F32),
            jax.ShapeDtypeStruct((ms, LANES), F32),
        ],
        scratch_shapes=[pltpu.VMEM((tm, k), BF16), pltpu.VMEM((ms, k), BF16)],
        compiler_params=_params("arbitrary", "arbitrary"),
        name="in_proj",
    )(x, xs, g.reshape(1, k), w_t, w_ga_t)


def _norm_mm_kernel(x_ref, g_ref, w_ref, o_ref, xn_ref, *, tm, chunk):
    @pl.when(pl.program_id(1) == 0)
    def _():
        _norm_rows(x_ref, g_ref, xn_ref, tm, chunk)

    o_ref[...] = jnp.dot(xn_ref[...], w_ref[...], preferred_element_type=F32).astype(o_ref.dtype)


def norm_mm(x, g, w, layer, *, tm, tn, out_dtype=F32):
    m, k = x.shape
    n = w.shape[2]
    tm = min(tm, m)
    tn = min(tn, n)
    assert m % tm == 0 and n % tn == 0
    chunk = min(256, tm)
    return pl.pallas_call(
        functools.partial(_norm_mm_kernel, tm=tm, chunk=chunk),
        grid=(m // tm, n // tn),
        in_specs=[
            pl.BlockSpec((tm, k), lambda i, j: (i, 0)),
            pl.BlockSpec((1, k), lambda i, j: (0, 0)),
            pl.BlockSpec((None, k, tn), lambda i, j: (layer, 0, j)),
        ],
        out_specs=pl.BlockSpec((tm, tn), lambda i, j: (i, j)),
        out_shape=jax.ShapeDtypeStruct((m, n), out_dtype),
        scratch_shapes=[pltpu.VMEM((tm, k), BF16)],
        compiler_params=_params("parallel", "arbitrary"),
        name="norm_mm",
    )(x, g.reshape(1, k), w)


def _mm_post_kernel(a_ref, w_ref, g_ref, res_ref, o_ref, acc_ref, *, nk):
    kk = pl.program_id(1)
    part = jnp.dot(a_ref[...], w_ref[...], preferred_element_type=F32)
    if nk == 1:
        o_ref[...] = res_ref[...] + _rms_scale(part, g_ref[...])
        return

    @pl.when(kk == 0)
    def _():
        acc_ref[...] = part

    @pl.when(jnp.logical_and(kk > 0, kk < nk - 1))
    def _():
        acc_ref[...] += part

    @pl.when(kk == nk - 1)
    def _():
        o_ref[...] = res_ref[...] + _rms_scale(acc_ref[...] + part, g_ref[...])


def mm_post(a, w, layer, g, res, *, tm, tk):
    m, k = a.shape
    n = w.shape[2]
    tm = min(tm, m)
    tk = k if k <= tk else k // 2
    assert m % tm == 0 and k % tk == 0 and tk % LANES == 0
    return pl.pallas_call(
        functools.partial(_mm_post_kernel, nk=k // tk),
        grid=(m // tm, k // tk),
        in_specs=[
            pl.BlockSpec((tm, tk), lambda i, kk: (i, kk)),
            pl.BlockSpec((None, tk, n), lambda i, kk: (layer, kk, 0)),
            pl.BlockSpec((1, n), lambda i, kk: (0, 0)),
            pl.BlockSpec((tm, n), lambda i, kk: (i, 0)),
        ],
        out_specs=pl.BlockSpec((tm, n), lambda i, kk: (i, 0)),
        out_shape=jax.ShapeDtypeStruct((m, n), F32),
        scratch_shapes=[pltpu.VMEM((tm, n), F32)],
        compiler_params=_params("parallel", "arbitrary"),
        name="mm_post",
    )(a, w, g.reshape(1, n), res)


def _branch_merge_kernel(yp_ref, yd_ref, yg_ref, wp_ref, wd_ref, wg_ref, g0_ref, g1_ref, g2_ref, o_ref):
    bp = jnp.dot(yp_ref[...], wp_ref[...], preferred_element_type=F32)
    bd = jnp.dot(yd_ref[...], wd_ref[...], preferred_element_type=F32)
    bg = jnp.dot(yg_ref[...], wg_ref[...], preferred_element_type=F32)
    merged = _sigmoid(g0_ref[...]) * bp + _sigmoid(g1_ref[...]) * bd + _sigmoid(g2_ref[...]) * bg
    o_ref[...] = merged.astype(o_ref.dtype)


def branch_merge(y_pool, y_dil, y_gla, w_pool, w_dil, w_gla, layer, z_main, *, tm, tn):
    m = y_pool.shape[0]
    tm = min(tm, m)
    assert m % tm == 0 and D_MODEL % tn == 0
    gate_blk = [(N_MAIN + b * D_MODEL) // tn for b in range(3)]

    def gate_spec(b):
        return pl.BlockSpec((tm, tn), lambda i, j: (i, gate_blk[b] + j))

    return pl.pallas_call(
        _branch_merge_kernel,
        grid=(m // tm, D_MODEL // tn),
        in_specs=[
            pl.BlockSpec((tm, POOL_WIDTH), lambda i, j: (i, 0)),
            pl.BlockSpec((tm, DIL_WIDTH), lambda i, j: (i, 0)),
            pl.BlockSpec((tm, GLA_VAL_WIDTH), lambda i, j: (i, 0)),
            pl.BlockSpec((None, POOL_WIDTH, tn), lambda i, j: (layer, 0, j)),
            pl.BlockSpec((None, DIL_WIDTH, tn), lambda i, j: (layer, 0, j)),
            pl.BlockSpec((None, GLA_VAL_WIDTH, tn), lambda i, j: (layer, 0, j)),
            gate_spec(0),
            gate_spec(1),
            gate_spec(2),
        ],
        out_specs=pl.BlockSpec((tm, tn), lambda i, j: (i, j)),
        out_shape=jax.ShapeDtypeStruct((m, D_MODEL), BF16),
        compiler_params=_params("parallel", "arbitrary"),
        name="branch_merge",
    )(y_pool, y_dil, y_gla, w_pool, w_dil, w_gla, z_main, z_main, z_main)


def _pool_prompt_kernel(u_ref, w_ref, s_ref, o_ref, buf_a, buf_b, *, seq):
    pad = POOL_HIST + 1
    zeros = jnp.zeros((pad, POOL_GROUP), F32)
    t = lax.broadcasted_iota(jnp.int32, (seq, 1), 0)
    for g, win in enumerate(POOL_WINDOWS):
        cols = slice(g * POOL_GROUP, (g + 1) * POOL_GROUP)
        u = u_ref[:, cols]
        cur, nxt = buf_a, buf_b
        cur[0:pad, :] = zeros
        nxt[0:pad, :] = zeros
        cur[pad:pad + seq, :] = u
        k = 1
        while k < win:
            nxt[pad:pad + seq, :] = cur[pad:pad + seq, :] + cur[pad - k:pad - k + seq, :]
            cur, nxt = nxt, cur
            k *= 2
        cnt = jnp.minimum(win, t + 1).astype(F32)
        d = cur[pad:pad + seq, :] / cnt - u
        y = jnp.dot(d.astype(BF16), w_ref[g], preferred_element_type=F32) * s_ref[:, cols]
        o_ref[:, cols] = y.astype(o_ref.dtype)


def pool_prompt(z3, pool_w, layer, pool_scale):
    b, seq, _ = z3.shape
    return pl.pallas_call(
        functools.partial(_pool_prompt_kernel, seq=seq),
        grid=(b,),
        in_specs=[
            pl.BlockSpec((None, seq, POOL_WIDTH), lambda i: (i, 0, COL_POOL // POOL_WIDTH)),
            pl.BlockSpec((None, len(POOL_WINDOWS), POOL_GROUP, POOL_GROUP), lambda i: (layer, 0, 0, 0)),
            pl.BlockSpec((1, POOL_WIDTH), lambda i: (0, 0)),
        ],
        out_specs=pl.BlockSpec((None, seq, POOL_WIDTH), lambda i: (i, 0, 0)),
        out_shape=jax.ShapeDtypeStruct((b, seq, POOL_WIDTH), BF16),
        scratch_shapes=[pltpu.VMEM((seq + POOL_HIST + 1, POOL_GROUP), F32)] * 2,
        compiler_params=_params("parallel"),
        name="pool_prompt",
    )(z3, pool_w, pool_scale.reshape(1, POOL_WIDTH))


def _pool_step_kernel(hist_ref, u_ref, w_ref, s_ref, y_ref, new_ref):
    u = u_ref[...]
    for r in range(POOL_HIST - 1):
        new_ref[r] = hist_ref[r + 1]
    new_ref[POOL_HIST - 1] = u
    for g, win in enumerate(POOL_WINDOWS):
        cols = slice(g * POOL_GROUP, (g + 1) * POOL_GROUP)
        ug = u[:, cols]
        acc = ug
        for r in range(POOL_HIST - (win - 1), POOL_HIST):
            acc = acc + hist_ref[r, :, cols]
        d = acc / float(win) - ug
        y = jnp.dot(d.astype(BF16), w_ref[g], preferred_element_type=F32) * s_ref[:, cols]
        y_ref[:, cols] = y.astype(y_ref.dtype)


def pool_step(hist_t, z_s, pool_w, layer, pool_scale):
    bd = z_s.shape[0]
    return pl.pallas_call(
        _pool_step_kernel,
        grid=(1,),
        in_specs=[
            pl.BlockSpec((POOL_HIST, bd, POOL_WIDTH), lambda i: (0, 0, 0)),
            pl.BlockSpec((bd, POOL_WIDTH), lambda i: (0, COL_POOL // POOL_WIDTH)),
            pl.BlockSpec((None, len(POOL_WINDOWS), POOL_GROUP, POOL_GROUP), lambda i: (layer, 0, 0, 0)),
            pl.BlockSpec((1, POOL_WIDTH), lambda i: (0, 0)),
        ],
        out_specs=[
            pl.BlockSpec((bd, POOL_WIDTH), lambda i: (0, 0)),
            pl.BlockSpec((POOL_HIST, bd, POOL_WIDTH), lambda i: (0, 0, 0)),
        ],
        out_shape=[
            jax.ShapeDtypeStruct((bd, POOL_WIDTH), BF16),
            jax.ShapeDtypeStruct((POOL_HIST, bd, POOL_WIDTH), F32),
        ],
        compiler_params=_params("arbitrary"),
        name="pool_step",
    )(hist_t, z_s, pool_w, pool_scale.reshape(1, POOL_WIDTH))


def _rel_bucket(dist):
    max_exact = REL_BUCKETS // 2
    scaled = jnp.log(jnp.maximum(dist, 1).astype(F32) / max_exact) / math.log(REL_MAX_DIST / max_exact)
    large = jnp.minimum(max_exact + (scaled * (REL_BUCKETS - max_exact)).astype(jnp.int32), REL_BUCKETS - 1)
    return jnp.where(dist < max_exact, dist, large)


def _select_rows(table_t, index, n):
    onehot = (index.reshape(1, -1) == jnp.arange(n, dtype=jnp.int32)[:, None]).astype(F32)
    out = jnp.dot(table_t, onehot, precision=lax.Precision.HIGHEST, preferred_element_type=F32)
    return out.reshape((table_t.shape[0],) + index.shape)


def _slot_biases(rel_bias):
    out = []
    for g, (win, dil) in enumerate(DIL_CONFIGS):
        dist = jnp.arange(win // dil + 1, dtype=jnp.int32) * dil
        table_t = rel_bias[:, g * DIL_HEADS:(g + 1) * DIL_HEADS].T.astype(F32)
        out.append(_select_rows(table_t, _rel_bucket(dist), REL_BUCKETS))
    return out


def _band_bias(slot_bias):
    qi = jnp.arange(DIL_BLOCK, dtype=jnp.int32)[:, None] + DIL_BLOCK
    ki = jnp.arange(2 * DIL_BLOCK, dtype=jnp.int32)[None, :]
    rel = qi - ki
    out = []
    for g, (win, dil) in enumerate(DIL_CONFIGS):
        n_slots = win // dil
        ok = (rel >= 0) & (rel <= n_slots)
        b = _select_rows(slot_bias[g], jnp.clip(rel, 0, n_slots), n_slots + 1)
        out.append(jnp.where(ok[None], b, NEG_INF))
    return jnp.stack(out, axis=0)


def _dil_prompt_kernel(q0, k0, v0, q1, k1, v1, q2, k2, v2, bias_ref, o_ref, o_scr, lse_scr, *, seq):
    qs, ks, vs = (q0, q1, q2), (k0, k1, k2), (v0, v1, v2)
    scale = DIL_HEAD_DIM ** -0.5
    blk = DIL_BLOCK
    nt = (((1,), (1,)), ((), ()))

    def rows(ref, start, dil):
        if dil == 1:
            return ref[pl.ds(start, blk), :]
        return ref[pl.ds(start, blk, stride=dil), :]

    for g, (_, dil) in enumerate(DIL_CONFIGS):
        nb = seq // dil // blk
        for r in range(dil):
            for ub in range(nb):
                start = r + dil * ub * blk
                q = rows(qs[g], start, dil).astype(BF16)
                kc = rows(ks[g], start, dil).astype(BF16)
                vc = rows(vs[g], start, dil).astype(BF16)
                if ub == 0:
                    kk, vv, bias = kc, vc, bias_ref[g, :, blk:]
                else:
                    prev = start - dil * blk
                    kk = jnp.concatenate([rows(ks[g], prev, dil).astype(BF16), kc], axis=0)
                    vv = jnp.concatenate([rows(vs[g], prev, dil).astype(BF16), vc], axis=0)
                    bias = bias_ref[g]
                s = lax.dot_general(q, kk, nt, preferred_element_type=F32) * scale + bias
                m = jnp.max(s, axis=-1, keepdims=True)
                e = jnp.exp(s - m)
                l = jnp.sum(e, axis=-1, keepdims=True)
                o = jnp.dot(e.astype(BF16), vv, preferred_element_type=F32) / l
                lse = jnp.broadcast_to(m + jnp.log(l), (blk, LANES))
                if dil == 1:
                    o_scr[g, pl.ds(start, blk), :] = o
                    lse_scr[g, pl.ds(start, blk), :] = lse
                else:
                    o_scr[g, pl.ds(start, blk, stride=dil), :] = o
                    lse_scr[g, pl.ds(start, blk, stride=dil), :] = lse

    chunk = 256

    def combine(c, carry):
        sl = pl.ds(pl.multiple_of(c * chunk, chunk), chunk)
        l0, l1, l2 = lse_scr[0, sl, :], lse_scr[1, sl, :], lse_scr[2, sl, :]
        mx = jnp.maximum(jnp.maximum(l0, l1), l2)
        w0, w1, w2 = jnp.exp(l0 - mx), jnp.exp(l1 - mx), jnp.exp(l2 - mx)
        y = (w0 * o_scr[0, sl, :] + w1 * o_scr[1, sl, :] + w2 * o_scr[2, sl, :]) / (w0 + w1 + w2)
        o_ref[sl, :] = y.astype(o_ref.dtype)
        return carry

    lax.fori_loop(0, seq // chunk, combine, 0)


def dil_prompt(z3, band_bias):
    b, seq, _ = z3.shape
    assert seq % (DIL_BLOCK * 16) == 0

    def col_spec(col):
        blk0 = col // DIL_HEAD_DIM
        return pl.BlockSpec((None, seq, DIL_HEAD_DIM), lambda i, h: (i, 0, blk0 + h))

    in_specs = []
    for g in range(3):
        in_specs += [col_spec(COL_DIL_Q[g]), col_spec(COL_DIL_K[g]), col_spec(COL_DIL_V[g])]
    in_specs.append(pl.BlockSpec((3, None, DIL_BLOCK, 2 * DIL_BLOCK), lambda i, h: (0, h, 0, 0)))
    return pl.pallas_call(
        functools.partial(_dil_prompt_kernel, seq=seq),
        grid=(b, DIL_HEADS),
        in_specs=in_specs,
        out_specs=pl.BlockSpec((None, seq, DIL_HEAD_DIM), lambda i, h: (i, 0, h)),
        out_shape=jax.ShapeDtypeStruct((b, seq, DIL_WIDTH), BF16),
        scratch_shapes=[pltpu.VMEM((3, seq, DIL_HEAD_DIM), F32), pltpu.VMEM((3, seq, LANES), F32)],
        compiler_params=_params("parallel", "arbitrary"),
        name="dil_prompt",
    )(*([z3] * 9), band_bias)


def _dil_step_kernel(qkv_ref, c0, c1, c2, bias_ref, bias0_ref, o_ref):
    caches = (c0, c1, c2)
    scale = DIL_HEAD_DIM ** -0.5
    outs, lses = [], []
    for g in range(3):
        q, kn, vn = qkv_ref[3 * g], qkv_ref[3 * g + 1], qkv_ref[3 * g + 2]
        kc = caches[g][:, 0]
        vc = caches[g][:, 1]
        s = jnp.sum(kc * q[None], axis=-1, keepdims=True) * scale + bias_ref[g]
        s_new = jnp.sum(q * kn, axis=-1, keepdims=True) * scale + bias0_ref[g]
        m = jnp.maximum(jnp.max(s, axis=0), s_new)
        p = jnp.exp(s - m[None])
        p_new = jnp.exp(s_new - m)
        l = jnp.sum(p, axis=0) + p_new
        outs.append((jnp.sum(p * vc, axis=0) + p_new * vn) / l)
        lses.append(m + jnp.log(l))
    mx = jnp.maximum(jnp.maximum(lses[0], lses[1]), lses[2])
    w0, w1, w2 = jnp.exp(lses[0] - mx), jnp.exp(lses[1] - mx), jnp.exp(lses[2] - mx)
    o_ref[...] = (w0 * outs[0] + w1 * outs[1] + w2 * outs[2]) / (w0 + w1 + w2)


def dil_step(z_s, caches, layer, slot_bias):
    bd = z_s.shape[0]
    n_slots = DIL_CONFIGS[0][0] // DIL_CONFIGS[0][1]
    qkv = z_s[:, COL_DIL_Q[0]:COL_DIL_V[2] + DIL_WIDTH].reshape(bd, 9, DIL_HEADS, DIL_HEAD_DIM)
    in_specs = [pl.BlockSpec((None, 9, DIL_HEADS, DIL_HEAD_DIM), lambda i: (i, 0, 0, 0))]
    cache_views = []
    for g, (win, dil) in enumerate(DIL_CONFIGS):
        depth = caches[g].shape[0]
        assert caches[g].shape[2] == win and win // dil == n_slots
        cache_views.append(caches[g].reshape(depth, bd, n_slots, dil, 2, DIL_HEADS, DIL_HEAD_DIM))
        in_specs.append(pl.BlockSpec((None, None, n_slots, None, 2, DIL_HEADS, DIL_HEAD_DIM),
                                     lambda i: (layer, i, 0, 0, 0, 0, 0)))
    bias_rows = jnp.stack([sb[:, n_slots:0:-1].T for sb in slot_bias], axis=0)
    bias_rows = jnp.broadcast_to(bias_rows[..., None], (3, n_slots, DIL_HEADS, LANES))
    bias_new = jnp.broadcast_to(jnp.stack([sb[:, 0] for sb in slot_bias], axis=0)[..., None], (3, DIL_HEADS, LANES))
    in_specs.append(pl.BlockSpec((3, n_slots, DIL_HEADS, LANES), lambda i: (0, 0, 0, 0)))
    in_specs.append(pl.BlockSpec((3, DIL_HEADS, LANES), lambda i: (0, 0, 0)))
    out = pl.pallas_call(
        _dil_step_kernel,
        grid=(bd,),
        in_specs=in_specs,
        out_specs=pl.BlockSpec((None, DIL_HEADS, DIL_HEAD_DIM), lambda i: (i, 0, 0)),
        out_shape=jax.ShapeDtypeStruct((bd, DIL_HEADS, DIL_HEAD_DIM), F32),
        compiler_params=_params("parallel"),
        name="dil_step",
    )(qkv, *cache_views, bias_rows, bias_new)
    return out.reshape(bd, DIL_WIDTH)


GLA_HEADS_PER_STEP = 2
GLA_CUMSUM_ROWS = 256


def _gla_prompt_kernel(q_ref, k_ref, v_ref, r_ref, ga_ref, wa_ref, ba_ref, gn_ref, y_ref, s_out_ref, st_ref, o_scr, b_scr, *, tq):
    t = pl.program_id(2)
    ck = GLA_CHUNK

    @pl.when(t == 0)
    def _():
        st_ref[...] = jnp.zeros_like(st_ref)

    la = jnp.dot(ga_ref[...].astype(BF16), wa_ref[...], preferred_element_type=F32) + ba_ref[...]
    log_a = _log_sigmoid(la) / GLA_TAU
    row = lax.broadcasted_iota(jnp.int32, (ck, ck), 0)
    col = lax.broadcasted_iota(jnp.int32, (ck, ck), 1)
    tril = row >= col
    rows_c = GLA_CUMSUM_ROWS
    rr = lax.broadcasted_iota(jnp.int32, (rows_c, rows_c), 0)
    cc = lax.broadcasted_iota(jnp.int32, (rows_c, rows_c), 1)
    prefix = jnp.logical_and(rr >= cc, rr // ck == cc // ck).astype(BF16)
    for r0 in range(0, tq, rows_c):
        la_blk = log_a[r0:r0 + rows_c, :]
        hi = la_blk.astype(BF16)
        lo = (la_blk - hi.astype(F32)).astype(BF16)
        b_scr[r0:r0 + rows_c, :] = (jnp.dot(prefix, hi, preferred_element_type=F32)
                                    + jnp.dot(prefix, lo, preferred_element_type=F32))
    nt = (((1,), (1,)), ((), ()))
    tn = (((0,), (0,)), ((), ()))
    for c in range(tq // ck):
        sl = slice(c * ck, (c + 1) * ck)
        for hh in range(GLA_HEADS_PER_STEP):
            ks = slice(hh * GLA_DK, (hh + 1) * GLA_DK)
            vs = slice(hh * GLA_DV, (hh + 1) * GLA_DV)
            b = b_scr[sl, ks]
            b_last = b[ck - 1:ck, :]
            q = q_ref[sl, ks] * (GLA_DK ** -0.5)
            k = k_ref[sl, ks]
            v = v_ref[sl, vs].astype(BF16)
            q_t = (q * jnp.exp(b)).astype(BF16)
            k_t = (k * jnp.exp(-b)).astype(BF16)
            k_h = (k * jnp.exp(b_last - b)).astype(BF16)
            decay = jnp.exp(b_last)
            a = lax.dot_general(q_t, k_t, nt, preferred_element_type=F32)
            a = jnp.where(tril, a, 0.0)
            st = st_ref[hh]
            o = jnp.dot(a.astype(BF16), v, preferred_element_type=F32)
            o = o + lax.dot_general(q_t, st.astype(BF16), nt, preferred_element_type=F32)
            o_scr[sl, vs] = o
            st_ref[hh] = decay * st + lax.dot_general(v, k_h, tn, preferred_element_type=F32)

    for hh in range(GLA_HEADS_PER_STEP):
        vs = slice(hh * GLA_DV, (hh + 1) * GLA_DV)
        o = _rms_scale(o_scr[:, vs], gn_ref[:, vs])
        r = r_ref[:, vs]
        y_ref[:, vs] = (o * (r * _sigmoid(r))).astype(y_ref.dtype)

    @pl.when(t == pl.num_programs(2) - 1)
    def _():
        for hh in range(GLA_HEADS_PER_STEP):
            s_out_ref[hh] = st_ref[hh].T


def gla_prompt(z3, za3, w_a2p, b_a, gla_norm, *, tq=1024):
    b, seq, _ = z3.shape
    hp = GLA_HEADS_PER_STEP
    kw, vw = hp * GLA_DK, hp * GLA_DV
    assert seq % tq == 0 and tq % GLA_CHUNK == 0 and GLA_HEADS % hp == 0
    return pl.pallas_call(
        functools.partial(_gla_prompt_kernel, tq=tq),
        grid=(b, GLA_HEADS // hp, seq // tq),
        in_specs=[
            pl.BlockSpec((None, tq, kw), lambda i, h, t: (i, t, COL_GQ // kw + h)),
            pl.BlockSpec((None, tq, kw), lambda i, h, t: (i, t, COL_GK // kw + h)),
            pl.BlockSpec((None, tq, vw), lambda i, h, t: (i, t, COL_GV // vw + h)),
            pl.BlockSpec((None, tq, vw), lambda i, h, t: (i, t, COL_GR // vw + h)),
            pl.BlockSpec((None, tq, LANES), lambda i, h, t: (i, t, 0)),
            pl.BlockSpec((LANES, kw), lambda i, h, t: (0, h)),
            pl.BlockSpec((1, kw), lambda i, h, t: (0, h)),
            pl.BlockSpec((1, vw), lambda i, h, t: (0, h)),
        ],
        out_specs=[
            pl.BlockSpec((None, tq, vw), lambda i, h, t: (i, t, h)),
            pl.BlockSpec((None, hp, GLA_DK, GLA_DV), lambda i, h, t: (i, h, 0, 0)),
        ],
        out_shape=[
            jax.ShapeDtypeStruct((b, seq, GLA_VAL_WIDTH), BF16),
            jax.ShapeDtypeStruct((b, GLA_HEADS, GLA_DK, GLA_DV), F32),
        ],
        scratch_shapes=[
            pltpu.VMEM((hp, GLA_DV, GLA_DK), F32),
            pltpu.VMEM((tq, vw), F32),
            pltpu.VMEM((tq, kw), F32),
        ],
        compiler_params=_params("parallel", "parallel", "arbitrary"),
        name="gla_prompt",
    )(z3, z3, z3, z3, za3, w_a2p, b_a.reshape(1, GLA_KEY_WIDTH), gla_norm.reshape(1, GLA_VAL_WIDTH))


def _gla_step_kernel(q_ref, k_ref, v_ref, r_ref, ga_ref, wa_ref, ba_ref, gn_ref, s_ref, y_ref, s_out_ref):
    ga8 = jnp.broadcast_to(ga_ref[...], (8, LANES)).astype(BF16)
    la = jnp.dot(ga8, wa_ref[...], preferred_element_type=F32)[0:1, :] + ba_ref[...]
    ea = jnp.exp(_log_sigmoid(la) / GLA_TAU)
    q = q_ref[...] * (GLA_DK ** -0.5)
    k = k_ref[...]
    eye = lax.broadcasted_iota(jnp.int32, (GLA_DK, GLA_DK), 0) == lax.broadcasted_iota(jnp.int32, (GLA_DK, GLA_DK), 1)

    def column(row):
        return jnp.sum(jnp.where(eye, jnp.broadcast_to(row, (GLA_DK, GLA_DK)), 0.0), axis=-1, keepdims=True)

    for h in range(GLA_HEADS):
        ks = slice(h * GLA_DK, (h + 1) * GLA_DK)
        vs = slice(h * GLA_DV, (h + 1) * GLA_DV)
        s_new = column(ea[:, ks]) * s_ref[h] + column(k[:, ks]) * v_ref[:, vs]
        s_out_ref[h] = s_new
        o = jnp.sum(column(q[:, ks]) * s_new, axis=0, keepdims=True)
        o = _rms_scale(o, gn_ref[:, vs])
        r = r_ref[:, vs]
        y_ref[:, vs] = o * (r * _sigmoid(r))


def gla_step(z_s, za_s, state, layer, w_a2p, b_a, gla_norm):
    bd = z_s.shape[0]
    zs3 = z_s.reshape(bd, 1, N_PROJ)
    y, s_new = pl.pallas_call(
        _gla_step_kernel,
        grid=(bd,),
        in_specs=[
            pl.BlockSpec((None, 1, GLA_KEY_WIDTH), lambda i: (i, 0, COL_GQ // GLA_KEY_WIDTH)),
            pl.BlockSpec((None, 1, GLA_KEY_WIDTH), lambda i: (i, 0, COL_GK // GLA_KEY_WIDTH)),
            pl.BlockSpec((None, 1, GLA_VAL_WIDTH), lambda i: (i, 0, COL_GV // GLA_VAL_WIDTH)),
            pl.BlockSpec((None, 1, GLA_VAL_WIDTH), lambda i: (i, 0, COL_GR // GLA_VAL_WIDTH)),
            pl.BlockSpec((None, 1, LANES), lambda i: (i, 0, 0)),
            pl.BlockSpec((LANES, GLA_KEY_WIDTH), lambda i: (0, 0)),
            pl.BlockSpec((1, GLA_KEY_WIDTH), lambda i: (0, 0)),
            pl.BlockSpec((1, GLA_VAL_WIDTH), lambda i: (0, 0)),
            pl.BlockSpec((None, None, GLA_HEADS, GLA_DK, GLA_DV), lambda i: (layer, i, 0, 0, 0)),
        ],
        out_specs=[
            pl.BlockSpec((None, 1, GLA_VAL_WIDTH), lambda i: (i, 0, 0)),
            pl.BlockSpec((None, GLA_HEADS, GLA_DK, GLA_DV), lambda i: (i, 0, 0, 0)),
        ],
        out_shape=[
            jax.ShapeDtypeStruct((bd, 1, GLA_VAL_WIDTH), F32),
            jax.ShapeDtypeStruct((bd, GLA_HEADS, GLA_DK, GLA_DV), F32),
        ],
        compiler_params=_params("parallel"),
        name="gla_step",
    )(zs3, zs3, zs3, zs3, za_s.reshape(bd, 1, LANES), w_a2p, b_a.reshape(1, GLA_KEY_WIDTH),
      gla_norm.reshape(1, GLA_VAL_WIDTH), state)
    return y.reshape(bd, GLA_VAL_WIDTH), s_new


def _xattn_kernel(q_ref, kv_ref, o_ref):
    scale = X_HEAD_DIM ** -0.5
    nt = (((1,), (1,)), ((), ()))
    for h in range(X_HEADS):
        hs = slice(h * X_HEAD_DIM, (h + 1) * X_HEAD_DIM)
        k = kv_ref[:, h * X_HEAD_DIM:(h + 1) * X_HEAD_DIM].astype(BF16)
        v = kv_ref[:, X_WIDTH + h * X_HEAD_DIM:X_WIDTH + (h + 1) * X_HEAD_DIM].astype(BF16)
        s = lax.dot_general(q_ref[:, hs], k, nt, preferred_element_type=F32) * scale
        e = jnp.exp(s - jnp.max(s, axis=-1, keepdims=True))
        l = jnp.sum(e, axis=-1, keepdims=True)
        o = jnp.dot(e.astype(BF16), v, preferred_element_type=F32) / l
        o_ref[:, hs] = o.astype(o_ref.dtype)


def xattn(q3, mem_kv, *, tt):
    b, t, _ = q3.shape
    tt = min(tt, t)
    assert t % tt == 0
    return pl.pallas_call(
        _xattn_kernel,
        grid=(b, t // tt),
        in_specs=[
            pl.BlockSpec((None, tt, X_WIDTH), lambda i, j: (i, j, 0)),
            pl.BlockSpec((None, MEM_LEN, 2 * X_WIDTH), lambda i, j: (i, 0, 0)),
        ],
        out_specs=pl.BlockSpec((None, tt, X_WIDTH), lambda i, j: (i, j, 0)),
        out_shape=jax.ShapeDtypeStruct((b, t, X_WIDTH), BF16),
        compiler_params=_params("parallel", "arbitrary"),
        name="xattn",
    )(q3, mem_kv)


def _xattn_step_kernel(q_ref, kv_ref, o_ref):
    q = q_ref[...]
    k = kv_ref[:, 0]
    v = kv_ref[:, 1]
    s = jnp.sum(k * q[None], axis=-1, keepdims=True) * (X_HEAD_DIM ** -0.5)
    p = jnp.exp(s - jnp.max(s, axis=0)[None])
    o_ref[...] = jnp.sum(p * v, axis=0) / jnp.sum(p, axis=0)


def xattn_step(q, mem_kv, layer):
    bd = q.shape[0]
    out = pl.pallas_call(
        _xattn_step_kernel,
        grid=(bd,),
        in_specs=[
            pl.BlockSpec((None, X_HEADS, X_HEAD_DIM), lambda i: (i, 0, 0)),
            pl.BlockSpec((None, None, MEM_LEN, 2, X_HEADS, X_HEAD_DIM), lambda i: (layer, i, 0, 0, 0, 0)),
        ],
        out_specs=pl.BlockSpec((None, X_HEADS, X_HEAD_DIM), lambda i: (i, 0, 0)),
        out_shape=jax.ShapeDtypeStruct((bd, X_HEADS, X_HEAD_DIM), F32),
        compiler_params=_params("parallel"),
        name="xattn_step",
    )(q.reshape(bd, X_HEADS, X_HEAD_DIM), mem_kv)
    return out.reshape(bd, X_WIDTH)


FFN_HALO = 16
FFN_TAIL = 8


def _ffn_up_prompt_kernel(x_ref, xh_ref, g_ref, wg_ref, wv_ref, cwg_ref, cwv_ref, cbg_ref, cbv_ref,
                          act_ref, tg_ref, tv_ref, xn_ref, ug_ref, uv_ref, *, tm, seq_tiles, chunk):
    i = pl.program_id(0)

    @pl.when(pl.program_id(1) == 0)
    def _():
        _norm_rows(x_ref, g_ref, xn_ref, tm, chunk, dst_offset=FFN_HALO)
        halo = _rms_scale(xh_ref[...], g_ref[...])
        halo = jnp.where(i % seq_tiles == 0, 0.0, halo)
        xn_ref[0:FFN_HALO, :] = halo.astype(BF16)

    xn = xn_ref[...]
    ug_ref[...] = jnp.dot(xn, wg_ref[...], preferred_element_type=F32)
    uv_ref[...] = jnp.dot(xn, wv_ref[...], preferred_element_type=F32)

    def conv(u_ref, cw_ref, cb_ref):
        lead = FFN_TAIL
        u = u_ref[FFN_HALO - lead:FFN_HALO + tm, :]
        c = cb_ref[...] + cw_ref[0:1, :] * pltpu.roll(u, 2, 0) + cw_ref[1:2, :] * pltpu.roll(u, 1, 0) + cw_ref[2:3, :] * u
        return c[lead:lead + tm]

    act_ref[...] = (_gelu_tanh(conv(ug_ref, cwg_ref, cbg_ref)) * conv(uv_ref, cwv_ref, cbv_ref)).astype(act_ref.dtype)
    tg_ref[...] = ug_ref[FFN_HALO + tm - FFN_TAIL:FFN_HALO + tm, :]
    tv_ref[...] = uv_ref[FFN_HALO + tm - FFN_TAIL:FFN_HALO + tm, :]


def ffn_up_prompt(x, g, w_up, conv_w, conv_b3, layer, *, seq, tm, tn):
    m, k = x.shape
    assert seq % tm == 0 and D_FF % tn == 0 and tm % FFN_HALO == 0
    nj = D_FF // tn
    seq_tiles = seq // tm
    chunk = min(256, tm)
    halo_blocks = tm // FFN_HALO
    act, tail_g, tail_v = pl.pallas_call(
        functools.partial(_ffn_up_prompt_kernel, tm=tm, seq_tiles=seq_tiles, chunk=chunk),
        grid=(m // tm, nj),
        in_specs=[
            pl.BlockSpec((tm, k), lambda i, j: (i, 0)),
            pl.BlockSpec((FFN_HALO, k), lambda i, j: (jnp.maximum(i * halo_blocks - 1, 0), 0)),
            pl.BlockSpec((1, k), lambda i, j: (0, 0)),
            pl.BlockSpec((None, k, tn), lambda i, j: (layer, 0, j)),
            pl.BlockSpec((None, k, tn), lambda i, j: (layer, 0, j + nj)),
            pl.BlockSpec((None, 3, tn), lambda i, j: (layer, 0, j)),
            pl.BlockSpec((None, 3, tn), lambda i, j: (layer, 0, j + nj)),
            pl.BlockSpec((None, 1, tn), lambda i, j: (layer, 0, j)),
            pl.BlockSpec((None, 1, tn), lambda i, j: (layer, 0, j + nj)),
        ],
        out_specs=[
            pl.BlockSpec((tm, tn), lambda i, j: (i, j)),
            pl.BlockSpec((None, FFN_TAIL, tn), lambda i, j: (i, 0, j)),
            pl.BlockSpec((None, FFN_TAIL, tn), lambda i, j: (i, 0, j)),
        ],
        out_shape=[
            jax.ShapeDtypeStruct((m, D_FF), BF16),
            jax.ShapeDtypeStruct((m // tm, FFN_TAIL, D_FF), F32),
            jax.ShapeDtypeStruct((m // tm, FFN_TAIL, D_FF), F32),
        ],
        scratch_shapes=[
            pltpu.VMEM((tm + FFN_HALO, k), BF16),
            pltpu.VMEM((tm + FFN_HALO, tn), F32),
            pltpu.VMEM((tm + FFN_HALO, tn), F32),
        ],
        compiler_params=_params("parallel", "arbitrary"),
        name="ffn_up_prompt",
    )(x, x, g.reshape(1, k), w_up, w_up, conv_w, conv_w, conv_b3, conv_b3)
    return act, jnp.concatenate([tail_g, tail_v], axis=-1)


def _ffn_up_step_kernel(x_ref, g_ref, wg_ref, wv_ref, cwg_ref, cwv_ref, cbg_ref, cbv_ref, hg_ref, hv_ref,
                        act_ref, ng_ref, nv_ref, xn_ref, *, bd):
    @pl.when(pl.program_id(0) == 0)
    def _():
        _norm_rows(x_ref, g_ref, xn_ref, bd, bd)

    xn = xn_ref[...]
    ug = jnp.dot(xn, wg_ref[...], preferred_element_type=F32)
    uv = jnp.dot(xn, wv_ref[...], preferred_element_type=F32)

    def conv(u, h_ref, cw_ref, cb_ref):
        return cb_ref[...] + cw_ref[0:1, :] * h_ref[0] + cw_ref[1:2, :] * h_ref[1] + cw_ref[2:3, :] * u

    act_ref[...] = (_gelu_tanh(conv(ug, hg_ref, cwg_ref, cbg_ref)) * conv(uv, hv_ref, cwv_ref, cbv_ref)).astype(act_ref.dtype)
    ng_ref[0] = hg_ref[1]
    ng_ref[1] = ug
    nv_ref[0] = hv_ref[1]
    nv_ref[1] = uv


def ffn_up_step(x, g, w_up, conv_w, conv_b3, layer, hist_t, *, tn):
    bd, k = x.shape
    nj = D_FF // tn
    act, new_g, new_v = pl.pallas_call(
        functools.partial(_ffn_up_step_kernel, bd=bd),
        grid=(nj,),
        in_specs=[
            pl.BlockSpec((bd, k), lambda j: (0, 0)),
            pl.BlockSpec((1, k), lambda j: (0, 0)),
            pl.BlockSpec((None, k, tn), lambda j: (layer, 0, j)),
            pl.BlockSpec((None, k, tn), lambda j: (layer, 0, j + nj)),
            pl.BlockSpec((None, 3, tn), lambda j: (layer, 0, j)),
            pl.BlockSpec((None, 3, tn), lambda j: (layer, 0, j + nj)),
            pl.BlockSpec((None, 1, tn), lambda j: (layer, 0, j)),
            pl.BlockSpec((None, 1, tn), lambda j: (layer, 0, j + nj)),
            pl.BlockSpec((2, bd, tn), lambda j: (0, 0, j)),
            pl.BlockSpec((2, bd, tn), lambda j: (0, 0, j + nj)),
        ],
        out_specs=[
            pl.BlockSpec((bd, tn), lambda j: (0, j)),
            pl.BlockSpec((2, bd, tn), lambda j: (0, 0, j)),
            pl.BlockSpec((2, bd, tn), lambda j: (0, 0, j)),
        ],
        out_shape=[
            jax.ShapeDtypeStruct((bd, D_FF), BF16),
            jax.ShapeDtypeStruct((2, bd, D_FF), F32),
            jax.ShapeDtypeStruct((2, bd, D_FF), F32),
        ],
        scratch_shapes=[pltpu.VMEM((bd, k), BF16)],
        compiler_params=_params("arbitrary"),
        name="ffn_up_step",
    )(x, g.reshape(1, k), w_up, w_up, conv_w, conv_w, conv_b3, conv_b3, hist_t, hist_t)
    return act, jnp.concatenate([new_g, new_v], axis=-1)


TM = 1024
TN = 1024
TM_POST = 512
TK_POST = 2048
TN_MERGE = 512
TN_FFN = 512
XATTN_ROWS = 512


def _bf16_weights(w_in, pool_w, gla_w_a2, w_br_pool, w_br_dil, w_br_gla, w_mix_out, w_xq, w_xkv, w_xo, w_up, w_down):
    w_in_t, w_ga_t = prep_w_in(jnp.swapaxes(w_in, 1, 2), tn=TN)
    return dict(
        w_in_t=w_in_t,
        w_ga_t=w_ga_t,
        w_a2p=jnp.pad(gla_w_a2, ((0, 0), (0, LANES - GLA_RANK), (0, 0))).astype(BF16),
        pool_w=pool_w.astype(BF16),
        w_br_pool=w_br_pool.astype(BF16),
        w_br_dil=w_br_dil.astype(BF16),
        w_br_gla=w_br_gla.astype(BF16),
        w_mix_out=w_mix_out.astype(BF16),
        w_xq=w_xq.astype(BF16),
        w_xkv=w_xkv.astype(BF16),
        w_xo=w_xo.astype(BF16),
        w_up=w_up.astype(BF16),
        w_down=w_down.astype(BF16),
    )


def kernel(x_prompt, x_sample, state_pool, cache_dil1_kv, cache_dil2_kv, cache_dil3_kv, state_gla, cache_mem_kv, state_ffn_conv, mem_prompt, rel_bias, norm_mix_pre, norm_mix_post, w_in, pool_w, pool_scale, gla_w_a2, gla_b_a, gla_norm, w_br_pool, w_br_dil, w_br_gla, w_mix_out, norm_x_pre, norm_x_post, norm_mem, w_xq, w_xkv, w_xo, norm_ffn_pre, norm_ffn_post, w_up, conv_w, conv_b, w_down):
    b, seq, d = x_prompt.shape
    bd = x_sample.shape[0]
    depth = w_in.shape[0]
    m = b * seq
    assert x_sample.shape[1] == 1 and d == D_MODEL and w_in.shape[2] == N_MAIN + GLA_RANK + N_GATES
    slot_bias = _slot_biases(rel_bias)
    band_bias = _band_bias(slot_bias)
    dil_caches = (cache_dil1_kv, cache_dil2_kv, cache_dil3_kv)

    xp = x_prompt.reshape(m, d)
    xs = x_sample.reshape(bd, d)
    pool_p, gla_p, mem_p, conv_p = [], [], [], []
    pool_s, gla_s, conv_s = [], [], []
    dil_p = [[] for _ in range(3)]
    dil_s = [[] for _ in range(3)]
    seq_tiles = seq // TM

    w = _bf16_weights(w_in, pool_w, gla_w_a2, w_br_pool, w_br_dil, w_br_gla, w_mix_out, w_xq, w_xkv, w_xo, w_up, w_down)
    conv_b3 = conv_b.reshape(depth, 1, 2 * D_FF)
    mem_rows = mem_prompt.reshape(b * MEM_LEN, d)
    for l in range(depth):
        w_a2p = w["w_a2p"][l]

        zm, zkv_wide, za, zs, zas = in_proj(xp, xs, norm_mix_pre[l], w["w_in_t"], w["w_ga_t"], l, tm=TM, tn=TN)
        z3 = zm.reshape(b, seq, N_PROJ)
        y_pool = pool_prompt(z3, w["pool_w"], l, pool_scale[l]).reshape(m, POOL_WIDTH)
        y_dil = dil_prompt(z3, band_bias).reshape(m, DIL_WIDTH)
        y_gla, gla_new = gla_prompt(z3, za.reshape(b, seq, LANES), w_a2p, gla_b_a[l], gla_norm[l])
        merged = branch_merge(y_pool, y_dil, y_gla.reshape(m, GLA_VAL_WIDTH), w["w_br_pool"], w["w_br_dil"], w["w_br_gla"], l,
                              zm, tm=TM, tn=TN_MERGE)
        xp = mm_post(merged, w["w_mix_out"], l, norm_mix_post[l], xp, tm=TM_POST, tk=TK_POST)
        mem_kv = norm_mm(mem_rows, norm_mem[l], w["w_xkv"], l, tm=TM, tn=TN)
        q = norm_mm(xp, norm_x_pre[l], w["w_xq"], l, tm=TM, tn=TN, out_dtype=BF16)
        o = xattn(q.reshape(b, seq, X_WIDTH), mem_kv.reshape(b, MEM_LEN, 2 * X_WIDTH), tt=XATTN_ROWS)
        xp = mm_post(o.reshape(m, X_WIDTH), w["w_xo"], l, norm_x_post[l], xp, tm=TM_POST, tk=TK_POST)
        act, tails = ffn_up_prompt(xp, norm_ffn_pre[l], w["w_up"], conv_w, conv_b3, l, seq=seq, tm=TM, tn=TN_FFN)
        xp = mm_post(act, w["w_down"], l, norm_ffn_post[l], xp, tm=TM_POST, tk=TK_POST)

        pool_p.append(z3[:, seq - POOL_HIST:, COL_POOL:COL_POOL + POOL_WIDTH])
        for g, (win, _) in enumerate(DIL_CONFIGS):
            keep = min(win, seq)
            if g == 2:
                kv = zkv_wide.reshape(b, seq, 2 * DIL_WIDTH)[:, seq - keep:]
            else:
                kv = z3[:, seq - keep:, COL_DIL_K[g]:COL_DIL_K[g] + 2 * DIL_WIDTH]
            dil_p[g].append(kv.reshape(b, keep, 2, DIL_HEADS, DIL_HEAD_DIM))
        gla_p.append(gla_new)
        mem_p.append(mem_kv.reshape(b, MEM_LEN, 2, X_HEADS, X_HEAD_DIM))
        conv_p.append(tails.reshape(b, seq_tiles, FFN_TAIL, 2 * D_FF)[:, seq_tiles - 1, FFN_TAIL - 2:, :])

        y_pool_s, pool_new_t = pool_step(jnp.swapaxes(state_pool[l], 0, 1), zs, w["pool_w"], l, pool_scale[l])
        y_dil_s = dil_step(zs, dil_caches, l, slot_bias).astype(BF16)
        y_gla_s, gla_new_s = gla_step(zs, zas, state_gla, l, w_a2p, gla_b_a[l], gla_norm[l])
        merged_s = branch_merge(y_pool_s, y_dil_s, y_gla_s.astype(BF16), w["w_br_pool"], w["w_br_dil"], w["w_br_gla"], l,
                                zs, tm=bd, tn=TN_MERGE)
        xs = mm_post(merged_s, w["w_mix_out"], l, norm_mix_post[l], xs, tm=bd, tk=TK_POST)
        q_s = norm_mm(xs, norm_x_pre[l], w["w_xq"], l, tm=bd, tn=TN)
        o_s = xattn_step(q_s, cache_mem_kv, l).astype(BF16)
        xs = mm_post(o_s, w["w_xo"], l, norm_x_post[l], xs, tm=bd, tk=TK_POST)
        act_s, conv_new_t = ffn_up_step(xs, norm_ffn_pre[l], w["w_up"], conv_w, conv_b3, l,
                                        jnp.swapaxes(state_ffn_conv[l], 0, 1), tn=TN_FFN)
        xs = mm_post(act_s, w["w_down"], l, norm_ffn_post[l], xs, tm=bd, tk=TK_POST)

        pool_s.append(jnp.swapaxes(pool_new_t, 0, 1))
        for g in range(3):
            kv = zs[:, COL_DIL_K[g]:COL_DIL_K[g] + 2 * DIL_WIDTH]
            dil_s[g].append(kv.reshape(bd, 1, 2, DIL_HEADS, DIL_HEAD_DIM))
        gla_s.append(gla_new_s)
        conv_s.append(jnp.swapaxes(conv_new_t, 0, 1))

    return (xp.reshape(b, seq, d), xs.reshape(bd, 1, d),
            jnp.stack(pool_p), jnp.stack(dil_p[0]), jnp.stack(dil_p[1]), jnp.stack(dil_p[2]), jnp.stack(gla_p), jnp.stack(mem_p), jnp.stack(conv_p),
            jnp.stack(pool_s), jnp.stack(dil_s[0]), jnp.stack(dil_s[1]), jnp.stack(dil_s[2]), jnp.stack(gla_s), jnp.stack(conv_s))
```

```python
import functools
import math

import jax
import jax.numpy as jnp
from jax import lax
from jax.experimental import pallas as pl
from jax.experimental.pallas import tpu as pltpu

F32 = jnp.float32
BF16 = jnp.bfloat16
EPS = 1e-6
NEG_INF = -1e30

VMEM_LIMIT_BYTES = 56 * 1024 * 1024
LANES = 128
BF16_SUBLANES = 16

D_MODEL = 2048
POOL_WINDOWS = (2, 4, 8, 16)
POOL_GROUP = 256
POOL_WIDTH = 1024
POOL_HIST = 15
DIL_CONFIGS = ((128, 1), (512, 4), (2048, 16))
DIL_HEADS = 8
DIL_HEAD_DIM = 128
DIL_WIDTH = 1024
DIL_BLOCK = 128
GLA_HEADS = 4
GLA_DK = 256
GLA_DV = 512
GLA_KEY_WIDTH = 1024
GLA_VAL_WIDTH = 2048
GLA_RANK = 16
GLA_TAU = 16.0
GLA_CHUNK = 64
REL_BUCKETS = 32
REL_MAX_DIST = 2048
MEM_LEN = 256
X_HEADS = 4
X_HEAD_DIM = 256
X_WIDTH = 1024
D_FF = 5632
PAST_LEN = 8192

COL_POOL = 0
COL_DIL_Q = tuple(1024 + 3072 * g for g in range(3))
COL_DIL_K = tuple(2048 + 3072 * g for g in range(3))
COL_DIL_V = tuple(3072 + 3072 * g for g in range(3))
COL_GQ = 10240
COL_GK = 11264
COL_GV = 12288
COL_GR = 14336
N_MAIN = 16384
N_GATES = 3 * D_MODEL
N_PROJ = N_MAIN + N_GATES


def _params(*semantics):
    return pltpu.CompilerParams(dimension_semantics=semantics, vmem_limit_bytes=VMEM_LIMIT_BYTES)


def _sigmoid(x):
    return 0.5 * jnp.tanh(0.5 * x) + 0.5


def _log_sigmoid(x):
    return jnp.minimum(x, 0.0) - jnp.log(1.0 + jnp.exp(-jnp.abs(x)))


def _gelu_tanh(x):
    return x * (0.5 * (1.0 + jnp.tanh(math.sqrt(2.0 / math.pi) * (x + 0.044715 * (x * x * x)))))


def _rms_scale(y, g):
    return y * lax.rsqrt(jnp.mean(y * y, axis=-1, keepdims=True) + EPS) * g


def _norm_rows(x_ref, g_ref, xn_ref, rows, chunk, dst_offset=0):
    g = g_ref[...]

    def body(c, carry):
        r0 = pl.multiple_of(c * chunk, chunk)
        xn_ref[pl.ds(dst_offset + r0, chunk), :] = _rms_scale(x_ref[pl.ds(r0, chunk), :], g).astype(BF16)
        return carry

    lax.fori_loop(0, rows // chunk, body, 0)


def _prep_w_in_kernel(w_ref, wx_ref, o_ref, ga_ref, *, n_plain, shift):
    j = pl.program_id(1)

    @pl.when(j < n_plain)
    def _():
        o_ref[...] = w_ref[...].astype(BF16)

    @pl.when(j >= n_plain)
    def _():
        tn = o_ref.shape[0]
        o_ref[0:tn - shift, :] = w_ref[shift:tn, :].astype(BF16)
        o_ref[tn - shift:tn, :] = wx_ref[...].astype(BF16)

    @pl.when(j == n_plain)
    def _():
        ga_ref[0:shift, :] = w_ref[0:shift, :].astype(BF16)
        ga_ref[shift:LANES, :] = jnp.zeros((LANES - shift, ga_ref.shape[1]), BF16)


def prep_w_in(w_in_t, *, tn):
    depth, _, k = w_in_t.shape
    n_plain = N_MAIN // tn
    nj = (N_MAIN + N_GATES) // tn
    return pl.pallas_call(
        functools.partial(_prep_w_in_kernel, n_plain=n_plain, shift=GLA_RANK),
        grid=(depth, nj),
        in_specs=[
            pl.BlockSpec((None, tn, k), lambda l, j: (l, j, 0)),
            pl.BlockSpec((None, GLA_RANK, k), lambda l, j: (l, jnp.maximum(j + 1, n_plain) * (tn // GLA_RANK), 0)),
        ],
        out_specs=[
            pl.BlockSpec((None, tn, k), lambda l, j: (l, j, 0)),
            pl.BlockSpec((None, LANES, k), lambda l, j: (l, 0, 0)),
        ],
        out_shape=[jax.ShapeDtypeStruct((depth, N_MAIN + N_GATES, k), BF16), jax.ShapeDtypeStruct((depth, LANES, k), BF16)],
        compiler_params=_params("parallel", "arbitrary"),
        name="prep_w_in",
    )(w_in_t, w_in_t)


def _in_proj_kernel(x_ref, xs_ref, g_ref, w_ref, wga_ref, o_ref, kv_ref, ga_ref, os_ref, gas_ref, xn_ref,
                    *, tm, ms, chunk, kv_lo, kv_hi):
    i = pl.program_id(0)
    j = pl.program_id(1)
    nt = (((1,), (1,)), ((), ()))

    @pl.when(j == 0)
    def _():
        _norm_rows(x_ref, g_ref, xn_ref, tm, chunk)
        ga_ref[...] = lax.dot_general(xn_ref[0:tm, :], wga_ref[...], nt, preferred_element_type=F32)

    @pl.when(jnp.logical_and(i == 0, j == 0))
    def _():
        xsn = _rms_scale(xs_ref[...], g_ref[...]).astype(BF16)
        xn_ref[tm:tm + ms, :] = xsn
        gas_ref[...] = lax.dot_general(xsn, wga_ref[...], nt, preferred_element_type=F32)

    @pl.when(i == 0)
    def _():
        res = lax.dot_general(xn_ref[...], w_ref[...], nt, preferred_element_type=F32)
        o_ref[...] = res[0:tm, :]
        os_ref[...] = res[tm:tm + ms, :]

    @pl.when(i > 0)
    def _():
        o_ref[...] = lax.dot_general(xn_ref[0:tm, :], w_ref[...], nt, preferred_element_type=F32)

    @pl.when(jnp.logical_and(j >= kv_lo, j < kv_hi))
    def _():
        kv_ref[...] = o_ref[...]


def in_proj(x, xs, g, w_t, w_ga_t, layer, *, tm, tn):
    m, k = x.shape
    ms = xs.shape[0]
    n = w_t.shape[1]
    assert m % tm == 0 and n % tn == 0 and COL_DIL_K[2] % tn == 0 and ms % BF16_SUBLANES == 0
    chunk = min(256, tm)
    nj = n // tn
    kv_lo = COL_DIL_K[2] // tn
    n_kv = 2 * DIL_WIDTH // tn
    return pl.pallas_call(
        functools.partial(_in_proj_kernel, tm=tm, ms=ms, chunk=chunk, kv_lo=kv_lo, kv_hi=kv_lo + n_kv),
        grid=(m // tm, nj),
        in_specs=[
            pl.BlockSpec((tm, k), lambda i, j: (i, 0)),
            pl.BlockSpec((ms, k), lambda i, j: (0, 0)),
            pl.BlockSpec((1, k), lambda i, j: (0, 0)),
            pl.BlockSpec((None, tn, k), lambda i, j: (layer, j, 0)),
            pl.BlockSpec((None, LANES, k), lambda i, j: (layer, 0, 0)),
        ],
        out_specs=[
            pl.BlockSpec((tm, tn), lambda i, j: (i, j)),
            pl.BlockSpec((tm, tn), lambda i, j: (i, jnp.clip(j - kv_lo, 0, n_kv - 1))),
            pl.BlockSpec((tm, LANES), lambda i, j: (i, 0)),
            pl.BlockSpec((ms, tn), lambda i, j: (0, jnp.where(i == 0, j, nj - 1))),
            pl.BlockSpec((ms, LANES), lambda i, j: (0, 0)),
        ],
        out_shape=[
            jax.ShapeDtypeStruct((m, n), F32),
            jax.ShapeDtypeStruct((m, 2 * DIL_WIDTH), F32),
            jax.ShapeDtypeStruct((m, LANES), F32),
            jax.ShapeDtypeStruct((ms, n), F32),
            jax.ShapeDtypeStruct((ms, LANES), F32),
        ],
        scratch_shapes=[pltpu.VMEM((tm + ms, k), BF16)],
        compiler_params=_params("arbitrary", "arbitrary"),
        name="in_proj",
    )(x, xs, g.reshape(1, k), w_t, w_ga_t)


def _norm_mm_kernel(x_ref, g_ref, w_ref, o_ref, xn_ref, *, tm, chunk):
    @pl.when(pl.program_id(1) == 0)
    def _():
        _norm_rows(x_ref, g_ref, xn_ref, tm, chunk)

    o_ref[...] = jnp.dot(xn_ref[...], w_ref[...], preferred_element_type=F32).astype(o_ref.dtype)


def norm_mm(x, g, w, layer, *, tm, tn, out_dtype=F32):
    m, k = x.shape
    n = w.shape[2]
    tm = min(tm, m)
    tn = min(tn, n)
    assert m % tm == 0 and n % tn == 0
    chunk = min(256, tm)
    return pl.pallas_call(
        functools.partial(_norm_mm_kernel, tm=tm, chunk=chunk),
        grid=(m // tm, n // tn),
        in_specs=[
            pl.BlockSpec((tm, k), lambda i, j: (i, 0)),
            pl.BlockSpec((1, k), lambda i, j: (0, 0)),
            pl.BlockSpec((None, k, tn), lambda i, j: (layer, 0, j)),
        ],
        out_specs=pl.BlockSpec((tm, tn), lambda i, j: (i, j)),
        out_shape=jax.ShapeDtypeStruct((m, n), out_dtype),
        scratch_shapes=[pltpu.VMEM((tm, k), BF16)],
        compiler_params=_params("parallel", "arbitrary"),
        name="norm_mm",
    )(x, g.reshape(1, k), w)


def _mm_post_kernel(a_ref, w_ref, g_ref, res_ref, o_ref, acc_ref, *, nk):
    kk = pl.program_id(1)
    part = jnp.dot(a_ref[...], w_ref[...], preferred_element_type=F32)
    if nk == 1:
        o_ref[...] = res_ref[...] + _rms_scale(part, g_ref[...])
        return

    @pl.when(kk == 0)
    def _():
        acc_ref[...] = part

    @pl.when(jnp.logical_and(kk > 0, kk < nk - 1))
    def _():
        acc_ref[...] += part

    @pl.when(kk == nk - 1)
    def _():
        o_ref[...] = res_ref[...] + _rms_scale(acc_ref[...] + part, g_ref[...])


def mm_post(a, w, layer, g, res, *, tm, tk):
    m, k = a.shape
    n = w.shape[2]
    tm = min(tm, m)
    tk = k if k <= tk else k // 2
    assert m % tm == 0 and k % tk == 0 and tk % LANES == 0
    return pl.pallas_call(
        functools.partial(_mm_post_kernel, nk=k // tk),
        grid=(m // tm, k // tk),
        in_specs=[
            pl.BlockSpec((tm, tk), lambda i, kk: (i, kk)),
            pl.BlockSpec((None, tk, n), lambda i, kk: (layer, kk, 0)),
            pl.BlockSpec((1, n), lambda i, kk: (0, 0)),
            pl.BlockSpec((tm, n), lambda i, kk: (i, 0)),
        ],
        out_specs=pl.BlockSpec((tm, n), lambda i, kk: (i, 0)),
        out_shape=jax.ShapeDtypeStruct((m, n), F32),
        scratch_shapes=[pltpu.VMEM((tm, n), F32)],
        compiler_params=_params("parallel", "arbitrary"),
        name="mm_post",
    )(a, w, g.reshape(1, n), res)


def _branch_merge_kernel(yp_ref, yd_ref, yg_ref, wp_ref, wd_ref, wg_ref, g0_ref, g1_ref, g2_ref, o_ref):
    bp = jnp.dot(yp_ref[...], wp_ref[...], preferred_element_type=F32)
    bd = jnp.dot(yd_ref[...], wd_ref[...], preferred_element_type=F32)
    bg = jnp.dot(yg_ref[...], wg_ref[...], preferred_element_type=F32)
    merged = _sigmoid(g0_ref[...]) * bp + _sigmoid(g1_ref[...]) * bd + _sigmoid(g2_ref[...]) * bg
    o_ref[...] = merged.astype(o_ref.dtype)


def branch_merge(y_pool, y_dil, y_gla, w_pool, w_dil, w_gla, layer, z_main, *, tm, tn):
    m = y_pool.shape[0]
    tm = min(tm, m)
    assert m % tm == 0 and D_MODEL % tn == 0
    gate_blk = [(N_MAIN + b * D_MODEL) // tn for b in range(3)]

    def gate_spec(b):
        return pl.BlockSpec((tm, tn), lambda i, j: (i, gate_blk[b] + j))

    return pl.pallas_call(
        _branch_merge_kernel,
        grid=(m // tm, D_MODEL // tn),
        in_specs=[
            pl.BlockSpec((tm, POOL_WIDTH), lambda i, j: (i, 0)),
            pl.BlockSpec((tm, DIL_WIDTH), lambda i, j: (i, 0)),
            pl.BlockSpec((tm, GLA_VAL_WIDTH), lambda i, j: (i, 0)),
            pl.BlockSpec((None, POOL_WIDTH, tn), lambda i, j: (layer, 0, j)),
            pl.BlockSpec((None, DIL_WIDTH, tn), lambda i, j: (layer, 0, j)),
            pl.BlockSpec((None, GLA_VAL_WIDTH, tn), lambda i, j: (layer, 0, j)),
            gate_spec(0),
            gate_spec(1),
            gate_spec(2),
        ],
        out_specs=pl.BlockSpec((tm, tn), lambda i, j: (i, j)),
        out_shape=jax.ShapeDtypeStruct((m, D_MODEL), BF16),
        compiler_params=_params("parallel", "arbitrary"),
        name="branch_merge",
    )(y_pool, y_dil, y_gla, w_pool, w_dil, w_gla, z_main, z_main, z_main)


def _pool_prompt_kernel(u_ref, w_ref, s_ref, o_ref, buf_a, buf_b, *, seq):
    pad = POOL_HIST + 1
    zeros = jnp.zeros((pad, POOL_GROUP), F32)
    t = lax.broadcasted_iota(jnp.int32, (seq, 1), 0)
    for g, win in enumerate(POOL_WINDOWS):
        cols = slice(g * POOL_GROUP, (g + 1) * POOL_GROUP)
        u = u_ref[:, cols]
        cur, nxt = buf_a, buf_b
        cur[0:pad, :] = zeros
        nxt[0:pad, :] = zeros
        cur[pad:pad + seq, :] = u
        k = 1
        while k < win:
            nxt[pad:pad + seq, :] = cur[pad:pad + seq, :] + cur[pad - k:pad - k + seq, :]
            cur, nxt = nxt, cur
            k *= 2
        cnt = jnp.minimum(win, t + 1).astype(F32)
        d = cur[pad:pad + seq, :] / cnt - u
        y = jnp.dot(d.astype(BF16), w_ref[g], preferred_element_type=F32) * s_ref[:, cols]
        o_ref[:, cols] = y.astype(o_ref.dtype)


def pool_prompt(z3, pool_w, layer, pool_scale):
    b, seq, _ = z3.shape
    return pl.pallas_call(
        functools.partial(_pool_prompt_kernel, seq=seq),
        grid=(b,),
        in_specs=[
            pl.BlockSpec((None, seq, POOL_WIDTH), lambda i: (i, 0, COL_POOL // POOL_WIDTH)),
            pl.BlockSpec((None, len(POOL_WINDOWS), POOL_GROUP, POOL_GROUP), lambda i: (layer, 0, 0, 0)),
            pl.BlockSpec((1, POOL_WIDTH), lambda i: (0, 0)),
        ],
        out_specs=pl.BlockSpec((None, seq, POOL_WIDTH), lambda i: (i, 0, 0)),
        out_shape=jax.ShapeDtypeStruct((b, seq, POOL_WIDTH), BF16),
        scratch_shapes=[pltpu.VMEM((seq + POOL_HIST + 1, POOL_GROUP), F32)] * 2,
        compiler_params=_params("parallel"),
        name="pool_prompt",
    )(z3, pool_w, pool_scale.reshape(1, POOL_WIDTH))


def _pool_step_kernel(hist_ref, u_ref, w_ref, s_ref, y_ref, new_ref):
    u = u_ref[...]
    for r in range(POOL_HIST - 1):
        new_ref[r] = hist_ref[r + 1]
    new_ref[POOL_HIST - 1] = u
    for g, win in enumerate(POOL_WINDOWS):
        cols = slice(g * POOL_GROUP, (g + 1) * POOL_GROUP)
        ug = u[:, cols]
        acc = ug
        for r in range(POOL_HIST - (win - 1), POOL_HIST):
            acc = acc + hist_ref[r, :, cols]
        d = acc / float(win) - ug
        y = jnp.dot(d.astype(BF16), w_ref[g], preferred_element_type=F32) * s_ref[:, cols]
        y_ref[:, cols] = y.astype(y_ref.dtype)


def pool_step(hist_t, z_s, pool_w, layer, pool_scale):
    bd = z_s.shape[0]
    return pl.pallas_call(
        _pool_step_kernel,
        grid=(1,),
        in_specs=[
            pl.BlockSpec((POOL_HIST, bd, POOL_WIDTH), lambda i: (0, 0, 0)),
            pl.BlockSpec((bd, POOL_WIDTH), lambda i: (0, COL_POOL // POOL_WIDTH)),
            pl.BlockSpec((None, len(POOL_WINDOWS), POOL_GROUP, POOL_GROUP), lambda i: (layer, 0, 0, 0)),
            pl.BlockSpec((1, POOL_WIDTH), lambda i: (0, 0)),
        ],
        out_specs=[
            pl.BlockSpec((bd, POOL_WIDTH), lambda i: (0, 0)),
            pl.BlockSpec((POOL_HIST, bd, POOL_WIDTH), lambda i: (0, 0, 0)),
        ],
        out_shape=[
            jax.ShapeDtypeStruct((bd, POOL_WIDTH), BF16),
            jax.ShapeDtypeStruct((POOL_HIST, bd, POOL_WIDTH), F32),
        ],
        compiler_params=_params("arbitrary"),
        name="pool_step",
    )(hist_t, z_s, pool_w, pool_scale.reshape(1, POOL_WIDTH))


def _rel_bucket(dist):
    max_exact = REL_BUCKETS // 2
    scaled = jnp.log(jnp.maximum(dist, 1).astype(F32) / max_exact) / math.log(REL_MAX_DIST / max_exact)
    large = jnp.minimum(max_exact + (scaled * (REL_BUCKETS - max_exact)).astype(jnp.int32), REL_BUCKETS - 1)
    return jnp.where(dist < max_exact, dist, large)


def _select_rows(table_t, index, n):
    onehot = (index.reshape(1, -1) == jnp.arange(n, dtype=jnp.int32)[:, None]).astype(F32)
    out = jnp.dot(table_t, onehot, precision=lax.Precision.HIGHEST, preferred_element_type=F32)
    return out.reshape((table_t.shape[0],) + index.shape)


def _slot_biases(rel_bias):
    out = []
    for g, (win, dil) in enumerate(DIL_CONFIGS):
        dist = jnp.arange(win // dil + 1, dtype=jnp.int32) * dil
        table_t = rel_bias[:, g * DIL_HEADS:(g + 1) * DIL_HEADS].T.astype(F32)
        out.append(_select_rows(table_t, _rel_bucket(dist), REL_BUCKETS))
    return out


def _band_bias(slot_bias):
    qi = jnp.arange(DIL_BLOCK, dtype=jnp.int32)[:, None] + DIL_BLOCK
    ki = jnp.arange(2 * DIL_BLOCK, dtype=jnp.int32)[None, :]
    rel = qi - ki
    out = []
    for g, (win, dil) in enumerate(DIL_CONFIGS):
        n_slots = win // dil
        ok = (rel >= 0) & (rel <= n_slots)
        b = _select_rows(slot_bias[g], jnp.clip(rel, 0, n_slots), n_slots + 1)
        out.append(jnp.where(ok[None], b, NEG_INF))
    return jnp.stack(out, axis=0)


def _dil_prompt_kernel(q0, k0, v0, q1, k1, v1, q2, k2, v2, bias_ref, o_ref, o_scr, lse_scr, *, seq):
    qs, ks, vs = (q0, q1, q2), (k0, k1, k2), (v0, v1, v2)
    scale = DIL_HEAD_DIM ** -0.5
    blk = DIL_BLOCK
    nt = (((1,), (1,)), ((), ()))

    def rows(ref, start, dil):
        if dil == 1:
            return ref[pl.ds(start, blk), :]
        return ref[pl.ds(start, blk, stride=dil), :]

    for g, (_, dil) in enumerate(DIL_CONFIGS):
        nb = seq // dil // blk
        for r in range(dil):
            for ub in range(nb):
                start = r + dil * ub * blk
                q = rows(qs[g], start, dil).astype(BF16)
                kc = rows(ks[g], start, dil).astype(BF16)
                vc = rows(vs[g], start, dil).astype(BF16)
                if ub == 0:
                    kk, vv, bias = kc, vc, bias_ref[g, :, blk:]
                else:
                    prev = start - dil * blk
                    kk = jnp.concatenate([rows(ks[g], prev, dil).astype(BF16), kc], axis=0)
                    vv = jnp.concatenate([rows(vs[g], prev, dil).astype(BF16), vc], axis=0)
                    bias = bias_ref[g]
                s = lax.dot_general(q, kk, nt, preferred_element_type=F32) * scale + bias
                m = jnp.max(s, axis=-1, keepdims=True)
                e = jnp.exp(s - m)
                l = jnp.sum(e, axis=-1, keepdims=True)
                o = jnp.dot(e.astype(BF16), vv, preferred_element_type=F32) / l
                lse = jnp.broadcast_to(m + jnp.log(l), (blk, LANES))
                if dil == 1:
                    o_scr[g, pl.ds(start, blk), :] = o
                    lse_scr[g, pl.ds(start, blk), :] = lse
                else:
                    o_scr[g, pl.ds(start, blk, stride=dil), :] = o
                    lse_scr[g, pl.ds(start, blk, stride=dil), :] = lse

    chunk = 256

    def combine(c, carry):
        sl = pl.ds(pl.multiple_of(c * chunk, chunk), chunk)
        l0, l1, l2 = lse_scr[0, sl, :], lse_scr[1, sl, :], lse_scr[2, sl, :]
        mx = jnp.maximum(jnp.maximum(l0, l1), l2)
        w0, w1, w2 = jnp.exp(l0 - mx), jnp.exp(l1 - mx), jnp.exp(l2 - mx)
        y = (w0 * o_scr[0, sl, :] + w1 * o_scr[1, sl, :] + w2 * o_scr[2, sl, :]) / (w0 + w1 + w2)
        o_ref[sl, :] = y.astype(o_ref.dtype)
        return carry

    lax.fori_loop(0, seq // chunk, combine, 0)


def dil_prompt(z3, band_bias):
    b, seq, _ = z3.shape
    assert seq % (DIL_BLOCK * 16) == 0

    def col_spec(col):
        blk0 = col // DIL_HEAD_DIM
        return pl.BlockSpec((None, seq, DIL_HEAD_DIM), lambda i, h: (i, 0, blk0 + h))

    in_specs = []
    for g in range(3):
        in_specs += [col_spec(COL_DIL_Q[g]), col_spec(COL_DIL_K[g]), col_spec(COL_DIL_V[g])]
    in_specs.append(pl.BlockSpec((3, None, DIL_BLOCK, 2 * DIL_BLOCK), lambda i, h: (0, h, 0, 0)))
    return pl.pallas_call(
        functools.partial(_dil_prompt_kernel, seq=seq),
        grid=(b, DIL_HEADS),
        in_specs=in_specs,
        out_specs=pl.BlockSpec((None, seq, DIL_HEAD_DIM), lambda i, h: (i, 0, h)),
        out_shape=jax.ShapeDtypeStruct((b, seq, DIL_WIDTH), BF16),
        scratch_shapes=[pltpu.VMEM((3, seq, DIL_HEAD_DIM), F32), pltpu.VMEM((3, seq, LANES), F32)],
        compiler_params=_params("parallel", "arbitrary"),
        name="dil_prompt",
    )(*([z3] * 9), band_bias)


def _dil_step_kernel(qkv_ref, c0, c1, c2, bias_ref, bias0_ref, o_ref):
    caches = (c0, c1, c2)
    scale = DIL_HEAD_DIM ** -0.5
    outs, lses = [], []
    for g in range(3):
        q, kn, vn = qkv_ref[3 * g], qkv_ref[3 * g + 1], qkv_ref[3 * g + 2]
        kc = caches[g][:, 0]
        vc = caches[g][:, 1]
        s = jnp.sum(kc * q[None], axis=-1, keepdims=True) * scale + bias_ref[g]
        s_new = jnp.sum(q * kn, axis=-1, keepdims=True) * scale + bias0_ref[g]
        m = jnp.maximum(jnp.max(s, axis=0), s_new)
        p = jnp.exp(s - m[None])
        p_new = jnp.exp(s_new - m)
        l = jnp.sum(p, axis=0) + p_new
        outs.append((jnp.sum(p * vc, axis=0) + p_new * vn) / l)
        lses.append(m + jnp.log(l))
    mx = jnp.maximum(jnp.maximum(lses[0], lses[1]), lses[2])
    w0, w1, w2 = jnp.exp(lses[0] - mx), jnp.exp(lses[1] - mx), jnp.exp(lses[2] - mx)
    o_ref[...] = (w0 * outs[0] + w1 * outs[1] + w2 * outs[2]) / (w0 + w1 + w2)


def dil_step(z_s, caches, layer, slot_bias):
    bd = z_s.shape[0]
    n_slots = DIL_CONFIGS[0][0] // DIL_CONFIGS[0][1]
    qkv = z_s[:, COL_DIL_Q[0]:COL_DIL_V[2] + DIL_WIDTH].reshape(bd, 9, DIL_HEADS, DIL_HEAD_DIM)
    in_specs = [pl.BlockSpec((None, 9, DIL_HEADS, DIL_HEAD_DIM), lambda i: (i, 0, 0, 0))]
    cache_views = []
    for g, (win, dil) in enumerate(DIL_CONFIGS):
        depth = caches[g].shape[0]
        assert caches[g].shape[2] == win and win // dil == n_slots
        cache_views.append(caches[g].reshape(depth, bd, n_slots, dil, 2, DIL_HEADS, DIL_HEAD_DIM))
        in_specs.append(pl.BlockSpec((None, None, n_slots, None, 2, DIL_HEADS, DIL_HEAD_DIM),
                                     lambda i: (layer, i, 0, 0, 0, 0, 0)))
    bias_rows = jnp.stack([sb[:, n_slots:0:-1].T for sb in slot_bias], axis=0)
    bias_rows = jnp.broadcast_to(bias_rows[..., None], (3, n_slots, DIL_HEADS, LANES))
    bias_new = jnp.broadcast_to(jnp.stack([sb[:, 0] for sb in slot_bias], axis=0)[..., None], (3, DIL_HEADS, LANES))
    in_specs.append(pl.BlockSpec((3, n_slots, DIL_HEADS, LANES), lambda i: (0, 0, 0, 0)))
    in_specs.append(pl.BlockSpec((3, DIL_HEADS, LANES), lambda i: (0, 0, 0)))
    out = pl.pallas_call(
        _dil_step_kernel,
        grid=(bd,),
        in_specs=in_specs,
        out_specs=pl.BlockSpec((None, DIL_HEADS, DIL_HEAD_DIM), lambda i: (i, 0, 0)),
        out_shape=jax.ShapeDtypeStruct((bd, DIL_HEADS, DIL_HEAD_DIM), F32),
        compiler_params=_params("parallel"),
        name="dil_step",
    )(qkv, *cache_views, bias_rows, bias_new)
    return out.reshape(bd, DIL_WIDTH)


GLA_HEADS_PER_STEP = 2
GLA_CUMSUM_ROWS = 256


def _gla_prompt_kernel(q_ref, k_ref, v_ref, r_ref, ga_ref, wa_ref, ba_ref, gn_ref, y_ref, s_out_ref, st_ref, o_scr, b_scr, *, tq):
    t = pl.program_id(2)
    ck = GLA_CHUNK

    @pl.when(t == 0)
    def _():
        st_ref[...] = jnp.zeros_like(st_ref)

    la = jnp.dot(ga_ref[...].astype(BF16), wa_ref[...], preferred_element_type=F32) + ba_ref[...]
    log_a = _log_sigmoid(la) / GLA_TAU
    row = lax.broadcasted_iota(jnp.int32, (ck, ck), 0)
    col = lax.broadcasted_iota(jnp.int32, (ck, ck), 1)
    tril = row >= col
    rows_c = GLA_CUMSUM_ROWS
    rr = lax.broadcasted_iota(jnp.int32, (rows_c, rows_c), 0)
    cc = lax.broadcasted_iota(jnp.int32, (rows_c, rows_c), 1)
    prefix = jnp.logical_and(rr >= cc, rr // ck == cc // ck).astype(BF16)
    for r0 in range(0, tq, rows_c):
        la_blk = log_a[r0:r0 + rows_c, :]
        hi = la_blk.astype(BF16)
        lo = (la_blk - hi.astype(F32)).astype(BF16)
        b_scr[r0:r0 + rows_c, :] = (jnp.dot(prefix, hi, preferred_element_type=F32)
                                    + jnp.dot(prefix, lo, preferred_element_type=F32))
    nt = (((1,), (1,)), ((), ()))
    tn = (((0,), (0,)), ((), ()))
    for c in range(tq // ck):
        sl = slice(c * ck, (c + 1) * ck)
        for hh in range(GLA_HEADS_PER_STEP):
            ks = slice(hh * GLA_DK, (hh + 1) * GLA_DK)
            vs = slice(hh * GLA_DV, (hh + 1) * GLA_DV)
            b = b_scr[sl, ks]
            b_last = b[ck - 1:ck, :]
            q = q_ref[sl, ks] * (GLA_DK ** -0.5)
            k = k_ref[sl, ks]
            v = v_ref[sl, vs].astype(BF16)
            q_t = (q * jnp.exp(b)).astype(BF16)
            k_t = (k * jnp.exp(-b)).astype(BF16)
            k_h = (k * jnp.exp(b_last - b)).astype(BF16)
            decay = jnp.exp(b_last)
            a = lax.dot_general(q_t, k_t, nt, preferred_element_type=F32)
            a = jnp.where(tril, a, 0.0)
            st = st_ref[hh]
            o = jnp.dot(a.astype(BF16), v, preferred_element_type=F32)
            o = o + lax.dot_general(q_t, st.astype(BF16), nt, preferred_element_type=F32)
            o_scr[sl, vs] = o
            st_ref[hh] = decay * st + lax.dot_general(v, k_h, tn, preferred_element_type=F32)

    for hh in range(GLA_HEADS_PER_STEP):
        vs = slice(hh * GLA_DV, (hh + 1) * GLA_DV)
        o = _rms_scale(o_scr[:, vs], gn_ref[:, vs])
        r = r_ref[:, vs]
        y_ref[:, vs] = (o * (r * _sigmoid(r))).astype(y_ref.dtype)

    @pl.when(t == pl.num_programs(2) - 1)
    def _():
        for hh in range(GLA_HEADS_PER_STEP):
            s_out_ref[hh] = st_ref[hh].T


def gla_prompt(z3, za3, w_a2p, b_a, gla_norm, *, tq=1024):
    b, seq, _ = z3.shape
    hp = GLA_HEADS_PER_STEP
    kw, vw = hp * GLA_DK, hp * GLA_DV
    assert seq % tq == 0 and tq % GLA_CHUNK == 0 and GLA_HEADS % hp == 0
    return pl.pallas_call(
        functools.partial(_gla_prompt_kernel, tq=tq),
        grid=(b, GLA_HEADS // hp, seq // tq),
        in_specs=[
            pl.BlockSpec((None, tq, kw), lambda i, h, t: (i, t, COL_GQ // kw + h)),
            pl.BlockSpec((None, tq, kw), lambda i, h, t: (i, t, COL_GK // kw + h)),
            pl.BlockSpec((None, tq, vw), lambda i, h, t: (i, t, COL_GV // vw + h)),
            pl.BlockSpec((None, tq, vw), lambda i, h, t: (i, t, COL_GR // vw + h)),
            pl.BlockSpec((None, tq, LANES), lambda i, h, t: (i, t, 0)),
            pl.BlockSpec((LANES, kw), lambda i, h, t: (0, h)),
            pl.BlockSpec((1, kw), lambda i, h, t: (0, h)),
            pl.BlockSpec((1, vw), lambda i, h, t: (0, h)),
        ],
        out_specs=[
            pl.BlockSpec((None, tq, vw), lambda i, h, t: (i, t, h)),
            pl.BlockSpec((None, hp, GLA_DK, GLA_DV), lambda i, h, t: (i, h, 0, 0)),
        ],
        out_shape=[
            jax.ShapeDtypeStruct((b, seq, GLA_VAL_WIDTH), BF16),
            jax.ShapeDtypeStruct((b, GLA_HEADS, GLA_DK, GLA_DV), F32),
        ],
        scratch_shapes=[
            pltpu.VMEM((hp, GLA_DV, GLA_DK), F32),
            pltpu.VMEM((tq, vw), F32),
            pltpu.VMEM((tq, kw), F32),
        ],
        compiler_params=_params("parallel", "parallel", "arbitrary"),
        name="gla_prompt",
    )(z3, z3, z3, z3, za3, w_a2p, b_a.reshape(1, GLA_KEY_WIDTH), gla_norm.reshape(1, GLA_VAL_WIDTH))


def _gla_step_kernel(q_ref, k_ref, v_ref, r_ref, ga_ref, wa_ref, ba_ref, gn_ref, s_ref, y_ref, s_out_ref):
    ga8 = jnp.broadcast_to(ga_ref[...], (8, LANES)).astype(BF16)
    la = jnp.dot(ga8, wa_ref[...], preferred_element_type=F32)[0:1, :] + ba_ref[...]
    ea = jnp.exp(_log_sigmoid(la) / GLA_TAU)
    q = q_ref[...] * (GLA_DK ** -0.5)
    k = k_ref[...]
    eye = lax.broadcasted_iota(jnp.int32, (GLA_DK, GLA_DK), 0) == lax.broadcasted_iota(jnp.int32, (GLA_DK, GLA_DK), 1)

    def column(row):
        return jnp.sum(jnp.where(eye, jnp.broadcast_to(row, (GLA_DK, GLA_DK)), 0.0), axis=-1, keepdims=True)

    for h in range(GLA_HEADS):
        ks = slice(h * GLA_DK, (h + 1) * GLA_DK)
        vs = slice(h * GLA_DV, (h + 1) * GLA_DV)
        s_new = column(ea[:, ks]) * s_ref[h] + column(k[:, ks]) * v_ref[:, vs]
        s_out_ref[h] = s_new
        o = jnp.sum(column(q[:, ks]) * s_new, axis=0, keepdims=True)
        o = _rms_scale(o, gn_ref[:, vs])
        r = r_ref[:, vs]
        y_ref[:, vs] = o * (r * _sigmoid(r))


def gla_step(z_s, za_s, state, layer, w_a2p, b_a, gla_norm):
    bd = z_s.shape[0]
    zs3 = z_s.reshape(bd, 1, N_PROJ)
    y, s_new = pl.pallas_call(
        _gla_step_kernel,
        grid=(bd,),
        in_specs=[
            pl.BlockSpec((None, 1, GLA_KEY_WIDTH), lambda i: (i, 0, COL_GQ // GLA_KEY_WIDTH)),
            pl.BlockSpec((None, 1, GLA_KEY_WIDTH), lambda i: (i, 0, COL_GK // GLA_KEY_WIDTH)),
            pl.BlockSpec((None, 1, GLA_VAL_WIDTH), lambda i: (i, 0, COL_GV // GLA_VAL_WIDTH)),
            pl.BlockSpec((None, 1, GLA_VAL_WIDTH), lambda i: (i, 0, COL_GR // GLA_VAL_WIDTH)),
            pl.BlockSpec((None, 1, LANES), lambda i: (i, 0, 0)),
            pl.BlockSpec((LANES, GLA_KEY_WIDTH), lambda i: (0, 0)),
            pl.BlockSpec((1, GLA_KEY_WIDTH), lambda i: (0, 0)),
            pl.BlockSpec((1, GLA_VAL_WIDTH), lambda i: (0, 0)),
            pl.BlockSpec((None, None, GLA_HEADS, GLA_DK, GLA_DV), lambda i: (layer, i, 0, 0, 0)),
        ],
        out_specs=[
            pl.BlockSpec((None, 1, GLA_VAL_WIDTH), lambda i: (i, 0, 0)),
            pl.BlockSpec((None, GLA_HEADS, GLA_DK, GLA_DV), lambda i: (i, 0, 0, 0)),
        ],
        out_shape=[
            jax.ShapeDtypeStruct((bd, 1, GLA_VAL_WIDTH), F32),
            jax.ShapeDtypeStruct((bd, GLA_HEADS, GLA_DK, GLA_DV), F32),
        ],
        compiler_params=_params("parallel"),
        name="gla_step",
    )(zs3, zs3, zs3, zs3, za_s.reshape(bd, 1, LANES), w_a2p, b_a.reshape(1, GLA_KEY_WIDTH),
      gla_norm.reshape(1, GLA_VAL_WIDTH), state)
    return y.reshape(bd, GLA_VAL_WIDTH), s_new


def _xattn_kernel(q_ref, kv_ref, o_ref):
    scale = X_HEAD_DIM ** -0.5
    nt = (((1,), (1,)), ((), ()))
    for h in range(X_HEADS):
        hs = slice(h * X_HEAD_DIM, (h + 1) * X_HEAD_DIM)
        k = kv_ref[:, h * X_HEAD_DIM:(h + 1) * X_HEAD_DIM].astype(BF16)
        v = kv_ref[:, X_WIDTH + h * X_HEAD_DIM:X_WIDTH + (h + 1) * X_HEAD_DIM].astype(BF16)
        s = lax.dot_general(q_ref[:, hs], k, nt, preferred_element_type=F32) * scale
        e = jnp.exp(s - jnp.max(s, axis=-1, keepdims=True))
        l = jnp.sum(e, axis=-1, keepdims=True)
        o = jnp.dot(e.astype(BF16), v, preferred_element_type=F32) / l
        o_ref[:, hs] = o.astype(o_ref.dtype)


def xattn(q3, mem_kv, *, tt):
    b, t, _ = q3.shape
    tt = min(tt, t)
    assert t % tt == 0
    return pl.pallas_call(
        _xattn_kernel,
        grid=(b, t // tt),
        in_specs=[
            pl.BlockSpec((None, tt, X_WIDTH), lambda i, j: (i, j, 0)),
            pl.BlockSpec((None, MEM_LEN, 2 * X_WIDTH), lambda i, j: (i, 0, 0)),
        ],
        out_specs=pl.BlockSpec((None, tt, X_WIDTH), lambda i, j: (i, j, 0)),
        out_shape=jax.ShapeDtypeStruct((b, t, X_WIDTH), BF16),
        compiler_params=_params("parallel", "arbitrary"),
        name="xattn",
    )(q3, mem_kv)


def _xattn_step_kernel(q_ref, kv_ref, o_ref):
    q = q_ref[...]
    k = kv_ref[:, 0]
    v = kv_ref[:, 1]
    s = jnp.sum(k * q[None], axis=-1, keepdims=True) * (X_HEAD_DIM ** -0.5)
    p = jnp.exp(s - jnp.max(s, axis=0)[None])
    o_ref[...] = jnp.sum(p * v, axis=0) / jnp.sum(p, axis=0)


def xattn_step(q, mem_kv, layer):
    bd = q.shape[0]
    out = pl.pallas_call(
        _xattn_step_kernel,
        grid=(bd,),
        in_specs=[
            pl.BlockSpec((None, X_HEADS, X_HEAD_DIM), lambda i: (i, 0, 0)),
            pl.BlockSpec((None, None, MEM_LEN, 2, X_HEADS, X_HEAD_DIM), lambda i: (layer, i, 0, 0, 0, 0)),
        ],
        out_specs=pl.BlockSpec((None, X_HEADS, X_HEAD_DIM), lambda i: (i, 0, 0)),
        out_shape=jax.ShapeDtypeStruct((bd, X_HEADS, X_HEAD_DIM), F32),
        compiler_params=_params("parallel"),
        name="xattn_step",
    )(q.reshape(bd, X_HEADS, X_HEAD_DIM), mem_kv)
    return out.reshape(bd, X_WIDTH)


FFN_HALO = 16
FFN_TAIL = 8


def _ffn_up_prompt_kernel(x_ref, xh_ref, g_ref, wg_ref, wv_ref, cwg_ref, cwv_ref, cbg_ref, cbv_ref,
                          act_ref, tg_ref, tv_ref, xn_ref, ug_ref, uv_ref, *, tm, seq_tiles, chunk):
    i = pl.program_id(0)

    @pl.when(pl.program_id(1) == 0)
    def _():
        _norm_rows(x_ref, g_ref, xn_ref, tm, chunk, dst_offset=FFN_HALO)
        halo = _rms_scale(xh_ref[...], g_ref[...])
        halo = jnp.where(i % seq_tiles == 0, 0.0, halo)
        xn_ref[0:FFN_HALO, :] = halo.astype(BF16)

    xn = xn_ref[...]
    ug_ref[...] = jnp.dot(xn, wg_ref[...], preferred_element_type=F32)
    uv_ref[...] = jnp.dot(xn, wv_ref[...], preferred_element_type=F32)

    def conv(u_ref, cw_ref, cb_ref):
        lead = FFN_TAIL
        u = u_ref[FFN_HALO - lead:FFN_HALO + tm, :]
        c = cb_ref[...] + cw_ref[0:1, :] * pltpu.roll(u, 2, 0) + cw_ref[1:2, :] * pltpu.roll(u, 1, 0) + cw_ref[2:3, :] * u
        return c[lead:lead + tm]

    act_ref[...] = (_gelu_tanh(conv(ug_ref, cwg_ref, cbg_ref)) * conv(uv_ref, cwv_ref, cbv_ref)).astype(act_ref.dtype)
    tg_ref[...] = ug_ref[FFN_HALO + tm - FFN_TAIL:FFN_HALO + tm, :]
    tv_ref[...] = uv_ref[FFN_HALO + tm - FFN_TAIL:FFN_HALO + tm, :]


def ffn_up_prompt(x, g, w_up, conv_w, conv_b3, layer, *, seq, tm, tn):
    m, k = x.shape
    assert seq % tm == 0 and D_FF % tn == 0 and tm % FFN_HALO == 0
    nj = D_FF // tn
    seq_tiles = seq // tm
    chunk = min(256, tm)
    halo_blocks = tm // FFN_HALO
    act, tail_g, tail_v = pl.pallas_call(
        functools.partial(_ffn_up_prompt_kernel, tm=tm, seq_tiles=seq_tiles, chunk=chunk),
        grid=(m // tm, nj),
        in_specs=[
            pl.BlockSpec((tm, k), lambda i, j: (i, 0)),
            pl.BlockSpec((FFN_HALO, k), lambda i, j: (jnp.maximum(i * halo_blocks - 1, 0), 0)),
            pl.BlockSpec((1, k), lambda i, j: (0, 0)),
            pl.BlockSpec((None, k, tn), lambda i, j: (layer, 0, j)),
            pl.BlockSpec((None, k, tn), lambda i, j: (layer, 0, j + nj)),
            pl.BlockSpec((None, 3, tn), lambda i, j: (layer, 0, j)),
            pl.BlockSpec((None, 3, tn), lambda i, j: (layer, 0, j + nj)),
            pl.BlockSpec((None, 1, tn), lambda i, j: (layer, 0, j)),
            pl.BlockSpec((None, 1, tn), lambda i, j: (layer, 0, j + nj)),
        ],
        out_specs=[
            pl.BlockSpec((tm, tn), lambda i, j: (i, j)),
            pl.BlockSpec((None, FFN_TAIL, tn), lambda i, j: (i, 0, j)),
            pl.BlockSpec((None, FFN_TAIL, tn), lambda i, j: (i, 0, j)),
        ],
        out_shape=[
            jax.ShapeDtypeStruct((m, D_FF), BF16),
            jax.ShapeDtypeStruct((m // tm, FFN_TAIL, D_FF), F32),
            jax.ShapeDtypeStruct((m // tm, FFN_TAIL, D_FF), F32),
        ],
        scratch_shapes=[
            pltpu.VMEM((tm + FFN_HALO, k), BF16),
            pltpu.VMEM((tm + FFN_HALO, tn), F32),
            pltpu.VMEM((tm + FFN_HALO, tn), F32),
        ],
        compiler_params=_params("parallel", "arbitrary"),
        name="ffn_up_prompt",
    )(x, x, g.reshape(1, k), w_up, w_up, conv_w, conv_w, conv_b3, conv_b3)
    return act, jnp.concatenate([tail_g, tail_v], axis=-1)


def _ffn_up_step_kernel(x_ref, g_ref, wg_ref, wv_ref, cwg_ref, cwv_ref, cbg_ref, cbv_ref, hg_ref, hv_ref,
                        act_ref, ng_ref, nv_ref, xn_ref, *, bd):
    @pl.when(pl.program_id(0) == 0)
    def _():
        _norm_rows(x_ref, g_ref, xn_ref, bd, bd)

    xn = xn_ref[...]
    ug = jnp.dot(xn, wg_ref[...], preferred_element_type=F32)
    uv = jnp.dot(xn, wv_ref[...], preferred_element_type=F32)

    def conv(u, h_ref, cw_ref, cb_ref):
        return cb_ref[...] + cw_ref[0:1, :] * h_ref[0] + cw_ref[1:2, :] * h_ref[1] + cw_ref[2:3, :] * u

    act_ref[...] = (_gelu_tanh(conv(ug, hg_ref, cwg_ref, cbg_ref)) * conv(uv, hv_ref, cwv_ref, cbv_ref)).astype(act_ref.dtype)
    ng_ref[0] = hg_ref[1]
    ng_ref[1] = ug
    nv_ref[0] = hv_ref[1]
    nv_ref[1] = uv


def ffn_up_step(x, g, w_up, conv_w, conv_b3, layer, hist_t, *, tn):
    bd, k = x.shape
    nj = D_FF // tn
    act, new_g, new_v = pl.pallas_call(
        functools.partial(_ffn_up_step_kernel, bd=bd),
        grid=(nj,),
        in_specs=[
            pl.BlockSpec((bd, k), lambda j: (0, 0)),
            pl.BlockSpec((1, k), lambda j: (0, 0)),
            pl.BlockSpec((None, k, tn), lambda j: (layer, 0, j)),
            pl.BlockSpec((None, k, tn), lambda j: (layer, 0, j + nj)),
            pl.BlockSpec((None, 3, tn), lambda j: (layer, 0, j)),
            pl.BlockSpec((None, 3, tn), lambda j: (layer, 0, j + nj)),
            pl.BlockSpec((None, 1, tn), lambda j: (layer, 0, j)),
            pl.BlockSpec((None, 1, tn), lambda j: (layer, 0, j + nj)),
            pl.BlockSpec((2, bd, tn), lambda j: (0, 0, j)),
            pl.BlockSpec((2, bd, tn), lambda j: (0, 0, j + nj)),
        ],
        out_specs=[
            pl.BlockSpec((bd, tn), lambda j: (0, j)),
            pl.BlockSpec((2, bd, tn), lambda j: (0, 0, j)),
            pl.BlockSpec((2, bd, tn), lambda j: (0, 0, j)),
        ],
        out_shape=[
            jax.ShapeDtypeStruct((bd, D_FF), BF16),
            jax.ShapeDtypeStruct((2, bd, D_FF), F32),
            jax.ShapeDtypeStruct((2, bd, D_FF), F32),
        ],
        scratch_shapes=[pltpu.VMEM((bd, k), BF16)],
        compiler_params=_params("arbitrary"),
        name="ffn_up_step",
    )(x, g.reshape(1, k), w_up, w_up, conv_w, conv_w, conv_b3, conv_b3, hist_t, hist_t)
    return act, jnp.concatenate([new_g, new_v], axis=-1)


TM = 1024
TN = 1024
TM_POST = 512
TK_POST = 2048
TN_MERGE = 512
TN_FFN = 512
XATTN_ROWS = 512


def _bf16_weights(w_in, pool_w, gla_w_a2, w_br_pool, w_br_dil, w_br_gla, w_mix_out, w_xq, w_xkv, w_xo, w_up, w_down):
    w_in_t, w_ga_t = prep_w_in(jnp.swapaxes(w_in, 1, 2), tn=TN)
    return dict(
        w_in_t=w_in_t,
        w_ga_t=w_ga_t,
        w_a2p=jnp.pad(gla_w_a2, ((0, 0), (0, LANES - GLA_RANK), (0, 0))).astype(BF16),
        pool_w=pool_w.astype(BF16),
        w_br_pool=w_br_pool.astype(BF16),
        w_br_dil=w_br_dil.astype(BF16),
        w_br_gla=w_br_gla.astype(BF16),
        w_mix_out=w_mix_out.astype(BF16),
        w_xq=w_xq.astype(BF16),
        w_xkv=w_xkv.astype(BF16),
        w_xo=w_xo.astype(BF16),
        w_up=w_up.astype(BF16),
        w_down=w_down.astype(BF16),
    )


def kernel(x_prompt, x_sample, state_pool, cache_dil1_kv, cache_dil2_kv, cache_dil3_kv, state_gla, cache_mem_kv, state_ffn_conv, mem_prompt, rel_bias, norm_mix_pre, norm_mix_post, w_in, pool_w, pool_scale, gla_w_a2, gla_b_a, gla_norm, w_br_pool, w_br_dil, w_br_gla, w_mix_out, norm_x_pre, norm_x_post, norm_mem, w_xq, w_xkv, w_xo, norm_ffn_pre, norm_ffn_post, w_up, conv_w, conv_b, w_down):
    b, seq, d = x_prompt.shape
    bd = x_sample.shape[0]
    depth = w_in.shape[0]
    m = b * seq
    assert x_sample.shape[1] == 1 and d == D_MODEL and w_in.shape[2] == N_MAIN + GLA_RANK + N_GATES
    slot_bias = _slot_biases(rel_bias)
    band_bias = _band_bias(slot_bias)
    dil_caches = (cache_dil1_kv, cache_dil2_kv, cache_dil3_kv)

    xp = x_prompt.reshape(m, d)
    xs = x_sample.reshape(bd, d)
    pool_p, gla_p, mem_p, conv_p = [], [], [], []
    pool_s, gla_s, conv_s = [], [], []
    dil_p = [[] for _ in range(3)]
    dil_s = [[] for _ in range(3)]
    seq_tiles = seq // TM

    w = _bf16_weights(w_in, pool_w, gla_w_a2, w_br_pool, w_br_dil, w_br_gla, w_mix_out, w_xq, w_xkv, w_xo, w_up, w_down)
    conv_b3 = conv_b.reshape(depth, 1, 2 * D_FF)
    mem_rows = mem_prompt.reshape(b * MEM_LEN, d)
    for l in range(depth):
        w_a2p = w["w_a2p"][l]

        zm, zkv_wide, za, zs, zas = in_proj(xp, xs, norm_mix_pre[l], w["w_in_t"], w["w_ga_t"], l, tm=TM, tn=TN)
        z3 = zm.reshape(b, seq, N_PROJ)
        y_pool = pool_prompt(z3, w["pool_w"], l, pool_scale[l]).reshape(m, POOL_WIDTH)
        y_dil = dil_prompt(z3, band_bias).reshape(m, DIL_WIDTH)
        y_gla, gla_new = gla_prompt(z3, za.reshape(b, seq, LANES), w_a2p, gla_b_a[l], gla_norm[l])
        merged = branch_merge(y_pool, y_dil, y_gla.reshape(m, GLA_VAL_WIDTH), w["w_br_pool"], w["w_br_dil"], w["w_br_gla"], l,
                              zm, tm=TM, tn=TN_MERGE)
        xp = mm_post(merged, w["w_mix_out"], l, norm_mix_post[l], xp, tm=TM_POST, tk=TK_POST)
        mem_kv = norm_mm(mem_rows, norm_mem[l], w["w_xkv"], l, tm=TM, tn=TN)
        q = norm_mm(xp, norm_x_pre[l], w["w_xq"], l, tm=TM, tn=TN, out_dtype=BF16)
        o = xattn(q.reshape(b, seq, X_WIDTH), mem_kv.reshape(b, MEM_LEN, 2 * X_WIDTH), tt=XATTN_ROWS)
        xp = mm_post(o.reshape(m, X_WIDTH), w["w_xo"], l, norm_x_post[l], xp, tm=TM_POST, tk=TK_POST)
        act, tails = ffn_up_prompt(xp, norm_ffn_pre[l], w["w_up"], conv_w, conv_b3, l, seq=seq, tm=TM, tn=TN_FFN)
        xp = mm_post(act, w["w_down"], l, norm_ffn_post[l], xp, tm=TM_POST, tk=TK_POST)

        pool_p.append(z3[:, seq - POOL_HIST:, COL_POOL:COL_POOL + POOL_WIDTH])
        for g, (win, _) in enumerate(DIL_CONFIGS):
            keep = min(win, seq)
            if g == 2:
                kv = zkv_wide.reshape(b, seq, 2 * DIL_WIDTH)[:, seq - keep:]
            else:
                kv = z3[:, seq - keep:, COL_DIL_K[g]:COL_DIL_K[g] + 2 * DIL_WIDTH]
            dil_p[g].append(kv.reshape(b, keep, 2, DIL_HEADS, DIL_HEAD_DIM))
        gla_p.append(gla_new)
        mem_p.append(mem_kv.reshape(b, MEM_LEN, 2, X_HEADS, X_HEAD_DIM))
        conv_p.append(tails.reshape(b, seq_tiles, FFN_TAIL, 2 * D_FF)[:, seq_tiles - 1, FFN_TAIL - 2:, :])

        y_pool_s, pool_new_t = pool_step(jnp.swapaxes(state_pool[l], 0, 1), zs, w["pool_w"], l, pool_scale[l])
        y_dil_s = dil_step(zs, dil_caches, l, slot_bias).astype(BF16)
        y_gla_s, gla_new_s = gla_step(zs, zas, state_gla, l, w_a2p, gla_b_a[l], gla_norm[l])
        merged_s = branch_merge(y_pool_s, y_dil_s, y_gla_s.astype(BF16), w["w_br_pool"], w["w_br_dil"], w["w_br_gla"], l,
                                zs, tm=bd, tn=TN_MERGE)
        xs = mm_post(merged_s, w["w_mix_out"], l, norm_mix_post[l], xs, tm=bd, tk=TK_POST)
        q_s = norm_mm(xs, norm_x_pre[l], w["w_xq"], l, tm=bd, tn=TN)
        o_s = xattn_step(q_s, cache_mem_kv, l).astype(BF16)
        xs = mm_post(o_s, w["w_xo"], l, norm_x_post[l], xs, tm=bd, tk=TK_POST)
        act_s, conv_new_t = ffn_up_step(xs, norm_ffn_pre[l], w["w_up"], conv_w, conv_b3, l,
                                        jnp.swapaxes(state_ffn_conv[l], 0, 1), tn=TN_FFN)
        xs = mm_post(act_s, w["w_down"], l, norm_ffn_post[l], xs, tm=bd, tk=TK_POST)

        pool_s.append(jnp.swapaxes(pool_new_t, 0, 1))
        for g in range(3):
            kv = zs[:, COL_DIL_K[g]:COL_DIL_K[g] + 2 * DIL_WIDTH]
            dil_s[g].append(kv.reshape(bd, 1, 2, DIL_HEADS, DIL_HEAD_DIM))
        gla_s.append(gla_new_s)
        conv_s.append(jnp.swapaxes(conv_new_t, 0, 1))

    return (xp.reshape(b, seq, d), xs.reshape(bd, 1, d),
            jnp.stack(pool_p), jnp.stack(dil_p[0]), jnp.stack(dil_p[1]), jnp.stack(dil_p[2]), jnp.stack(gla_p), jnp.stack(mem_p), jnp.stack(conv_p),
            jnp.stack(pool_s), jnp.stack(dil_s[0]), jnp.stack(dil_s[1]), jnp.stack(dil_s[2]), jnp.stack(gla_s), jnp.stack(conv_s))
```

```python
import functools
import math

import jax
import jax.numpy as jnp
from jax import lax
from jax.experimental import pallas as pl
from jax.experimental.pallas import tpu as pltpu

F32 = jnp.float32
BF16 = jnp.bfloat16
EPS = 1e-6
NEG_INF = -1e30

VMEM_LIMIT_BYTES = 56 * 1024 * 1024
LANES = 128
BF16_SUBLANES = 16

D_MODEL = 2048
POOL_WINDOWS = (2, 4, 8, 16)
POOL_GROUP = 256
POOL_WIDTH = 1024
POOL_HIST = 15
DIL_CONFIGS = ((128, 1), (512, 4), (2048, 16))
DIL_HEADS = 8
DIL_HEAD_DIM = 128
DIL_WIDTH = 1024
DIL_BLOCK = 128
GLA_HEADS = 4
GLA_DK = 256
GLA_DV = 512
GLA_KEY_WIDTH = 1024
GLA_VAL_WIDTH = 2048
GLA_RANK = 16
GLA_TAU = 16.0
GLA_CHUNK = 64
REL_BUCKETS = 32
REL_MAX_DIST = 2048
MEM_LEN = 256
X_HEADS = 4
X_HEAD_DIM = 256
X_WIDTH = 1024
D_FF = 5632
PAST_LEN = 8192

COL_POOL = 0
COL_DIL_Q = tuple(1024 + 3072 * g for g in range(3))
COL_DIL_K = tuple(2048 + 3072 * g for g in range(3))
COL_DIL_V = tuple(3072 + 3072 * g for g in range(3))
COL_GQ = 10240
COL_GK = 11264
COL_GV = 12288
COL_GR = 14336
N_MAIN = 16384
N_GATES = 3 * D_MODEL
N_PROJ = N_MAIN + N_GATES


def _params(*semantics):
    return pltpu.CompilerParams(dimension_semantics=semantics, vmem_limit_bytes=VMEM_LIMIT_BYTES)


def _sigmoid(x):
    return 0.5 * jnp.tanh(0.5 * x) + 0.5


def _log_sigmoid(x):
    return jnp.minimum(x, 0.0) - jnp.log(1.0 + jnp.exp(-jnp.abs(x)))


def _gelu_tanh(x):
    return x * (0.5 * (1.0 + jnp.tanh(math.sqrt(2.0 / math.pi) * (x + 0.044715 * (x * x * x)))))


def _rms_scale(y, g):
    return y * lax.rsqrt(jnp.mean(y * y, axis=-1, keepdims=True) + EPS) * g


def _norm_rows(x_ref, g_ref, xn_ref, rows, chunk, dst_offset=0):
    g = g_ref[...]

    def body(c, carry):
        r0 = pl.multiple_of(c * chunk, chunk)
        xn_ref[pl.ds(dst_offset + r0, chunk), :] = _rms_scale(x_ref[pl.ds(r0, chunk), :], g).astype(BF16)
        return carry

    lax.fori_loop(0, rows // chunk, body, 0)


def _prep_w_in_kernel(w_ref, wx_ref, o_ref, ga_ref, *, n_plain, shift):
    j = pl.program_id(1)

    @pl.when(j < n_plain)
    def _():
        o_ref[...] = w_ref[...].astype(BF16)

    @pl.when(j >= n_plain)
    def _():
        tn = o_ref.shape[0]
        o_ref[0:tn - shift, :] = w_ref[shift:tn, :].astype(BF16)
        o_ref[tn - shift:tn, :] = wx_ref[...].astype(BF16)

    @pl.when(j == n_plain)
    def _():
        ga_ref[0:shift, :] = w_ref[0:shift, :].astype(BF16)
        ga_ref[shift:LANES, :] = jnp.zeros((LANES - shift, ga_ref.shape[1]), BF16)


def prep_w_in(w_in_t, *, tn):
    depth, _, k = w_in_t.shape
    n_plain = N_MAIN // tn
    nj = (N_MAIN + N_GATES) // tn
    return pl.pallas_call(
        functools.partial(_prep_w_in_kernel, n_plain=n_plain, shift=GLA_RANK),
        grid=(depth, nj),
        in_specs=[
            pl.BlockSpec((None, tn, k), lambda l, j: (l, j, 0)),
            pl.BlockSpec((None, GLA_RANK, k), lambda l, j: (l, jnp.maximum(j + 1, n_plain) * (tn // GLA_RANK), 0)),
        ],
        out_specs=[
            pl.BlockSpec((None, tn, k), lambda l, j: (l, j, 0)),
            pl.BlockSpec((None, LANES, k), lambda l, j: (l, 0, 0)),
        ],
        out_shape=[jax.ShapeDtypeStruct((depth, N_MAIN + N_GATES, k), BF16), jax.ShapeDtypeStruct((depth, LANES, k), BF16)],
        compiler_params=_params("parallel", "arbitrary"),
        name="prep_w_in",
    )(w_in_t, w_in_t)


def _in_proj_kernel(x_ref, xs_ref, g_ref, w_ref, wga_ref, o_ref, kv_ref, ga_ref, os_ref, gas_ref, xn_ref,
                    *, tm, ms, chunk, kv_lo, kv_hi):
    i = pl.program_id(0)
    j = pl.program_id(1)
    nt = (((1,), (1,)), ((), ()))

    @pl.when(j == 0)
    def _():
        _norm_rows(x_ref, g_ref, xn_ref, tm, chunk)
        ga_ref[...] = lax.dot_general(xn_ref[0:tm, :], wga_ref[...], nt, preferred_element_type=F32)

    @pl.when(jnp.logical_and(i == 0, j == 0))
    def _():
        xsn = _rms_scale(xs_ref[...], g_ref[...]).astype(BF16)
        xn_ref[tm:tm + ms, :] = xsn
        gas_ref[...] = lax.dot_general(xsn, wga_ref[...], nt, preferred_element_type=F32)

    @pl.when(i == 0)
    def _():
        res = lax.dot_general(xn_ref[...], w_ref[...], nt, preferred_element_type=F32)
        o_ref[...] = res[0:tm, :]
        os_ref[...] = res[tm:tm + ms, :]

    @pl.when(i > 0)
    def _():
        o_ref[...] = lax.dot_general(xn_ref[0:tm, :], w_ref[...], nt, preferred_element_type=F32)

    @pl.when(jnp.logical_and(j >= kv_lo, j < kv_hi))
    def _():
        kv_ref[...] = o_ref[...]


def in_proj(x, xs, g, w_t, w_ga_t, layer, *, tm, tn):
    m, k = x.shape
    ms = xs.shape[0]
    n = w_t.shape[1]
    assert m % tm == 0 and n % tn == 0 and COL_DIL_K[2] % tn == 0 and ms % BF16_SUBLANES == 0
    chunk = min(256, tm)
    nj = n // tn
    kv_lo = COL_DIL_K[2] // tn
    n_kv = 2 * DIL_WIDTH // tn
    return pl.pallas_call(
        functools.partial(_in_proj_kernel, tm=tm, ms=ms, chunk=chunk, kv_lo=kv_lo, kv_hi=kv_lo + n_kv),
        grid=(m // tm, nj),
        in_specs=[
            pl.BlockSpec((tm, k), lambda i, j: (i, 0)),
            pl.BlockSpec((ms, k), lambda i, j: (0, 0)),
            pl.BlockSpec((1, k), lambda i, j: (0, 0)),
            pl.BlockSpec((None, tn, k), lambda i, j: (layer, j, 0)),
            pl.BlockSpec((None, LANES, k), lambda i, j: (layer, 0, 0)),
        ],
        out_specs=[
            pl.BlockSpec((tm, tn), lambda i, j: (i, j)),
            pl.BlockSpec((tm, tn), lambda i, j: (i, jnp.clip(j - kv_lo, 0, n_kv - 1))),
            pl.BlockSpec((tm, LANES), lambda i, j: (i, 0)),
            pl.BlockSpec((ms, tn), lambda i, j: (0, jnp.where(i == 0, j, nj - 1))),
            pl.BlockSpec((ms, LANES), lambda i, j: (0, 0)),
        ],
        out_shape=[
            jax.ShapeDtypeStruct((m, n), F32),
            jax.ShapeDtypeStruct((m, 2 * DIL_WIDTH), F32),
            jax.ShapeDtypeStruct((m, LANES), F32),
            jax.ShapeDtypeStruct((ms, n), F32),
            jax.ShapeDtypeStruct((ms, LANES), F32),
        ],
        scratch_shapes=[pltpu.VMEM((tm + ms, k), BF16)],
        compiler_params=_params("arbitrary", "arbitrary"),
        name="in_proj",
    )(x, xs, g.reshape(1, k), w_t, w_ga_t)


def _norm_mm_kernel(x_ref, g_ref, w_ref, o_ref, xn_ref, *, tm, chunk):
    @pl.when(pl.program_id(1) == 0)
    def _():
        _norm_rows(x_ref, g_ref, xn_ref, tm, chunk)

    o_ref[...] = jnp.dot(xn_ref[...], w_ref[...], preferred_element_type=F32).astype(o_ref.dtype)


def norm_mm(x, g, w, layer, *, tm, tn, out_dtype=F32):
    m, k = x.shape
    n = w.shape[2]
    tm = min(tm, m)
    tn = min(tn, n)
    assert m % tm == 0 and n % tn == 0
    chunk = min(256, tm)
    return pl.pallas_call(
        functools.partial(_norm_mm_kernel, tm=tm, chunk=chunk),
        grid=(m // tm, n // tn),
        in_specs=[
            pl.BlockSpec((tm, k), lambda i, j: (i, 0)),
            pl.BlockSpec((1, k), lambda i, j: (0, 0)),
            pl.BlockSpec((None, k, tn), lambda i, j: (layer, 0, j)),
        ],
        out_specs=pl.BlockSpec((tm, tn), lambda i, j: (i, j)),
        out_shape=jax.ShapeDtypeStruct((m, n), out_dtype),
        scratch_shapes=[pltpu.VMEM((tm, k), BF16)],
        compiler_params=_params("parallel", "arbitrary"),
        name="norm_mm",
    )(x, g.reshape(1, k), w)


def _mm_post_kernel(a_ref, w_ref, g_ref, res_ref, o_ref, acc_ref, *, nk):
    kk = pl.program_id(1)
    part = jnp.dot(a_ref[...], w_ref[...], preferred_element_type=F32)
    if nk == 1:
        o_ref[...] = res_ref[...] + _rms_scale(part, g_ref[...])
        return

    @pl.when(kk == 0)
    def _():
        acc_ref[...] = part

    @pl.when(jnp.logical_and(kk > 0, kk < nk - 1))
    def _():
        acc_ref[...] += part

    @pl.when(kk == nk - 1)
    def _():
        o_ref[...] = res_ref[...] + _rms_scale(acc_ref[...] + part, g_ref[...])


def mm_post(a, w, layer, g, res, *, tm, tk):
    m, k = a.shape
    n = w.shape[2]
    tm = min(tm, m)
    tk = k if k <= tk else k // 2
    assert m % tm == 0 and k % tk == 0 and tk % LANES == 0
    return pl.pallas_call(
        functools.partial(_mm_post_kernel, nk=k // tk),
        grid=(m // tm, k // tk),
        in_specs=[
            pl.BlockSpec((tm, tk), lambda i, kk: (i, kk)),
            pl.BlockSpec((None, tk, n), lambda i, kk: (layer, kk, 0)),
            pl.BlockSpec((1, n), lambda i, kk: (0, 0)),
            pl.BlockSpec((tm, n), lambda i, kk: (i, 0)),
        ],
        out_specs=pl.BlockSpec((tm, n), lambda i, kk: (i, 0)),
        out_shape=jax.ShapeDtypeStruct((m, n), F32),
        scratch_shapes=[pltpu.VMEM((tm, n), F32)],
        compiler_params=_params("parallel", "arbitrary"),
        name="mm_post",
    )(a, w, g.reshape(1, n), res)


def _branch_merge_kernel(yp_ref, yd_ref, yg_ref, wp_ref, wd_ref, wg_ref, g0_ref, g1_ref, g2_ref, o_ref):
    bp = jnp.dot(yp_ref[...], wp_ref[...], preferred_element_type=F32)
    bd = jnp.dot(yd_ref[...], wd_ref[...], preferred_element_type=F32)
    bg = jnp.dot(yg_ref[...], wg_ref[...], preferred_element_type=F32)
    merged = _sigmoid(g0_ref[...]) * bp + _sigmoid(g1_ref[...]) * bd + _sigmoid(g2_ref[...]) * bg
    o_ref[...] = merged.astype(o_ref.dtype)


def branch_merge(y_pool, y_dil, y_gla, w_pool, w_dil, w_gla, layer, z_main, *, tm, tn):
    m = y_pool.shape[0]
    tm = min(tm, m)
    assert m % tm == 0 and D_MODEL % tn == 0
    gate_blk = [(N_MAIN + b * D_MODEL) // tn for b in range(3)]

    def gate_spec(b):
        return pl.BlockSpec((tm, tn), lambda i, j: (i, gate_blk[b] + j))

    return pl.pallas_call(
        _branch_merge_kernel,
        grid=(m // tm, D_MODEL // tn),
        in_specs=[
            pl.BlockSpec((tm, POOL_WIDTH), lambda i, j: (i, 0)),
            pl.BlockSpec((tm, DIL_WIDTH), lambda i, j: (i, 0)),
            pl.BlockSpec((tm, GLA_VAL_WIDTH), lambda i, j: (i, 0)),
            pl.BlockSpec((None, POOL_WIDTH, tn), lambda i, j: (layer, 0, j)),
            pl.BlockSpec((None, DIL_WIDTH, tn), lambda i, j: (layer, 0, j)),
            pl.BlockSpec((None, GLA_VAL_WIDTH, tn), lambda i, j: (layer, 0, j)),
            gate_spec(0),
            gate_spec(1),
            gate_spec(2),
        ],
        out_specs=pl.BlockSpec((tm, tn), lambda i, j: (i, j)),
        out_shape=jax.ShapeDtypeStruct((m, D_MODEL), BF16),
        compiler_params=_params("parallel", "arbitrary"),
        name="branch_merge",
    )(y_pool, y_dil, y_gla, w_pool, w_dil, w_gla, z_main, z_main, z_main)


def _pool_prompt_kernel(u_ref, w_ref, s_ref, o_ref, buf_a, buf_b, *, seq):
    pad = POOL_HIST + 1
    zeros = jnp.zeros((pad, POOL_GROUP), F32)
    t = lax.broadcasted_iota(jnp.int32, (seq, 1), 0)
    for g, win in enumerate(POOL_WINDOWS):
        cols = slice(g * POOL_GROUP, (g + 1) * POOL_GROUP)
        u = u_ref[:, cols]
        cur, nxt = buf_a, buf_b
        cur[0:pad, :] = zeros
        nxt[0:pad, :] = zeros
        cur[pad:pad + seq, :] = u
        k = 1
        while k < win:
            nxt[pad:pad + seq, :] = cur[pad:pad + seq, :] + cur[pad - k:pad - k + seq, :]
            cur, nxt = nxt, cur
            k *= 2
        cnt = jnp.minimum(win, t + 1).astype(F32)
        d = cur[pad:pad + seq, :] / cnt - u
        y = jnp.dot(d.astype(BF16), w_ref[g], preferred_element_type=F32) * s_ref[:, cols]
        o_ref[:, cols] = y.astype(o_ref.dtype)


def pool_prompt(z3, pool_w, layer, pool_scale):
    b, seq, _ = z3.shape
    return pl.pallas_call(
        functools.partial(_pool_prompt_kernel, seq=seq),
        grid=(b,),
        in_specs=[
            pl.BlockSpec((None, seq, POOL_WIDTH), lambda i: (i, 0, COL_POOL // POOL_WIDTH)),
            pl.BlockSpec((None, len(POOL_WINDOWS), POOL_GROUP, POOL_GROUP), lambda i: (layer, 0, 0, 0)),
            pl.BlockSpec((1, POOL_WIDTH), lambda i: (0, 0)),
        ],
        out_specs=pl.BlockSpec((None, seq, POOL_WIDTH), lambda i: (i, 0, 0)),
        out_shape=jax.ShapeDtypeStruct((b, seq, POOL_WIDTH), BF16),
        scratch_shapes=[pltpu.VMEM((seq + POOL_HIST + 1, POOL_GROUP), F32)] * 2,
        compiler_params=_params("parallel"),
        name="pool_prompt",
    )(z3, pool_w, pool_scale.reshape(1, POOL_WIDTH))


def _pool_step_kernel(hist_ref, u_ref, w_ref, s_ref, y_ref, new_ref):
    u = u_ref[...]
    for r in range(POOL_HIST - 1):
        new_ref[r] = hist_ref[r + 1]
    new_ref[POOL_HIST - 1] = u
    for g, win in enumerate(POOL_WINDOWS):
        cols = slice(g * POOL_GROUP, (g + 1) * POOL_GROUP)
        ug = u[:, cols]
        acc = ug
        for r in range(POOL_HIST - (win - 1), POOL_HIST):
            acc = acc + hist_ref[r, :, cols]
        d = acc / float(win) - ug
        y = jnp.dot(d.astype(BF16), w_ref[g], preferred_element_type=F32) * s_ref[:, cols]
        y_ref[:, cols] = y.astype(y_ref.dtype)


def pool_step(hist_t, z_s, pool_w, layer, pool_scale):
    bd = z_s.shape[0]
    return pl.pallas_call(
        _pool_step_kernel,
        grid=(1,),
        in_specs=[
            pl.BlockSpec((POOL_HIST, bd, POOL_WIDTH), lambda i: (0, 0, 0)),
            pl.BlockSpec((bd, POOL_WIDTH), lambda i: (0, COL_POOL // POOL_WIDTH)),
            pl.BlockSpec((None, len(POOL_WINDOWS), POOL_GROUP, POOL_GROUP), lambda i: (layer, 0, 0, 0)),
            pl.BlockSpec((1, POOL_WIDTH), lambda i: (0, 0)),
        ],
        out_specs=[
            pl.BlockSpec((bd, POOL_WIDTH), lambda i: (0, 0)),
            pl.BlockSpec((POOL_HIST, bd, POOL_WIDTH), lambda i: (0, 0, 0)),
        ],
        out_shape=[
            jax.ShapeDtypeStruct((bd, POOL_WIDTH), BF16),
            jax.ShapeDtypeStruct((POOL_HIST, bd, POOL_WIDTH), F32),
        ],
        compiler_params=_params("arbitrary"),
        name="pool_step",
    )(hist_t, z_s, pool_w, pool_scale.reshape(1, POOL_WIDTH))


def _rel_bucket(dist):
    max_exact = REL_BUCKETS // 2
    scaled = jnp.log(jnp.maximum(dist, 1).astype(F32) / max_exact) / math.log(REL_MAX_DIST / max_exact)
    large = jnp.minimum(max_exact + (scaled * (REL_BUCKETS - max_exact)).astype(jnp.int32), REL_BUCKETS - 1)
    return jnp.where(dist < max_exact, dist, large)


def _select_rows(table_t, index, n):
    onehot = (index.reshape(1, -1) == jnp.arange(n, dtype=jnp.int32)[:, None]).astype(F32)
    out = jnp.dot(table_t, onehot, precision=lax.Precision.HIGHEST, preferred_element_type=F32)
    return out.reshape((table_t.shape[0],) + index.shape)


def _slot_biases(rel_bias):
    out = []
    for g, (win, dil) in enumerate(DIL_CONFIGS):
        dist = jnp.arange(win // dil + 1, dtype=jnp.int32) * dil
        table_t = rel_bias[:, g * DIL_HEADS:(g + 1) * DIL_HEADS].T.astype(F32)
        out.append(_select_rows(table_t, _rel_bucket(dist), REL_BUCKETS))
    return out


def _band_bias(slot_bias):
    qi = jnp.arange(DIL_BLOCK, dtype=jnp.int32)[:, None] + DIL_BLOCK
    ki = jnp.arange(2 * DIL_BLOCK, dtype=jnp.int32)[None, :]
    rel = qi - ki
    out = []
    for g, (win, dil) in enumerate(DIL_CONFIGS):
        n_slots = win // dil
        ok = (rel >= 0) & (rel <= n_slots)
        b = _select_rows(slot_bias[g], jnp.clip(rel, 0, n_slots), n_slots + 1)
        out.append(jnp.where(ok[None], b, NEG_INF))
    return jnp.stack(out, axis=0)


def _dil_prompt_kernel(q0, k0, v0, q1, k1, v1, q2, k2, v2, bias_ref, o_ref, o_scr, lse_scr, *, seq):
    qs, ks, vs = (q0, q1, q2), (k0, k1, k2), (v0, v1, v2)
    scale = DIL_HEAD_DIM ** -0.5
    blk = DIL_BLOCK
    nt = (((1,), (1,)), ((), ()))

    def rows(ref, start, dil):
        if dil == 1:
            return ref[pl.ds(start, blk), :]
        return ref[pl.ds(start, blk, stride=dil), :]

    for g, (_, dil) in enumerate(DIL_CONFIGS):
        nb = seq // dil // blk
        for r in range(dil):
            for ub in range(nb):
                start = r + dil * ub * blk
                q = rows(qs[g], start, dil).astype(BF16)
                kc = rows(ks[g], start, dil).astype(BF16)
                vc = rows(vs[g], start, dil).astype(BF16)
                if ub == 0:
                    kk, vv, bias = kc, vc, bias_ref[g, :, blk:]
                else:
                    prev = start - dil * blk
                    kk = jnp.concatenate([rows(ks[g], prev, dil).astype(BF16), kc], axis=0)
                    vv = jnp.concatenate([rows(vs[g], prev, dil).astype(BF16), vc], axis=0)
                    bias = bias_ref[g]
                s = lax.dot_general(q, kk, nt, preferred_element_type=F32) * scale + bias
                m = jnp.max(s, axis=-1, keepdims=True)
                e = jnp.exp(s - m)
                l = jnp.sum(e, axis=-1, keepdims=True)
                o = jnp.dot(e.astype(BF16), vv, preferred_element_type=F32) / l
                lse = jnp.broadcast_to(m + jnp.log(l), (blk, LANES))
                if dil == 1:
                    o_scr[g, pl.ds(start, blk), :] = o
                    lse_scr[g, pl.ds(start, blk), :] = lse
                else:
                    o_scr[g, pl.ds(start, blk, stride=dil), :] = o
                    lse_scr[g, pl.ds(start, blk, stride=dil), :] = lse

    chunk = 256

    def combine(c, carry):
        sl = pl.ds(pl.multiple_of(c * chunk, chunk), chunk)
        l0, l1, l2 = lse_scr[0, sl, :], lse_scr[1, sl, :], lse_scr[2, sl, :]
        mx = jnp.maximum(jnp.maximum(l0, l1), l2)
        w0, w1, w2 = jnp.exp(l0 - mx), jnp.exp(l1 - mx), jnp.exp(l2 - mx)
        y = (w0 * o_scr[0, sl, :] + w1 * o_scr[1, sl, :] + w2 * o_scr[2, sl, :]) / (w0 + w1 + w2)
        o_ref[sl, :] = y.astype(o_ref.dtype)
        return carry

    lax.fori_loop(0, seq // chunk, combine, 0)


def dil_prompt(z3, band_bias):
    b, seq, _ = z3.shape
    assert seq % (DIL_BLOCK * 16) == 0

    def col_spec(col):
        blk0 = col // DIL_HEAD_DIM
        return pl.BlockSpec((None, seq, DIL_HEAD_DIM), lambda i, h: (i, 0, blk0 + h))

    in_specs = []
    for g in range(3):
        in_specs += [col_spec(COL_DIL_Q[g]), col_spec(COL_DIL_K[g]), col_spec(COL_DIL_V[g])]
    in_specs.append(pl.BlockSpec((3, None, DIL_BLOCK, 2 * DIL_BLOCK), lambda i, h: (0, h, 0, 0)))
    return pl.pallas_call(
        functools.partial(_dil_prompt_kernel, seq=seq),
        grid=(b, DIL_HEADS),
        in_specs=in_specs,
        out_specs=pl.BlockSpec((None, seq, DIL_HEAD_DIM), lambda i, h: (i, 0, h)),
        out_shape=jax.ShapeDtypeStruct((b, seq, DIL_WIDTH), BF16),
        scratch_shapes=[pltpu.VMEM((3, seq, DIL_HEAD_DIM), F32), pltpu.VMEM((3, seq, LANES), F32)],
        compiler_params=_params("parallel", "arbitrary"),
        name="dil_prompt",
    )(*([z3] * 9), band_bias)


def _dil_step_kernel(qkv_ref, c0, c1, c2, bias_ref, bias0_ref, o_ref):
    caches = (c0, c1, c2)
    scale = DIL_HEAD_DIM ** -0.5
    outs, lses = [], []
    for g in range(3):
        q, kn, vn = qkv_ref[3 * g], qkv_ref[3 * g + 1], qkv_ref[3 * g + 2]
        kc = caches[g][:, 0]
        vc = caches[g][:, 1]
        s = jnp.sum(kc * q[None], axis=-1, keepdims=True) * scale + bias_ref[g]
        s_new = jnp.sum(q * kn, axis=-1, keepdims=True) * scale + bias0_ref[g]
        m = jnp.maximum(jnp.max(s, axis=0), s_new)
        p = jnp.exp(s - m[None])
        p_new = jnp.exp(s_new - m)
        l = jnp.sum(p, axis=0) + p_new
        outs.append((jnp.sum(p * vc, axis=0) + p_new * vn) / l)
        lses.append(m + jnp.log(l))
    mx = jnp.maximum(jnp.maximum(lses[0], lses[1]), lses[2])
    w0, w1, w2 = jnp.exp(lses[0] - mx), jnp.exp(lses[1] - mx), jnp.exp(lses[2] - mx)
    o_ref[...] = (w0 * outs[0] + w1 * outs[1] + w2 * outs[2]) / (w0 + w1 + w2)


def dil_step(z_s, caches, layer, slot_bias):
    bd = z_s.shape[0]
    n_slots = DIL_CONFIGS[0][0] // DIL_CONFIGS[0][1]
    qkv = z_s[:, COL_DIL_Q[0]:COL_DIL_V[2] + DIL_WIDTH].reshape(bd, 9, DIL_HEADS, DIL_HEAD_DIM)
    in_specs = [pl.BlockSpec((None, 9, DIL_HEADS, DIL_HEAD_DIM), lambda i: (i, 0, 0, 0))]
    cache_views = []
    for g, (win, dil) in enumerate(DIL_CONFIGS):
        depth = caches[g].shape[0]
        assert caches[g].shape[2] == win and win // dil == n_slots
        cache_views.append(caches[g].reshape(depth, bd, n_slots, dil, 2, DIL_HEADS, DIL_HEAD_DIM))
        in_specs.append(pl.BlockSpec((None, None, n_slots, None, 2, DIL_HEADS, DIL_HEAD_DIM),
                                     lambda i: (layer, i, 0, 0, 0, 0, 0)))
    bias_rows = jnp.stack([sb[:, n_slots:0:-1].T for sb in slot_bias], axis=0)
    bias_rows = jnp.broadcast_to(bias_rows[..., None], (3, n_slots, DIL_HEADS, LANES))
    bias_new = jnp.broadcast_to(jnp.stack([sb[:, 0] for sb in slot_bias], axis=0)[..., None], (3, DIL_HEADS, LANES))
    in_specs.append(pl.BlockSpec((3, n_slots, DIL_HEADS, LANES), lambda i: (0, 0, 0, 0)))
    in_specs.append(pl.BlockSpec((3, DIL_HEADS, LANES), lambda i: (0, 0, 0)))
    out = pl.pallas_call(
        _dil_step_kernel,
        grid=(bd,),
        in_specs=in_specs,
        out_specs=pl.BlockSpec((None, DIL_HEADS, DIL_HEAD_DIM), lambda i: (i, 0, 0)),
        out_shape=jax.ShapeDtypeStruct((bd, DIL_HEADS, DIL_HEAD_DIM), F32),
        compiler_params=_params("parallel"),
        name="dil_step",
    )(qkv, *cache_views, bias_rows, bias_new)
    return out.reshape(bd, DIL_WIDTH)


GLA_HEADS_PER_STEP = 2
GLA_CUMSUM_ROWS = 256


def _gla_prompt_kernel(q_ref, k_ref, v_ref, r_ref, ga_ref, wa_ref, ba_ref, gn_ref, y_ref, s_out_ref, st_ref, o_scr, b_scr, *, tq):
    t = pl.program_id(2)
    ck = GLA_CHUNK

    @pl.when(t == 0)
    def _():
        st_ref[...] = jnp.zeros_like(st_ref)

    la = jnp.dot(ga_ref[...].astype(BF16), wa_ref[...], preferred_element_type=F32) + ba_ref[...]
    log_a = _log_sigmoid(la) / GLA_TAU
    row = lax.broadcasted_iota(jnp.int32, (ck, ck), 0)
    col = lax.broadcasted_iota(jnp.int32, (ck, ck), 1)
    tril = row >= col
    rows_c = GLA_CUMSUM_ROWS
    rr = lax.broadcasted_iota(jnp.int32, (rows_c, rows_c), 0)
    cc = lax.broadcasted_iota(jnp.int32, (rows_c, rows_c), 1)
    prefix = jnp.logical_and(rr >= cc, rr // ck == cc // ck).astype(BF16)
    for r0 in range(0, tq, rows_c):
        la_blk = log_a[r0:r0 + rows_c, :]
        hi = la_blk.astype(BF16)
        lo = (la_blk - hi.astype(F32)).astype(BF16)
        b_scr[r0:r0 + rows_c, :] = (jnp.dot(prefix, hi, preferred_element_type=F32)
                                    + jnp.dot(prefix, lo, preferred_element_type=F32))
    nt = (((1,), (1,)), ((), ()))
    tn = (((0,), (0,)), ((), ()))
    for c in range(tq // ck):
        sl = slice(c * ck, (c + 1) * ck)
        for hh in range(GLA_HEADS_PER_STEP):
            ks = slice(hh * GLA_DK, (hh + 1) * GLA_DK)
            vs = slice(hh * GLA_DV, (hh + 1) * GLA_DV)
            b = b_scr[sl, ks]
            b_last = b[ck - 1:ck, :]
            q = q_ref[sl, ks] * (GLA_DK ** -0.5)
            k = k_ref[sl, ks]
            v = v_ref[sl, vs].astype(BF16)
            q_t = (q * jnp.exp(b)).astype(BF16)
            k_t = (k * jnp.exp(-b)).astype(BF16)
            k_h = (k * jnp.exp(b_last - b)).astype(BF16)
            decay = jnp.exp(b_last)
            a = lax.dot_general(q_t, k_t, nt, preferred_element_type=F32)
            a = jnp.where(tril, a, 0.0)
            st = st_ref[hh]
            o = jnp.dot(a.astype(BF16), v, preferred_element_type=F32)
            o = o + lax.dot_general(q_t, st.astype(BF16), nt, preferred_element_type=F32)
            o_scr[sl, vs] = o
            st_ref[hh] = decay * st + lax.dot_general(v, k_h, tn, preferred_element_type=F32)

    for hh in range(GLA_HEADS_PER_STEP):
        vs = slice(hh * GLA_DV, (hh + 1) * GLA_DV)
        o = _rms_scale(o_scr[:, vs], gn_ref[:, vs])
        r = r_ref[:, vs]
        y_ref[:, vs] = (o * (r * _sigmoid(r))).astype(y_ref.dtype)

    @pl.when(t == pl.num_programs(2) - 1)
    def _():
        for hh in range(GLA_HEADS_PER_STEP):
            s_out_ref[hh] = st_ref[hh].T


def gla_prompt(z3, za3, w_a2p, b_a, gla_norm, *, tq=1024):
    b, seq, _ = z3.shape
    hp = GLA_HEADS_PER_STEP
    kw, vw = hp * GLA_DK, hp * GLA_DV
    assert seq % tq == 0 and tq % GLA_CHUNK == 0 and GLA_HEADS % hp == 0
    return pl.pallas_call(
        functools.partial(_gla_prompt_kernel, tq=tq),
        grid=(b, GLA_HEADS // hp, seq // tq),
        in_specs=[
            pl.BlockSpec((None, tq, kw), lambda i, h, t: (i, t, COL_GQ // kw + h)),
            pl.BlockSpec((None, tq, kw), lambda i, h, t: (i, t, COL_GK // kw + h)),
            pl.BlockSpec((None, tq, vw), lambda i, h, t: (i, t, COL_GV // vw + h)),
            pl.BlockSpec((None, tq, vw), lambda i, h, t: (i, t, COL_GR // vw + h)),
            pl.BlockSpec((None, tq, LANES), lambda i, h, t: (i, t, 0)),
            pl.BlockSpec((LANES, kw), lambda i, h, t: (0, h)),
            pl.BlockSpec((1, kw), lambda i, h, t: (0, h)),
            pl.BlockSpec((1, vw), lambda i, h, t: (0, h)),
        ],
        out_specs=[
            pl.BlockSpec((None, tq, vw), lambda i, h, t: (i, t, h)),
            pl.BlockSpec((None, hp, GLA_DK, GLA_DV), lambda i, h, t: (i, h, 0, 0)),
        ],
        out_shape=[
            jax.ShapeDtypeStruct((b, seq, GLA_VAL_WIDTH), BF16),
            jax.ShapeDtypeStruct((b, GLA_HEADS, GLA_DK, GLA_DV), F32),
        ],
        scratch_shapes=[
            pltpu.VMEM((hp, GLA_DV, GLA_DK), F32),
            pltpu.VMEM((tq, vw), F32),
            pltpu.VMEM((tq, kw), F32),
        ],
        compiler_params=_params("parallel", "parallel", "arbitrary"),
        name="gla_prompt",
    )(z3, z3, z3, z3, za3, w_a2p, b_a.reshape(1, GLA_KEY_WIDTH), gla_norm.reshape(1, GLA_VAL_WIDTH))


def _gla_step_kernel(q_ref, k_ref, v_ref, r_ref, ga_ref, wa_ref, ba_ref, gn_ref, s_ref, y_ref, s_out_ref):
    ga8 = jnp.broadcast_to(ga_ref[...], (8, LANES)).astype(BF16)
    la = jnp.dot(ga8, wa_ref[...], preferred_element_type=F32)[0:1, :] + ba_ref[...]
    ea = jnp.exp(_log_sigmoid(la) / GLA_TAU)
    q = q_ref[...] * (GLA_DK ** -0.5)
    k = k_ref[...]
    eye = lax.broadcasted_iota(jnp.int32, (GLA_DK, GLA_DK), 0) == lax.broadcasted_iota(jnp.int32, (GLA_DK, GLA_DK), 1)

    def column(row):
        return jnp.sum(jnp.where(eye, jnp.broadcast_to(row, (GLA_DK, GLA_DK)), 0.0), axis=-1, keepdims=True)

    for h in range(GLA_HEADS):
        ks = slice(h * GLA_DK, (h + 1) * GLA_DK)
        vs = slice(h * GLA_DV, (h + 1) * GLA_DV)
        s_new = column(ea[:, ks]) * s_ref[h] + column(k[:, ks]) * v_ref[:, vs]
        s_out_ref[h] = s_new
        o = jnp.sum(column(q[:, ks]) * s_new, axis=0, keepdims=True)
        o = _rms_scale(o, gn_ref[:, vs])
        r = r_ref[:, vs]
        y_ref[:, vs] = o * (r * _sigmoid(r))


def gla_step(z_s, za_s, state, layer, w_a2p, b_a, gla_norm):
    bd = z_s.shape[0]
    zs3 = z_s.reshape(bd, 1, N_PROJ)
    y, s_new = pl.pallas_call(
        _gla_step_kernel,
        grid=(bd,),
        in_specs=[
            pl.BlockSpec((None, 1, GLA_KEY_WIDTH), lambda i: (i, 0, COL_GQ // GLA_KEY_WIDTH)),
            pl.BlockSpec((None, 1, GLA_KEY_WIDTH), lambda i: (i, 0, COL_GK // GLA_KEY_WIDTH)),
            pl.BlockSpec((None, 1, GLA_VAL_WIDTH), lambda i: (i, 0, COL_GV // GLA_VAL_WIDTH)),
            pl.BlockSpec((None, 1, GLA_VAL_WIDTH), lambda i: (i, 0, COL_GR // GLA_VAL_WIDTH)),
            pl.BlockSpec((None, 1, LANES), lambda i: (i, 0, 0)),
            pl.BlockSpec((LANES, GLA_KEY_WIDTH), lambda i: (0, 0)),
            pl.BlockSpec((1, GLA_KEY_WIDTH), lambda i: (0, 0)),
            pl.BlockSpec((1, GLA_VAL_WIDTH), lambda i: (0, 0)),
            pl.BlockSpec((None, None, GLA_HEADS, GLA_DK, GLA_DV), lambda i: (layer, i, 0, 0, 0)),
        ],
        out_specs=[
            pl.BlockSpec((None, 1, GLA_VAL_WIDTH), lambda i: (i, 0, 0)),
            pl.BlockSpec((None, GLA_HEADS, GLA_DK, GLA_DV), lambda i: (i, 0, 0, 0)),
        ],
        out_shape=[
            jax.ShapeDtypeStruct((bd, 1, GLA_VAL_WIDTH), F32),
            jax.ShapeDtypeStruct((bd, GLA_HEADS, GLA_DK, GLA_DV), F32),
        ],
        compiler_params=_params("parallel"),
        name="gla_step",
    )(zs3, zs3, zs3, zs3, za_s.reshape(bd, 1, LANES), w_a2p, b_a.reshape(1, GLA_KEY_WIDTH),
      gla_norm.reshape(1, GLA_VAL_WIDTH), state)
    return y.reshape(bd, GLA_VAL_WIDTH), s_new


def _xattn_kernel(q_ref, kv_ref, o_ref):
    scale = X_HEAD_DIM ** -0.5
    nt = (((1,), (1,)), ((), ()))
    for h in range(X_HEADS):
        hs = slice(h * X_HEAD_DIM, (h + 1) * X_HEAD_DIM)
        k = kv_ref[:, h * X_HEAD_DIM:(h + 1) * X_HEAD_DIM].astype(BF16)
        v = kv_ref[:, X_WIDTH + h * X_HEAD_DIM:X_WIDTH + (h + 1) * X_HEAD_DIM].astype(BF16)
        s = lax.dot_general(q_ref[:, hs], k, nt, preferred_element_type=F32) * scale
        e = jnp.exp(s - jnp.max(s, axis=-1, keepdims=True))
        l = jnp.sum(e, axis=-1, keepdims=True)
        o = jnp.dot(e.astype(BF16), v, preferred_element_type=F32) / l
        o_ref[:, hs] = o.astype(o_ref.dtype)


def xattn(q3, mem_kv, *, tt):
    b, t, _ = q3.shape
    tt = min(tt, t)
    assert t % tt == 0
    return pl.pallas_call(
        _xattn_kernel,
        grid=(b, t // tt),
        in_specs=[
            pl.BlockSpec((None, tt, X_WIDTH), lambda i, j: (i, j, 0)),
            pl.BlockSpec((None, MEM_LEN, 2 * X_WIDTH), lambda i, j: (i, 0, 0)),
        ],
        out_specs=pl.BlockSpec((None, tt, X_WIDTH), lambda i, j: (i, j, 0)),
        out_shape=jax.ShapeDtypeStruct((b, t, X_WIDTH), BF16),
        compiler_params=_params("parallel", "arbitrary"),
        name="xattn",
    )(q3, mem_kv)


def _xattn_step_kernel(q_ref, kv_ref, o_ref):
    q = q_ref[...]
    k = kv_ref[:, 0]
    v = kv_ref[:, 1]
    s = jnp.sum(k * q[None], axis=-1, keepdims=True) * (X_HEAD_DIM ** -0.5)
    p = jnp.exp(s - jnp.max(s, axis=0)[None])
    o_ref[...] = jnp.sum(p * v, axis=0) / jnp.sum(p, axis=0)


def xattn_step(q, mem_kv, layer):
    bd = q.shape[0]
    out = pl.pallas_call(
        _xattn_step_kernel,
        grid=(bd,),
        in_specs=[
            pl.BlockSpec((None, X_HEADS, X_HEAD_DIM), lambda i: (i, 0, 0)),
            pl.BlockSpec((None, None, MEM_LEN, 2, X_HEADS, X_HEAD_DIM), lambda i: (layer, i, 0, 0, 0, 0)),
        ],
        out_specs=pl.BlockSpec((None, X_HEADS, X_HEAD_DIM), lambda i: (i, 0, 0)),
        out_shape=jax.ShapeDtypeStruct((bd, X_HEADS, X_HEAD_DIM), F32),
        compiler_params=_params("parallel"),
        name="xattn_step",
    )(q.reshape(bd, X_HEADS, X_HEAD_DIM), mem_kv)
    return out.reshape(bd, X_WIDTH)


FFN_HALO = 16
FFN_TAIL = 8


def _ffn_up_prompt_kernel(x_ref, xh_ref, g_ref, wg_ref, wv_ref, cwg_ref, cwv_ref, cbg_ref, cbv_ref,
                          act_ref, tg_ref, tv_ref, xn_ref, ug_ref, uv_ref, *, tm, seq_tiles, chunk):
    i = pl.program_id(0)

    @pl.when(pl.program_id(1) == 0)
    def _():
        _norm_rows(x_ref, g_ref, xn_ref, tm, chunk, dst_offset=FFN_HALO)
        halo = _rms_scale(xh_ref[...], g_ref[...])
        halo = jnp.where(i % seq_tiles == 0, 0.0, halo)
        xn_ref[0:FFN_HALO, :] = halo.astype(BF16)

    xn = xn_ref[...]
    ug_ref[...] = jnp.dot(xn, wg_ref[...], preferred_element_type=F32)
    uv_ref[...] = jnp.dot(xn, wv_ref[...], preferred_element_type=F32)

    def conv(u_ref, cw_ref, cb_ref):
        lead = FFN_TAIL
        u = u_ref[FFN_HALO - lead:FFN_HALO + tm, :]
        c = cb_ref[...] + cw_ref[0:1, :] * pltpu.roll(u, 2, 0) + cw_ref[1:2, :] * pltpu.roll(u, 1, 0) + cw_ref[2:3, :] * u
        return c[lead:lead + tm]

    act_ref[...] = (_gelu_tanh(conv(ug_ref, cwg_ref, cbg_ref)) * conv(uv_ref, cwv_ref, cbv_ref)).astype(act_ref.dtype)
    tg_ref[...] = ug_ref[FFN_HALO + tm - FFN_TAIL:FFN_HALO + tm, :]
    tv_ref[...] = uv_ref[FFN_HALO + tm - FFN_TAIL:FFN_HALO + tm, :]


def ffn_up_prompt(x, g, w_up, conv_w, conv_b3, layer, *, seq, tm, tn):
    m, k = x.shape
    assert seq % tm == 0 and D_FF % tn == 0 and tm % FFN_HALO == 0
    nj = D_FF // tn
    seq_tiles = seq // tm
    chunk = min(256, tm)
    halo_blocks = tm // FFN_HALO
    act, tail_g, tail_v = pl.pallas_call(
        functools.partial(_ffn_up_prompt_kernel, tm=tm, seq_tiles=seq_tiles, chunk=chunk),
        grid=(m // tm, nj),
        in_specs=[
            pl.BlockSpec((tm, k), lambda i, j: (i, 0)),
            pl.BlockSpec((FFN_HALO, k), lambda i, j: (jnp.maximum(i * halo_blocks - 1, 0), 0)),
            pl.BlockSpec((1, k), lambda i, j: (0, 0)),
            pl.BlockSpec((None, k, tn), lambda i, j: (layer, 0, j)),
            pl.BlockSpec((None, k, tn), lambda i, j: (layer, 0, j + nj)),
            pl.BlockSpec((None, 3, tn), lambda i, j: (layer, 0, j)),
            pl.BlockSpec((None, 3, tn), lambda i, j: (layer, 0, j + nj)),
            pl.BlockSpec((None, 1, tn), lambda i, j: (layer, 0, j)),
            pl.BlockSpec((None, 1, tn), lambda i, j: (layer, 0, j + nj)),
        ],
        out_specs=[
            pl.BlockSpec((tm, tn), lambda i, j: (i, j)),
            pl.BlockSpec((None, FFN_TAIL, tn), lambda i, j: (i, 0, j)),
            pl.BlockSpec((None, FFN_TAIL, tn), lambda i, j: (i, 0, j)),
        ],
        out_shape=[
            jax.ShapeDtypeStruct((m, D_FF), BF16),
            jax.ShapeDtypeStruct((m // tm, FFN_TAIL, D_FF), F32),
            jax.ShapeDtypeStruct((m // tm, FFN_TAIL, D_FF), F32),
        ],
        scratch_shapes=[
            pltpu.VMEM((tm + FFN_HALO, k), BF16),
            pltpu.VMEM((tm + FFN_HALO, tn), F32),
            pltpu.VMEM((tm + FFN_HALO, tn), F32),
        ],
        compiler_params=_params("parallel", "arbitrary"),
        name="ffn_up_prompt",
    )(x, x, g.reshape(1, k), w_up, w_up, conv_w, conv_w, conv_b3, conv_b3)
    return act, jnp.concatenate([tail_g, tail_v], axis=-1)


def _ffn_up_step_kernel(x_ref, g_ref, wg_ref, wv_ref, cwg_ref, cwv_ref, cbg_ref, cbv_ref, hg_ref, hv_ref,
                        act_ref, ng_ref, nv_ref, xn_ref, *, bd):
    @pl.when(pl.program_id(0) == 0)
    def _():
        _norm_rows(x_ref, g_ref, xn_ref, bd, bd)

    xn = xn_ref[...]
    ug = jnp.dot(xn, wg_ref[...], preferred_element_type=F32)
    uv = jnp.dot(xn, wv_ref[...], preferred_element_type=F32)

    def conv(u, h_ref, cw_ref, cb_ref):
        return cb_ref[...] + cw_ref[0:1, :] * h_ref[0] + cw_ref[1:2, :] * h_ref[1] + cw_ref[2:3, :] * u

    act_ref[...] = (_gelu_tanh(conv(ug, hg_ref, cwg_ref, cbg_ref)) * conv(uv, hv_ref, cwv_ref, cbv_ref)).astype(act_ref.dtype)
    ng_ref[0] = hg_ref[1]
    ng_ref[1] = ug
    nv_ref[0] = hv_ref[1]
    nv_ref[1] = uv


def ffn_up_step(x, g, w_up, conv_w, conv_b3, layer, hist_t, *, tn):
    bd, k = x.shape
    nj = D_FF // tn
    act, new_g, new_v = pl.pallas_call(
        functools.partial(_ffn_up_step_kernel, bd=bd),
        grid=(nj,),
        in_specs=[
            pl.BlockSpec((bd, k), lambda j: (0, 0)),
            pl.BlockSpec((1, k), lambda j: (0, 0)),
            pl.BlockSpec((None, k, tn), lambda j: (layer, 0, j)),
            pl.BlockSpec((None, k, tn), lambda j: (layer, 0, j + nj)),
            pl.BlockSpec((None, 3, tn), lambda j: (layer, 0, j)),
            pl.BlockSpec((None, 3, tn), lambda j: (layer, 0, j + nj)),
            pl.BlockSpec((None, 1, tn), lambda j: (layer, 0, j)),
            pl.BlockSpec((None, 1, tn), lambda j: (layer, 0, j + nj)),
            pl.BlockSpec((2, bd, tn), lambda j: (0, 0, j)),
            pl.BlockSpec((2, bd, tn), lambda j: (0, 0, j + nj)),
        ],
        out_specs=[
            pl.BlockSpec((bd, tn), lambda j: (0, j)),
            pl.BlockSpec((2, bd, tn), lambda j: (0, 0, j)),
            pl.BlockSpec((2, bd, tn), lambda j: (0, 0, j)),
        ],
        out_shape=[
            jax.ShapeDtypeStruct((bd, D_FF), BF16),
            jax.ShapeDtypeStruct((2, bd, D_FF), F32),
            jax.ShapeDtypeStruct((2, bd, D_FF), F32),
        ],
        scratch_shapes=[pltpu.VMEM((bd, k), BF16)],
        compiler_params=_params("arbitrary"),
        name="ffn_up_step",
    )(x, g.reshape(1, k), w_up, w_up, conv_w, conv_w, conv_b3, conv_b3, hist_t, hist_t)
    return act, jnp.concatenate([new_g, new_v], axis=-1)


TM = 1024
TN = 1024
TM_POST = 512
TK_POST = 2048
TN_MERGE = 512
TN_FFN = 512
XATTN_ROWS = 1024


def _bf16_weights(w_in, pool_w, gla_w_a2, w_br_pool, w_br_dil, w_br_gla, w_mix_out, w_xq, w_xkv, w_xo, w_up, w_down):
    w_in_t, w_ga_t = prep_w_in(jnp.swapaxes(w_in, 1, 2), tn=TN)
    return dict(
        w_in_t=w_in_t,
        w_ga_t=w_ga_t,
        w_a2p=jnp.pad(gla_w_a2, ((0, 0), (0, LANES - GLA_RANK), (0, 0))).astype(BF16),
        pool_w=pool_w.astype(BF16),
        w_br_pool=w_br_pool.astype(BF16),
        w_br_dil=w_br_dil.astype(BF16),
        w_br_gla=w_br_gla.astype(BF16),
        w_mix_out=w_mix_out.astype(BF16),
        w_xq=w_xq.astype(BF16),
        w_xkv=w_xkv.astype(BF16),
        w_xo=w_xo.astype(BF16),
        w_up=w_up.astype(BF16),
        w_down=w_down.astype(BF16),
    )


def kernel(x_prompt, x_sample, state_pool, cache_dil1_kv, cache_dil2_kv, cache_dil3_kv, state_gla, cache_mem_kv, state_ffn_conv, mem_prompt, rel_bias, norm_mix_pre, norm_mix_post, w_in, pool_w, pool_scale, gla_w_a2, gla_b_a, gla_norm, w_br_pool, w_br_dil, w_br_gla, w_mix_out, norm_x_pre, norm_x_post, norm_mem, w_xq, w_xkv, w_xo, norm_ffn_pre, norm_ffn_post, w_up, conv_w, conv_b, w_down):
    b, seq, d = x_prompt.shape
    bd = x_sample.shape[0]
    depth = w_in.shape[0]
    m = b * seq
    assert x_sample.shape[1] == 1 and d == D_MODEL and w_in.shape[2] == N_MAIN + GLA_RANK + N_GATES
    slot_bias = _slot_biases(rel_bias)
    band_bias = _band_bias(slot_bias)
    dil_caches = (cache_dil1_kv, cache_dil2_kv, cache_dil3_kv)

    xp = x_prompt.reshape(m, d)
    xs = x_sample.reshape(bd, d)
    pool_p, gla_p, mem_p, conv_p = [], [], [], []
    pool_s, gla_s, conv_s = [], [], []
    dil_p = [[] for _ in range(3)]
    dil_s = [[] for _ in range(3)]
    seq_tiles = seq // TM

    w = _bf16_weights(w_in, pool_w, gla_w_a2, w_br_pool, w_br_dil, w_br_gla, w_mix_out, w_xq, w_xkv, w_xo, w_up, w_down)
    conv_b3 = conv_b.reshape(depth, 1, 2 * D_FF)
    mem_rows = mem_prompt.reshape(b * MEM_LEN, d)
    for l in range(depth):
        w_a2p = w["w_a2p"][l]

        zm, zkv_wide, za, zs, zas = in_proj(xp, xs, norm_mix_pre[l], w["w_in_t"], w["w_ga_t"], l, tm=TM, tn=TN)
        z3 = zm.reshape(b, seq, N_PROJ)
        y_pool = pool_prompt(z3, w["pool_w"], l, pool_scale[l]).reshape(m, POOL_WIDTH)
        y_dil = dil_prompt(z3, band_bias).reshape(m, DIL_WIDTH)
        y_gla, gla_new = gla_prompt(z3, za.reshape(b, seq, LANES), w_a2p, gla_b_a[l], gla_norm[l])
        merged = branch_merge(y_pool, y_dil, y_gla.reshape(m, GLA_VAL_WIDTH), w["w_br_pool"], w["w_br_dil"], w["w_br_gla"], l,
                              zm, tm=TM, tn=TN_MERGE)
        xp = mm_post(merged, w["w_mix_out"], l, norm_mix_post[l], xp, tm=TM_POST, tk=TK_POST)
        mem_kv = norm_mm(mem_rows, norm_mem[l], w["w_xkv"], l, tm=TM, tn=TN)
        q = norm_mm(xp, norm_x_pre[l], w["w_xq"], l, tm=TM, tn=TN, out_dtype=BF16)
        o = xattn(q.reshape(b, seq, X_WIDTH), mem_kv.reshape(b, MEM_LEN, 2 * X_WIDTH), tt=XATTN_ROWS)
        xp = mm_post(o.reshape(m, X_WIDTH), w["w_xo"], l, norm_x_post[l], xp, tm=TM_POST, tk=TK_POST)
        act, tails = ffn_up_prompt(xp, norm_ffn_pre[l], w["w_up"], conv_w, conv_b3, l, seq=seq, tm=TM, tn=TN_FFN)
        xp = mm_post(act, w["w_down"], l, norm_ffn_post[l], xp, tm=TM_POST, tk=TK_POST)

        pool_p.append(z3[:, seq - POOL_HIST:, COL_POOL:COL_POOL + POOL_WIDTH])
        for g, (win, _) in enumerate(DIL_CONFIGS):
            keep = min(win, seq)
            if g == 2:
                kv = zkv_wide.reshape(b, seq, 2 * DIL_WIDTH)[:, seq - keep:]
            else:
                kv = z3[:, seq - keep:, COL_DIL_K[g]:COL_DIL_K[g] + 2 * DIL_WIDTH]
            dil_p[g].append(kv.reshape(b, keep, 2, DIL_HEADS, DIL_HEAD_DIM))
        gla_p.append(gla_new)
        mem_p.append(mem_kv.reshape(b, MEM_LEN, 2, X_HEADS, X_HEAD_DIM))
        conv_p.append(tails.reshape(b, seq_tiles, FFN_TAIL, 2 * D_FF)[:, seq_tiles - 1, FFN_TAIL - 2:, :])

        y_pool_s, pool_new_t = pool_step(jnp.swapaxes(state_pool[l], 0, 1), zs, w["pool_w"], l, pool_scale[l])
        y_dil_s = dil_step(zs, dil_caches, l, slot_bias).astype(BF16)
        y_gla_s, gla_new_s = gla_step(zs, zas, state_gla, l, w_a2p, gla_b_a[l], gla_norm[l])
        merged_s = branch_merge(y_pool_s, y_dil_s, y_gla_s.astype(BF16), w["w_br_pool"], w["w_br_dil"], w["w_br_gla"], l,
                                zs, tm=bd, tn=TN_MERGE)
        xs = mm_post(merged_s, w["w_mix_out"], l, norm_mix_post[l], xs, tm=bd, tk=TK_POST)
        q_s = norm_mm(xs, norm_x_pre[l], w["w_xq"], l, tm=bd, tn=TN)
        o_s = xattn_step(q_s, cache_mem_kv, l).astype(BF16)
        xs = mm_post(o_s, w["w_xo"], l, norm_x_post[l], xs, tm=bd, tk=TK_POST)
        act_s, conv_new_t = ffn_up_step(xs, norm_ffn_pre[l], w["w_up"], conv_w, conv_b3, l,
                                        jnp.swapaxes(state_ffn_conv[l], 0, 1), tn=TN_FFN)
        xs = mm_post(act_s, w["w_down"], l, norm_ffn_post[l], xs, tm=bd, tk=TK_POST)

        pool_s.append(jnp.swapaxes(pool_new_t, 0, 1))
        for g in range(3):
            kv = zs[:, COL_DIL_K[g]:COL_DIL_K[g] + 2 * DIL_WIDTH]
            dil_s[g].append(kv.reshape(bd, 1, 2, DIL_HEADS, DIL_HEAD_DIM))
        gla_s.append(gla_new_s)
        conv_s.append(jnp.swapaxes(conv_new_t, 0, 1))

    return (xp.reshape(b, seq, d), xs.reshape(bd, 1, d),
            jnp.stack(pool_p), jnp.stack(dil_p[0]), jnp.stack(dil_p[1]), jnp.stack(dil_p[2]), jnp.stack(gla_p), jnp.stack(mem_p), jnp.stack(conv_p),
            jnp.stack(pool_s), jnp.stack(dil_s[0]), jnp.stack(dil_s[1]), jnp.stack(dil_s[2]), jnp.stack(gla_s), jnp.stack(conv_s))
```

```python
import functools
import math

import jax
import jax.numpy as jnp
from jax import lax
from jax.experimental import pallas as pl
from jax.experimental.pallas import tpu as pltpu

F32 = jnp.float32
BF16 = jnp.bfloat16
EPS = 1e-6
NEG_INF = -1e30

VMEM_LIMIT_BYTES = 56 * 1024 * 1024
LANES = 128
BF16_SUBLANES = 16

D_MODEL = 2048
POOL_WINDOWS = (2, 4, 8, 16)
POOL_GROUP = 256
POOL_WIDTH = 1024
POOL_HIST = 15
DIL_CONFIGS = ((128, 1), (512, 4), (2048, 16))
DIL_HEADS = 8
DIL_HEAD_DIM = 128
DIL_WIDTH = 1024
DIL_BLOCK = 128
GLA_HEADS = 4
GLA_DK = 256
GLA_DV = 512
GLA_KEY_WIDTH = 1024
GLA_VAL_WIDTH = 2048
GLA_RANK = 16
GLA_TAU = 16.0
GLA_CHUNK = 64
REL_BUCKETS = 32
REL_MAX_DIST = 2048
MEM_LEN = 256
X_HEADS = 4
X_HEAD_DIM = 256
X_WIDTH = 1024
D_FF = 5632
PAST_LEN = 8192

COL_POOL = 0
COL_DIL_Q = tuple(1024 + 3072 * g for g in range(3))
COL_DIL_K = tuple(2048 + 3072 * g for g in range(3))
COL_DIL_V = tuple(3072 + 3072 * g for g in range(3))
COL_GQ = 10240
COL_GK = 11264
COL_GV = 12288
COL_GR = 14336
N_MAIN = 16384
N_GATES = 3 * D_MODEL
N_PROJ = N_MAIN + N_GATES


def _params(*semantics):
    return pltpu.CompilerParams(dimension_semantics=semantics, vmem_limit_bytes=VMEM_LIMIT_BYTES)


def _sigmoid(x):
    return 0.5 * jnp.tanh(0.5 * x) + 0.5


def _log_sigmoid(x):
    return jnp.minimum(x, 0.0) - jnp.log(1.0 + jnp.exp(-jnp.abs(x)))


def _gelu_tanh(x):
    return x * (0.5 * (1.0 + jnp.tanh(math.sqrt(2.0 / math.pi) * (x + 0.044715 * (x * x * x)))))


def _rms_scale(y, g):
    return y * lax.rsqrt(jnp.mean(y * y, axis=-1, keepdims=True) + EPS) * g


def _norm_rows(x_ref, g_ref, xn_ref, rows, chunk, dst_offset=0):
    g = g_ref[...]

    def body(c, carry):
        r0 = pl.multiple_of(c * chunk, chunk)
        xn_ref[pl.ds(dst_offset + r0, chunk), :] = _rms_scale(x_ref[pl.ds(r0, chunk), :], g).astype(BF16)
        return carry

    lax.fori_loop(0, rows // chunk, body, 0)


def _prep_w_in_kernel(w_ref, wx_ref, o_ref, ga_ref, *, n_plain, shift):
    j = pl.program_id(1)

    @pl.when(j < n_plain)
    def _():
        o_ref[...] = w_ref[...].astype(BF16)

    @pl.when(j >= n_plain)
    def _():
        tn = o_ref.shape[0]
        o_ref[0:tn - shift, :] = w_ref[shift:tn, :].astype(BF16)
        o_ref[tn - shift:tn, :] = wx_ref[...].astype(BF16)

    @pl.when(j == n_plain)
    def _():
        ga_ref[0:shift, :] = w_ref[0:shift, :].astype(BF16)
        ga_ref[shift:LANES, :] = jnp.zeros((LANES - shift, ga_ref.shape[1]), BF16)


def prep_w_in(w_in_t, *, tn):
    depth, _, k = w_in_t.shape
    n_plain = N_MAIN // tn
    nj = (N_MAIN + N_GATES) // tn
    return pl.pallas_call(
        functools.partial(_prep_w_in_kernel, n_plain=n_plain, shift=GLA_RANK),
        grid=(depth, nj),
        in_specs=[
            pl.BlockSpec((None, tn, k), lambda l, j: (l, j, 0)),
            pl.BlockSpec((None, GLA_RANK, k), lambda l, j: (l, jnp.maximum(j + 1, n_plain) * (tn // GLA_RANK), 0)),
        ],
        out_specs=[
            pl.BlockSpec((None, tn, k), lambda l, j: (l, j, 0)),
            pl.BlockSpec((None, LANES, k), lambda l, j: (l, 0, 0)),
        ],
        out_shape=[jax.ShapeDtypeStruct((depth, N_MAIN + N_GATES, k), BF16), jax.ShapeDtypeStruct((depth, LANES, k), BF16)],
        compiler_params=_params("parallel", "arbitrary"),
        name="prep_w_in",
    )(w_in_t, w_in_t)


def _in_proj_kernel(x_ref, xs_ref, g_ref, w_ref, wga_ref, o_ref, kv_ref, ga_ref, os_ref, gas_ref, xn_ref,
                    *, tm, ms, chunk, kv_lo, kv_hi):
    i = pl.program_id(0)
    j = pl.program_id(1)
    nt = (((1,), (1,)), ((), ()))

    @pl.when(j == 0)
    def _():
        _norm_rows(x_ref, g_ref, xn_ref, tm, chunk)
        ga_ref[...] = lax.dot_general(xn_ref[0:tm, :], wga_ref[...], nt, preferred_element_type=F32)

    @pl.when(jnp.logical_and(i == 0, j == 0))
    def _():
        xsn = _rms_scale(xs_ref[...], g_ref[...]).astype(BF16)
        xn_ref[tm:tm + ms, :] = xsn
        gas_ref[...] = lax.dot_general(xsn, wga_ref[...], nt, preferred_element_type=F32)

    @pl.when(i == 0)
    def _():
        res = lax.dot_general(xn_ref[...], w_ref[...], nt, preferred_element_type=F32)
        o_ref[...] = res[0:tm, :]
        os_ref[...] = res[tm:tm + ms, :]

    @pl.when(i > 0)
    def _():
        o_ref[...] = lax.dot_general(xn_ref[0:tm, :], w_ref[...], nt, preferred_element_type=F32)

    @pl.when(jnp.logical_and(j >= kv_lo, j < kv_hi))
    def _():
        kv_ref[...] = o_ref[...]


def in_proj(x, xs, g, w_t, w_ga_t, layer, *, tm, tn):
    m, k = x.shape
    ms = xs.shape[0]
    n = w_t.shape[1]
    assert m % tm == 0 and n % tn == 0 and COL_DIL_K[2] % tn == 0 and ms % BF16_SUBLANES == 0
    chunk = min(256, tm)
    nj = n // tn
    kv_lo = COL_DIL_K[2] // tn
    n_kv = 2 * DIL_WIDTH // tn
    return pl.pallas_call(
        functools.partial(_in_proj_kernel, tm=tm, ms=ms, chunk=chunk, kv_lo=kv_lo, kv_hi=kv_lo + n_kv),
        grid=(m // tm, nj),
        in_specs=[
            pl.BlockSpec((tm, k), lambda i, j: (i, 0)),
            pl.BlockSpec((ms, k), lambda i, j: (0, 0)),
            pl.BlockSpec((1, k), lambda i, j: (0, 0)),
            pl.BlockSpec((None, tn, k), lambda i, j: (layer, j, 0)),
            pl.BlockSpec((None, LANES, k), lambda i, j: (layer, 0, 0)),
        ],
        out_specs=[
            pl.BlockSpec((tm, tn), lambda i, j: (i, j)),
            pl.BlockSpec((tm, tn), lambda i, j: (i, jnp.clip(j - kv_lo, 0, n_kv - 1))),
            pl.BlockSpec((tm, LANES), lambda i, j: (i, 0)),
            pl.BlockSpec((ms, tn), lambda i, j: (0, jnp.where(i == 0, j, nj - 1))),
            pl.BlockSpec((ms, LANES), lambda i, j: (0, 0)),
        ],
        out_shape=[
            jax.ShapeDtypeStruct((m, n), F32),
            jax.ShapeDtypeStruct((m, 2 * DIL_WIDTH), F32),
            jax.ShapeDtypeStruct((m, LANES), F32),
            jax.ShapeDtypeStruct((ms, n), F32),
            jax.ShapeDtypeStruct((ms, LANES), F32),
        ],
        scratch_shapes=[pltpu.VMEM((tm + ms, k), BF16)],
        compiler_params=_params("arbitrary", "arbitrary"),
        name="in_proj",
    )(x, xs, g.reshape(1, k), w_t, w_ga_t)


def _norm_mm_kernel(x_ref, g_ref, w_ref, o_ref, xn_ref, *, tm, chunk):
    @pl.when(pl.program_id(1) == 0)
    def _():
        _norm_rows(x_ref, g_ref, xn_ref, tm, chunk)

    o_ref[...] = jnp.dot(xn_ref[...], w_ref[...], preferred_element_type=F32).astype(o_ref.dtype)


def norm_mm(x, g, w, layer, *, tm, tn, out_dtype=F32):
    m, k = x.shape
    n = w.shape[2]
    tm = min(tm, m)
    tn = min(tn, n)
    assert m % tm == 0 and n % tn == 0
    chunk = min(256, tm)
    return pl.pallas_call(
        functools.partial(_norm_mm_kernel, tm=tm, chunk=chunk),
        grid=(m // tm, n // tn),
        in_specs=[
            pl.BlockSpec((tm, k), lambda i, j: (i, 0)),
            pl.BlockSpec((1, k), lambda i, j: (0, 0)),
            pl.BlockSpec((None, k, tn), lambda i, j: (layer, 0, j)),
        ],
        out_specs=pl.BlockSpec((tm, tn), lambda i, j: (i, j)),
        out_shape=jax.ShapeDtypeStruct((m, n), out_dtype),
        scratch_shapes=[pltpu.VMEM((tm, k), BF16)],
        compiler_params=_params("parallel", "arbitrary"),
        name="norm_mm",
    )(x, g.reshape(1, k), w)


def _mm_post_kernel(a_ref, w_ref, g_ref, res_ref, o_ref, acc_ref, *, nk):
    kk = pl.program_id(1)
    part = jnp.dot(a_ref[...], w_ref[...], preferred_element_type=F32)
    if nk == 1:
        o_ref[...] = res_ref[...] + _rms_scale(part, g_ref[...])
        return

    @pl.when(kk == 0)
    def _():
        acc_ref[...] = part

    @pl.when(jnp.logical_and(kk > 0, kk < nk - 1))
    def _():
        acc_ref[...] += part

    @pl.when(kk == nk - 1)
    def _():
        o_ref[...] = res_ref[...] + _rms_scale(acc_ref[...] + part, g_ref[...])


def mm_post(a, w, layer, g, res, *, tm, tk):
    m, k = a.shape
    n = w.shape[2]
    tm = min(tm, m)
    tk = k if k <= tk else k // 2
    assert m % tm == 0 and k % tk == 0 and tk % LANES == 0
    return pl.pallas_call(
        functools.partial(_mm_post_kernel, nk=k // tk),
        grid=(m // tm, k // tk),
        in_specs=[
            pl.BlockSpec((tm, tk), lambda i, kk: (i, kk)),
            pl.BlockSpec((None, tk, n), lambda i, kk: (layer, kk, 0)),
            pl.BlockSpec((1, n), lambda i, kk: (0, 0)),
            pl.BlockSpec((tm, n), lambda i, kk: (i, 0)),
        ],
        out_specs=pl.BlockSpec((tm, n), lambda i, kk: (i, 0)),
        out_shape=jax.ShapeDtypeStruct((m, n), F32),
        scratch_shapes=[pltpu.VMEM((tm, n), F32)],
        compiler_params=_params("parallel", "arbitrary"),
        name="mm_post",
    )(a, w, g.reshape(1, n), res)


def _branch_merge_kernel(yp_ref, yd_ref, yg_ref, wp_ref, wd_ref, wg_ref, g0_ref, g1_ref, g2_ref, o_ref):
    bp = jnp.dot(yp_ref[...], wp_ref[...], preferred_element_type=F32)
    bd = jnp.dot(yd_ref[...], wd_ref[...], preferred_element_type=F32)
    bg = jnp.dot(yg_ref[...], wg_ref[...], preferred_element_type=F32)
    merged = _sigmoid(g0_ref[...]) * bp + _sigmoid(g1_ref[...]) * bd + _sigmoid(g2_ref[...]) * bg
    o_ref[...] = merged.astype(o_ref.dtype)


def branch_merge(y_pool, y_dil, y_gla, w_pool, w_dil, w_gla, layer, z_main, *, tm, tn):
    m = y_pool.shape[0]
    tm = min(tm, m)
    assert m % tm == 0 and D_MODEL % tn == 0
    gate_blk = [(N_MAIN + b * D_MODEL) // tn for b in range(3)]

    def gate_spec(b):
        return pl.BlockSpec((tm, tn), lambda i, j: (i, gate_blk[b] + j))

    return pl.pallas_call(
        _branch_merge_kernel,
        grid=(m // tm, D_MODEL // tn),
        in_specs=[
            pl.BlockSpec((tm, POOL_WIDTH), lambda i, j: (i, 0)),
            pl.BlockSpec((tm, DIL_WIDTH), lambda i, j: (i, 0)),
            pl.BlockSpec((tm, GLA_VAL_WIDTH), lambda i, j: (i, 0)),
            pl.BlockSpec((None, POOL_WIDTH, tn), lambda i, j: (layer, 0, j)),
            pl.BlockSpec((None, DIL_WIDTH, tn), lambda i, j: (layer, 0, j)),
            pl.BlockSpec((None, GLA_VAL_WIDTH, tn), lambda i, j: (layer, 0, j)),
            gate_spec(0),
            gate_spec(1),
            gate_spec(2),
        ],
        out_specs=pl.BlockSpec((tm, tn), lambda i, j: (i, j)),
        out_shape=jax.ShapeDtypeStruct((m, D_MODEL), BF16),
        compiler_params=_params("parallel", "arbitrary"),
        name="branch_merge",
    )(y_pool, y_dil, y_gla, w_pool, w_dil, w_gla, z_main, z_main, z_main)


def _pool_prompt_kernel(u_ref, w_ref, s_ref, o_ref, buf_a, buf_b, *, seq):
    pad = POOL_HIST + 1
    zeros = jnp.zeros((pad, POOL_GROUP), F32)
    t = lax.broadcasted_iota(jnp.int32, (seq, 1), 0)
    for g, win in enumerate(POOL_WINDOWS):
        cols = slice(g * POOL_GROUP, (g + 1) * POOL_GROUP)
        u = u_ref[:, cols]
        cur, nxt = buf_a, buf_b
        cur[0:pad, :] = zeros
        nxt[0:pad, :] = zeros
        cur[pad:pad + seq, :] = u
        k = 1
        while k < win:
            nxt[pad:pad + seq, :] = cur[pad:pad + seq, :] + cur[pad - k:pad - k + seq, :]
            cur, nxt = nxt, cur
            k *= 2
        cnt = jnp.minimum(win, t + 1).astype(F32)
        d = cur[pad:pad + seq, :] / cnt - u
        y = jnp.dot(d.astype(BF16), w_ref[g], preferred_element_type=F32) * s_ref[:, cols]
        o_ref[:, cols] = y.astype(o_ref.dtype)


def pool_prompt(z3, pool_w, layer, pool_scale):
    b, seq, _ = z3.shape
    return pl.pallas_call(
        functools.partial(_pool_prompt_kernel, seq=seq),
        grid=(b,),
        in_specs=[
            pl.BlockSpec((None, seq, POOL_WIDTH), lambda i: (i, 0, COL_POOL // POOL_WIDTH)),
            pl.BlockSpec((None, len(POOL_WINDOWS), POOL_GROUP, POOL_GROUP), lambda i: (layer, 0, 0, 0)),
            pl.BlockSpec((1, POOL_WIDTH), lambda i: (0, 0)),
        ],
        out_specs=pl.BlockSpec((None, seq, POOL_WIDTH), lambda i: (i, 0, 0)),
        out_shape=jax.ShapeDtypeStruct((b, seq, POOL_WIDTH), BF16),
        scratch_shapes=[pltpu.VMEM((seq + POOL_HIST + 1, POOL_GROUP), F32)] * 2,
        compiler_params=_params("parallel"),
        name="pool_prompt",
    )(z3, pool_w, pool_scale.reshape(1, POOL_WIDTH))


def _pool_step_kernel(hist_ref, u_ref, w_ref, s_ref, y_ref, new_ref):
    u = u_ref[...]
    for r in range(POOL_HIST - 1):
        new_ref[r] = hist_ref[r + 1]
    new_ref[POOL_HIST - 1] = u
    for g, win in enumerate(POOL_WINDOWS):
        cols = slice(g * POOL_GROUP, (g + 1) * POOL_GROUP)
        ug = u[:, cols]
        acc = ug
        for r in range(POOL_HIST - (win - 1), POOL_HIST):
            acc = acc + hist_ref[r, :, cols]
        d = acc / float(win) - ug
        y = jnp.dot(d.astype(BF16), w_ref[g], preferred_element_type=F32) * s_ref[:, cols]
        y_ref[:, cols] = y.astype(y_ref.dtype)


def pool_step(hist_t, z_s, pool_w, layer, pool_scale):
    bd = z_s.shape[0]
    return pl.pallas_call(
        _pool_step_kernel,
        grid=(1,),
        in_specs=[
            pl.BlockSpec((POOL_HIST, bd, POOL_WIDTH), lambda i: (0, 0, 0)),
            pl.BlockSpec((bd, POOL_WIDTH), lambda i: (0, COL_POOL // POOL_WIDTH)),
            pl.BlockSpec((None, len(POOL_WINDOWS), POOL_GROUP, POOL_GROUP), lambda i: (layer, 0, 0, 0)),
            pl.BlockSpec((1, POOL_WIDTH), lambda i: (0, 0)),
        ],
        out_specs=[
            pl.BlockSpec((bd, POOL_WIDTH), lambda i: (0, 0)),
            pl.BlockSpec((POOL_HIST, bd, POOL_WIDTH), lambda i: (0, 0, 0)),
        ],
        out_shape=[
            jax.ShapeDtypeStruct((bd, POOL_WIDTH), BF16),
            jax.ShapeDtypeStruct((POOL_HIST, bd, POOL_WIDTH), F32),
        ],
        compiler_params=_params("arbitrary"),
        name="pool_step",
    )(hist_t, z_s, pool_w, pool_scale.reshape(1, POOL_WIDTH))


def _rel_bucket(dist):
    max_exact = REL_BUCKETS // 2
    scaled = jnp.log(jnp.maximum(dist, 1).astype(F32) / max_exact) / math.log(REL_MAX_DIST / max_exact)
    large = jnp.minimum(max_exact + (scaled * (REL_BUCKETS - max_exact)).astype(jnp.int32), REL_BUCKETS - 1)
    return jnp.where(dist < max_exact, dist, large)


def _select_rows(table_t, index, n):
    onehot = (index.reshape(1, -1) == jnp.arange(n, dtype=jnp.int32)[:, None]).astype(F32)
    out = jnp.dot(table_t, onehot, precision=lax.Precision.HIGHEST, preferred_element_type=F32)
    return out.reshape((table_t.shape[0],) + index.shape)


def _slot_biases(rel_bias):
    out = []
    for g, (win, dil) in enumerate(DIL_CONFIGS):
        dist = jnp.arange(win // dil + 1, dtype=jnp.int32) * dil
        table_t = rel_bias[:, g * DIL_HEADS:(g + 1) * DIL_HEADS].T.astype(F32)
        out.append(_select_rows(table_t, _rel_bucket(dist), REL_BUCKETS))
    return out


def _band_bias(slot_bias):
    qi = jnp.arange(DIL_BLOCK, dtype=jnp.int32)[:, None] + DIL_BLOCK
    ki = jnp.arange(2 * DIL_BLOCK, dtype=jnp.int32)[None, :]
    rel = qi - ki
    out = []
    for g, (win, dil) in enumerate(DIL_CONFIGS):
        n_slots = win // dil
        ok = (rel >= 0) & (rel <= n_slots)
        b = _select_rows(slot_bias[g], jnp.clip(rel, 0, n_slots), n_slots + 1)
        out.append(jnp.where(ok[None], b, NEG_INF))
    return jnp.stack(out, axis=0)


def _dil_prompt_kernel(q0, k0, v0, q1, k1, v1, q2, k2, v2, bias_ref, o_ref, o_scr, lse_scr, *, seq):
    qs, ks, vs = (q0, q1, q2), (k0, k1, k2), (v0, v1, v2)
    scale = DIL_HEAD_DIM ** -0.5
    blk = DIL_BLOCK
    nt = (((1,), (1,)), ((), ()))

    def rows(ref, start, dil):
        if dil == 1:
            return ref[pl.ds(start, blk), :]
        return ref[pl.ds(start, blk, stride=dil), :]

    for g, (_, dil) in enumerate(DIL_CONFIGS):
        nb = seq // dil // blk
        for r in range(dil):
            for ub in range(nb):
                start = r + dil * ub * blk
                q = rows(qs[g], start, dil).astype(BF16)
                kc = rows(ks[g], start, dil).astype(BF16)
                vc = rows(vs[g], start, dil).astype(BF16)
                if ub == 0:
                    kk, vv, bias = kc, vc, bias_ref[g, :, blk:]
                else:
                    prev = start - dil * blk
                    kk = jnp.concatenate([rows(ks[g], prev, dil).astype(BF16), kc], axis=0)
                    vv = jnp.concatenate([rows(vs[g], prev, dil).astype(BF16), vc], axis=0)
                    bias = bias_ref[g]
                s = lax.dot_general(q, kk, nt, preferred_element_type=F32) * scale + bias
                m = jnp.max(s, axis=-1, keepdims=True)
                e = jnp.exp(s - m)
                l = jnp.sum(e, axis=-1, keepdims=True)
                o = jnp.dot(e.astype(BF16), vv, preferred_element_type=F32) / l
                lse = jnp.broadcast_to(m + jnp.log(l), (blk, LANES))
                if dil == 1:
                    o_scr[g, pl.ds(start, blk), :] = o
                    lse_scr[g, pl.ds(start, blk), :] = lse
                else:
                    o_scr[g, pl.ds(start, blk, stride=dil), :] = o
                    lse_scr[g, pl.ds(start, blk, stride=dil), :] = lse

    chunk = 256

    def combine(c, carry):
        sl = pl.ds(pl.multiple_of(c * chunk, chunk), chunk)
        l0, l1, l2 = lse_scr[0, sl, :], lse_scr[1, sl, :], lse_scr[2, sl, :]
        mx = jnp.maximum(jnp.maximum(l0, l1), l2)
        w0, w1, w2 = jnp.exp(l0 - mx), jnp.exp(l1 - mx), jnp.exp(l2 - mx)
        y = (w0 * o_scr[0, sl, :] + w1 * o_scr[1, sl, :] + w2 * o_scr[2, sl, :]) / (w0 + w1 + w2)
        o_ref[sl, :] = y.astype(o_ref.dtype)
        return carry

    lax.fori_loop(0, seq // chunk, combine, 0)


def dil_prompt(z3, band_bias):
    b, seq, _ = z3.shape
    assert seq % (DIL_BLOCK * 16) == 0

    def col_spec(col):
        blk0 = col // DIL_HEAD_DIM
        return pl.BlockSpec((None, seq, DIL_HEAD_DIM), lambda i, h: (i, 0, blk0 + h))

    in_specs = []
    for g in range(3):
        in_specs += [col_spec(COL_DIL_Q[g]), col_spec(COL_DIL_K[g]), col_spec(COL_DIL_V[g])]
    in_specs.append(pl.BlockSpec((3, None, DIL_BLOCK, 2 * DIL_BLOCK), lambda i, h: (0, h, 0, 0)))
    return pl.pallas_call(
        functools.partial(_dil_prompt_kernel, seq=seq),
        grid=(b, DIL_HEADS),
        in_specs=in_specs,
        out_specs=pl.BlockSpec((None, seq, DIL_HEAD_DIM), lambda i, h: (i, 0, h)),
        out_shape=jax.ShapeDtypeStruct((b, seq, DIL_WIDTH), BF16),
        scratch_shapes=[pltpu.VMEM((3, seq, DIL_HEAD_DIM), F32), pltpu.VMEM((3, seq, LANES), F32)],
        compiler_params=_params("parallel", "arbitrary"),
        name="dil_prompt",
    )(*([z3] * 9), band_bias)


def _dil_step_kernel(qkv_ref, c0, c1, c2, bias_ref, bias0_ref, o_ref):
    caches = (c0, c1, c2)
    scale = DIL_HEAD_DIM ** -0.5
    outs, lses = [], []
    for g in range(3):
        q, kn, vn = qkv_ref[3 * g], qkv_ref[3 * g + 1], qkv_ref[3 * g + 2]
        kc = caches[g][:, 0]
        vc = caches[g][:, 1]
        s = jnp.sum(kc * q[None], axis=-1, keepdims=True) * scale + bias_ref[g]
        s_new = jnp.sum(q * kn, axis=-1, keepdims=True) * scale + bias0_ref[g]
        m = jnp.maximum(jnp.max(s, axis=0), s_new)
        p = jnp.exp(s - m[None])
        p_new = jnp.exp(s_new - m)
        l = jnp.sum(p, axis=0) + p_new
        outs.append((jnp.sum(p * vc, axis=0) + p_new * vn) / l)
        lses.append(m + jnp.log(l))
    mx = jnp.maximum(jnp.maximum(lses[0], lses[1]), lses[2])
    w0, w1, w2 = jnp.exp(lses[0] - mx), jnp.exp(lses[1] - mx), jnp.exp(lses[2] - mx)
    o_ref[...] = (w0 * outs[0] + w1 * outs[1] + w2 * outs[2]) / (w0 + w1 + w2)


def dil_step(z_s, caches, layer, slot_bias):
    bd = z_s.shape[0]
    n_slots = DIL_CONFIGS[0][0] // DIL_CONFIGS[0][1]
    qkv = z_s[:, COL_DIL_Q[0]:COL_DIL_V[2] + DIL_WIDTH].reshape(bd, 9, DIL_HEADS, DIL_HEAD_DIM)
    in_specs = [pl.BlockSpec((None, 9, DIL_HEADS, DIL_HEAD_DIM), lambda i: (i, 0, 0, 0))]
    cache_views = []
    for g, (win, dil) in enumerate(DIL_CONFIGS):
        depth = caches[g].shape[0]
        assert caches[g].shape[2] == win and win // dil == n_slots
        cache_views.append(caches[g].reshape(depth, bd, n_slots, dil, 2, DIL_HEADS, DIL_HEAD_DIM))
        in_specs.append(pl.BlockSpec((None, None, n_slots, None, 2, DIL_HEADS, DIL_HEAD_DIM),
                                     lambda i: (layer, i, 0, 0, 0, 0, 0)))
    bias_rows = jnp.stack([sb[:, n_slots:0:-1].T for sb in slot_bias], axis=0)
    bias_rows = jnp.broadcast_to(bias_rows[..., None], (3, n_slots, DIL_HEADS, LANES))
    bias_new = jnp.broadcast_to(jnp.stack([sb[:, 0] for sb in slot_bias], axis=0)[..., None], (3, DIL_HEADS, LANES))
    in_specs.append(pl.BlockSpec((3, n_slots, DIL_HEADS, LANES), lambda i: (0, 0, 0, 0)))
    in_specs.append(pl.BlockSpec((3, DIL_HEADS, LANES), lambda i: (0, 0, 0)))
    out = pl.pallas_call(
        _dil_step_kernel,
        grid=(bd,),
        in_specs=in_specs,
        out_specs=pl.BlockSpec((None, DIL_HEADS, DIL_HEAD_DIM), lambda i: (i, 0, 0)),
        out_shape=jax.ShapeDtypeStruct((bd, DIL_HEADS, DIL_HEAD_DIM), F32),
        compiler_params=_params("parallel"),
        name="dil_step",
    )(qkv, *cache_views, bias_rows, bias_new)
    return out.reshape(bd, DIL_WIDTH)


GLA_HEADS_PER_STEP = 2
GLA_CUMSUM_ROWS = 256


def _gla_prompt_kernel(q_ref, k_ref, v_ref, r_ref, ga_ref, wa_ref, ba_ref, gn_ref, y_ref, s_out_ref, st_ref, o_scr, b_scr, *, tq):
    t = pl.program_id(2)
    ck = GLA_CHUNK

    @pl.when(t == 0)
    def _():
        st_ref[...] = jnp.zeros_like(st_ref)

    la = jnp.dot(ga_ref[...].astype(BF16), wa_ref[...], preferred_element_type=F32) + ba_ref[...]
    log_a = _log_sigmoid(la) / GLA_TAU
    row = lax.broadcasted_iota(jnp.int32, (ck, ck), 0)
    col = lax.broadcasted_iota(jnp.int32, (ck, ck), 1)
    tril = row >= col
    rows_c = GLA_CUMSUM_ROWS
    rr = lax.broadcasted_iota(jnp.int32, (rows_c, rows_c), 0)
    cc = lax.broadcasted_iota(jnp.int32, (rows_c, rows_c), 1)
    prefix = jnp.logical_and(rr >= cc, rr // ck == cc // ck).astype(BF16)
    for r0 in range(0, tq, rows_c):
        la_blk = log_a[r0:r0 + rows_c, :]
        hi = la_blk.astype(BF16)
        lo = (la_blk - hi.astype(F32)).astype(BF16)
        b_scr[r0:r0 + rows_c, :] = (jnp.dot(prefix, hi, preferred_element_type=F32)
                                    + jnp.dot(prefix, lo, preferred_element_type=F32))
    nt = (((1,), (1,)), ((), ()))
    tn = (((0,), (0,)), ((), ()))
    for c in range(tq // ck):
        sl = slice(c * ck, (c + 1) * ck)
        for hh in range(GLA_HEADS_PER_STEP):
            ks = slice(hh * GLA_DK, (hh + 1) * GLA_DK)
            vs = slice(hh * GLA_DV, (hh + 1) * GLA_DV)
            b = b_scr[sl, ks]
            b_last = b[ck - 1:ck, :]
            q = q_ref[sl, ks] * (GLA_DK ** -0.5)
            k = k_ref[sl, ks]
            v = v_ref[sl, vs].astype(BF16)
            q_t = (q * jnp.exp(b)).astype(BF16)
            k_t = (k * jnp.exp(-b)).astype(BF16)
            k_h = (k * jnp.exp(b_last - b)).astype(BF16)
            decay = jnp.exp(b_last)
            a = lax.dot_general(q_t, k_t, nt, preferred_element_type=F32)
            a = jnp.where(tril, a, 0.0)
            st = st_ref[hh]
            o = jnp.dot(a.astype(BF16), v, preferred_element_type=F32)
            o = o + lax.dot_general(q_t, st.astype(BF16), nt, preferred_element_type=F32)
            o_scr[sl, vs] = o
            st_ref[hh] = decay * st + lax.dot_general(v, k_h, tn, preferred_element_type=F32)

    for hh in range(GLA_HEADS_PER_STEP):
        vs = slice(hh * GLA_DV, (hh + 1) * GLA_DV)
        o = _rms_scale(o_scr[:, vs], gn_ref[:, vs])
        r = r_ref[:, vs]
        y_ref[:, vs] = (o * (r * _sigmoid(r))).astype(y_ref.dtype)

    @pl.when(t == pl.num_programs(2) - 1)
    def _():
        for hh in range(GLA_HEADS_PER_STEP):
            s_out_ref[hh] = st_ref[hh].T


def gla_prompt(z3, za3, w_a2p, b_a, gla_norm, *, tq=1024):
    b, seq, _ = z3.shape
    hp = GLA_HEADS_PER_STEP
    kw, vw = hp * GLA_DK, hp * GLA_DV
    assert seq % tq == 0 and tq % GLA_CHUNK == 0 and GLA_HEADS % hp == 0
    return pl.pallas_call(
        functools.partial(_gla_prompt_kernel, tq=tq),
        grid=(b, GLA_HEADS // hp, seq // tq),
        in_specs=[
            pl.BlockSpec((None, tq, kw), lambda i, h, t: (i, t, COL_GQ // kw + h)),
            pl.BlockSpec((None, tq, kw), lambda i, h, t: (i, t, COL_GK // kw + h)),
            pl.BlockSpec((None, tq, vw), lambda i, h, t: (i, t, COL_GV // vw + h)),
            pl.BlockSpec((None, tq, vw), lambda i, h, t: (i, t, COL_GR // vw + h)),
            pl.BlockSpec((None, tq, LANES), lambda i, h, t: (i, t, 0)),
            pl.BlockSpec((LANES, kw), lambda i, h, t: (0, h)),
            pl.BlockSpec((1, kw), lambda i, h, t: (0, h)),
            pl.BlockSpec((1, vw), lambda i, h, t: (0, h)),
        ],
        out_specs=[
            pl.BlockSpec((None, tq, vw), lambda i, h, t: (i, t, h)),
            pl.BlockSpec((None, hp, GLA_DK, GLA_DV), lambda i, h, t: (i, h, 0, 0)),
        ],
        out_shape=[
            jax.ShapeDtypeStruct((b, seq, GLA_VAL_WIDTH), BF16),
            jax.ShapeDtypeStruct((b, GLA_HEADS, GLA_DK, GLA_DV), F32),
        ],
        scratch_shapes=[
            pltpu.VMEM((hp, GLA_DV, GLA_DK), F32),
            pltpu.VMEM((tq, vw), F32),
            pltpu.VMEM((tq, kw), F32),
        ],
        compiler_params=_params("parallel", "parallel", "arbitrary"),
        name="gla_prompt",
    )(z3, z3, z3, z3, za3, w_a2p, b_a.reshape(1, GLA_KEY_WIDTH), gla_norm.reshape(1, GLA_VAL_WIDTH))


def _gla_step_kernel(q_ref, k_ref, v_ref, r_ref, ga_ref, wa_ref, ba_ref, gn_ref, s_ref, y_ref, s_out_ref):
    ga8 = jnp.broadcast_to(ga_ref[...], (8, LANES)).astype(BF16)
    la = jnp.dot(ga8, wa_ref[...], preferred_element_type=F32)[0:1, :] + ba_ref[...]
    ea = jnp.exp(_log_sigmoid(la) / GLA_TAU)
    q = q_ref[...] * (GLA_DK ** -0.5)
    k = k_ref[...]
    eye = lax.broadcasted_iota(jnp.int32, (GLA_DK, GLA_DK), 0) == lax.broadcasted_iota(jnp.int32, (GLA_DK, GLA_DK), 1)

    def column(row):
        return jnp.sum(jnp.where(eye, jnp.broadcast_to(row, (GLA_DK, GLA_DK)), 0.0), axis=-1, keepdims=True)

    for h in range(GLA_HEADS):
        ks = slice(h * GLA_DK, (h + 1) * GLA_DK)
        vs = slice(h * GLA_DV, (h + 1) * GLA_DV)
        s_new = column(ea[:, ks]) * s_ref[h] + column(k[:, ks]) * v_ref[:, vs]
        s_out_ref[h] = s_new
        o = jnp.sum(column(q[:, ks]) * s_new, axis=0, keepdims=True)
        o = _rms_scale(o, gn_ref[:, vs])
        r = r_ref[:, vs]
        y_ref[:, vs] = o * (r * _sigmoid(r))


def gla_step(z_s, za_s, state, layer, w_a2p, b_a, gla_norm):
    bd = z_s.shape[0]
    zs3 = z_s.reshape(bd, 1, N_PROJ)
    y, s_new = pl.pallas_call(
        _gla_step_kernel,
        grid=(bd,),
        in_specs=[
            pl.BlockSpec((None, 1, GLA_KEY_WIDTH), lambda i: (i, 0, COL_GQ // GLA_KEY_WIDTH)),
            pl.BlockSpec((None, 1, GLA_KEY_WIDTH), lambda i: (i, 0, COL_GK // GLA_KEY_WIDTH)),
            pl.BlockSpec((None, 1, GLA_VAL_WIDTH), lambda i: (i, 0, COL_GV // GLA_VAL_WIDTH)),
            pl.BlockSpec((None, 1, GLA_VAL_WIDTH), lambda i: (i, 0, COL_GR // GLA_VAL_WIDTH)),
            pl.BlockSpec((None, 1, LANES), lambda i: (i, 0, 0)),
            pl.BlockSpec((LANES, GLA_KEY_WIDTH), lambda i: (0, 0)),
            pl.BlockSpec((1, GLA_KEY_WIDTH), lambda i: (0, 0)),
            pl.BlockSpec((1, GLA_VAL_WIDTH), lambda i: (0, 0)),
            pl.BlockSpec((None, None, GLA_HEADS, GLA_DK, GLA_DV), lambda i: (layer, i, 0, 0, 0)),
        ],
        out_specs=[
            pl.BlockSpec((None, 1, GLA_VAL_WIDTH), lambda i: (i, 0, 0)),
            pl.BlockSpec((None, GLA_HEADS, GLA_DK, GLA_DV), lambda i: (i, 0, 0, 0)),
        ],
        out_shape=[
            jax.ShapeDtypeStruct((bd, 1, GLA_VAL_WIDTH), F32),
            jax.ShapeDtypeStruct((bd, GLA_HEADS, GLA_DK, GLA_DV), F32),
        ],
        compiler_params=_params("parallel"),
        name="gla_step",
    )(zs3, zs3, zs3, zs3, za_s.reshape(bd, 1, LANES), w_a2p, b_a.reshape(1, GLA_KEY_WIDTH),
      gla_norm.reshape(1, GLA_VAL_WIDTH), state)
    return y.reshape(bd, GLA_VAL_WIDTH), s_new


def _xattn_kernel(q_ref, kv_ref, o_ref):
    scale = X_HEAD_DIM ** -0.5
    nt = (((1,), (1,)), ((), ()))
    for h in range(X_HEADS):
        hs = slice(h * X_HEAD_DIM, (h + 1) * X_HEAD_DIM)
        k = kv_ref[:, h * X_HEAD_DIM:(h + 1) * X_HEAD_DIM].astype(BF16)
        v = kv_ref[:, X_WIDTH + h * X_HEAD_DIM:X_WIDTH + (h + 1) * X_HEAD_DIM].astype(BF16)
        s = lax.dot_general(q_ref[:, hs], k, nt, preferred_element_type=F32) * scale
        e = jnp.exp(s - jnp.max(s, axis=-1, keepdims=True))
        l = jnp.sum(e, axis=-1, keepdims=True)
        o = jnp.dot(e.astype(BF16), v, preferred_element_type=F32) / l
        o_ref[:, hs] = o.astype(o_ref.dtype)


def xattn(q3, mem_kv, *, tt):
    b, t, _ = q3.shape
    tt = min(tt, t)
    assert t % tt == 0
    return pl.pallas_call(
        _xattn_kernel,
        grid=(b, t // tt),
        in_specs=[
            pl.BlockSpec((None, tt, X_WIDTH), lambda i, j: (i, j, 0)),
            pl.BlockSpec((None, MEM_LEN, 2 * X_WIDTH), lambda i, j: (i, 0, 0)),
        ],
        out_specs=pl.BlockSpec((None, tt, X_WIDTH), lambda i, j: (i, j, 0)),
        out_shape=jax.ShapeDtypeStruct((b, t, X_WIDTH), BF16),
        compiler_params=_params("parallel", "arbitrary"),
        name="xattn",
    )(q3, mem_kv)


def _xattn_step_kernel(q_ref, kv_ref, o_ref):
    q = q_ref[...]
    k = kv_ref[:, 0]
    v = kv_ref[:, 1]
    s = jnp.sum(k * q[None], axis=-1, keepdims=True) * (X_HEAD_DIM ** -0.5)
    p = jnp.exp(s - jnp.max(s, axis=0)[None])
    o_ref[...] = jnp.sum(p * v, axis=0) / jnp.sum(p, axis=0)


def xattn_step(q, mem_kv, layer):
    bd = q.shape[0]
    out = pl.pallas_call(
        _xattn_step_kernel,
        grid=(bd,),
        in_specs=[
            pl.BlockSpec((None, X_HEADS, X_HEAD_DIM), lambda i: (i, 0, 0)),
            pl.BlockSpec((None, None, MEM_LEN, 2, X_HEADS, X_HEAD_DIM), lambda i: (layer, i, 0, 0, 0, 0)),
        ],
        out_specs=pl.BlockSpec((None, X_HEADS, X_HEAD_DIM), lambda i: (i, 0, 0)),
        out_shape=jax.ShapeDtypeStruct((bd, X_HEADS, X_HEAD_DIM), F32),
        compiler_params=_params("parallel"),
        name="xattn_step",
    )(q.reshape(bd, X_HEADS, X_HEAD_DIM), mem_kv)
    return out.reshape(bd, X_WIDTH)


FFN_HALO = 16
FFN_TAIL = 8


def _ffn_up_prompt_kernel(x_ref, xh_ref, g_ref, wg_ref, wv_ref, cwg_ref, cwv_ref, cbg_ref, cbv_ref,
                          act_ref, tg_ref, tv_ref, xn_ref, ug_ref, uv_ref, *, tm, seq_tiles, chunk):
    i = pl.program_id(0)

    @pl.when(pl.program_id(1) == 0)
    def _():
        _norm_rows(x_ref, g_ref, xn_ref, tm, chunk, dst_offset=FFN_HALO)
        halo = _rms_scale(xh_ref[...], g_ref[...])
        halo = jnp.where(i % seq_tiles == 0, 0.0, halo)
        xn_ref[0:FFN_HALO, :] = halo.astype(BF16)

    xn = xn_ref[...]
    ug_ref[...] = jnp.dot(xn, wg_ref[...], preferred_element_type=F32)
    uv_ref[...] = jnp.dot(xn, wv_ref[...], preferred_element_type=F32)

    def conv(u_ref, cw_ref, cb_ref):
        lead = FFN_TAIL
        u = u_ref[FFN_HALO - lead:FFN_HALO + tm, :]
        c = cb_ref[...] + cw_ref[0:1, :] * pltpu.roll(u, 2, 0) + cw_ref[1:2, :] * pltpu.roll(u, 1, 0) + cw_ref[2:3, :] * u
        return c[lead:lead + tm]

    act_ref[...] = (_gelu_tanh(conv(ug_ref, cwg_ref, cbg_ref)) * conv(uv_ref, cwv_ref, cbv_ref)).astype(act_ref.dtype)
    tg_ref[...] = ug_ref[FFN_HALO + tm - FFN_TAIL:FFN_HALO + tm, :]
    tv_ref[...] = uv_ref[FFN_HALO + tm - FFN_TAIL:FFN_HALO + tm, :]


def ffn_up_prompt(x, g, w_up, conv_w, conv_b3, layer, *, seq, tm, tn):
    m, k = x.shape
    assert seq % tm == 0 and D_FF % tn == 0 and tm % FFN_HALO == 0
    nj = D_FF // tn
    seq_tiles = seq // tm
    chunk = min(256, tm)
    halo_blocks = tm // FFN_HALO
    act, tail_g, tail_v = pl.pallas_call(
        functools.partial(_ffn_up_prompt_kernel, tm=tm, seq_tiles=seq_tiles, chunk=chunk),
        grid=(m // tm, nj),
        in_specs=[
            pl.BlockSpec((tm, k), lambda i, j: (i, 0)),
            pl.BlockSpec((FFN_HALO, k), lambda i, j: (jnp.maximum(i * halo_blocks - 1, 0), 0)),
            pl.BlockSpec((1, k), lambda i, j: (0, 0)),
            pl.BlockSpec((None, k, tn), lambda i, j: (layer, 0, j)),
            pl.BlockSpec((None, k, tn), lambda i, j: (layer, 0, j + nj)),
            pl.BlockSpec((None, 3, tn), lambda i, j: (layer, 0, j)),
            pl.BlockSpec((None, 3, tn), lambda i, j: (layer, 0, j + nj)),
            pl.BlockSpec((None, 1, tn), lambda i, j: (layer, 0, j)),
            pl.BlockSpec((None, 1, tn), lambda i, j: (layer, 0, j + nj)),
        ],
        out_specs=[
            pl.BlockSpec((tm, tn), lambda i, j: (i, j)),
            pl.BlockSpec((None, FFN_TAIL, tn), lambda i, j: (i, 0, j)),
            pl.BlockSpec((None, FFN_TAIL, tn), lambda i, j: (i, 0, j)),
        ],
        out_shape=[
            jax.ShapeDtypeStruct((m, D_FF), BF16),
            jax.ShapeDtypeStruct((m // tm, FFN_TAIL, D_FF), F32),
            jax.ShapeDtypeStruct((m // tm, FFN_TAIL, D_FF), F32),
        ],
        scratch_shapes=[
            pltpu.VMEM((tm + FFN_HALO, k), BF16),
            pltpu.VMEM((tm + FFN_HALO, tn), F32),
            pltpu.VMEM((tm + FFN_HALO, tn), F32),
        ],
        compiler_params=_params("parallel", "arbitrary"),
        name="ffn_up_prompt",
    )(x, x, g.reshape(1, k), w_up, w_up, conv_w, conv_w, conv_b3, conv_b3)
    return act, jnp.concatenate([tail_g, tail_v], axis=-1)


def _ffn_up_step_kernel(x_ref, g_ref, wg_ref, wv_ref, cwg_ref, cwv_ref, cbg_ref, cbv_ref, hg_ref, hv_ref,
                        act_ref, ng_ref, nv_ref, xn_ref, *, bd):
    @pl.when(pl.program_id(0) == 0)
    def _():
        _norm_rows(x_ref, g_ref, xn_ref, bd, bd)

    xn = xn_ref[...]
    ug = jnp.dot(xn, wg_ref[...], preferred_element_type=F32)
    uv = jnp.dot(xn, wv_ref[...], preferred_element_type=F32)

    def conv(u, h_ref, cw_ref, cb_ref):
        return cb_ref[...] + cw_ref[0:1, :] * h_ref[0] + cw_ref[1:2, :] * h_ref[1] + cw_ref[2:3, :] * u

    act_ref[...] = (_gelu_tanh(conv(ug, hg_ref, cwg_ref, cbg_ref)) * conv(uv, hv_ref, cwv_ref, cbv_ref)).astype(act_ref.dtype)
    ng_ref[0] = hg_ref[1]
    ng_ref[1] = ug
    nv_ref[0] = hv_ref[1]
    nv_ref[1] = uv


def ffn_up_step(x, g, w_up, conv_w, conv_b3, layer, hist_t, *, tn):
    bd, k = x.shape
    nj = D_FF // tn
    act, new_g, new_v = pl.pallas_call(
        functools.partial(_ffn_up_step_kernel, bd=bd),
        grid=(nj,),
        in_specs=[
            pl.BlockSpec((bd, k), lambda j: (0, 0)),
            pl.BlockSpec((1, k), lambda j: (0, 0)),
            pl.BlockSpec((None, k, tn), lambda j: (layer, 0, j)),
            pl.BlockSpec((None, k, tn), lambda j: (layer, 0, j + nj)),
            pl.BlockSpec((None, 3, tn), lambda j: (layer, 0, j)),
            pl.BlockSpec((None, 3, tn), lambda j: (layer, 0, j + nj)),
            pl.BlockSpec((None, 1, tn), lambda j: (layer, 0, j)),
            pl.BlockSpec((None, 1, tn), lambda j: (layer, 0, j + nj)),
            pl.BlockSpec((2, bd, tn), lambda j: (0, 0, j)),
            pl.BlockSpec((2, bd, tn), lambda j: (0, 0, j + nj)),
        ],
        out_specs=[
            pl.BlockSpec((bd, tn), lambda j: (0, j)),
            pl.BlockSpec((2, bd, tn), lambda j: (0, 0, j)),
            pl.BlockSpec((2, bd, tn), lambda j: (0, 0, j)),
        ],
        out_shape=[
            jax.ShapeDtypeStruct((bd, D_FF), BF16),
            jax.ShapeDtypeStruct((2, bd, D_FF), F32),
            jax.ShapeDtypeStruct((2, bd, D_FF), F32),
        ],
        scratch_shapes=[pltpu.VMEM((bd, k), BF16)],
        compiler_params=_params("arbitrary"),
        name="ffn_up_step",
    )(x, g.reshape(1, k), w_up, w_up, conv_w, conv_w, conv_b3, conv_b3, hist_t, hist_t)
    return act, jnp.concatenate([new_g, new_v], axis=-1)


TM = 1024
TN = 1024
TM_POST = 512
TK_POST = 2048
TN_MERGE = 512
TN_FFN = 512
XATTN_ROWS = 2048


def _bf16_weights(w_in, pool_w, gla_w_a2, w_br_pool, w_br_dil, w_br_gla, w_mix_out, w_xq, w_xkv, w_xo, w_up, w_down):
    w_in_t, w_ga_t = prep_w_in(jnp.swapaxes(w_in, 1, 2), tn=TN)
    return dict(
        w_in_t=w_in_t,
        w_ga_t=w_ga_t,
        w_a2p=jnp.pad(gla_w_a2, ((0, 0), (0, LANES - GLA_RANK), (0, 0))).astype(BF16),
        pool_w=pool_w.astype(BF16),
        w_br_pool=w_br_pool.astype(BF16),
        w_br_dil=w_br_dil.astype(BF16),
        w_br_gla=w_br_gla.astype(BF16),
        w_mix_out=w_mix_out.astype(BF16),
        w_xq=w_xq.astype(BF16),
        w_xkv=w_xkv.astype(BF16),
        w_xo=w_xo.astype(BF16),
        w_up=w_up.astype(BF16),
        w_down=w_down.astype(BF16),
    )


def kernel(x_prompt, x_sample, state_pool, cache_dil1_kv, cache_dil2_kv, cache_dil3_kv, state_gla, cache_mem_kv, state_ffn_conv, mem_prompt, rel_bias, norm_mix_pre, norm_mix_post, w_in, pool_w, pool_scale, gla_w_a2, gla_b_a, gla_norm, w_br_pool, w_br_dil, w_br_gla, w_mix_out, norm_x_pre, norm_x_post, norm_mem, w_xq, w_xkv, w_xo, norm_ffn_pre, norm_ffn_post, w_up, conv_w, conv_b, w_down):
    b, seq, d = x_prompt.shape
    bd = x_sample.shape[0]
    depth = w_in.shape[0]
    m = b * seq
    assert x_sample.shape[1] == 1 and d == D_MODEL and w_in.shape[2] == N_MAIN + GLA_RANK + N_GATES
    slot_bias = _slot_biases(rel_bias)
    band_bias = _band_bias(slot_bias)
    dil_caches = (cache_dil1_kv, cache_dil2_kv, cache_dil3_kv)

    xp = x_prompt.reshape(m, d)
    xs = x_sample.reshape(bd, d)
    pool_p, gla_p, mem_p, conv_p = [], [], [], []
    pool_s, gla_s, conv_s = [], [], []
    dil_p = [[] for _ in range(3)]
    dil_s = [[] for _ in range(3)]
    seq_tiles = seq // TM

    w = _bf16_weights(w_in, pool_w, gla_w_a2, w_br_pool, w_br_dil, w_br_gla, w_mix_out, w_xq, w_xkv, w_xo, w_up, w_down)
    conv_b3 = conv_b.reshape(depth, 1, 2 * D_FF)
    mem_rows = mem_prompt.reshape(b * MEM_LEN, d)
    for l in range(depth):
        w_a2p = w["w_a2p"][l]

        zm, zkv_wide, za, zs, zas = in_proj(xp, xs, norm_mix_pre[l], w["w_in_t"], w["w_ga_t"], l, tm=TM, tn=TN)
        z3 = zm.reshape(b, seq, N_PROJ)
        y_pool = pool_prompt(z3, w["pool_w"], l, pool_scale[l]).reshape(m, POOL_WIDTH)
        y_dil = dil_prompt(z3, band_bias).reshape(m, DIL_WIDTH)
        y_gla, gla_new = gla_prompt(z3, za.reshape(b, seq, LANES), w_a2p, gla_b_a[l], gla_norm[l])
        merged = branch_merge(y_pool, y_dil, y_gla.reshape(m, GLA_VAL_WIDTH), w["w_br_pool"], w["w_br_dil"], w["w_br_gla"], l,
                              zm, tm=TM, tn=TN_MERGE)
        xp = mm_post(merged, w["w_mix_out"], l, norm_mix_post[l], xp, tm=TM_POST, tk=TK_POST)
        mem_kv = norm_mm(mem_rows, norm_mem[l], w["w_xkv"], l, tm=TM, tn=TN)
        q = norm_mm(xp, norm_x_pre[l], w["w_xq"], l, tm=TM, tn=TN, out_dtype=BF16)
        o = xattn(q.reshape(b, seq, X_WIDTH), mem_kv.reshape(b, MEM_LEN, 2 * X_WIDTH), tt=XATTN_ROWS)
        xp = mm_post(o.reshape(m, X_WIDTH), w["w_xo"], l, norm_x_post[l], xp, tm=TM_POST, tk=TK_POST)
        act, tails = ffn_up_prompt(xp, norm_ffn_pre[l], w["w_up"], conv_w, conv_b3, l, seq=seq, tm=TM, tn=TN_FFN)
        xp = mm_post(act, w["w_down"], l, norm_ffn_post[l], xp, tm=TM_POST, tk=TK_POST)

        pool_p.append(z3[:, seq - POOL_HIST:, COL_POOL:COL_POOL + POOL_WIDTH])
        for g, (win, _) in enumerate(DIL_CONFIGS):
            keep = min(win, seq)
            if g == 2:
                kv = zkv_wide.reshape(b, seq, 2 * DIL_WIDTH)[:, seq - keep:]
            else:
                kv = z3[:, seq - keep:, COL_DIL_K[g]:COL_DIL_K[g] + 2 * DIL_WIDTH]
            dil_p[g].append(kv.reshape(b, keep, 2, DIL_HEADS, DIL_HEAD_DIM))
        gla_p.append(gla_new)
        mem_p.append(mem_kv.reshape(b, MEM_LEN, 2, X_HEADS, X_HEAD_DIM))
        conv_p.append(tails.reshape(b, seq_tiles, FFN_TAIL, 2 * D_FF)[:, seq_tiles - 1, FFN_TAIL - 2:, :])

        y_pool_s, pool_new_t = pool_step(jnp.swapaxes(state_pool[l], 0, 1), zs, w["pool_w"], l, pool_scale[l])
        y_dil_s = dil_step(zs, dil_caches, l, slot_bias).astype(BF16)
        y_gla_s, gla_new_s = gla_step(zs, zas, state_gla, l, w_a2p, gla_b_a[l], gla_norm[l])
        merged_s = branch_merge(y_pool_s, y_dil_s, y_gla_s.astype(BF16), w["w_br_pool"], w["w_br_dil"], w["w_br_gla"], l,
                                zs, tm=bd, tn=TN_MERGE)
        xs = mm_post(merged_s, w["w_mix_out"], l, norm_mix_post[l], xs, tm=bd, tk=TK_POST)
        q_s = norm_mm(xs, norm_x_pre[l], w["w_xq"], l, tm=bd, tn=TN)
        o_s = xattn_step(q_s, cache_mem_kv, l).astype(BF16)
        xs = mm_post(o_s, w["w_xo"], l, norm_x_post[l], xs, tm=bd, tk=TK_POST)
        act_s, conv_new_t = ffn_up_step(xs, norm_ffn_pre[l], w["w_up"], conv_w, conv_b3, l,
                                        jnp.swapaxes(state_ffn_conv[l], 0, 1), tn=TN_FFN)
        xs = mm_post(act_s, w["w_down"], l, norm_ffn_post[l], xs, tm=bd, tk=TK_POST)

        pool_s.append(jnp.swapaxes(pool_new_t, 0, 1))
        for g in range(3):
            kv = zs[:, COL_DIL_K[g]:COL_DIL_K[g] + 2 * DIL_WIDTH]
            dil_s[g].append(kv.reshape(bd, 1, 2, DIL_HEADS, DIL_HEAD_DIM))
        gla_s.append(gla_new_s)
        conv_s.append(jnp.swapaxes(conv_new_t, 0, 1))

    return (xp.reshape(b, seq, d), xs.reshape(bd, 1, d),
            jnp.stack(pool_p), jnp.stack(dil_p[0]), jnp.stack(dil_p[1]), jnp.stack(dil_p[2]), jnp.stack(gla_p), jnp.stack(mem_p), jnp.stack(conv_p),
            jnp.stack(pool_s), jnp.stack(dil_s[0]), jnp.stack(dil_s[1]), jnp.stack(dil_s[2]), jnp.stack(gla_s), jnp.stack(conv_s))
```
